```python
import jax, jax.numpy as jnp
from jax import lax
import numpy as np

D_MODEL = 1024
BATCH = 8
SEQ = 2048
DEPTH = 1
DEC_BATCH = 128
DEC_SEQ = 1
PAST_LEN = 16384
PAGE_SIZE = 128

D_RNN = D_MODEL
RG_BLOCKS = 8
RG_BLOCK_W = D_RNN // RG_BLOCKS
RG_C = 8.0
CONV_W = 4
DN_HEADS = 8
DN_DK = 128
DN_DV = 128
DN_QK = DN_HEADS * DN_DK
DN_V = DN_HEADS * DN_DV
DN_QKV = 2 * DN_QK + DN_V
DN_CHUNK = 64
D_FF = 2816
N_ADA = 9
EPS = 1e-6
IN_SPLITS = (D_RNN, D_RNN, DN_QKV, DN_HEADS, DN_HEADS, DN_V, D_MODEL, D_MODEL)
D_IN = D_RNN * 2 + DN_QKV + 2 * DN_HEADS + DN_V + 2 * D_MODEL

kernel_name = 'hybrid_rglru_gdn_macaron_adaln_step'


def rmsnorm(x, g):
    xf = x.astype(jnp.float32)
    y = xf * lax.rsqrt(jnp.mean(xf * xf, axis=-1, keepdims=True) + EPS)
    return (y * g.astype(jnp.float32)).astype(x.dtype)


def l2norm(x):
    xf = x.astype(jnp.float32)
    return xf * lax.rsqrt(jnp.sum(xf * xf, axis=-1, keepdims=True) + EPS)


def modulate(h, shift, scale):
    return h * (1.0 + scale) + shift


def swiglu(h, w_up, w_down):
    gv = h @ w_up
    return (jax.nn.silu(gv[..., :D_FF]) * gv[..., D_FF:]) @ w_down


def split_cols(z, sizes):
    out, start = [], 0
    for s in sizes:
        out.append(z[..., start:start + s])
        start += s
    return out


def causal_dwconv(x, buf, w, b):
    L = x.shape[1]
    xp = jnp.concatenate([buf.astype(x.dtype), x], axis=1)
    y = xp[:, 0:L] * w[0]
    for j in range(1, CONV_W):
        y = y + xp[:, j:j + L] * w[j]
    if b is not None:
        y = y + b
    return y, xp[:, -(CONV_W - 1):]


def rglru(x, h0, w_a, b_a, w_x, b_x, lam, reset_first):
    B, L, _ = x.shape
    xb = x.reshape(B, L, RG_BLOCKS, RG_BLOCK_W)
    r = jax.nn.sigmoid(jnp.einsum('blnc,ncd->blnd', xb, w_a).reshape(B, L, D_RNN) + b_a)
    i = jax.nn.sigmoid(jnp.einsum('blnc,ncd->blnd', xb, w_x).reshape(B, L, D_RNN) + b_x)
    log_a = -RG_C * r.astype(jnp.float32) * jax.nn.softplus(-lam.astype(jnp.float32))
    a = jnp.exp(log_a)
    mult = jnp.sqrt(-jnp.expm1(2.0 * log_a))
    if reset_first:
        mult = mult.at[:, 0].set(1.0)
    bterm = mult * (i * x).astype(jnp.float32)
    bterm = bterm.at[:, 0].add(a[:, 0] * h0.astype(jnp.float32))

    def combine(lhs, rhs):
        a1, b1 = lhs
        a2, b2 = rhs
        return a1 * a2, a2 * b1 + b2

    _, h = lax.associative_scan(combine, (a, bterm), axis=1)
    return h.astype(x.dtype), h[:, -1].astype(x.dtype)


def gated_delta_rule(q, k, v, g, beta, S0):
    B, L = q.shape[0], q.shape[1]
    C = min(DN_CHUNK, L)
    n = -(-L // C)
    pad = n * C - L

    def prep(t):
        t = jnp.pad(t, [(0, 0), (0, pad)] + [(0, 0)] * (t.ndim - 2))
        t = t.reshape((B, n, C) + t.shape[2:])
        return jnp.moveaxis(t, 3, 1)

    q, k, v, g, beta = prep(q), prep(k), prep(v), prep(g), prep(beta)
    gcum = jnp.cumsum(g, axis=-1)
    idx = jnp.arange(C)
    incl = idx[:, None] >= idx[None, :]
    strict = idx[:, None] > idx[None, :]
    decay = jnp.exp(jnp.where(incl, gcum[..., :, None] - gcum[..., None, :], -jnp.inf))
    kb = k * beta[..., None]
    lmat = jnp.where(strict, jnp.einsum('bhnik,bhnjk->bhnij', kb, k) * decay, 0.0)
    tmat = lmat + jnp.eye(C, dtype=lmat.dtype)
    rhs = jnp.concatenate([v * beta[..., None], kb * jnp.exp(gcum)[..., None]], axis=-1)
    sol = lax.linalg.triangular_solve(tmat, rhs, left_side=True, lower=True, unit_diagonal=True)
    u, w = sol[..., :DN_DV], sol[..., DN_DV:]
    attn = jnp.einsum('bhnik,bhnjk->bhnij', q, k) * decay
    q_dec = q * jnp.exp(gcum)[..., None]
    g_last = gcum[..., -1]
    k_dec = k * jnp.exp(g_last[..., None] - gcum)[..., None]

    def step(S, xs):
        u_c, w_c, attn_c, qd_c, kd_c, gl_c = xs
        v_new = u_c - jnp.einsum('bhck,bhkv->bhcv', w_c, S)
        o = jnp.einsum('bhck,bhkv->bhcv', qd_c, S) + jnp.einsum('bhij,bhjv->bhiv', attn_c, v_new)
        S = S * jnp.exp(gl_c)[..., None, None] + jnp.einsum('bhck,bhcv->bhkv', kd_c, v_new)
        return S, o

    xs = (jnp.moveaxis(u, 2, 0), jnp.moveaxis(w, 2, 0), jnp.moveaxis(attn, 2, 0),
          jnp.moveaxis(q_dec, 2, 0), jnp.moveaxis(k_dec, 2, 0), jnp.moveaxis(g_last, 2, 0))
    S, o = lax.scan(step, S0.astype(jnp.float32), xs)
    o = jnp.moveaxis(o, 0, 2)
    o = jnp.moveaxis(o, 1, 3).reshape(B, n * C, DN_HEADS, DN_DV)[:, :L]
    return o, S


def layer(x, c, h0, conv_rnn0, S0, conv_qkv0, reset_first, p):
    B, L, _ = x.shape
    ada = (c @ p['w_ada'] + p['b_ada']).reshape(B, N_ADA, D_MODEL)[:, :, None, :]
    sh1, sc1, gt1 = ada[:, 0], ada[:, 1], ada[:, 2]
    sh2, sc2, gt2 = ada[:, 3], ada[:, 4], ada[:, 5]
    sh3, sc3, gt3 = ada[:, 6], ada[:, 7], ada[:, 8]

    h = modulate(rmsnorm(x, p['norm_ffn1']), sh1, sc1)
    x = x + 0.5 * gt1 * swiglu(h, p['w_ffn1_up'], p['w_ffn1_down'])

    h = modulate(rmsnorm(x, p['norm_mix']), sh2, sc2)
    z = h @ p['w_in']
    xr, gr, qkv, a_logit, b_logit, zg, mg_a, mg_b = split_cols(z, IN_SPLITS)

    xr, conv_rnn_new = causal_dwconv(xr, conv_rnn0, p['conv_rnn_w'], p['conv_rnn_b'])
    yr, h_new = rglru(xr, h0, p['rg_w_a'], p['rg_b_a'], p['rg_w_x'], p['rg_b_x'], p['rg_lambda'], reset_first)
    o_a = yr * jax.nn.gelu(gr)

    qkv, conv_qkv_new = causal_dwconv(qkv, conv_qkv0, p['conv_qkv_w'], None)
    qkv = jax.nn.silu(qkv)
    q, k, v = split_cols(qkv, (DN_QK, DN_QK, DN_V))
    q = l2norm(q.reshape(B, L, DN_HEADS, DN_DK)) * (DN_DK ** -0.5)
    k = l2norm(k.reshape(B, L, DN_HEADS, DN_DK))
    v = v.reshape(B, L, DN_HEADS, DN_DV).astype(jnp.float32)
    g = -jnp.exp(p['dn_a_log'].astype(jnp.float32)) * jax.nn.softplus(
        a_logit.astype(jnp.float32) + p['dn_dt_bias'].astype(jnp.float32))
    beta = jax.nn.sigmoid(b_logit.astype(jnp.float32))
    o, S_new = gated_delta_rule(q, k, v, g, beta, S0)
    o = rmsnorm(o, p['dn_norm']) * jax.nn.silu(zg.reshape(B, L, DN_HEADS, DN_DV).astype(jnp.float32))
    o_b = o.reshape(B, L, DN_V).astype(x.dtype)

    y_a = o_a @ p['w_branch'][0]
    y_b = o_b @ p['w_branch'][1]
    merged = jax.nn.sigmoid(mg_a) * y_a + jax.nn.sigmoid(mg_b) * y_b
    x = x + gt2 * (merged @ p['w_out'])

    h = modulate(rmsnorm(x, p['norm_ffn2']), sh3, sc3)
    x = x + 0.5 * gt3 * swiglu(h, p['w_ffn2_up'], p['w_ffn2_down'])
    return x, h_new, conv_rnn_new, S_new.astype(x.dtype), conv_qkv_new


def setup_inputs(seed: int = 0) -> dict:
    key = jax.random.key(seed)
    ks = jax.random.split(key, 40)
    f32 = jnp.float32
    nrm = lambda k, s, sc: jax.random.normal(k, s, f32) * sc
    a0 = jax.random.uniform(ks[30], (DEPTH, D_RNN), f32, 0.9, 0.999)
    s0 = a0 ** (1.0 / RG_C)
    dt = jnp.exp(jax.random.uniform(ks[31], (DEPTH, DN_HEADS), f32, np.log(1e-3), np.log(1e-1)))
    return {
        'x_prompt': nrm(ks[0], (BATCH, SEQ, D_MODEL), 1.0),
        'x_sample': nrm(ks[1], (DEC_BATCH, DEC_SEQ, D_MODEL), 1.0),
        'c_prompt': nrm(ks[2], (BATCH, D_MODEL), 1.0),
        'c_sample': nrm(ks[3], (DEC_BATCH, D_MODEL), 1.0),
        'state_rglru_h': nrm(ks[4], (DEPTH, DEC_BATCH, D_RNN), 0.5),
        'state_rglru_conv': nrm(ks[5], (DEPTH, DEC_BATCH, CONV_W - 1, D_RNN), 1.0),
        'state_delta_S': nrm(ks[6], (DEPTH, DEC_BATCH, DN_HEADS, DN_DK, DN_DV), 0.05),
        'state_delta_conv': nrm(ks[7], (DEPTH, DEC_BATCH, CONV_W - 1, DN_QKV), 1.0),
        'w_ada': nrm(ks[8], (DEPTH, D_MODEL, N_ADA * D_MODEL), 0.5 * D_MODEL ** -0.5),
        'b_ada': nrm(ks[9], (DEPTH, N_ADA * D_MODEL), 0.1),
        'norm_ffn1': 1.0 + nrm(ks[10], (DEPTH, D_MODEL), 0.02),
        'w_ffn1_up': nrm(ks[11], (DEPTH, D_MODEL, 2 * D_FF), D_MODEL ** -0.5),
        'w_ffn1_down': nrm(ks[12], (DEPTH, D_FF, D_MODEL), D_FF ** -0.5),
        'norm_mix': 1.0 + nrm(ks[13], (DEPTH, D_MODEL), 0.02),
        'w_in': nrm(ks[14], (DEPTH, D_MODEL, D_IN), D_MODEL ** -0.5),
        'conv_rnn_w': nrm(ks[15], (DEPTH, CONV_W, D_RNN), CONV_W ** -0.5),
        'conv_rnn_b': nrm(ks[16], (DEPTH, D_RNN), 0.02),
        'rg_w_a': nrm(ks[17], (DEPTH, RG_BLOCKS, RG_BLOCK_W, RG_BLOCK_W), RG_BLOCK_W ** -0.5),
        'rg_b_a': nrm(ks[18], (DEPTH, D_RNN), 0.02),
        'rg_w_x': nrm(ks[19], (DEPTH, RG_BLOCKS, RG_BLOCK_W, RG_BLOCK_W), RG_BLOCK_W ** -0.5),
        'rg_b_x': nrm(ks[20], (DEPTH, D_RNN), 0.02),
        'rg_lambda': jnp.log(s0) - jnp.log1p(-s0),
        'conv_qkv_w': nrm(ks[21], (DEPTH, CONV_W, DN_QKV), CONV_W ** -0.5),
        'dn_a_log': jnp.log(jax.random.uniform(ks[22], (DEPTH, DN_HEADS), f32, 1.0, 16.0)),
        'dn_dt_bias': dt + jnp.log(-jnp.expm1(-dt)),
        'dn_norm': 1.0 + nrm(ks[23], (DEPTH, DN_DV), 0.02),
        'w_branch': nrm(ks[24], (DEPTH, 2, D_RNN, D_MODEL), D_RNN ** -0.5),
        'w_out': nrm(ks[25], (DEPTH, D_MODEL, D_MODEL), D_MODEL ** -0.5),
        'norm_ffn2': 1.0 + nrm(ks[26], (DEPTH, D_MODEL), 0.02),
        'w_ffn2_up': nrm(ks[27], (DEPTH, D_MODEL, 2 * D_FF), D_MODEL ** -0.5),
        'w_ffn2_down': nrm(ks[28], (DEPTH, D_FF, D_MODEL), D_FF ** -0.5),
        'norm_final': 1.0 + nrm(ks[29], (D_MODEL,), 0.02),
    }


def reference(x_prompt, x_sample, c_prompt, c_sample, state_rglru_h, state_rglru_conv, state_delta_S,
              state_delta_conv, w_ada, b_ada, norm_ffn1, w_ffn1_up, w_ffn1_down, norm_mix, w_in,
              conv_rnn_w, conv_rnn_b, rg_w_a, rg_b_a, rg_w_x, rg_b_x, rg_lambda, conv_qkv_w,
              dn_a_log, dn_dt_bias, dn_norm, w_branch, w_out, norm_ffn2, w_ffn2_up, w_ffn2_down,
              norm_final):
    dt = x_prompt.dtype
    xp, xs = x_prompt, x_sample
    hp, cp, sp, qp = [], [], [], []
    hs, cs, ss, qs = [], [], [], []
    for l in range(DEPTH):
        p = {
            'w_ada': w_ada[l], 'b_ada': b_ada[l], 'norm_ffn1': norm_ffn1[l], 'w_ffn1_up': w_ffn1_up[l],
            'w_ffn1_down': w_ffn1_down[l], 'norm_mix': norm_mix[l], 'w_in': w_in[l],
            'conv_rnn_w': conv_rnn_w[l], 'conv_rnn_b': conv_rnn_b[l], 'rg_w_a': rg_w_a[l], 'rg_b_a': rg_b_a[l],
            'rg_w_x': rg_w_x[l], 'rg_b_x': rg_b_x[l], 'rg_lambda': rg_lambda[l], 'conv_qkv_w': conv_qkv_w[l],
            'dn_a_log': dn_a_log[l], 'dn_dt_bias': dn_dt_bias[l], 'dn_norm': dn_norm[l],
            'w_branch': w_branch[l], 'w_out': w_out[l], 'norm_ffn2': norm_ffn2[l],
            'w_ffn2_up': w_ffn2_up[l], 'w_ffn2_down': w_ffn2_down[l],
        }
        xp, h1, c1, s1, q1 = layer(
            xp, c_prompt,
            jnp.zeros((BATCH, D_RNN), dt),
            jnp.zeros((BATCH, CONV_W - 1, D_RNN), dt),
            jnp.zeros((BATCH, DN_HEADS, DN_DK, DN_DV), dt),
            jnp.zeros((BATCH, CONV_W - 1, DN_QKV), dt),
            True, p)
        xs, h2, c2, s2, q2 = layer(
            xs, c_sample, state_rglru_h[l], state_rglru_conv[l], state_delta_S[l], state_delta_conv[l],
            False, p)
        hp.append(h1); cp.append(c1); sp.append(s1); qp.append(q1)
        hs.append(h2); cs.append(c2); ss.append(s2); qs.append(q2)
    y_prompt = rmsnorm(xp, norm_final)
    y_sample = rmsnorm(xs, norm_final)
    return (y_prompt, y_sample,
            jnp.stack(hp), jnp.stack(cp), jnp.stack(sp), jnp.stack(qp),
            jnp.stack(hs), jnp.stack(cs), jnp.stack(ss), jnp.stack(qs))
```

```python
import functools

import jax
import jax.numpy as jnp
from jax import lax
from jax.experimental import pallas as pl
from jax.experimental.pallas import tpu as pltpu

D_MODEL = 1024
D_RNN = 1024
RG_BLOCKS = 8
RG_BLOCK_W = D_RNN // RG_BLOCKS
RG_C = 8.0
CONV_W = 4
DN_HEADS = 8
DN_DK = 128
DN_DV = 128
DN_QK = DN_HEADS * DN_DK
DN_V = DN_HEADS * DN_DV
DN_QKV = 2 * DN_QK + DN_V
DN_CHUNK = 64
D_FF = 2816
N_ADA = 9
EPS = 1e-6
LANES = 128
SUBLANES = 8
AB_OFF = D_RNN * 2 + DN_QKV
N_MAIN = 8 * D_MODEL

BF16 = jnp.bfloat16
F32 = jnp.float32
HI = lax.Precision.HIGHEST

VMEM_LIMIT = 56 * 1024 * 1024


def _params(sem):
    return pltpu.CompilerParams(dimension_semantics=sem, vmem_limit_bytes=VMEM_LIMIT)


def _resident(shape):
    nd = len(shape)
    return pl.BlockSpec(shape, lambda *_: (0,) * nd, pipeline_mode=pl.Buffered(1))


def _dot(a, b):
    return jnp.dot(a, b, preferred_element_type=F32)


def _dot_nt(a, b, precision=None):
    return lax.dot_general(a, b, (((1,), (1,)), ((), ())), precision=precision,
                           preferred_element_type=F32)


def _dot_tn(a, b, precision=None):
    return lax.dot_general(a, b, (((0,), (0,)), ((), ())), precision=precision,
                           preferred_element_type=F32)


def _sigmoid(x):
    return jax.nn.sigmoid(x)


def _silu(x):
    return x * jax.nn.sigmoid(x)


def _softplus(x):
    return jnp.maximum(x, 0.0) + jnp.log1p(jnp.exp(-jnp.abs(x)))


def _neg_expm1(x):
    return -jnp.tanh(0.5 * x) * (jnp.exp(x) + 1.0)


def _ada_rows(ada_ref, k, per_row):
    if per_row:
        return ada_ref[k]
    return ada_ref[k:k + 1, :]


def _norm_mod(x, nw, shift, scale):
    ms = jnp.mean(x * x, axis=-1, keepdims=True)
    hn = x * lax.rsqrt(ms + EPS) * nw
    return hn * (1.0 + scale) + shift


def _ada_kernel(c_ref, w_ref, b_ref, o_ref):
    o_ref[...] = _dot(c_ref[...].astype(BF16), w_ref[...].astype(BF16)) + b_ref[...]


def _ada_call(c_all, w_ada, b_ada):
    n = c_all.shape[0]
    tn = D_MODEL
    return pl.pallas_call(
        _ada_kernel,
        grid=(N_ADA * D_MODEL // tn,),
        in_specs=[pl.BlockSpec((n, D_MODEL), lambda j: (0, 0)),
                  pl.BlockSpec((D_MODEL, tn), lambda j: (0, j)),
                  pl.BlockSpec((1, tn), lambda j: (0, j))],
        out_specs=pl.BlockSpec((n, tn), lambda j: (0, j)),
        out_shape=jax.ShapeDtypeStruct((n, N_ADA * D_MODEL), F32),
        compiler_params=_params(("arbitrary",)),
        name="ada",
    )(c_all, w_ada, b_ada)


FFN_TF = 256


def _ffn_kernel(x_ref, ada_ref, nw_ref, wup_ref, wdn_ref, nf_ref, o_ref, *, k0, per_row, final_norm):
    x = x_ref[...]
    h = _norm_mod(x, nw_ref[...], _ada_rows(ada_ref, k0, per_row),
                  _ada_rows(ada_ref, k0 + 1, per_row)).astype(BF16)
    acc = jnp.zeros(x.shape, F32)
    for j in range(D_FF // FFN_TF):
        g = _dot(h, wup_ref[:, j * FFN_TF:(j + 1) * FFN_TF])
        v = _dot(h, wup_ref[:, D_FF + j * FFN_TF:D_FF + (j + 1) * FFN_TF])
        a = (_silu(g) * v).astype(BF16)
        acc = acc + _dot(a, wdn_ref[j * FFN_TF:(j + 1) * FFN_TF, :])
    y = x + 0.5 * _ada_rows(ada_ref, k0 + 2, per_row) * acc
    if final_norm:
        ms = jnp.mean(y * y, axis=-1, keepdims=True)
        y = y * lax.rsqrt(ms + EPS) * nf_ref[...]
    o_ref[...] = y


def _ada_spec(per_row, rows, tiles_per_seq):
    if per_row:
        return pl.BlockSpec((N_ADA, rows, D_MODEL), lambda i: (0, 0, 0))
    return pl.BlockSpec((None, N_ADA, D_MODEL), lambda i: (i // tiles_per_seq, 0, 0))


def _ffn_call(x, ada, nw, wup, wdn, nf, *, k0, per_row, final_norm, tm, seq_len):
    m = x.shape[0]
    kern = functools.partial(_ffn_kernel, k0=k0, per_row=per_row, final_norm=final_norm)
    return pl.pallas_call(
        kern,
        grid=(m // tm,),
        in_specs=[pl.BlockSpec((tm, D_MODEL), lambda i: (i, 0)),
                  _ada_spec(per_row, m, seq_len // tm),
                  _resident((1, D_MODEL)),
                  _resident((D_MODEL, 2 * D_FF)),
                  _resident((D_FF, D_MODEL)),
                  _resident((1, D_MODEL))],
        out_specs=pl.BlockSpec((tm, D_MODEL), lambda i: (i, 0)),
        out_shape=jax.ShapeDtypeStruct((m, D_MODEL), F32),
        compiler_params=_params(("arbitrary",)),
        name="ffn",
    )(x, ada, nw, wup, wdn, nf)


def _inproj_kernel(x_ref, ada_ref, nw_ref, w_ref, wab_ref,
                   xr_ref, ggr_ref, qkv_ref, szg_ref, sga_ref, sgb_ref, gb_ref, *, per_row):
    h = _norm_mod(x_ref[...], nw_ref[...], _ada_rows(ada_ref, 3, per_row),
                  _ada_rows(ada_ref, 4, per_row)).astype(BF16)

    def col(c):
        return _dot(h, w_ref[:, c * D_MODEL:(c + 1) * D_MODEL])

    xr_ref[...] = col(0)
    ggr_ref[...] = jax.nn.gelu(col(1))
    for c in range(3):
        qkv_ref[:, c * D_MODEL:(c + 1) * D_MODEL] = col(2 + c)
    szg_ref[...] = _silu(col(5))
    sga_ref[...] = _sigmoid(col(6))
    sgb_ref[...] = _sigmoid(col(7))
    gb_ref[...] = _dot(h, wab_ref[...])


def _inproj_call(x, ada, nw, w_main, w_ab, *, per_row, tm, seq_len):
    m = x.shape[0]
    row = lambda n: pl.BlockSpec((tm, n), lambda i: (i, 0))
    shp = lambda n: jax.ShapeDtypeStruct((m, n), F32)
    return pl.pallas_call(
        functools.partial(_inproj_kernel, per_row=per_row),
        grid=(m // tm,),
        in_specs=[row(D_MODEL), _ada_spec(per_row, m, seq_len // tm),
                  _resident((1, D_MODEL)), _resident((D_MODEL, N_MAIN)), _resident((D_MODEL, LANES))],
        out_specs=[row(D_RNN), row(D_RNN), row(DN_QKV), row(DN_V), row(D_MODEL), row(D_MODEL), row(LANES)],
        out_shape=[shp(D_RNN), shp(D_RNN), shp(DN_QKV), shp(DN_V), shp(D_MODEL), shp(D_MODEL), shp(LANES)],
        compiler_params=_params(("arbitrary",)),
        name="inproj",
    )(x, ada, nw, w_main, w_ab)


def _rg_gates(xc, wa_ref, ba, wx_ref, bx, lam):
    ra, ix = [], []
    for n in range(RG_BLOCKS):
        xb = xc[:, n * RG_BLOCK_W:(n + 1) * RG_BLOCK_W].astype(BF16)
        ra.append(_dot(xb, wa_ref[n]))
        ix.append(_dot(xb, wx_ref[n]))
    r = _sigmoid(jnp.concatenate(ra, axis=1) + ba)
    i = _sigmoid(jnp.concatenate(ix, axis=1) + bx)
    log_a = (-RG_C) * r * _softplus(-lam)
    return jnp.exp(log_a), log_a, i


def _rglru_kernel(xr_ref, ggr_ref, cw_ref, cb_ref, wa_ref, ba_ref, wx_ref, bx_ref, lam_ref,
                  oa_ref, hl_ref, cs_ref, xbuf, hcar, a_s, b_s, y_s, *, tl):
    t = pl.program_id(1)

    @pl.when(t == 0)
    def _():
        xbuf[0:SUBLANES, :] = jnp.zeros((SUBLANES, D_RNN), F32)
        hcar[...] = jnp.zeros(hcar.shape, F32)

    xbuf[SUBLANES:SUBLANES + tl, :] = xr_ref[...]
    off = SUBLANES - (CONV_W - 1)
    xc = cb_ref[...] + cw_ref[0:1, :] * xbuf[off:off + tl, :]
    for j in range(1, CONV_W):
        xc = xc + cw_ref[j:j + 1, :] * xbuf[off + j:off + j + tl, :]
    tail = xbuf[tl:tl + SUBLANES, :]
    xbuf[0:SUBLANES, :] = tail
    cs_ref[0] = tail[SUBLANES - (CONV_W - 1):, :]

    a, log_a, i = _rg_gates(xc, wa_ref, ba_ref[...], wx_ref, bx_ref[...], lam_ref[...])
    mult = jnp.sqrt(_neg_expm1(2.0 * log_a))
    row = lax.broadcasted_iota(jnp.int32, (tl, 1), 0)
    mult = jnp.where(jnp.logical_and(row == 0, t == 0), 1.0, mult)
    a_s[...] = a
    b_s[...] = mult * (i * xc)

    def step(s, h):
        h = a_s[pl.ds(s, 1), :] * h + b_s[pl.ds(s, 1), :]
        y_s[pl.ds(s, 1), :] = h
        return h

    h = lax.fori_loop(0, tl, step, hcar[0:1, :], unroll=8)
    hcar[0:1, :] = h
    hl_ref[0] = h
    oa_ref[...] = y_s[...] * ggr_ref[...]


def _rglru_call(xr, ggr, cw, cb, wa, ba, wx, bx, lam, *, batch, seq_len, tl):
    nt = seq_len // tl
    row = pl.BlockSpec((tl, D_RNN), lambda b, t: (b * nt + t, 0))
    return pl.pallas_call(
        functools.partial(_rglru_kernel, tl=tl),
        grid=(batch, nt),
        in_specs=[row, row,
                  _resident((CONV_W, D_RNN)), _resident((1, D_RNN)),
                  _resident((RG_BLOCKS, RG_BLOCK_W, RG_BLOCK_W)), _resident((1, D_RNN)),
                  _resident((RG_BLOCKS, RG_BLOCK_W, RG_BLOCK_W)), _resident((1, D_RNN)),
                  _resident((1, D_RNN))],
        out_specs=[row,
                   pl.BlockSpec((1, 1, D_RNN), lambda b, t: (b, 0, 0)),
                   pl.BlockSpec((1, CONV_W - 1, D_RNN), lambda b, t: (b, 0, 0))],
        out_shape=[jax.ShapeDtypeStruct((batch * seq_len, D_RNN), F32),
                   jax.ShapeDtypeStruct((batch, 1, D_RNN), F32),
                   jax.ShapeDtypeStruct((batch, CONV_W - 1, D_RNN), F32)],
        scratch_shapes=[pltpu.VMEM((SUBLANES + tl, D_RNN), F32),
                        pltpu.VMEM((SUBLANES, D_RNN), F32),
                        pltpu.VMEM((tl, D_RNN), F32),
                        pltpu.VMEM((tl, D_RNN), F32),
                        pltpu.VMEM((tl, D_RNN), F32)],
        compiler_params=_params(("arbitrary", "arbitrary")),
        name="rglru",
    )(xr, ggr, cw, cb, wa, ba, wx, bx, lam)


def _unit_lower_inverse(lm):
    c = lm.shape[0]
    eye = (lax.broadcasted_iota(jnp.int32, (c, c), 0) == lax.broadcasted_iota(jnp.int32, (c, c), 1)).astype(F32)
    m = -lm
    p = eye + m
    span = 2
    while span < c:
        m = jnp.dot(m, m, precision=HI, preferred_element_type=F32)
        p = p + jnp.dot(p, m, precision=HI, preferred_element_type=F32)
        span *= 2
    return p


def _delta_kernel(qkv_ref, gb_ref, szg_ref, cw_ref, nal_ref, dtb_ref, dnw_ref,
                  ob_ref, s_ref, cs_ref, qbuf, s_acc, *, nt):
    t = pl.program_id(1)
    c = DN_CHUNK

    @pl.when(t == 0)
    def _():
        qbuf[0:SUBLANES, :] = jnp.zeros((SUBLANES, DN_QKV), F32)
        s_acc[...] = jnp.zeros(s_acc.shape, F32)

    qbuf[SUBLANES:SUBLANES + c, :] = qkv_ref[...]
    off = SUBLANES - (CONV_W - 1)
    xc = cw_ref[0:1, :] * qbuf[off:off + c, :]
    for j in range(1, CONV_W):
        xc = xc + cw_ref[j:j + 1, :] * qbuf[off + j:off + j + c, :]
    tail = qbuf[c:c + SUBLANES, :]
    qbuf[0:SUBLANES, :] = tail
    cs_ref[0] = tail[SUBLANES - (CONV_W - 1):, :]
    act = _silu(xc)

    gbv = gb_ref[...]
    g_full = -jnp.exp(nal_ref[...]) * _softplus(gbv + dtb_ref[...])
    beta_full = _sigmoid(gbv)
    ri = lax.broadcasted_iota(jnp.int32, (c, c), 0)
    ci = lax.broadcasted_iota(jnp.int32, (c, c), 1)
    incl = ri >= ci
    strict = ri > ci
    tri = incl.astype(F32)
    gcum = jnp.dot(tri, g_full, precision=HI, preferred_element_type=F32)
    gcum_t = _dot_tn(g_full, (ri <= ci).astype(F32), precision=HI)

    for h in range(DN_HEADS):
        gc = gcum[:, h:h + 1]
        decay = jnp.exp(jnp.where(incl, gc - gcum_t[h:h + 1, :], -jnp.inf))
        bh = beta_full[:, DN_HEADS + h:DN_HEADS + h + 1]
        qh = act[:, h * DN_DK:(h + 1) * DN_DK]
        kh = act[:, DN_QK + h * DN_DK:DN_QK + (h + 1) * DN_DK]
        vh = act[:, 2 * DN_QK + h * DN_DV:2 * DN_QK + (h + 1) * DN_DV]
        qn = qh * lax.rsqrt(jnp.sum(qh * qh, axis=-1, keepdims=True) + EPS) * (DN_DK ** -0.5)
        kn = kh * lax.rsqrt(jnp.sum(kh * kh, axis=-1, keepdims=True) + EPS)
        kb = kn * bh
        kn16 = kn.astype(BF16)
        lm = jnp.where(strict, _dot_nt(kb.astype(BF16), kn16) * decay, 0.0)
        ainv = _unit_lower_inverse(lm)
        eg = jnp.exp(gc)
        u = jnp.dot(ainv, vh * bh, precision=HI, preferred_element_type=F32)
        w = jnp.dot(ainv, kb * eg, precision=HI, preferred_element_type=F32)
        attn = _dot_nt(qn.astype(BF16), kn16) * decay
        gl = gcum[c - 1:c, h:h + 1]
        kd = kn * jnp.exp(gl - gc)
        s_h = s_acc[h]
        s16 = s_h.astype(BF16)
        vnew = u - _dot(w.astype(BF16), s16)
        vnew16 = vnew.astype(BF16)
        o = _dot((qn * eg).astype(BF16), s16) + _dot(attn.astype(BF16), vnew16)
        s_acc[h] = s_h * jnp.exp(gl) + _dot_tn(kd.astype(BF16), vnew16)
        on = o * lax.rsqrt(jnp.mean(o * o, axis=-1, keepdims=True) + EPS) * dnw_ref[...]
        ob_ref[:, h * DN_DV:(h + 1) * DN_DV] = on * szg_ref[:, h * DN_DV:(h + 1) * DN_DV]

    @pl.when(t == nt - 1)
    def _():
        s_ref[0] = s_acc[...]


def _delta_call(qkv, gb, szg, cw, nal, dtb, dnw, *, batch, seq_len):
    c = DN_CHUNK
    nt = seq_len // c
    row = lambda n: pl.BlockSpec((c, n), lambda b, t: (b * nt + t, 0))
    return pl.pallas_call(
        functools.partial(_delta_kernel, nt=nt),
        grid=(batch, nt),
        in_specs=[row(DN_QKV), row(LANES), row(DN_V),
                  _resident((CONV_W, DN_QKV)), _resident((1, LANES)), _resident((1, LANES)),
                  _resident((1, DN_DV))],
        out_specs=[row(DN_V),
                   pl.BlockSpec((1, DN_HEADS, DN_DK, DN_DV), lambda b, t: (b, 0, 0, 0)),
                   pl.BlockSpec((1, CONV_W - 1, DN_QKV), lambda b, t: (b, 0, 0))],
        out_shape=[jax.ShapeDtypeStruct((batch * seq_len, DN_V), F32),
                   jax.ShapeDtypeStruct((batch, DN_HEADS, DN_DK, DN_DV), F32),
                   jax.ShapeDtypeStruct((batch, CONV_W - 1, DN_QKV), F32)],
        scratch_shapes=[pltpu.VMEM((SUBLANES + c, DN_QKV), F32),
                        pltpu.VMEM((DN_HEADS, DN_DK, DN_DV), F32)],
        compiler_params=_params(("arbitrary", "arbitrary")),
        name="delta",
    )(qkv, gb, szg, cw, nal, dtb, dnw)


def _sample_pre_kernel(xr_ref, ggr_ref, qkv_ref, gb_ref, crnn_ref, cqkv_ref, h0_ref,
                       cw_ref, cb_ref, wa_ref, ba_ref, wx_ref, bx_ref, lam_ref,
                       cqw_ref, nal_ref, dtb_ref,
                       oa_ref, hn_ref, qn_ref, kn_ref, v_ref, eg_ref, beta_ref):
    xc = cb_ref[...] + cw_ref[CONV_W - 1:CONV_W, :] * xr_ref[...]
    for j in range(CONV_W - 1):
        xc = xc + cw_ref[j:j + 1, :] * crnn_ref[j]
    a, log_a, i = _rg_gates(xc, wa_ref, ba_ref[...], wx_ref, bx_ref[...], lam_ref[...])
    mult = jnp.sqrt(_neg_expm1(2.0 * log_a))
    h = a * h0_ref[...] + mult * (i * xc)
    hn_ref[...] = h
    oa_ref[...] = h * ggr_ref[...]

    qc = cqw_ref[CONV_W - 1:CONV_W, :] * qkv_ref[...]
    for j in range(CONV_W - 1):
        qc = qc + cqw_ref[j:j + 1, :] * cqkv_ref[j]
    act = _silu(qc)
    for h_ in range(DN_HEADS):
        qh = act[:, h_ * DN_DK:(h_ + 1) * DN_DK]
        kh = act[:, DN_QK + h_ * DN_DK:DN_QK + (h_ + 1) * DN_DK]
        qn_ref[:, h_ * DN_DK:(h_ + 1) * DN_DK] = (
            qh * lax.rsqrt(jnp.sum(qh * qh, axis=-1, keepdims=True) + EPS) * (DN_DK ** -0.5))
        kn_ref[:, h_ * DN_DK:(h_ + 1) * DN_DK] = (
            kh * lax.rsqrt(jnp.sum(kh * kh, axis=-1, keepdims=True) + EPS))
    v_ref[...] = act[:, 2 * DN_QK:]
    gbv = gb_ref[...]
    eg_ref[...] = jnp.exp(-jnp.exp(nal_ref[...]) * _softplus(gbv + dtb_ref[...]))
    beta_ref[...] = _sigmoid(gbv)


def _sample_pre_call(xr, ggr, qkv, gb, crnn, cqkv, h0, cw, cb, wa, ba, wx, bx, lam, cqw, nal, dtb):
    n = xr.shape[0]
    args = (xr, ggr, qkv, gb, crnn, cqkv, h0, cw, cb, wa, ba, wx, bx, lam, cqw, nal, dtb)
    shp = lambda w: jax.ShapeDtypeStruct((n, w), F32)
    outs = [shp(D_RNN), shp(D_RNN), shp(DN_QK), shp(DN_QK), shp(DN_V), shp(LANES), shp(LANES)]
    return pl.pallas_call(
        _sample_pre_kernel,
        grid=(1,),
        in_specs=[_resident(a.shape) for a in args],
        out_specs=[pl.BlockSpec(o.shape, lambda i: (0, 0)) for o in outs],
        out_shape=outs,
        compiler_params=_params(("arbitrary",)),
        name="sample_pre",
    )(*args)


def _sample_state_kernel(qn_ref, kn_ref, v_ref, eg_ref, beta_ref, szg_ref, dnw_ref, s_ref,
                         ob_ref, sn_ref):
    srow = lax.broadcasted_iota(jnp.int32, (SUBLANES, DN_DK), 0)
    rows = lambda r0, r1: jnp.where(srow == 0, r0, jnp.where(srow == 1, r1, 0.0))
    for h in range(DN_HEADS):
        sl = slice(h * DN_DK, (h + 1) * DN_DK)
        q = qn_ref[0, :, sl]
        k = kn_ref[0, :, sl]
        v = v_ref[0, :, sl]
        eg = eg_ref[0, :, h:h + 1]
        beta = beta_ref[0, :, DN_HEADS + h:DN_HEADS + h + 1]
        s_h = s_ref[0, h]
        ws_qs = _dot(rows((k * beta) * eg, q * eg).astype(BF16), s_h.astype(BF16))
        vnew = v * beta - ws_qs[0:1, :]
        o = ws_qs[1:2, :] + jnp.sum(q * k, axis=-1, keepdims=True) * vnew
        zero = jnp.zeros_like(k)
        sn_ref[0, h] = s_h * eg + _dot_tn(rows(k, zero).astype(BF16), rows(vnew, zero).astype(BF16))
        on = o * lax.rsqrt(jnp.mean(o * o, axis=-1, keepdims=True) + EPS) * dnw_ref[...]
        ob_ref[0, :, sl] = on * szg_ref[0, :, sl]


def _sample_state_call(qn, kn, v, eg, beta, szg, dnw, s0):
    n = qn.shape[0]
    r3 = lambda a: a.reshape(n, 1, a.shape[-1])
    vec = lambda w: pl.BlockSpec((1, 1, w), lambda b: (b, 0, 0))
    s_spec = pl.BlockSpec((1, DN_HEADS, DN_DK, DN_DV), lambda b: (b, 0, 0, 0))
    ob, sn = pl.pallas_call(
        _sample_state_kernel,
        grid=(n,),
        in_specs=[vec(DN_QK), vec(DN_QK), vec(DN_V), vec(LANES), vec(LANES), vec(DN_V),
                  _resident((1, DN_DV)), s_spec],
        out_specs=[vec(DN_V), s_spec],
        out_shape=[jax.ShapeDtypeStruct((n, 1, DN_V), F32),
                   jax.ShapeDtypeStruct((n, DN_HEADS, DN_DK, DN_DV), F32)],
        compiler_params=_params(("arbitrary",)),
        name="sample_state",
    )(r3(qn), r3(kn), r3(v), r3(eg), r3(beta), r3(szg), dnw, s0)
    return ob.reshape(n, DN_V), sn


def _merge_kernel(x_ref, oa_ref, ob_ref, sga_ref, sgb_ref, ada_ref, wb_ref, wo_ref, o_ref, *, per_row):
    ya = _dot(oa_ref[...].astype(BF16), wb_ref[0])
    yb = _dot(ob_ref[...].astype(BF16), wb_ref[1])
    merged = (sga_ref[...] * ya + sgb_ref[...] * yb).astype(BF16)
    o_ref[...] = x_ref[...] + _ada_rows(ada_ref, 5, per_row) * _dot(merged, wo_ref[...])


def _merge_call(x, oa, ob, sga, sgb, ada, wb, wo, *, per_row, tm, seq_len):
    m = x.shape[0]
    row = pl.BlockSpec((tm, D_MODEL), lambda i: (i, 0))
    return pl.pallas_call(
        functools.partial(_merge_kernel, per_row=per_row),
        grid=(m // tm,),
        in_specs=[row, row, row, row, row, _ada_spec(per_row, m, seq_len // tm),
                  _resident((2, D_RNN, D_MODEL)), _resident((D_MODEL, D_MODEL))],
        out_specs=row,
        out_shape=jax.ShapeDtypeStruct((m, D_MODEL), F32),
        compiler_params=_params(("arbitrary",)),
        name="merge",
    )(x, oa, ob, sga, sgb, ada, wb, wo)


def kernel(x_prompt, x_sample, c_prompt, c_sample, state_rglru_h, state_rglru_conv, state_delta_S, state_delta_conv, w_ada, b_ada, norm_ffn1, w_ffn1_up, w_ffn1_down, norm_mix, w_in, conv_rnn_w, conv_rnn_b, rg_w_a, rg_b_a, rg_w_x, rg_b_x, rg_lambda, conv_qkv_w, dn_a_log, dn_dt_bias, dn_norm, w_branch, w_out, norm_ffn2, w_ffn2_up, w_ffn2_down, norm_final):
    batch, seq_len, _ = x_prompt.shape
    n_dec = x_sample.shape[0]
    assert w_ada.shape[0] == 1 and x_sample.shape[1] == 1

    row = lambda a: a.reshape(1, -1).astype(F32)
    wup1, wdn1 = w_ffn1_up[0].astype(BF16), w_ffn1_down[0].astype(BF16)
    wup2, wdn2 = w_ffn2_up[0].astype(BF16), w_ffn2_down[0].astype(BF16)
    w_main = jnp.concatenate([w_in[0][:, :AB_OFF], w_in[0][:, AB_OFF + 2 * DN_HEADS:]], axis=1).astype(BF16)
    w_ab = jnp.pad(w_in[0][:, AB_OFF:AB_OFF + 2 * DN_HEADS], ((0, 0), (0, LANES - 2 * DN_HEADS))).astype(BF16)
    wb, wo = w_branch[0].astype(BF16), w_out[0].astype(BF16)
    wa, wx = rg_w_a[0].astype(BF16), rg_w_x[0].astype(BF16)
    lane_pad = lambda a: jnp.pad(a.reshape(1, -1).astype(F32), ((0, 0), (0, LANES - a.size)))
    nal = lane_pad(dn_a_log[0])
    dtb = lane_pad(dn_dt_bias[0])
    nf = row(norm_final)

    ada = _ada_call(jnp.concatenate([c_prompt, c_sample], axis=0), w_ada[0], row(b_ada[0]))
    ada = ada.reshape(batch + n_dec, N_ADA, D_MODEL)
    ada_p = ada[:batch]
    ada_s = jnp.transpose(ada[batch:], (1, 0, 2))

    mixer_w = (conv_rnn_w[0], row(conv_rnn_b[0]), wa, row(rg_b_a[0]), wx, row(rg_b_x[0]), row(rg_lambda[0]))

    kw = dict(per_row=False, seq_len=seq_len)
    xp = x_prompt.reshape(batch * seq_len, D_MODEL)
    xp = _ffn_call(xp, ada_p, row(norm_ffn1[0]), wup1, wdn1, nf, k0=0, final_norm=False, tm=512, **kw)
    xr, ggr, qkv, szg, sga, sgb, gb = _inproj_call(xp, ada_p, row(norm_mix[0]), w_main, w_ab, tm=256, **kw)
    oa, hp, cp = _rglru_call(xr, ggr, *mixer_w, batch=batch, seq_len=seq_len, tl=256)
    ob, sp, qp = _delta_call(qkv, gb, szg, conv_qkv_w[0], nal, dtb, row(dn_norm[0]), batch=batch, seq_len=seq_len)
    xp = _merge_call(xp, oa, ob, sga, sgb, ada_p, wb, wo, tm=512, **kw)
    y_prompt = _ffn_call(xp, ada_p, row(norm_ffn2[0]), wup2, wdn2, nf, k0=6, final_norm=True, tm=512, **kw)

    kw = dict(per_row=True, seq_len=n_dec, tm=n_dec)
    xs = x_sample.reshape(n_dec, D_MODEL)
    xs = _ffn_call(xs, ada_s, row(norm_ffn1[0]), wup1, wdn1, nf, k0=0, final_norm=False, **kw)
    xr_s, ggr_s, qkv_s, szg_s, sga_s, sgb_s, gb_s = _inproj_call(xs, ada_s, row(norm_mix[0]), w_main, w_ab, **kw)
    crnn = jnp.transpose(state_rglru_conv[0], (1, 0, 2))
    cqkv = jnp.transpose(state_delta_conv[0], (1, 0, 2))
    oa_s, hs, qn, kn, v, eg, beta = _sample_pre_call(
        xr_s, ggr_s, qkv_s, gb_s, crnn, cqkv, state_rglru_h[0], *mixer_w, conv_qkv_w[0], nal, dtb)
    ob_s, ss = _sample_state_call(qn, kn, v, eg, beta, szg_s, row(dn_norm[0]), state_delta_S[0])
    xs = _merge_call(xs, oa_s, ob_s, sga_s, sgb_s, ada_s, wb, wo, **kw)
    y_sample = _ffn_call(xs, ada_s, row(norm_ffn2[0]), wup2, wdn2, nf, k0=6, final_norm=True, **kw)
    cs = jnp.concatenate([state_rglru_conv[0][:, 1:], xr_s[:, None, :]], axis=1)
    qs = jnp.concatenate([state_delta_conv[0][:, 1:], qkv_s[:, None, :]], axis=1)

    return (y_prompt.reshape(batch, seq_len, D_MODEL), y_sample.reshape(n_dec, 1, D_MODEL),
            hp.reshape(1, batch, D_RNN), cp[None], sp[None], qp[None],
            hs[None], cs[None], ss[None], qs[None])
```

```python
import functools

import jax
import jax.numpy as jnp
from jax import lax
from jax.experimental import pallas as pl
from jax.experimental.pallas import tpu as pltpu

D_MODEL = 1024
D_RNN = 1024
RG_BLOCKS = 8
RG_BLOCK_W = D_RNN // RG_BLOCKS
RG_C = 8.0
CONV_W = 4
DN_HEADS = 8
DN_DK = 128
DN_DV = 128
DN_QK = DN_HEADS * DN_DK
DN_V = DN_HEADS * DN_DV
DN_QKV = 2 * DN_QK + DN_V
DN_CHUNK = 64
D_FF = 2816
N_ADA = 9
EPS = 1e-6
LANES = 128
SUBLANES = 8
AB_OFF = D_RNN * 2 + DN_QKV
N_MAIN = 8 * D_MODEL

BF16 = jnp.bfloat16
F32 = jnp.float32
HI = lax.Precision.HIGHEST

VMEM_LIMIT = 56 * 1024 * 1024


def _params(sem):
    return pltpu.CompilerParams(dimension_semantics=sem, vmem_limit_bytes=VMEM_LIMIT)


def _resident(shape):
    nd = len(shape)
    return pl.BlockSpec(shape, lambda *_: (0,) * nd, pipeline_mode=pl.Buffered(1))


def _dot(a, b):
    return jnp.dot(a, b, preferred_element_type=F32)


def _dot_nt(a, b, precision=None):
    return lax.dot_general(a, b, (((1,), (1,)), ((), ())), precision=precision,
                           preferred_element_type=F32)


def _dot_tn(a, b, precision=None):
    return lax.dot_general(a, b, (((0,), (0,)), ((), ())), precision=precision,
                           preferred_element_type=F32)


def _sigmoid(x):
    return jax.nn.sigmoid(x)


def _silu(x):
    return x * jax.nn.sigmoid(x)


def _softplus(x):
    return jnp.maximum(x, 0.0) + jnp.log1p(jnp.exp(-jnp.abs(x)))


def _neg_expm1(x):
    return -jnp.tanh(0.5 * x) * (jnp.exp(x) + 1.0)


def _ada_rows(ada_ref, k, per_row):
    if per_row:
        return ada_ref[k]
    return ada_ref[k:k + 1, :]


def _norm_mod(x, nw, shift, scale):
    ms = jnp.mean(x * x, axis=-1, keepdims=True)
    hn = x * lax.rsqrt(ms + EPS) * nw
    return hn * (1.0 + scale) + shift


def _ada_kernel(c_ref, w_ref, b_ref, o_ref):
    o_ref[...] = _dot(c_ref[...].astype(BF16), w_ref[...].astype(BF16)) + b_ref[...]


def _ada_call(c_all, w_ada, b_ada):
    n = c_all.shape[0]
    tn = D_MODEL
    return pl.pallas_call(
        _ada_kernel,
        grid=(N_ADA * D_MODEL // tn,),
        in_specs=[pl.BlockSpec((n, D_MODEL), lambda j: (0, 0)),
                  pl.BlockSpec((D_MODEL, tn), lambda j: (0, j)),
                  pl.BlockSpec((1, tn), lambda j: (0, j))],
        out_specs=pl.BlockSpec((n, tn), lambda j: (0, j)),
        out_shape=jax.ShapeDtypeStruct((n, N_ADA * D_MODEL), F32),
        compiler_params=_params(("arbitrary",)),
        name="ada",
    )(c_all, w_ada, b_ada)


FFN_TF = 256


def _ffn_kernel(x_ref, ada_ref, nw_ref, wup_ref, wdn_ref, nf_ref, o_ref, *, k0, per_row, final_norm):
    x = x_ref[...]
    h = _norm_mod(x, nw_ref[...], _ada_rows(ada_ref, k0, per_row),
                  _ada_rows(ada_ref, k0 + 1, per_row)).astype(BF16)
    acc = jnp.zeros(x.shape, F32)
    for j in range(D_FF // FFN_TF):
        g = _dot(h, wup_ref[:, j * FFN_TF:(j + 1) * FFN_TF])
        v = _dot(h, wup_ref[:, D_FF + j * FFN_TF:D_FF + (j + 1) * FFN_TF])
        a = (_silu(g) * v).astype(BF16)
        acc = acc + _dot(a, wdn_ref[j * FFN_TF:(j + 1) * FFN_TF, :])
    y = x + 0.5 * _ada_rows(ada_ref, k0 + 2, per_row) * acc
    if final_norm:
        ms = jnp.mean(y * y, axis=-1, keepdims=True)
        y = y * lax.rsqrt(ms + EPS) * nf_ref[...]
    o_ref[...] = y


def _ada_spec(per_row, rows, tiles_per_seq):
    if per_row:
        return pl.BlockSpec((N_ADA, rows, D_MODEL), lambda i: (0, 0, 0))
    return pl.BlockSpec((None, N_ADA, D_MODEL), lambda i: (i // tiles_per_seq, 0, 0))


def _ffn_call(x, ada, nw, wup, wdn, nf, *, k0, per_row, final_norm, tm, seq_len):
    m = x.shape[0]
    kern = functools.partial(_ffn_kernel, k0=k0, per_row=per_row, final_norm=final_norm)
    return pl.pallas_call(
        kern,
        grid=(m // tm,),
        in_specs=[pl.BlockSpec((tm, D_MODEL), lambda i: (i, 0)),
                  _ada_spec(per_row, m, seq_len // tm),
                  _resident((1, D_MODEL)),
                  _resident((D_MODEL, 2 * D_FF)),
                  _resident((D_FF, D_MODEL)),
                  _resident((1, D_MODEL))],
        out_specs=pl.BlockSpec((tm, D_MODEL), lambda i: (i, 0)),
        out_shape=jax.ShapeDtypeStruct((m, D_MODEL), F32),
        compiler_params=_params(("arbitrary",)),
        name="ffn",
    )(x, ada, nw, wup, wdn, nf)


def _inproj_kernel(x_ref, ada_ref, nw_ref, w_ref, wab_ref,
                   xr_ref, ggr_ref, qkv_ref, szg_ref, sga_ref, sgb_ref, gb_ref, *, per_row):
    h = _norm_mod(x_ref[...], nw_ref[...], _ada_rows(ada_ref, 3, per_row),
                  _ada_rows(ada_ref, 4, per_row)).astype(BF16)

    def col(c):
        return _dot(h, w_ref[:, c * D_MODEL:(c + 1) * D_MODEL])

    xr_ref[...] = col(0)
    ggr_ref[...] = jax.nn.gelu(col(1))
    for c in range(3):
        qkv_ref[:, c * D_MODEL:(c + 1) * D_MODEL] = col(2 + c)
    szg_ref[...] = _silu(col(5))
    sga_ref[...] = _sigmoid(col(6))
    sgb_ref[...] = _sigmoid(col(7))
    gb_ref[...] = _dot(h, wab_ref[...])


def _inproj_call(x, ada, nw, w_main, w_ab, *, per_row, tm, seq_len):
    m = x.shape[0]
    row = lambda n: pl.BlockSpec((tm, n), lambda i: (i, 0))
    shp = lambda n: jax.ShapeDtypeStruct((m, n), F32)
    return pl.pallas_call(
        functools.partial(_inproj_kernel, per_row=per_row),
        grid=(m // tm,),
        in_specs=[row(D_MODEL), _ada_spec(per_row, m, seq_len // tm),
                  _resident((1, D_MODEL)), _resident((D_MODEL, N_MAIN)), _resident((D_MODEL, LANES))],
        out_specs=[row(D_RNN), row(D_RNN), row(DN_QKV), row(DN_V), row(D_MODEL), row(D_MODEL), row(LANES)],
        out_shape=[shp(D_RNN), shp(D_RNN), shp(DN_QKV), shp(DN_V), shp(D_MODEL), shp(D_MODEL), shp(LANES)],
        compiler_params=_params(("arbitrary",)),
        name="inproj",
    )(x, ada, nw, w_main, w_ab)


def _rg_gates(xc, wa_ref, ba, wx_ref, bx, lam):
    ra, ix = [], []
    for n in range(RG_BLOCKS):
        xb = xc[:, n * RG_BLOCK_W:(n + 1) * RG_BLOCK_W].astype(BF16)
        ra.append(_dot(xb, wa_ref[n]))
        ix.append(_dot(xb, wx_ref[n]))
    r = _sigmoid(jnp.concatenate(ra, axis=1) + ba)
    i = _sigmoid(jnp.concatenate(ix, axis=1) + bx)
    log_a = (-RG_C) * r * _softplus(-lam)
    return jnp.exp(log_a), log_a, i


def _rglru_kernel(xr_ref, ggr_ref, cw_ref, cb_ref, wa_ref, ba_ref, wx_ref, bx_ref, lam_ref,
                  oa_ref, hl_ref, cs_ref, xbuf, hcar, a_s, b_s, y_s, *, tl):
    t = pl.program_id(1)

    @pl.when(t == 0)
    def _():
        xbuf[0:SUBLANES, :] = jnp.zeros((SUBLANES, D_RNN), F32)
        hcar[...] = jnp.zeros(hcar.shape, F32)

    xbuf[SUBLANES:SUBLANES + tl, :] = xr_ref[...]
    off = SUBLANES - (CONV_W - 1)
    xc = cb_ref[...] + cw_ref[0:1, :] * xbuf[off:off + tl, :]
    for j in range(1, CONV_W):
        xc = xc + cw_ref[j:j + 1, :] * xbuf[off + j:off + j + tl, :]
    tail = xbuf[tl:tl + SUBLANES, :]
    xbuf[0:SUBLANES, :] = tail
    cs_ref[0] = tail[SUBLANES - (CONV_W - 1):, :]

    a, log_a, i = _rg_gates(xc, wa_ref, ba_ref[...], wx_ref, bx_ref[...], lam_ref[...])
    mult = jnp.sqrt(_neg_expm1(2.0 * log_a))
    row = lax.broadcasted_iota(jnp.int32, (tl, 1), 0)
    mult = jnp.where(jnp.logical_and(row == 0, t == 0), 1.0, mult)
    a_s[...] = a
    b_s[...] = mult * (i * xc)

    def step(s, h):
        h = a_s[pl.ds(s, 1), :] * h + b_s[pl.ds(s, 1), :]
        y_s[pl.ds(s, 1), :] = h
        return h

    h = lax.fori_loop(0, tl, step, hcar[0:1, :], unroll=8)
    hcar[0:1, :] = h
    hl_ref[0] = h
    oa_ref[...] = y_s[...] * ggr_ref[...]


def _rglru_call(xr, ggr, cw, cb, wa, ba, wx, bx, lam, *, batch, seq_len, tl):
    nt = seq_len // tl
    row = pl.BlockSpec((tl, D_RNN), lambda b, t: (b * nt + t, 0))
    return pl.pallas_call(
        functools.partial(_rglru_kernel, tl=tl),
        grid=(batch, nt),
        in_specs=[row, row,
                  _resident((CONV_W, D_RNN)), _resident((1, D_RNN)),
                  _resident((RG_BLOCKS, RG_BLOCK_W, RG_BLOCK_W)), _resident((1, D_RNN)),
                  _resident((RG_BLOCKS, RG_BLOCK_W, RG_BLOCK_W)), _resident((1, D_RNN)),
                  _resident((1, D_RNN))],
        out_specs=[row,
                   pl.BlockSpec((1, 1, D_RNN), lambda b, t: (b, 0, 0)),
                   pl.BlockSpec((1, CONV_W - 1, D_RNN), lambda b, t: (b, 0, 0))],
        out_shape=[jax.ShapeDtypeStruct((batch * seq_len, D_RNN), F32),
                   jax.ShapeDtypeStruct((batch, 1, D_RNN), F32),
                   jax.ShapeDtypeStruct((batch, CONV_W - 1, D_RNN), F32)],
        scratch_shapes=[pltpu.VMEM((SUBLANES + tl, D_RNN), F32),
                        pltpu.VMEM((SUBLANES, D_RNN), F32),
                        pltpu.VMEM((tl, D_RNN), F32),
                        pltpu.VMEM((tl, D_RNN), F32),
                        pltpu.VMEM((tl, D_RNN), F32)],
        compiler_params=_params(("arbitrary", "arbitrary")),
        name="rglru",
    )(xr, ggr, cw, cb, wa, ba, wx, bx, lam)


def _split(x):
    hi = x.astype(BF16)
    return hi, (x - hi.astype(F32)).astype(BF16)


def _dot3(a_hi, a_lo, b):
    n = b.shape[1]
    b_hi, b_lo = _split(b)
    r = _dot(a_hi, jnp.concatenate([b_hi, b_lo], axis=1))
    return r[:, :n] + r[:, n:] + _dot(a_lo, b_hi)


def _delta_kernel(qkv_ref, gb_ref, szg_ref, cw_ref, nal_ref, dtb_ref, dnw_ref,
                  ob_ref, s_ref, cs_ref, qbuf, s_acc, *, nt):
    t = pl.program_id(1)
    c = DN_CHUNK

    @pl.when(t == 0)
    def _():
        qbuf[0:SUBLANES, :] = jnp.zeros((SUBLANES, DN_QKV), F32)
        s_acc[...] = jnp.zeros(s_acc.shape, F32)

    qbuf[SUBLANES:SUBLANES + c, :] = qkv_ref[...]
    off = SUBLANES - (CONV_W - 1)
    xc = cw_ref[0:1, :] * qbuf[off:off + c, :]
    for j in range(1, CONV_W):
        xc = xc + cw_ref[j:j + 1, :] * qbuf[off + j:off + j + c, :]
    tail = qbuf[c:c + SUBLANES, :]
    qbuf[0:SUBLANES, :] = tail
    cs_ref[0] = tail[SUBLANES - (CONV_W - 1):, :]
    act = _silu(xc)

    gbv = gb_ref[...]
    g_full = -jnp.exp(nal_ref[...]) * _softplus(gbv + dtb_ref[...])
    beta_full = _sigmoid(gbv)
    ri = lax.broadcasted_iota(jnp.int32, (c, 2 * c), 0)
    ci = lax.broadcasted_iota(jnp.int32, (c, 2 * c), 1)
    left = ci < c
    incl = jnp.logical_and(ri >= ci, left)
    strict = jnp.logical_and(ri > ci, left)
    right = ci >= c
    eye_r = (ci - c == ri).astype(F32)
    gcum = jnp.dot(incl[:, :c].astype(F32), g_full, precision=HI, preferred_element_type=F32)
    gcum_t = _dot_tn(g_full, jnp.logical_and(ri <= ci, left).astype(F32), precision=HI)

    heads = range(DN_HEADS)
    zpad = jnp.zeros((c, DN_DK), BF16)
    gc, decay, qn, kn, kn16, kb, vb, r, attn = [], [], [], [], [], [], [], [], []
    for h in heads:
        gc.append(gcum[:, h:h + 1])
        decay.append(jnp.exp(jnp.where(incl, gc[h] - gcum_t[h:h + 1, :], -jnp.inf)))
        bh = beta_full[:, DN_HEADS + h:DN_HEADS + h + 1]
        qh = act[:, h * DN_DK:(h + 1) * DN_DK]
        kh = act[:, DN_QK + h * DN_DK:DN_QK + (h + 1) * DN_DK]
        vh = act[:, 2 * DN_QK + h * DN_DV:2 * DN_QK + (h + 1) * DN_DV]
        qn.append(qh * lax.rsqrt(jnp.sum(qh * qh, axis=-1, keepdims=True) + EPS) * (DN_DK ** -0.5))
        kn.append(kh * lax.rsqrt(jnp.sum(kh * kh, axis=-1, keepdims=True) + EPS))
        kn16.append(kn[h].astype(BF16))
        kb.append(kn[h] * bh)
        vb.append(vh * bh)
    for h in heads:
        kq = _dot_nt(jnp.concatenate([kb[h], qn[h]], axis=0).astype(BF16),
                     jnp.concatenate([kn16[h], zpad], axis=0))
        r.append(eye_r - jnp.where(strict, kq[:c] * decay[h], 0.0))
        attn.append((kq[c:] * decay[h])[:, :c].astype(BF16))
    for _ in range(c.bit_length() - 1):
        nxt = []
        for h in heads:
            hi, lo = _split(r[h])
            res = _dot(hi[:, :c], jnp.concatenate([hi, lo], axis=1))
            res = res[:, :2 * c] + res[:, 2 * c:] + _dot(lo[:, :c], hi)
            nxt.append(res + jnp.where(right, r[h], 0.0))
        r = nxt
    eg = [jnp.exp(gc[h]) for h in heads]
    gl = [gcum[c - 1:c, h:h + 1] for h in heads]
    x = []
    for h in heads:
        a_hi, a_lo = _split(r[h][:, c:])
        x.append(_dot3(a_hi, a_lo, jnp.concatenate([vb[h], kb[h] * eg[h]], axis=1)))
    s_old = [s_acc[h] for h in heads]
    wq = [_dot(jnp.concatenate([x[h][:, DN_DV:], qn[h] * eg[h]], axis=0).astype(BF16), s_old[h].astype(BF16))
          for h in heads]
    vnew16 = [(x[h][:, :DN_DV] - wq[h][:c]).astype(BF16) for h in heads]
    o = [wq[h][c:] + _dot(attn[h], vnew16[h]) for h in heads]
    for h in heads:
        kd16 = (kn[h] * jnp.exp(gl[h] - gc[h])).astype(BF16)
        s_acc[h] = s_old[h] * jnp.exp(gl[h]) + _dot_tn(kd16, vnew16[h])
    for h in heads:
        on = o[h] * lax.rsqrt(jnp.mean(o[h] * o[h], axis=-1, keepdims=True) + EPS) * dnw_ref[...]
        ob_ref[:, h * DN_DV:(h + 1) * DN_DV] = on * szg_ref[:, h * DN_DV:(h + 1) * DN_DV]

    @pl.when(t == nt - 1)
    def _():
        s_ref[0] = s_acc[...]


def _delta_call(qkv, gb, szg, cw, nal, dtb, dnw, *, batch, seq_len):
    c = DN_CHUNK
    nt = seq_len // c
    row = lambda n: pl.BlockSpec((c, n), lambda b, t: (b * nt + t, 0))
    return pl.pallas_call(
        functools.partial(_delta_kernel, nt=nt),
        grid=(batch, nt),
        in_specs=[row(DN_QKV), row(LANES), row(DN_V),
                  _resident((CONV_W, DN_QKV)), _resident((1, LANES)), _resident((1, LANES)),
                  _resident((1, DN_DV))],
        out_specs=[row(DN_V),
                   pl.BlockSpec((1, DN_HEADS, DN_DK, DN_DV), lambda b, t: (b, 0, 0, 0)),
                   pl.BlockSpec((1, CONV_W - 1, DN_QKV), lambda b, t: (b, 0, 0))],
        out_shape=[jax.ShapeDtypeStruct((batch * seq_len, DN_V), F32),
                   jax.ShapeDtypeStruct((batch, DN_HEADS, DN_DK, DN_DV), F32),
                   jax.ShapeDtypeStruct((batch, CONV_W - 1, DN_QKV), F32)],
        scratch_shapes=[pltpu.VMEM((SUBLANES + c, DN_QKV), F32),
                        pltpu.VMEM((DN_HEADS, DN_DK, DN_DV), F32)],
        compiler_params=_params(("arbitrary", "arbitrary")),
        name="delta",
    )(qkv, gb, szg, cw, nal, dtb, dnw)


def _sample_pre_kernel(xr_ref, ggr_ref, qkv_ref, gb_ref, crnn_ref, cqkv_ref, h0_ref,
                       cw_ref, cb_ref, wa_ref, ba_ref, wx_ref, bx_ref, lam_ref,
                       cqw_ref, nal_ref, dtb_ref,
                       oa_ref, hn_ref, qn_ref, kn_ref, v_ref, eg_ref, beta_ref):
    xc = cb_ref[...] + cw_ref[CONV_W - 1:CONV_W, :] * xr_ref[...]
    for j in range(CONV_W - 1):
        xc = xc + cw_ref[j:j + 1, :] * crnn_ref[j]
    a, log_a, i = _rg_gates(xc, wa_ref, ba_ref[...], wx_ref, bx_ref[...], lam_ref[...])
    mult = jnp.sqrt(_neg_expm1(2.0 * log_a))
    h = a * h0_ref[...] + mult * (i * xc)
    hn_ref[...] = h
    oa_ref[...] = h * ggr_ref[...]

    qc = cqw_ref[CONV_W - 1:CONV_W, :] * qkv_ref[...]
    for j in range(CONV_W - 1):
        qc = qc + cqw_ref[j:j + 1, :] * cqkv_ref[j]
    act = _silu(qc)
    for h_ in range(DN_HEADS):
        qh = act[:, h_ * DN_DK:(h_ + 1) * DN_DK]
        kh = act[:, DN_QK + h_ * DN_DK:DN_QK + (h_ + 1) * DN_DK]
        qn_ref[:, h_ * DN_DK:(h_ + 1) * DN_DK] = (
            qh * lax.rsqrt(jnp.sum(qh * qh, axis=-1, keepdims=True) + EPS) * (DN_DK ** -0.5))
        kn_ref[:, h_ * DN_DK:(h_ + 1) * DN_DK] = (
            kh * lax.rsqrt(jnp.sum(kh * kh, axis=-1, keepdims=True) + EPS))
    v_ref[...] = act[:, 2 * DN_QK:]
    gbv = gb_ref[...]
    eg_ref[...] = jnp.exp(-jnp.exp(nal_ref[...]) * _softplus(gbv + dtb_ref[...]))
    beta_ref[...] = _sigmoid(gbv)


def _sample_pre_call(xr, ggr, qkv, gb, crnn, cqkv, h0, cw, cb, wa, ba, wx, bx, lam, cqw, nal, dtb):
    n = xr.shape[0]
    args = (xr, ggr, qkv, gb, crnn, cqkv, h0, cw, cb, wa, ba, wx, bx, lam, cqw, nal, dtb)
    shp = lambda w: jax.ShapeDtypeStruct((n, w), F32)
    outs = [shp(D_RNN), shp(D_RNN), shp(DN_QK), shp(DN_QK), shp(DN_V), shp(LANES), shp(LANES)]
    return pl.pallas_call(
        _sample_pre_kernel,
        grid=(1,),
        in_specs=[_resident(a.shape) for a in args],
        out_specs=[pl.BlockSpec(o.shape, lambda i: (0, 0)) for o in outs],
        out_shape=outs,
        compiler_params=_params(("arbitrary",)),
        name="sample_pre",
    )(*args)


def _sample_state_kernel(qn_ref, kn_ref, v_ref, eg_ref, beta_ref, szg_ref, dnw_ref, s_ref,
                         ob_ref, sn_ref):
    srow = lax.broadcasted_iota(jnp.int32, (SUBLANES, DN_DK), 0)
    rows = lambda r0, r1: jnp.where(srow == 0, r0, jnp.where(srow == 1, r1, 0.0))
    for h in range(DN_HEADS):
        sl = slice(h * DN_DK, (h + 1) * DN_DK)
        q = qn_ref[0, :, sl]
        k = kn_ref[0, :, sl]
        v = v_ref[0, :, sl]
        eg = eg_ref[0, :, h:h + 1]
        beta = beta_ref[0, :, DN_HEADS + h:DN_HEADS + h + 1]
        s_h = s_ref[0, h]
        ws_qs = _dot(rows((k * beta) * eg, q * eg).astype(BF16), s_h.astype(BF16))
        vnew = v * beta - ws_qs[0:1, :]
        o = ws_qs[1:2, :] + jnp.sum(q * k, axis=-1, keepdims=True) * vnew
        zero = jnp.zeros_like(k)
        sn_ref[0, h] = s_h * eg + _dot_tn(rows(k, zero).astype(BF16), rows(vnew, zero).astype(BF16))
        on = o * lax.rsqrt(jnp.mean(o * o, axis=-1, keepdims=True) + EPS) * dnw_ref[...]
        ob_ref[0, :, sl] = on * szg_ref[0, :, sl]


def _sample_state_call(qn, kn, v, eg, beta, szg, dnw, s0):
    n = qn.shape[0]
    r3 = lambda a: a.reshape(n, 1, a.shape[-1])
    vec = lambda w: pl.BlockSpec((1, 1, w), lambda b: (b, 0, 0))
    s_spec = pl.BlockSpec((1, DN_HEADS, DN_DK, DN_DV), lambda b: (b, 0, 0, 0))
    ob, sn = pl.pallas_call(
        _sample_state_kernel,
        grid=(n,),
        in_specs=[vec(DN_QK), vec(DN_QK), vec(DN_V), vec(LANES), vec(LANES), vec(DN_V),
                  _resident((1, DN_DV)), s_spec],
        out_specs=[vec(DN_V), s_spec],
        out_shape=[jax.ShapeDtypeStruct((n, 1, DN_V), F32),
                   jax.ShapeDtypeStruct((n, DN_HEADS, DN_DK, DN_DV), F32)],
        compiler_params=_params(("arbitrary",)),
        name="sample_state",
    )(r3(qn), r3(kn), r3(v), r3(eg), r3(beta), r3(szg), dnw, s0)
    return ob.reshape(n, DN_V), sn


def _merge_kernel(x_ref, oa_ref, ob_ref, sga_ref, sgb_ref, ada_ref, wb_ref, wo_ref, o_ref, *, per_row):
    ya = _dot(oa_ref[...].astype(BF16), wb_ref[0])
    yb = _dot(ob_ref[...].astype(BF16), wb_ref[1])
    merged = (sga_ref[...] * ya + sgb_ref[...] * yb).astype(BF16)
    o_ref[...] = x_ref[...] + _ada_rows(ada_ref, 5, per_row) * _dot(merged, wo_ref[...])


def _merge_call(x, oa, ob, sga, sgb, ada, wb, wo, *, per_row, tm, seq_len):
    m = x.shape[0]
    row = pl.BlockSpec((tm, D_MODEL), lambda i: (i, 0))
    return pl.pallas_call(
        functools.partial(_merge_kernel, per_row=per_row),
        grid=(m // tm,),
        in_specs=[row, row, row, row, row, _ada_spec(per_row, m, seq_len // tm),
                  _resident((2, D_RNN, D_MODEL)), _resident((D_MODEL, D_MODEL))],
        out_specs=row,
        out_shape=jax.ShapeDtypeStruct((m, D_MODEL), F32),
        compiler_params=_params(("arbitrary",)),
        name="merge",
    )(x, oa, ob, sga, sgb, ada, wb, wo)


def kernel(x_prompt, x_sample, c_prompt, c_sample, state_rglru_h, state_rglru_conv, state_delta_S, state_delta_conv, w_ada, b_ada, norm_ffn1, w_ffn1_up, w_ffn1_down, norm_mix, w_in, conv_rnn_w, conv_rnn_b, rg_w_a, rg_b_a, rg_w_x, rg_b_x, rg_lambda, conv_qkv_w, dn_a_log, dn_dt_bias, dn_norm, w_branch, w_out, norm_ffn2, w_ffn2_up, w_ffn2_down, norm_final):
    batch, seq_len, _ = x_prompt.shape
    n_dec = x_sample.shape[0]
    assert w_ada.shape[0] == 1 and x_sample.shape[1] == 1

    row = lambda a: a.reshape(1, -1).astype(F32)
    wup1, wdn1 = w_ffn1_up[0].astype(BF16), w_ffn1_down[0].astype(BF16)
    wup2, wdn2 = w_ffn2_up[0].astype(BF16), w_ffn2_down[0].astype(BF16)
    w_main = jnp.concatenate([w_in[0][:, :AB_OFF], w_in[0][:, AB_OFF + 2 * DN_HEADS:]], axis=1).astype(BF16)
    w_ab = jnp.pad(w_in[0][:, AB_OFF:AB_OFF + 2 * DN_HEADS], ((0, 0), (0, LANES - 2 * DN_HEADS))).astype(BF16)
    wb, wo = w_branch[0].astype(BF16), w_out[0].astype(BF16)
    wa, wx = rg_w_a[0].astype(BF16), rg_w_x[0].astype(BF16)
    lane_pad = lambda a: jnp.pad(a.reshape(1, -1).astype(F32), ((0, 0), (0, LANES - a.size)))
    nal = lane_pad(dn_a_log[0])
    dtb = lane_pad(dn_dt_bias[0])
    nf = row(norm_final)

    ada = _ada_call(jnp.concatenate([c_prompt, c_sample], axis=0), w_ada[0], row(b_ada[0]))
    ada = ada.reshape(batch + n_dec, N_ADA, D_MODEL)
    ada_p = ada[:batch]
    ada_s = jnp.transpose(ada[batch:], (1, 0, 2))

    mixer_w = (conv_rnn_w[0], row(conv_rnn_b[0]), wa, row(rg_b_a[0]), wx, row(rg_b_x[0]), row(rg_lambda[0]))

    kw = dict(per_row=False, seq_len=seq_len)
    xp = x_prompt.reshape(batch * seq_len, D_MODEL)
    xp = _ffn_call(xp, ada_p, row(norm_ffn1[0]), wup1, wdn1, nf, k0=0, final_norm=False, tm=512, **kw)
    xr, ggr, qkv, szg, sga, sgb, gb = _inproj_call(xp, ada_p, row(norm_mix[0]), w_main, w_ab, tm=256, **kw)
    oa, hp, cp = _rglru_call(xr, ggr, *mixer_w, batch=batch, seq_len=seq_len, tl=256)
    ob, sp, qp = _delta_call(qkv, gb, szg, conv_qkv_w[0], nal, dtb, row(dn_norm[0]), batch=batch, seq_len=seq_len)
    xp = _merge_call(xp, oa, ob, sga, sgb, ada_p, wb, wo, tm=512, **kw)
    y_prompt = _ffn_call(xp, ada_p, row(norm_ffn2[0]), wup2, wdn2, nf, k0=6, final_norm=True, tm=512, **kw)

    kw = dict(per_row=True, seq_len=n_dec, tm=n_dec)
    xs = x_sample.reshape(n_dec, D_MODEL)
    xs = _ffn_call(xs, ada_s, row(norm_ffn1[0]), wup1, wdn1, nf, k0=0, final_norm=False, **kw)
    xr_s, ggr_s, qkv_s, szg_s, sga_s, sgb_s, gb_s = _inproj_call(xs, ada_s, row(norm_mix[0]), w_main, w_ab, **kw)
    crnn = jnp.transpose(state_rglru_conv[0], (1, 0, 2))
    cqkv = jnp.transpose(state_delta_conv[0], (1, 0, 2))
    oa_s, hs, qn, kn, v, eg, beta = _sample_pre_call(
        xr_s, ggr_s, qkv_s, gb_s, crnn, cqkv, state_rglru_h[0], *mixer_w, conv_qkv_w[0], nal, dtb)
    ob_s, ss = _sample_state_call(qn, kn, v, eg, beta, szg_s, row(dn_norm[0]), state_delta_S[0])
    xs = _merge_call(xs, oa_s, ob_s, sga_s, sgb_s, ada_s, wb, wo, **kw)
    y_sample = _ffn_call(xs, ada_s, row(norm_ffn2[0]), wup2, wdn2, nf, k0=6, final_norm=True, **kw)
    cs = jnp.concatenate([state_rglru_conv[0][:, 1:], xr_s[:, None, :]], axis=1)
    qs = jnp.concatenate([state_delta_conv[0][:, 1:], qkv_s[:, None, :]], axis=1)

    return (y_prompt.reshape(batch, seq_len, D_MODEL), y_sample.reshape(n_dec, 1, D_MODEL),
            hp.reshape(1, batch, D_RNN), cp[None], sp[None], qp[None],
            hs[None], cs[None], ss[None], qs[None])
```

```python
import functools

import jax
import jax.numpy as jnp
from jax import lax
from jax.experimental import pallas as pl
from jax.experimental.pallas import tpu as pltpu

D_MODEL = 1024
D_RNN = 1024
RG_BLOCKS = 8
RG_BLOCK_W = D_RNN // RG_BLOCKS
RG_C = 8.0
CONV_W = 4
DN_HEADS = 8
DN_DK = 128
DN_DV = 128
DN_QK = DN_HEADS * DN_DK
DN_V = DN_HEADS * DN_DV
DN_QKV = 2 * DN_QK + DN_V
DN_CHUNK = 64
D_FF = 2816
N_ADA = 9
EPS = 1e-6
LANES = 128
SUBLANES = 8
AB_OFF = D_RNN * 2 + DN_QKV
N_MAIN = 8 * D_MODEL

BF16 = jnp.bfloat16
F32 = jnp.float32
HI = lax.Precision.HIGHEST

VMEM_LIMIT = 56 * 1024 * 1024


def _params(sem):
    return pltpu.CompilerParams(dimension_semantics=sem, vmem_limit_bytes=VMEM_LIMIT)


def _resident(shape):
    nd = len(shape)
    return pl.BlockSpec(shape, lambda *_: (0,) * nd, pipeline_mode=pl.Buffered(1))


def _dot(a, b):
    return jnp.dot(a, b, preferred_element_type=F32)


def _dot_nt(a, b, precision=None):
    return lax.dot_general(a, b, (((1,), (1,)), ((), ())), precision=precision,
                           preferred_element_type=F32)


def _dot_tn(a, b, precision=None):
    return lax.dot_general(a, b, (((0,), (0,)), ((), ())), precision=precision,
                           preferred_element_type=F32)


def _sigmoid(x):
    return jax.nn.sigmoid(x)


def _silu(x):
    return x * jax.nn.sigmoid(x)


def _softplus(x):
    return jnp.maximum(x, 0.0) + jnp.log1p(jnp.exp(-jnp.abs(x)))


def _neg_expm1(x):
    return -jnp.tanh(0.5 * x) * (jnp.exp(x) + 1.0)


def _ada_rows(ada_ref, k, per_row):
    if per_row:
        return ada_ref[k]
    return ada_ref[k:k + 1, :]


def _norm_mod(x, nw, shift, scale):
    ms = jnp.mean(x * x, axis=-1, keepdims=True)
    hn = x * lax.rsqrt(ms + EPS) * nw
    return hn * (1.0 + scale) + shift


def _ada_kernel(c_ref, w_ref, b_ref, o_ref):
    o_ref[...] = _dot(c_ref[...].astype(BF16), w_ref[...].astype(BF16)) + b_ref[...]


def _ada_call(c_all, w_ada, b_ada):
    n = c_all.shape[0]
    tn = D_MODEL
    return pl.pallas_call(
        _ada_kernel,
        grid=(N_ADA * D_MODEL // tn,),
        in_specs=[pl.BlockSpec((n, D_MODEL), lambda j: (0, 0)),
                  pl.BlockSpec((D_MODEL, tn), lambda j: (0, j)),
                  pl.BlockSpec((1, tn), lambda j: (0, j))],
        out_specs=pl.BlockSpec((n, tn), lambda j: (0, j)),
        out_shape=jax.ShapeDtypeStruct((n, N_ADA * D_MODEL), F32),
        compiler_params=_params(("arbitrary",)),
        name="ada",
    )(c_all, w_ada, b_ada)


FFN_TF = 256


def _ffn_kernel(x_ref, ada_ref, nw_ref, wup_ref, wdn_ref, nf_ref, o_ref, *, k0, per_row, final_norm):
    x = x_ref[...]
    h = _norm_mod(x, nw_ref[...], _ada_rows(ada_ref, k0, per_row),
                  _ada_rows(ada_ref, k0 + 1, per_row)).astype(BF16)
    acc = jnp.zeros(x.shape, F32)
    for j in range(D_FF // FFN_TF):
        g = _dot(h, wup_ref[:, j * FFN_TF:(j + 1) * FFN_TF])
        v = _dot(h, wup_ref[:, D_FF + j * FFN_TF:D_FF + (j + 1) * FFN_TF])
        a = (_silu(g) * v).astype(BF16)
        acc = acc + _dot(a, wdn_ref[j * FFN_TF:(j + 1) * FFN_TF, :])
    y = x + 0.5 * _ada_rows(ada_ref, k0 + 2, per_row) * acc
    if final_norm:
        ms = jnp.mean(y * y, axis=-1, keepdims=True)
        y = y * lax.rsqrt(ms + EPS) * nf_ref[...]
    o_ref[...] = y


def _ada_spec(per_row, rows, tiles_per_seq):
    if per_row:
        return pl.BlockSpec((N_ADA, rows, D_MODEL), lambda i: (0, 0, 0))
    return pl.BlockSpec((None, N_ADA, D_MODEL), lambda i: (i // tiles_per_seq, 0, 0))


def _ffn_call(x, ada, nw, wup, wdn, nf, *, k0, per_row, final_norm, tm, seq_len):
    m = x.shape[0]
    kern = functools.partial(_ffn_kernel, k0=k0, per_row=per_row, final_norm=final_norm)
    return pl.pallas_call(
        kern,
        grid=(m // tm,),
        in_specs=[pl.BlockSpec((tm, D_MODEL), lambda i: (i, 0)),
                  _ada_spec(per_row, m, seq_len // tm),
                  _resident((1, D_MODEL)),
                  _resident((D_MODEL, 2 * D_FF)),
                  _resident((D_FF, D_MODEL)),
                  _resident((1, D_MODEL))],
        out_specs=pl.BlockSpec((tm, D_MODEL), lambda i: (i, 0)),
        out_shape=jax.ShapeDtypeStruct((m, D_MODEL), F32),
        compiler_params=_params(("arbitrary",)),
        name="ffn",
    )(x, ada, nw, wup, wdn, nf)


def _inproj_kernel(x_ref, ada_ref, nw_ref, w_ref, wab_ref,
                   xr_ref, ggr_ref, qkv_ref, szg_ref, sga_ref, sgb_ref, gb_ref, *, per_row):
    h = _norm_mod(x_ref[...], nw_ref[...], _ada_rows(ada_ref, 3, per_row),
                  _ada_rows(ada_ref, 4, per_row)).astype(BF16)

    def col(c):
        return _dot(h, w_ref[:, c * D_MODEL:(c + 1) * D_MODEL])

    xr_ref[...] = col(0)
    ggr_ref[...] = jax.nn.gelu(col(1))
    for c in range(3):
        qkv_ref[:, c * D_MODEL:(c + 1) * D_MODEL] = col(2 + c)
    szg_ref[...] = _silu(col(5))
    sga_ref[...] = _sigmoid(col(6))
    sgb_ref[...] = _sigmoid(col(7))
    gb_ref[...] = _dot(h, wab_ref[...])


def _inproj_call(x, ada, nw, w_main, w_ab, *, per_row, tm, seq_len):
    m = x.shape[0]
    row = lambda n: pl.BlockSpec((tm, n), lambda i: (i, 0))
    shp = lambda n: jax.ShapeDtypeStruct((m, n), F32)
    return pl.pallas_call(
        functools.partial(_inproj_kernel, per_row=per_row),
        grid=(m // tm,),
        in_specs=[row(D_MODEL), _ada_spec(per_row, m, seq_len // tm),
                  _resident((1, D_MODEL)), _resident((D_MODEL, N_MAIN)), _resident((D_MODEL, LANES))],
        out_specs=[row(D_RNN), row(D_RNN), row(DN_QKV), row(DN_V), row(D_MODEL), row(D_MODEL), row(LANES)],
        out_shape=[shp(D_RNN), shp(D_RNN), shp(DN_QKV), shp(DN_V), shp(D_MODEL), shp(D_MODEL), shp(LANES)],
        compiler_params=_params(("arbitrary",)),
        name="inproj",
    )(x, ada, nw, w_main, w_ab)


def _rg_gates(xc, wa_ref, ba, wx_ref, bx, lam):
    ra, ix = [], []
    for n in range(RG_BLOCKS):
        xb = xc[:, n * RG_BLOCK_W:(n + 1) * RG_BLOCK_W].astype(BF16)
        ra.append(_dot(xb, wa_ref[n]))
        ix.append(_dot(xb, wx_ref[n]))
    r = _sigmoid(jnp.concatenate(ra, axis=1) + ba)
    i = _sigmoid(jnp.concatenate(ix, axis=1) + bx)
    log_a = (-RG_C) * r * _softplus(-lam)
    return jnp.exp(log_a), log_a, i


def _rglru_kernel(xr_ref, ggr_ref, cw_ref, cb_ref, wa_ref, ba_ref, wx_ref, bx_ref, lam_ref,
                  oa_ref, hl_ref, cs_ref, xbuf, hcar, a_s, b_s, y_s, *, tl):
    t = pl.program_id(1)

    @pl.when(t == 0)
    def _():
        xbuf[0:SUBLANES, :] = jnp.zeros((SUBLANES, D_RNN), F32)
        hcar[...] = jnp.zeros(hcar.shape, F32)

    xbuf[SUBLANES:SUBLANES + tl, :] = xr_ref[...]
    off = SUBLANES - (CONV_W - 1)
    xc = cb_ref[...] + cw_ref[0:1, :] * xbuf[off:off + tl, :]
    for j in range(1, CONV_W):
        xc = xc + cw_ref[j:j + 1, :] * xbuf[off + j:off + j + tl, :]
    tail = xbuf[tl:tl + SUBLANES, :]
    xbuf[0:SUBLANES, :] = tail
    cs_ref[0] = tail[SUBLANES - (CONV_W - 1):, :]

    a, log_a, i = _rg_gates(xc, wa_ref, ba_ref[...], wx_ref, bx_ref[...], lam_ref[...])
    mult = jnp.sqrt(_neg_expm1(2.0 * log_a))
    row = lax.broadcasted_iota(jnp.int32, (tl, 1), 0)
    mult = jnp.where(jnp.logical_and(row == 0, t == 0), 1.0, mult)
    a_s[...] = a
    b_s[...] = mult * (i * xc)

    def step(s, h):
        h = a_s[pl.ds(s, 1), :] * h + b_s[pl.ds(s, 1), :]
        y_s[pl.ds(s, 1), :] = h
        return h

    h = lax.fori_loop(0, tl, step, hcar[0:1, :], unroll=8)
    hcar[0:1, :] = h
    hl_ref[0] = h
    oa_ref[...] = y_s[...] * ggr_ref[...]


def _rglru_call(xr, ggr, cw, cb, wa, ba, wx, bx, lam, *, batch, seq_len, tl):
    nt = seq_len // tl
    row = pl.BlockSpec((tl, D_RNN), lambda b, t: (b * nt + t, 0))
    return pl.pallas_call(
        functools.partial(_rglru_kernel, tl=tl),
        grid=(batch, nt),
        in_specs=[row, row,
                  _resident((CONV_W, D_RNN)), _resident((1, D_RNN)),
                  _resident((RG_BLOCKS, RG_BLOCK_W, RG_BLOCK_W)), _resident((1, D_RNN)),
                  _resident((RG_BLOCKS, RG_BLOCK_W, RG_BLOCK_W)), _resident((1, D_RNN)),
                  _resident((1, D_RNN))],
        out_specs=[row,
                   pl.BlockSpec((1, 1, D_RNN), lambda b, t: (b, 0, 0)),
                   pl.BlockSpec((1, CONV_W - 1, D_RNN), lambda b, t: (b, 0, 0))],
        out_shape=[jax.ShapeDtypeStruct((batch * seq_len, D_RNN), F32),
                   jax.ShapeDtypeStruct((batch, 1, D_RNN), F32),
                   jax.ShapeDtypeStruct((batch, CONV_W - 1, D_RNN), F32)],
        scratch_shapes=[pltpu.VMEM((SUBLANES + tl, D_RNN), F32),
                        pltpu.VMEM((SUBLANES, D_RNN), F32),
                        pltpu.VMEM((tl, D_RNN), F32),
                        pltpu.VMEM((tl, D_RNN), F32),
                        pltpu.VMEM((tl, D_RNN), F32)],
        compiler_params=_params(("arbitrary", "arbitrary")),
        name="rglru",
    )(xr, ggr, cw, cb, wa, ba, wx, bx, lam)


def _split(x):
    hi = x.astype(BF16)
    return hi, (x - hi.astype(F32)).astype(BF16)


def _dot3(a_hi, a_lo, b):
    n = b.shape[1]
    b_hi, b_lo = _split(b)
    r = _dot(a_hi, jnp.concatenate([b_hi, b_lo], axis=1))
    return r[:, :n] + r[:, n:] + _dot(a_lo, b_hi)


def _pair_mask(ri, ci, lvl):
    same = (ri >> (lvl + 1)) == (ci >> (lvl + 1))
    return jnp.logical_and(same, jnp.logical_and(((ri >> lvl) & 1) == 1, ((ci >> lvl) & 1) == 0))


def _delta_kernel(qkv_ref, gb_ref, szg_ref, cw_ref, nal_ref, dtb_ref, dnw_ref,
                  ob_ref, s_ref, cs_ref, qbuf, s_acc, *, nt):
    t = pl.program_id(1)
    c = DN_CHUNK

    @pl.when(t == 0)
    def _():
        qbuf[0:SUBLANES, :] = jnp.zeros((SUBLANES, DN_QKV), F32)
        s_acc[...] = jnp.zeros(s_acc.shape, F32)

    qbuf[SUBLANES:SUBLANES + c, :] = qkv_ref[...]
    off = SUBLANES - (CONV_W - 1)
    xc = cw_ref[0:1, :] * qbuf[off:off + c, :]
    for j in range(1, CONV_W):
        xc = xc + cw_ref[j:j + 1, :] * qbuf[off + j:off + j + c, :]
    tail = qbuf[c:c + SUBLANES, :]
    qbuf[0:SUBLANES, :] = tail
    cs_ref[0] = tail[SUBLANES - (CONV_W - 1):, :]
    act = _silu(xc)

    gbv = gb_ref[...]
    g_full = -jnp.exp(nal_ref[...]) * _softplus(gbv + dtb_ref[...])
    beta_full = _sigmoid(gbv)
    ri = lax.broadcasted_iota(jnp.int32, (c, 2 * c), 0)
    ci = lax.broadcasted_iota(jnp.int32, (c, 2 * c), 1)
    left = ci < c
    incl = jnp.logical_and(ri >= ci, left)
    strict = jnp.logical_and(ri > ci, left)
    eye_l = (ci == ri).astype(F32)
    gcum = jnp.dot(incl[:, :c].astype(F32), g_full, precision=HI, preferred_element_type=F32)
    gcum_t = _dot_tn(g_full, jnp.logical_and(ri <= ci, left).astype(F32), precision=HI)

    heads = range(DN_HEADS)
    zpad = jnp.zeros((c, DN_DK), BF16)
    gc, decay, qn, kn, kn16, kb, vb, lm, attn = [], [], [], [], [], [], [], [], []
    for h in heads:
        gc.append(gcum[:, h:h + 1])
        decay.append(jnp.exp(jnp.where(incl, gc[h] - gcum_t[h:h + 1, :], -jnp.inf)))
        bh = beta_full[:, DN_HEADS + h:DN_HEADS + h + 1]
        qh = act[:, h * DN_DK:(h + 1) * DN_DK]
        kh = act[:, DN_QK + h * DN_DK:DN_QK + (h + 1) * DN_DK]
        vh = act[:, 2 * DN_QK + h * DN_DV:2 * DN_QK + (h + 1) * DN_DV]
        qn.append(qh * lax.rsqrt(jnp.sum(qh * qh, axis=-1, keepdims=True) + EPS) * (DN_DK ** -0.5))
        kn.append(kh * lax.rsqrt(jnp.sum(kh * kh, axis=-1, keepdims=True) + EPS))
        kn16.append(kn[h].astype(BF16))
        kb.append(kn[h] * bh)
        vb.append(vh * bh)
    for h in heads:
        kq = _dot_nt(jnp.concatenate([kb[h], qn[h]], axis=0).astype(BF16),
                     jnp.concatenate([kn16[h], zpad], axis=0))
        lm.append(jnp.where(strict, kq[:c] * decay[h], 0.0))
        attn.append((kq[c:] * decay[h])[:, :c].astype(BF16))
    d = [eye_l - jnp.where(_pair_mask(ri, ci, 0), lm[h], 0.0) for h in heads]
    for lvl in range(1, c.bit_length() - 1):
        pair = _pair_mask(ri, ci, lvl)
        d16 = [d[h].astype(BF16) for h in heads]
        ed = [_dot(jnp.where(pair, lm[h], 0.0).astype(BF16)[:, :c], d16[h]) for h in heads]
        d = [d[h] - _dot(d16[h][:, :c], ed[h].astype(BF16)) for h in heads]
    eg = [jnp.exp(gc[h]) for h in heads]
    gl = [gcum[c - 1:c, h:h + 1] for h in heads]
    a16 = [d[h].astype(BF16)[:, :c] for h in heads]
    rhs = [jnp.concatenate([vb[h], kb[h] * eg[h]], axis=1) for h in heads]
    x0 = [_dot(a16[h], rhs[h].astype(BF16)) for h in heads]
    res = []
    for h in heads:
        l_hi, l_lo = _split(lm[h])
        res.append(rhs[h] - x0[h] - _dot3(l_hi[:, :c], l_lo[:, :c], x0[h]))
    x = [x0[h] + _dot(a16[h], res[h].astype(BF16)) for h in heads]
    s_old = [s_acc[h] for h in heads]
    wq = [_dot(jnp.concatenate([x[h][:, DN_DV:], qn[h] * eg[h]], axis=0).astype(BF16), s_old[h].astype(BF16))
          for h in heads]
    vnew16 = [(x[h][:, :DN_DV] - wq[h][:c]).astype(BF16) for h in heads]
    o = [wq[h][c:] + _dot(attn[h], vnew16[h]) for h in heads]
    for h in heads:
        kd16 = (kn[h] * jnp.exp(gl[h] - gc[h])).astype(BF16)
        s_acc[h] = s_old[h] * jnp.exp(gl[h]) + _dot_tn(kd16, vnew16[h])
    for h in heads:
        on = o[h] * lax.rsqrt(jnp.mean(o[h] * o[h], axis=-1, keepdims=True) + EPS) * dnw_ref[...]
        ob_ref[:, h * DN_DV:(h + 1) * DN_DV] = on * szg_ref[:, h * DN_DV:(h + 1) * DN_DV]

    @pl.when(t == nt - 1)
    def _():
        s_ref[0] = s_acc[...]


def _delta_call(qkv, gb, szg, cw, nal, dtb, dnw, *, batch, seq_len):
    c = DN_CHUNK
    nt = seq_len // c
    row = lambda n: pl.BlockSpec((c, n), lambda b, t: (b * nt + t, 0))
    return pl.pallas_call(
        functools.partial(_delta_kernel, nt=nt),
        grid=(batch, nt),
        in_specs=[row(DN_QKV), row(LANES), row(DN_V),
                  _resident((CONV_W, DN_QKV)), _resident((1, LANES)), _resident((1, LANES)),
                  _resident((1, DN_DV))],
        out_specs=[row(DN_V),
                   pl.BlockSpec((1, DN_HEADS, DN_DK, DN_DV), lambda b, t: (b, 0, 0, 0)),
                   pl.BlockSpec((1, CONV_W - 1, DN_QKV), lambda b, t: (b, 0, 0))],
        out_shape=[jax.ShapeDtypeStruct((batch * seq_len, DN_V), F32),
                   jax.ShapeDtypeStruct((batch, DN_HEADS, DN_DK, DN_DV), F32),
                   jax.ShapeDtypeStruct((batch, CONV_W - 1, DN_QKV), F32)],
        scratch_shapes=[pltpu.VMEM((SUBLANES + c, DN_QKV), F32),
                        pltpu.VMEM((DN_HEADS, DN_DK, DN_DV), F32)],
        compiler_params=_params(("arbitrary", "arbitrary")),
        name="delta",
    )(qkv, gb, szg, cw, nal, dtb, dnw)


def _sample_pre_kernel(xr_ref, ggr_ref, qkv_ref, gb_ref, crnn_ref, cqkv_ref, h0_ref,
                       cw_ref, cb_ref, wa_ref, ba_ref, wx_ref, bx_ref, lam_ref,
                       cqw_ref, nal_ref, dtb_ref,
                       oa_ref, hn_ref, qn_ref, kn_ref, v_ref, eg_ref, beta_ref):
    xc = cb_ref[...] + cw_ref[CONV_W - 1:CONV_W, :] * xr_ref[...]
    for j in range(CONV_W - 1):
        xc = xc + cw_ref[j:j + 1, :] * crnn_ref[j]
    a, log_a, i = _rg_gates(xc, wa_ref, ba_ref[...], wx_ref, bx_ref[...], lam_ref[...])
    mult = jnp.sqrt(_neg_expm1(2.0 * log_a))
    h = a * h0_ref[...] + mult * (i * xc)
    hn_ref[...] = h
    oa_ref[...] = h * ggr_ref[...]

    qc = cqw_ref[CONV_W - 1:CONV_W, :] * qkv_ref[...]
    for j in range(CONV_W - 1):
        qc = qc + cqw_ref[j:j + 1, :] * cqkv_ref[j]
    act = _silu(qc)
    for h_ in range(DN_HEADS):
        qh = act[:, h_ * DN_DK:(h_ + 1) * DN_DK]
        kh = act[:, DN_QK + h_ * DN_DK:DN_QK + (h_ + 1) * DN_DK]
        qn_ref[:, h_ * DN_DK:(h_ + 1) * DN_DK] = (
            qh * lax.rsqrt(jnp.sum(qh * qh, axis=-1, keepdims=True) + EPS) * (DN_DK ** -0.5))
        kn_ref[:, h_ * DN_DK:(h_ + 1) * DN_DK] = (
            kh * lax.rsqrt(jnp.sum(kh * kh, axis=-1, keepdims=True) + EPS))
    v_ref[...] = act[:, 2 * DN_QK:]
    gbv = gb_ref[...]
    eg_ref[...] = jnp.exp(-jnp.exp(nal_ref[...]) * _softplus(gbv + dtb_ref[...]))
    beta_ref[...] = _sigmoid(gbv)


def _sample_pre_call(xr, ggr, qkv, gb, crnn, cqkv, h0, cw, cb, wa, ba, wx, bx, lam, cqw, nal, dtb):
    n = xr.shape[0]
    args = (xr, ggr, qkv, gb, crnn, cqkv, h0, cw, cb, wa, ba, wx, bx, lam, cqw, nal, dtb)
    shp = lambda w: jax.ShapeDtypeStruct((n, w), F32)
    outs = [shp(D_RNN), shp(D_RNN), shp(DN_QK), shp(DN_QK), shp(DN_V), shp(LANES), shp(LANES)]
    return pl.pallas_call(
        _sample_pre_kernel,
        grid=(1,),
        in_specs=[_resident(a.shape) for a in args],
        out_specs=[pl.BlockSpec(o.shape, lambda i: (0, 0)) for o in outs],
        out_shape=outs,
        compiler_params=_params(("arbitrary",)),
        name="sample_pre",
    )(*args)


def _sample_state_kernel(qn_ref, kn_ref, v_ref, eg_ref, beta_ref, szg_ref, dnw_ref, s_ref,
                         ob_ref, sn_ref):
    srow = lax.broadcasted_iota(jnp.int32, (SUBLANES, DN_DK), 0)
    rows = lambda r0, r1: jnp.where(srow == 0, r0, jnp.where(srow == 1, r1, 0.0))
    for h in range(DN_HEADS):
        sl = slice(h * DN_DK, (h + 1) * DN_DK)
        q = qn_ref[0, :, sl]
        k = kn_ref[0, :, sl]
        v = v_ref[0, :, sl]
        eg = eg_ref[0, :, h:h + 1]
        beta = beta_ref[0, :, DN_HEADS + h:DN_HEADS + h + 1]
        s_h = s_ref[0, h]
        ws_qs = _dot(rows((k * beta) * eg, q * eg).astype(BF16), s_h.astype(BF16))
        vnew = v * beta - ws_qs[0:1, :]
        o = ws_qs[1:2, :] + jnp.sum(q * k, axis=-1, keepdims=True) * vnew
        zero = jnp.zeros_like(k)
        sn_ref[0, h] = s_h * eg + _dot_tn(rows(k, zero).astype(BF16), rows(vnew, zero).astype(BF16))
        on = o * lax.rsqrt(jnp.mean(o * o, axis=-1, keepdims=True) + EPS) * dnw_ref[...]
        ob_ref[0, :, sl] = on * szg_ref[0, :, sl]


def _sample_state_call(qn, kn, v, eg, beta, szg, dnw, s0):
    n = qn.shape[0]
    r3 = lambda a: a.reshape(n, 1, a.shape[-1])
    vec = lambda w: pl.BlockSpec((1, 1, w), lambda b: (b, 0, 0))
    s_spec = pl.BlockSpec((1, DN_HEADS, DN_DK, DN_DV), lambda b: (b, 0, 0, 0))
    ob, sn = pl.pallas_call(
        _sample_state_kernel,
        grid=(n,),
        in_specs=[vec(DN_QK), vec(DN_QK), vec(DN_V), vec(LANES), vec(LANES), vec(DN_V),
                  _resident((1, DN_DV)), s_spec],
        out_specs=[vec(DN_V), s_spec],
        out_shape=[jax.ShapeDtypeStruct((n, 1, DN_V), F32),
                   jax.ShapeDtypeStruct((n, DN_HEADS, DN_DK, DN_DV), F32)],
        compiler_params=_params(("arbitrary",)),
        name="sample_state",
    )(r3(qn), r3(kn), r3(v), r3(eg), r3(beta), r3(szg), dnw, s0)
    return ob.reshape(n, DN_V), sn


def _merge_kernel(x_ref, oa_ref, ob_ref, sga_ref, sgb_ref, ada_ref, wb_ref, wo_ref, o_ref, *, per_row):
    ya = _dot(oa_ref[...].astype(BF16), wb_ref[0])
    yb = _dot(ob_ref[...].astype(BF16), wb_ref[1])
    merged = (sga_ref[...] * ya + sgb_ref[...] * yb).astype(BF16)
    o_ref[...] = x_ref[...] + _ada_rows(ada_ref, 5, per_row) * _dot(merged, wo_ref[...])


def _merge_call(x, oa, ob, sga, sgb, ada, wb, wo, *, per_row, tm, seq_len):
    m = x.shape[0]
    row = pl.BlockSpec((tm, D_MODEL), lambda i: (i, 0))
    return pl.pallas_call(
        functools.partial(_merge_kernel, per_row=per_row),
        grid=(m // tm,),
        in_specs=[row, row, row, row, row, _ada_spec(per_row, m, seq_len // tm),
                  _resident((2, D_RNN, D_MODEL)), _resident((D_MODEL, D_MODEL))],
        out_specs=row,
        out_shape=jax.ShapeDtypeStruct((m, D_MODEL), F32),
        compiler_params=_params(("arbitrary",)),
        name="merge",
    )(x, oa, ob, sga, sgb, ada, wb, wo)


def kernel(x_prompt, x_sample, c_prompt, c_sample, state_rglru_h, state_rglru_conv, state_delta_S, state_delta_conv, w_ada, b_ada, norm_ffn1, w_ffn1_up, w_ffn1_down, norm_mix, w_in, conv_rnn_w, conv_rnn_b, rg_w_a, rg_b_a, rg_w_x, rg_b_x, rg_lambda, conv_qkv_w, dn_a_log, dn_dt_bias, dn_norm, w_branch, w_out, norm_ffn2, w_ffn2_up, w_ffn2_down, norm_final):
    batch, seq_len, _ = x_prompt.shape
    n_dec = x_sample.shape[0]
    assert w_ada.shape[0] == 1 and x_sample.shape[1] == 1

    row = lambda a: a.reshape(1, -1).astype(F32)
    wup1, wdn1 = w_ffn1_up[0].astype(BF16), w_ffn1_down[0].astype(BF16)
    wup2, wdn2 = w_ffn2_up[0].astype(BF16), w_ffn2_down[0].astype(BF16)
    w_main = jnp.concatenate([w_in[0][:, :AB_OFF], w_in[0][:, AB_OFF + 2 * DN_HEADS:]], axis=1).astype(BF16)
    w_ab = jnp.pad(w_in[0][:, AB_OFF:AB_OFF + 2 * DN_HEADS], ((0, 0), (0, LANES - 2 * DN_HEADS))).astype(BF16)
    wb, wo = w_branch[0].astype(BF16), w_out[0].astype(BF16)
    wa, wx = rg_w_a[0].astype(BF16), rg_w_x[0].astype(BF16)
    lane_pad = lambda a: jnp.pad(a.reshape(1, -1).astype(F32), ((0, 0), (0, LANES - a.size)))
    nal = lane_pad(dn_a_log[0])
    dtb = lane_pad(dn_dt_bias[0])
    nf = row(norm_final)

    ada = _ada_call(jnp.concatenate([c_prompt, c_sample], axis=0), w_ada[0], row(b_ada[0]))
    ada = ada.reshape(batch + n_dec, N_ADA, D_MODEL)
    ada_p = ada[:batch]
    ada_s = jnp.transpose(ada[batch:], (1, 0, 2))

    mixer_w = (conv_rnn_w[0], row(conv_rnn_b[0]), wa, row(rg_b_a[0]), wx, row(rg_b_x[0]), row(rg_lambda[0]))

    kw = dict(per_row=False, seq_len=seq_len)
    xp = x_prompt.reshape(batch * seq_len, D_MODEL)
    xp = _ffn_call(xp, ada_p, row(norm_ffn1[0]), wup1, wdn1, nf, k0=0, final_norm=False, tm=512, **kw)
    xr, ggr, qkv, szg, sga, sgb, gb = _inproj_call(xp, ada_p, row(norm_mix[0]), w_main, w_ab, tm=256, **kw)
    oa, hp, cp = _rglru_call(xr, ggr, *mixer_w, batch=batch, seq_len=seq_len, tl=256)
    ob, sp, qp = _delta_call(qkv, gb, szg, conv_qkv_w[0], nal, dtb, row(dn_norm[0]), batch=batch, seq_len=seq_len)
    xp = _merge_call(xp, oa, ob, sga, sgb, ada_p, wb, wo, tm=512, **kw)
    y_prompt = _ffn_call(xp, ada_p, row(norm_ffn2[0]), wup2, wdn2, nf, k0=6, final_norm=True, tm=512, **kw)

    kw = dict(per_row=True, seq_len=n_dec, tm=n_dec)
    xs = x_sample.reshape(n_dec, D_MODEL)
    xs = _ffn_call(xs, ada_s, row(norm_ffn1[0]), wup1, wdn1, nf, k0=0, final_norm=False, **kw)
    xr_s, ggr_s, qkv_s, szg_s, sga_s, sgb_s, gb_s = _inproj_call(xs, ada_s, row(norm_mix[0]), w_main, w_ab, **kw)
    crnn = jnp.transpose(state_rglru_conv[0], (1, 0, 2))
    cqkv = jnp.transpose(state_delta_conv[0], (1, 0, 2))
    oa_s, hs, qn, kn, v, eg, beta = _sample_pre_call(
        xr_s, ggr_s, qkv_s, gb_s, crnn, cqkv, state_rglru_h[0], *mixer_w, conv_qkv_w[0], nal, dtb)
    ob_s, ss = _sample_state_call(qn, kn, v, eg, beta, szg_s, row(dn_norm[0]), state_delta_S[0])
    xs = _merge_call(xs, oa_s, ob_s, sga_s, sgb_s, ada_s, wb, wo, **kw)
    y_sample = _ffn_call(xs, ada_s, row(norm_ffn2[0]), wup2, wdn2, nf, k0=6, final_norm=True, **kw)
    cs = jnp.concatenate([state_rglru_conv[0][:, 1:], xr_s[:, None, :]], axis=1)
    qs = jnp.concatenate([state_delta_conv[0][:, 1:], qkv_s[:, None, :]], axis=1)

    return (y_prompt.reshape(batch, seq_len, D_MODEL), y_sample.reshape(n_dec, 1, D_MODEL),
            hp.reshape(1, batch, D_RNN), cp[None], sp[None], qp[None],
            hs[None], cs[None], ss[None], qs[None])
```

```python
import functools

import jax
import jax.numpy as jnp
from jax import lax
from jax.experimental import pallas as pl
from jax.experimental.pallas import tpu as pltpu

D_MODEL = 1024
D_RNN = 1024
RG_BLOCKS = 8
RG_BLOCK_W = D_RNN // RG_BLOCKS
RG_C = 8.0
CONV_W = 4
DN_HEADS = 8
DN_DK = 128
DN_DV = 128
DN_QK = DN_HEADS * DN_DK
DN_V = DN_HEADS * DN_DV
DN_QKV = 2 * DN_QK + DN_V
DN_CHUNK = 64
D_FF = 2816
N_ADA = 9
EPS = 1e-6
LANES = 128
SUBLANES = 8
AB_OFF = D_RNN * 2 + DN_QKV
N_MAIN = 8 * D_MODEL

BF16 = jnp.bfloat16
F32 = jnp.float32
HI = lax.Precision.HIGHEST

VMEM_LIMIT = 56 * 1024 * 1024


def _params(sem):
    return pltpu.CompilerParams(dimension_semantics=sem, vmem_limit_bytes=VMEM_LIMIT)


def _resident(shape):
    nd = len(shape)
    return pl.BlockSpec(shape, lambda *_: (0,) * nd, pipeline_mode=pl.Buffered(1))


def _dot(a, b):
    return jnp.dot(a, b, preferred_element_type=F32)


def _dot_nt(a, b, precision=None):
    return lax.dot_general(a, b, (((1,), (1,)), ((), ())), precision=precision,
                           preferred_element_type=F32)


def _dot_tn(a, b, precision=None):
    return lax.dot_general(a, b, (((0,), (0,)), ((), ())), precision=precision,
                           preferred_element_type=F32)


def _sigmoid(x):
    return jax.nn.sigmoid(x)


def _silu(x):
    return x * jax.nn.sigmoid(x)


def _softplus(x):
    return jnp.maximum(x, 0.0) + jnp.log1p(jnp.exp(-jnp.abs(x)))


def _neg_expm1(x):
    return -jnp.tanh(0.5 * x) * (jnp.exp(x) + 1.0)


def _ada_rows(ada_ref, k, per_row):
    if per_row:
        return ada_ref[k]
    return ada_ref[k:k + 1, :]


def _norm_mod(x, nw, shift, scale):
    ms = jnp.mean(x * x, axis=-1, keepdims=True)
    hn = x * lax.rsqrt(ms + EPS) * nw
    return hn * (1.0 + scale) + shift


def _ada_kernel(c_ref, w_ref, b_ref, o_ref):
    o_ref[...] = _dot(c_ref[...].astype(BF16), w_ref[...].astype(BF16)) + b_ref[...]


def _ada_call(c_all, w_ada, b_ada):
    n = c_all.shape[0]
    tn = D_MODEL
    return pl.pallas_call(
        _ada_kernel,
        grid=(N_ADA * D_MODEL // tn,),
        in_specs=[pl.BlockSpec((n, D_MODEL), lambda j: (0, 0)),
                  pl.BlockSpec((D_MODEL, tn), lambda j: (0, j)),
                  pl.BlockSpec((1, tn), lambda j: (0, j))],
        out_specs=pl.BlockSpec((n, tn), lambda j: (0, j)),
        out_shape=jax.ShapeDtypeStruct((n, N_ADA * D_MODEL), F32),
        compiler_params=_params(("arbitrary",)),
        name="ada",
    )(c_all, w_ada, b_ada)


FFN_TF = 256


def _ffn_kernel(x_ref, ada_ref, nw_ref, wup_ref, wdn_ref, nf_ref, o_ref, *, k0, per_row, final_norm):
    x = x_ref[...]
    h = _norm_mod(x, nw_ref[...], _ada_rows(ada_ref, k0, per_row),
                  _ada_rows(ada_ref, k0 + 1, per_row)).astype(BF16)
    acc = jnp.zeros(x.shape, F32)
    for j in range(D_FF // FFN_TF):
        g = _dot(h, wup_ref[:, j * FFN_TF:(j + 1) * FFN_TF])
        v = _dot(h, wup_ref[:, D_FF + j * FFN_TF:D_FF + (j + 1) * FFN_TF])
        a = (_silu(g) * v).astype(BF16)
        acc = acc + _dot(a, wdn_ref[j * FFN_TF:(j + 1) * FFN_TF, :])
    y = x + 0.5 * _ada_rows(ada_ref, k0 + 2, per_row) * acc
    if final_norm:
        ms = jnp.mean(y * y, axis=-1, keepdims=True)
        y = y * lax.rsqrt(ms + EPS) * nf_ref[...]
    o_ref[...] = y


def _ada_spec(per_row, rows, tiles_per_seq):
    if per_row:
        return pl.BlockSpec((N_ADA, rows, D_MODEL), lambda i: (0, 0, 0))
    return pl.BlockSpec((None, N_ADA, D_MODEL), lambda i: (i // tiles_per_seq, 0, 0))


def _ffn_call(x, ada, nw, wup, wdn, nf, *, k0, per_row, final_norm, tm, seq_len):
    m = x.shape[0]
    kern = functools.partial(_ffn_kernel, k0=k0, per_row=per_row, final_norm=final_norm)
    return pl.pallas_call(
        kern,
        grid=(m // tm,),
        in_specs=[pl.BlockSpec((tm, D_MODEL), lambda i: (i, 0)),
                  _ada_spec(per_row, m, seq_len // tm),
                  _resident((1, D_MODEL)),
                  _resident((D_MODEL, 2 * D_FF)),
                  _resident((D_FF, D_MODEL)),
                  _resident((1, D_MODEL))],
        out_specs=pl.BlockSpec((tm, D_MODEL), lambda i: (i, 0)),
        out_shape=jax.ShapeDtypeStruct((m, D_MODEL), F32),
        compiler_params=_params(("arbitrary",)),
        name="ffn",
    )(x, ada, nw, wup, wdn, nf)


TP = 512
NCH = TP // DN_CHUNK


def _inproj_kernel(x_ref, ada_ref, nw_ref, w_ref, wab_ref,
                   xr_ref, ggr_ref, qkv_ref, szg_ref, sga_ref, sgb_ref, gb_ref, *, per_row):
    h = _norm_mod(x_ref[...], nw_ref[...], _ada_rows(ada_ref, 3, per_row),
                  _ada_rows(ada_ref, 4, per_row)).astype(BF16)

    def col(c):
        return _dot(h, w_ref[:, c * D_MODEL:(c + 1) * D_MODEL])

    xr_ref[...] = col(0)
    ggr_ref[...] = jax.nn.gelu(col(1)).astype(BF16)
    for c in range(3):
        qkv_ref[:, c * D_MODEL:(c + 1) * D_MODEL] = col(2 + c)
    szg_ref[...] = _silu(col(5)).astype(BF16)
    sga_ref[...] = _sigmoid(col(6)).astype(BF16)
    sgb_ref[...] = _sigmoid(col(7)).astype(BF16)
    gb_ref[...] = _dot(h, wab_ref[...])


def _inproj_call(x, ada, nw, w_main, w_ab, *, per_row, tm, seq_len):
    m = x.shape[0]
    row = lambda n: pl.BlockSpec((tm, n), lambda i: (i, 0))
    shp = lambda n, dt: jax.ShapeDtypeStruct((m, n), dt)
    return pl.pallas_call(
        functools.partial(_inproj_kernel, per_row=per_row),
        grid=(m // tm,),
        in_specs=[row(D_MODEL), _ada_spec(per_row, m, seq_len // tm),
                  _resident((1, D_MODEL)), _resident((D_MODEL, N_MAIN)), _resident((D_MODEL, LANES))],
        out_specs=[row(D_RNN), row(D_RNN), row(DN_QKV), row(DN_V), row(D_MODEL), row(D_MODEL), row(LANES)],
        out_shape=[shp(D_RNN, F32), shp(D_RNN, BF16), shp(DN_QKV, F32), shp(DN_V, BF16),
                   shp(D_MODEL, BF16), shp(D_MODEL, BF16), shp(LANES, F32)],
        compiler_params=_params(("arbitrary",)),
        name="inproj",
    )(x, ada, nw, w_main, w_ab)


def _shift_rows(x, prev_row):
    rolled = pltpu.roll(x, 1, axis=0)
    row = lax.broadcasted_iota(jnp.int32, (SUBLANES, 1), 0)
    return jnp.concatenate([jnp.where(row == 0, prev_row, rolled[:SUBLANES]), rolled[SUBLANES:]], axis=0)


def _causal_conv(x, prev, cw):
    acc = cw[CONV_W - 1:CONV_W, :] * x
    sh = x
    for j in range(1, CONV_W):
        sh = _shift_rows(sh, prev[CONV_W - 1 - j:CONV_W - j, :])
        acc = acc + cw[CONV_W - 1 - j:CONV_W - j, :] * sh
    return acc


def _conv_tile(x_ref, carry, cw_ref, lanes):
    x = x_ref[:, lanes]
    y = _causal_conv(x, carry[0:CONV_W - 1, lanes], cw_ref[:, lanes])
    carry[0:CONV_W - 1, lanes] = x[x.shape[0] - (CONV_W - 1):, :]
    return y


def _rg_gates(xc, wa_ref, ba, wx_ref, bx, lam):
    ra, ix = [], []
    for n in range(RG_BLOCKS):
        xb = xc[:, n * RG_BLOCK_W:(n + 1) * RG_BLOCK_W].astype(BF16)
        ra.append(_dot(xb, wa_ref[n]))
        ix.append(_dot(xb, wx_ref[n]))
    r = _sigmoid(jnp.concatenate(ra, axis=1) + ba)
    i = _sigmoid(jnp.concatenate(ix, axis=1) + bx)
    log_a = (-RG_C) * r * _softplus(-lam)
    return jnp.exp(log_a), log_a, i


def _rglru_kernel(xr_ref, ggr_ref, cw_ref, cb_ref, wa_ref, ba_ref, wx_ref, bx_ref, lam_ref,
                  oa_ref, hl_ref, cs_ref, carry, hcar, acum_s, hloc_s, hin_s):
    t = pl.program_id(1)

    @pl.when(t == 0)
    def _():
        carry[...] = jnp.zeros(carry.shape, F32)
        hcar[...] = jnp.zeros(hcar.shape, F32)

    row = lax.broadcasted_iota(jnp.int32, (TP, 1), 0)
    sub = row & (SUBLANES - 1)
    first = jnp.logical_and(row == 0, t == 0)
    for n in range(RG_BLOCKS):
        ln = slice(n * RG_BLOCK_W, (n + 1) * RG_BLOCK_W)
        xc = _conv_tile(xr_ref, carry, cw_ref, ln) + cb_ref[:, ln]
        xb = xc.astype(BF16)
        r = _sigmoid(_dot(xb, wa_ref[n]) + ba_ref[:, ln])
        i = _sigmoid(_dot(xb, wx_ref[n]) + bx_ref[:, ln])
        log_a = (-RG_C) * r * _softplus(-lam_ref[:, ln])
        a = jnp.exp(log_a)
        mult = jnp.where(first, 1.0, jnp.sqrt(_neg_expm1(2.0 * log_a)))
        b = mult * (i * xc)
        d = 1
        while d < SUBLANES:
            keep = sub >= d
            a_prev = jnp.where(keep, pltpu.roll(a, d, axis=0), 1.0)
            b_prev = jnp.where(keep, pltpu.roll(b, d, axis=0), 0.0)
            b = a * b_prev + b
            a = a * a_prev
            d *= 2
        acum_s[:, ln] = a
        hloc_s[:, ln] = b
    cs_ref[0] = carry[0:CONV_W - 1, :]

    def step(g, h):
        r = g * SUBLANES
        hin_s[pl.ds(pl.multiple_of(r, SUBLANES), SUBLANES), :] = jnp.broadcast_to(h, (SUBLANES, D_RNN))
        e = r + SUBLANES - 1
        return acum_s[pl.ds(e, 1), :] * h + hloc_s[pl.ds(e, 1), :]

    h = lax.fori_loop(0, TP // SUBLANES, step, hcar[0:1, :], unroll=8)
    hcar[0:1, :] = h
    hl_ref[0] = h

    y = hloc_s[...] + acum_s[...] * hin_s[...]
    oa_ref[...] = (y * ggr_ref[...].astype(F32)).astype(BF16)


def _rglru_call(xr, ggr, cw, cb, wa, ba, wx, bx, lam, *, batch, seq_len):
    nt = seq_len // TP
    row = pl.BlockSpec((TP, D_RNN), lambda b, t: (b * nt + t, 0))
    return pl.pallas_call(
        _rglru_kernel,
        grid=(batch, nt),
        in_specs=[row, row,
                  _resident((CONV_W, D_RNN)), _resident((1, D_RNN)),
                  _resident((RG_BLOCKS, RG_BLOCK_W, RG_BLOCK_W)), _resident((1, D_RNN)),
                  _resident((RG_BLOCKS, RG_BLOCK_W, RG_BLOCK_W)), _resident((1, D_RNN)),
                  _resident((1, D_RNN))],
        out_specs=[row,
                   pl.BlockSpec((1, 1, D_RNN), lambda b, t: (b, 0, 0)),
                   pl.BlockSpec((1, CONV_W - 1, D_RNN), lambda b, t: (b, 0, 0))],
        out_shape=[jax.ShapeDtypeStruct((batch * seq_len, D_RNN), BF16),
                   jax.ShapeDtypeStruct((batch, 1, D_RNN), F32),
                   jax.ShapeDtypeStruct((batch, CONV_W - 1, D_RNN), F32)],
        scratch_shapes=[pltpu.VMEM((SUBLANES, D_RNN), F32),
                        pltpu.VMEM((SUBLANES, D_RNN), F32),
                        pltpu.VMEM((TP, D_RNN), F32),
                        pltpu.VMEM((TP, D_RNN), F32),
                        pltpu.VMEM((TP, D_RNN), F32)],
        compiler_params=_params(("arbitrary", "arbitrary")),
        name="rglru",
    )(xr, ggr, cw, cb, wa, ba, wx, bx, lam)


def _split(x):
    hi = x.astype(BF16)
    return hi, (x - hi.astype(F32)).astype(BF16)


def _dot3(a_hi, a_lo, b):
    n = b.shape[1]
    b_hi, b_lo = _split(b)
    r = _dot(a_hi, jnp.concatenate([b_hi, b_lo], axis=1))
    return r[:, :n] + r[:, n:] + _dot(a_lo, b_hi)


def _pair_mask(ti, tj, lvl):
    same = (ti >> (lvl + 1)) == (tj >> (lvl + 1))
    return jnp.logical_and(same, jnp.logical_and(((ti >> lvl) & 1) == 1, ((tj >> lvl) & 1) == 0))


DELTA_CPI = 2
DELTA_CONV_LANES = 512


def _delta_kernel(qkv_ref, gb_ref, szg_ref, cw_ref, alog_ref, dtb_ref, dnw_ref,
                  ob_ref, s_ref, cs_ref,
                  carry, s_acc, act_s, o_s, u_s, wq_s, attn_s, kd_s, egl_s, *, nt):
    t = pl.program_id(1)
    c = DN_CHUNK
    heads = range(DN_HEADS)

    @pl.when(t == 0)
    def _():
        carry[...] = jnp.zeros(carry.shape, F32)
        s_acc[...] = jnp.zeros(s_acc.shape, F32)

    for l0 in range(0, DN_QKV, DELTA_CONV_LANES):
        ln = slice(l0, l0 + DELTA_CONV_LANES)
        act_s[:, ln] = _silu(_conv_tile(qkv_ref, carry, cw_ref, ln))
    cs_ref[0] = carry[0:CONV_W - 1, :]

    ti = lax.broadcasted_iota(jnp.int32, (c, 2 * c), 0)
    tj = lax.broadcasted_iota(jnp.int32, (c, 2 * c), 1)
    left = tj < c
    incl = jnp.logical_and(ti >= tj, left)
    strict = jnp.logical_and(ti > tj, left)
    eye_l = (ti == tj).astype(F32)
    cum_l = incl[:, :c].astype(F32)
    cum_r = jnp.logical_and(ti <= tj, left).astype(F32)
    last = c - 1
    zpad = jnp.zeros((c, DN_DK), BF16)
    neg_a = -jnp.exp(alog_ref[...])
    dtb = dtb_ref[...]

    def chunk_rows(ref, j, lanes):
        return ref[pl.ds(pl.multiple_of(j * c, c), c), lanes]

    def prep(it, _):
        js = [it * DELTA_CPI + k for k in range(DELTA_CPI)]
        gcum, gcum_t, beta = [], [], []
        for j in js:
            gbv = chunk_rows(gb_ref, j, slice(None))
            g = neg_a * _softplus(gbv + dtb)
            beta.append(_sigmoid(gbv))
            gcum.append(jnp.dot(cum_l, g, precision=HI, preferred_element_type=F32))
            gcum_t.append(_dot_tn(g, cum_r, precision=HI))
        chains = [(k, h) for k in range(DELTA_CPI) for h in heads]
        gc, decay, qn, kn, kn16, kb, vb = {}, {}, {}, {}, {}, {}, {}
        for k, h in chains:
            j = js[k]
            gc[k, h] = gcum[k][:, h:h + 1]
            decay[k, h] = jnp.exp(jnp.where(incl, gc[k, h] - gcum_t[k][h:h + 1, :], -jnp.inf))
            bh = beta[k][:, DN_HEADS + h:DN_HEADS + h + 1]
            qh = chunk_rows(act_s, j, slice(h * DN_DK, (h + 1) * DN_DK))
            kh = chunk_rows(act_s, j, slice(DN_QK + h * DN_DK, DN_QK + (h + 1) * DN_DK))
            vh = chunk_rows(act_s, j, slice(2 * DN_QK + h * DN_DV, 2 * DN_QK + (h + 1) * DN_DV))
            qn[k, h] = qh * lax.rsqrt(jnp.sum(qh * qh, axis=-1, keepdims=True) + EPS) * (DN_DK ** -0.5)
            kn[k, h] = kh * lax.rsqrt(jnp.sum(kh * kh, axis=-1, keepdims=True) + EPS)
            kn16[k, h] = kn[k, h].astype(BF16)
            kb[k, h] = kn[k, h] * bh
            vb[k, h] = vh * bh
        lm = {}
        for ch in chains:
            kq = _dot_nt(jnp.concatenate([kb[ch], qn[ch]], axis=0).astype(BF16),
                         jnp.concatenate([kn16[ch], zpad], axis=0))
            lm[ch] = jnp.where(strict, kq[:c] * decay[ch], 0.0)
            attn_s[js[ch[0]], ch[1]] = (kq[c:] * decay[ch])[:, :c].astype(BF16)
        d = {ch: eye_l - jnp.where(_pair_mask(ti, tj, 0), lm[ch], 0.0) for ch in chains}
        for lvl in range(1, c.bit_length() - 1):
            pair = _pair_mask(ti, tj, lvl)
            d16 = {ch: d[ch].astype(BF16) for ch in chains}
            ed = {ch: _dot(jnp.where(pair, lm[ch], 0.0).astype(BF16)[:, :c], d16[ch]) for ch in chains}
            d = {ch: d[ch] - _dot(d16[ch][:, :c], ed[ch].astype(BF16)) for ch in chains}
        eg = {ch: jnp.exp(gc[ch]) for ch in chains}
        a16 = {ch: d[ch].astype(BF16)[:, :c] for ch in chains}
        rhs = {ch: jnp.concatenate([vb[ch], kb[ch] * eg[ch]], axis=1) for ch in chains}
        x0 = {ch: _dot(a16[ch], rhs[ch].astype(BF16)) for ch in chains}
        res = {}
        for ch in chains:
            l_hi, l_lo = _split(lm[ch])
            res[ch] = rhs[ch] - x0[ch] - _dot3(l_hi[:, :c], l_lo[:, :c], x0[ch])
        for ch in chains:
            k, h = ch
            x = x0[ch] + _dot(a16[ch], res[ch].astype(BF16))
            u_s[js[k], h] = x[:, :DN_DV]
            wq_s[js[k], h] = jnp.concatenate([x[:, DN_DV:], qn[ch] * eg[ch]], axis=0).astype(BF16)
            gl = gcum[k][last:last + 1, h:h + 1]
            kd_s[js[k], h] = (kn[ch] * jnp.exp(gl - gc[ch])).astype(BF16)
        for k, j in enumerate(js):
            egl_s[pl.ds(pl.multiple_of(j * SUBLANES, SUBLANES), SUBLANES), :] = jnp.broadcast_to(
                jnp.exp(gcum[k][last:last + 1, :]), (SUBLANES, LANES))
        return 0

    lax.fori_loop(0, NCH // DELTA_CPI, prep, 0)

    def recur(j, _):
        egl = egl_s[pl.ds(pl.multiple_of(j * SUBLANES, SUBLANES), 1), :]
        s_old = [s_acc[h] for h in heads]
        wq = [_dot(wq_s[j, h], s_old[h].astype(BF16)) for h in heads]
        vnew16 = [(u_s[j, h] - wq[h][:c]).astype(BF16) for h in heads]
        o = [wq[h][c:] + _dot(attn_s[j, h], vnew16[h]) for h in heads]
        for h in heads:
            s_acc[h] = s_old[h] * egl[:, h:h + 1] + _dot_tn(kd_s[j, h], vnew16[h])
        for h in heads:
            on = o[h] * lax.rsqrt(jnp.mean(o[h] * o[h], axis=-1, keepdims=True) + EPS) * dnw_ref[...]
            o_s[pl.ds(pl.multiple_of(j * c, c), c), h * DN_DV:(h + 1) * DN_DV] = on
        return 0

    lax.fori_loop(0, NCH, recur, 0)

    ob_ref[...] = (o_s[...] * szg_ref[...].astype(F32)).astype(BF16)

    @pl.when(t == nt - 1)
    def _():
        s_ref[0] = s_acc[...]


def _delta_call(qkv, gb, szg, cw, alog, dtb, dnw, *, batch, seq_len):
    c = DN_CHUNK
    nt = seq_len // TP
    row = lambda n: pl.BlockSpec((TP, n), lambda b, t: (b * nt + t, 0))
    return pl.pallas_call(
        functools.partial(_delta_kernel, nt=nt),
        grid=(batch, nt),
        in_specs=[row(DN_QKV), row(LANES), row(DN_V),
                  _resident((CONV_W, DN_QKV)), _resident((1, LANES)), _resident((1, LANES)),
                  _resident((1, DN_DV))],
        out_specs=[row(DN_V),
                   pl.BlockSpec((1, DN_HEADS, DN_DK, DN_DV), lambda b, t: (b, 0, 0, 0)),
                   pl.BlockSpec((1, CONV_W - 1, DN_QKV), lambda b, t: (b, 0, 0))],
        out_shape=[jax.ShapeDtypeStruct((batch * seq_len, DN_V), BF16),
                   jax.ShapeDtypeStruct((batch, DN_HEADS, DN_DK, DN_DV), F32),
                   jax.ShapeDtypeStruct((batch, CONV_W - 1, DN_QKV), F32)],
        scratch_shapes=[pltpu.VMEM((SUBLANES, DN_QKV), F32),
                        pltpu.VMEM((DN_HEADS, DN_DK, DN_DV), F32),
                        pltpu.VMEM((TP, DN_QKV), F32),
                        pltpu.VMEM((TP, DN_V), F32),
                        pltpu.VMEM((NCH, DN_HEADS, c, DN_DV), F32),
                        pltpu.VMEM((NCH, DN_HEADS, 2 * c, DN_DK), BF16),
                        pltpu.VMEM((NCH, DN_HEADS, c, c), BF16),
                        pltpu.VMEM((NCH, DN_HEADS, c, DN_DK), BF16),
                        pltpu.VMEM((NCH * SUBLANES, LANES), F32)],
        compiler_params=_params(("arbitrary", "arbitrary")),
        name="delta",
    )(qkv, gb, szg, cw, alog, dtb, dnw)


def _sample_pre_kernel(xr_ref, ggr_ref, qkv_ref, gb_ref, crnn_ref, cqkv_ref, h0_ref,
                       cw_ref, cb_ref, wa_ref, ba_ref, wx_ref, bx_ref, lam_ref,
                       cqw_ref, alog_ref, dtb_ref,
                       oa_ref, hn_ref, qn_ref, kn_ref, v_ref, eg_ref, beta_ref):
    xc = cb_ref[...] + cw_ref[CONV_W - 1:CONV_W, :] * xr_ref[...]
    for j in range(CONV_W - 1):
        xc = xc + cw_ref[j:j + 1, :] * crnn_ref[j]
    a, log_a, i = _rg_gates(xc, wa_ref, ba_ref[...], wx_ref, bx_ref[...], lam_ref[...])
    mult = jnp.sqrt(_neg_expm1(2.0 * log_a))
    h = a * h0_ref[...] + mult * (i * xc)
    hn_ref[...] = h
    oa_ref[...] = (h * ggr_ref[...].astype(F32)).astype(BF16)

    qc = cqw_ref[CONV_W - 1:CONV_W, :] * qkv_ref[...]
    for j in range(CONV_W - 1):
        qc = qc + cqw_ref[j:j + 1, :] * cqkv_ref[j]
    act = _silu(qc)
    for h_ in range(DN_HEADS):
        qh = act[:, h_ * DN_DK:(h_ + 1) * DN_DK]
        kh = act[:, DN_QK + h_ * DN_DK:DN_QK + (h_ + 1) * DN_DK]
        qn_ref[:, h_ * DN_DK:(h_ + 1) * DN_DK] = (
            qh * lax.rsqrt(jnp.sum(qh * qh, axis=-1, keepdims=True) + EPS) * (DN_DK ** -0.5))
        kn_ref[:, h_ * DN_DK:(h_ + 1) * DN_DK] = (
            kh * lax.rsqrt(jnp.sum(kh * kh, axis=-1, keepdims=True) + EPS))
    v_ref[...] = act[:, 2 * DN_QK:]
    gbv = gb_ref[...]
    eg_ref[...] = jnp.exp(-jnp.exp(alog_ref[...]) * _softplus(gbv + dtb_ref[...]))
    beta_ref[...] = _sigmoid(gbv)


def _sample_pre_call(xr, ggr, qkv, gb, crnn, cqkv, h0, cw, cb, wa, ba, wx, bx, lam, cqw, alog, dtb):
    n = xr.shape[0]
    args = (xr, ggr, qkv, gb, crnn, cqkv, h0, cw, cb, wa, ba, wx, bx, lam, cqw, alog, dtb)
    shp = lambda w: jax.ShapeDtypeStruct((n, w), F32)
    outs = [jax.ShapeDtypeStruct((n, D_RNN), BF16), shp(D_RNN), shp(DN_QK), shp(DN_QK), shp(DN_V), shp(LANES), shp(LANES)]
    return pl.pallas_call(
        _sample_pre_kernel,
        grid=(1,),
        in_specs=[_resident(a.shape) for a in args],
        out_specs=[pl.BlockSpec(o.shape, lambda i: (0, 0)) for o in outs],
        out_shape=outs,
        compiler_params=_params(("arbitrary",)),
        name="sample_pre",
    )(*args)


def _sample_state_kernel(qn_ref, kn_ref, v_ref, eg_ref, beta_ref, szg_ref, dnw_ref, s_ref,
                         ob_ref, sn_ref):
    srow = lax.broadcasted_iota(jnp.int32, (SUBLANES, DN_DK), 0)
    rows = lambda r0, r1: jnp.where(srow == 0, r0, jnp.where(srow == 1, r1, 0.0))
    for h in range(DN_HEADS):
        sl = slice(h * DN_DK, (h + 1) * DN_DK)
        q = qn_ref[0, :, sl]
        k = kn_ref[0, :, sl]
        v = v_ref[0, :, sl]
        eg = eg_ref[0, :, h:h + 1]
        beta = beta_ref[0, :, DN_HEADS + h:DN_HEADS + h + 1]
        s_h = s_ref[0, h]
        ws_qs = _dot(rows((k * beta) * eg, q * eg).astype(BF16), s_h.astype(BF16))
        vnew = v * beta - ws_qs[0:1, :]
        o = ws_qs[1:2, :] + jnp.sum(q * k, axis=-1, keepdims=True) * vnew
        zero = jnp.zeros_like(k)
        sn_ref[0, h] = s_h * eg + _dot_tn(rows(k, zero).astype(BF16), rows(vnew, zero).astype(BF16))
        on = o * lax.rsqrt(jnp.mean(o * o, axis=-1, keepdims=True) + EPS) * dnw_ref[...]
        ob_ref[0, :, sl] = on * szg_ref[0, :, sl]


def _sample_state_call(qn, kn, v, eg, beta, szg, dnw, s0):
    n = qn.shape[0]
    r3 = lambda a: a.reshape(n, 1, a.shape[-1])
    vec = lambda w: pl.BlockSpec((1, 1, w), lambda b: (b, 0, 0))
    s_spec = pl.BlockSpec((1, DN_HEADS, DN_DK, DN_DV), lambda b: (b, 0, 0, 0))
    ob, sn = pl.pallas_call(
        _sample_state_kernel,
        grid=(n,),
        in_specs=[vec(DN_QK), vec(DN_QK), vec(DN_V), vec(LANES), vec(LANES), vec(DN_V),
                  _resident((1, DN_DV)), s_spec],
        out_specs=[vec(DN_V), s_spec],
        out_shape=[jax.ShapeDtypeStruct((n, 1, DN_V), F32),
                   jax.ShapeDtypeStruct((n, DN_HEADS, DN_DK, DN_DV), F32)],
        compiler_params=_params(("arbitrary",)),
        name="sample_state",
    )(r3(qn), r3(kn), r3(v), r3(eg), r3(beta), r3(szg), dnw, s0)
    return ob.reshape(n, DN_V), sn


def _merge_kernel(x_ref, oa_ref, ob_ref, sga_ref, sgb_ref, ada_ref, wb_ref, wo_ref, o_ref, *, per_row):
    ya = _dot(oa_ref[...], wb_ref[0])
    yb = _dot(ob_ref[...], wb_ref[1])
    merged = (sga_ref[...].astype(F32) * ya + sgb_ref[...].astype(F32) * yb).astype(BF16)
    o_ref[...] = x_ref[...] + _ada_rows(ada_ref, 5, per_row) * _dot(merged, wo_ref[...])


def _merge_call(x, oa, ob, sga, sgb, ada, wb, wo, *, per_row, tm, seq_len):
    m = x.shape[0]
    row = pl.BlockSpec((tm, D_MODEL), lambda i: (i, 0))
    return pl.pallas_call(
        functools.partial(_merge_kernel, per_row=per_row),
        grid=(m // tm,),
        in_specs=[row, row, row, row, row, _ada_spec(per_row, m, seq_len // tm),
                  _resident((2, D_RNN, D_MODEL)), _resident((D_MODEL, D_MODEL))],
        out_specs=row,
        out_shape=jax.ShapeDtypeStruct((m, D_MODEL), F32),
        compiler_params=_params(("arbitrary",)),
        name="merge",
    )(x, oa, ob, sga, sgb, ada, wb, wo)


def kernel(x_prompt, x_sample, c_prompt, c_sample, state_rglru_h, state_rglru_conv, state_delta_S, state_delta_conv, w_ada, b_ada, norm_ffn1, w_ffn1_up, w_ffn1_down, norm_mix, w_in, conv_rnn_w, conv_rnn_b, rg_w_a, rg_b_a, rg_w_x, rg_b_x, rg_lambda, conv_qkv_w, dn_a_log, dn_dt_bias, dn_norm, w_branch, w_out, norm_ffn2, w_ffn2_up, w_ffn2_down, norm_final):
    batch, seq_len, _ = x_prompt.shape
    n_dec = x_sample.shape[0]
    assert w_ada.shape[0] == 1 and x_sample.shape[1] == 1 and seq_len % TP == 0

    row = lambda a: a.reshape(1, -1).astype(F32)
    wup1, wdn1 = w_ffn1_up[0].astype(BF16), w_ffn1_down[0].astype(BF16)
    wup2, wdn2 = w_ffn2_up[0].astype(BF16), w_ffn2_down[0].astype(BF16)
    w_main = jnp.concatenate([w_in[0][:, :AB_OFF], w_in[0][:, AB_OFF + 2 * DN_HEADS:]], axis=1).astype(BF16)
    w_ab = jnp.pad(w_in[0][:, AB_OFF:AB_OFF + 2 * DN_HEADS], ((0, 0), (0, LANES - 2 * DN_HEADS))).astype(BF16)
    wb, wo = w_branch[0].astype(BF16), w_out[0].astype(BF16)
    wa, wx = rg_w_a[0].astype(BF16), rg_w_x[0].astype(BF16)
    lane_pad = lambda a: jnp.pad(a.reshape(1, -1).astype(F32), ((0, 0), (0, LANES - a.size)))
    alog = lane_pad(dn_a_log[0])
    dtb = lane_pad(dn_dt_bias[0])
    nf = row(norm_final)

    ada = _ada_call(jnp.concatenate([c_prompt, c_sample], axis=0), w_ada[0], row(b_ada[0]))
    ada = ada.reshape(batch + n_dec, N_ADA, D_MODEL)
    ada_p = ada[:batch]
    ada_s = jnp.transpose(ada[batch:], (1, 0, 2))

    mixer_w = (conv_rnn_w[0], row(conv_rnn_b[0]), wa, row(rg_b_a[0]), wx, row(rg_b_x[0]), row(rg_lambda[0]))

    kw = dict(per_row=False, seq_len=seq_len, tm=TP)
    xp = x_prompt.reshape(batch * seq_len, D_MODEL)
    xp = _ffn_call(xp, ada_p, row(norm_ffn1[0]), wup1, wdn1, nf, k0=0, final_norm=False, **kw)
    xr, ggr, qkv, szg, sga, sgb, gb = _inproj_call(xp, ada_p, row(norm_mix[0]), w_main, w_ab, **kw)
    oa, hp, cp = _rglru_call(xr, ggr, *mixer_w, batch=batch, seq_len=seq_len)
    ob, sp, qp = _delta_call(qkv, gb, szg, conv_qkv_w[0], alog, dtb, row(dn_norm[0]), batch=batch, seq_len=seq_len)
    xp = _merge_call(xp, oa, ob, sga, sgb, ada_p, wb, wo, **kw)
    y_prompt = _ffn_call(xp, ada_p, row(norm_ffn2[0]), wup2, wdn2, nf, k0=6, final_norm=True, **kw)

    kw = dict(per_row=True, seq_len=n_dec, tm=n_dec)
    xs = x_sample.reshape(n_dec, D_MODEL)
    xs = _ffn_call(xs, ada_s, row(norm_ffn1[0]), wup1, wdn1, nf, k0=0, final_norm=False, **kw)
    xr_s, ggr_s, qkv_s, szg_s, sga_s, sgb_s, gb_s = _inproj_call(xs, ada_s, row(norm_mix[0]), w_main, w_ab, **kw)
    crnn = jnp.transpose(state_rglru_conv[0], (1, 0, 2))
    cqkv = jnp.transpose(state_delta_conv[0], (1, 0, 2))
    oa_s, hs, qn, kn, v, eg, beta = _sample_pre_call(
        xr_s, ggr_s, qkv_s, gb_s, crnn, cqkv, state_rglru_h[0], *mixer_w, conv_qkv_w[0], alog, dtb)
    ob_s, ss = _sample_state_call(qn, kn, v, eg, beta, szg_s.astype(F32), row(dn_norm[0]), state_delta_S[0])
    xs = _merge_call(xs, oa_s, ob_s.astype(BF16), sga_s, sgb_s, ada_s, wb, wo, **kw)
    y_sample = _ffn_call(xs, ada_s, row(norm_ffn2[0]), wup2, wdn2, nf, k0=6, final_norm=True, **kw)
    cs = jnp.concatenate([state_rglru_conv[0][:, 1:], xr_s[:, None, :]], axis=1)
    qs = jnp.concatenate([state_delta_conv[0][:, 1:], qkv_s[:, None, :]], axis=1)

    return (y_prompt.reshape(batch, seq_len, D_MODEL), y_sample.reshape(n_dec, 1, D_MODEL),
            hp.reshape(1, batch, D_RNN), cp[None], sp[None], qp[None],
            hs[None], cs[None], ss[None], qs[None])
```

```python
import functools

import jax
import jax.numpy as jnp
from jax import lax
from jax.experimental import pallas as pl
from jax.experimental.pallas import tpu as pltpu

D_MODEL = 1024
D_RNN = 1024
RG_BLOCKS = 8
RG_BLOCK_W = D_RNN // RG_BLOCKS
RG_C = 8.0
CONV_W = 4
DN_HEADS = 8
DN_DK = 128
DN_DV = 128
DN_QK = DN_HEADS * DN_DK
DN_V = DN_HEADS * DN_DV
DN_QKV = 2 * DN_QK + DN_V
DN_CHUNK = 64
D_FF = 2816
N_ADA = 9
EPS = 1e-6
LANES = 128
SUBLANES = 8
AB_OFF = D_RNN * 2 + DN_QKV
N_MAIN = 8 * D_MODEL

BF16 = jnp.bfloat16
F32 = jnp.float32
HI = lax.Precision.HIGHEST

VMEM_LIMIT = 56 * 1024 * 1024


def _params(sem):
    return pltpu.CompilerParams(dimension_semantics=sem, vmem_limit_bytes=VMEM_LIMIT)


def _resident(shape):
    nd = len(shape)
    return pl.BlockSpec(shape, lambda *_: (0,) * nd, pipeline_mode=pl.Buffered(1))


def _dot(a, b):
    return jnp.dot(a, b, preferred_element_type=F32)


def _dot_nt(a, b, precision=None):
    return lax.dot_general(a, b, (((1,), (1,)), ((), ())), precision=precision,
                           preferred_element_type=F32)


def _dot_tn(a, b, precision=None):
    return lax.dot_general(a, b, (((0,), (0,)), ((), ())), precision=precision,
                           preferred_element_type=F32)


def _sigmoid(x):
    return jax.nn.sigmoid(x)


def _silu(x):
    return x * jax.nn.sigmoid(x)


def _softplus(x):
    return jnp.maximum(x, 0.0) + jnp.log1p(jnp.exp(-jnp.abs(x)))


def _neg_expm1(x):
    return -jnp.tanh(0.5 * x) * (jnp.exp(x) + 1.0)


def _ada_rows(ada_ref, k, per_row):
    if per_row:
        return ada_ref[k]
    return ada_ref[k:k + 1, :]


def _norm_mod(x, nw, shift, scale):
    ms = jnp.mean(x * x, axis=-1, keepdims=True)
    hn = x * lax.rsqrt(ms + EPS) * nw
    return hn * (1.0 + scale) + shift


def _ada_kernel(c_ref, w_ref, b_ref, o_ref):
    o_ref[...] = _dot(c_ref[...].astype(BF16), w_ref[...].astype(BF16)) + b_ref[...]


def _ada_call(c_all, w_ada, b_ada):
    n = c_all.shape[0]
    tn = D_MODEL
    return pl.pallas_call(
        _ada_kernel,
        grid=(N_ADA * D_MODEL // tn,),
        in_specs=[pl.BlockSpec((n, D_MODEL), lambda j: (0, 0)),
                  pl.BlockSpec((D_MODEL, tn), lambda j: (0, j)),
                  pl.BlockSpec((1, tn), lambda j: (0, j))],
        out_specs=pl.BlockSpec((n, tn), lambda j: (0, j)),
        out_shape=jax.ShapeDtypeStruct((n, N_ADA * D_MODEL), F32),
        compiler_params=_params(("arbitrary",)),
        name="ada",
    )(c_all, w_ada, b_ada)


FFN_TF = 256


def _ffn_kernel(x_ref, ada_ref, nw_ref, wup_ref, wdn_ref, nf_ref, o_ref, *, k0, per_row, final_norm):
    x = x_ref[...]
    h = _norm_mod(x, nw_ref[...], _ada_rows(ada_ref, k0, per_row),
                  _ada_rows(ada_ref, k0 + 1, per_row)).astype(BF16)
    acc = jnp.zeros(x.shape, F32)
    for j in range(D_FF // FFN_TF):
        g = _dot(h, wup_ref[:, j * FFN_TF:(j + 1) * FFN_TF])
        v = _dot(h, wup_ref[:, D_FF + j * FFN_TF:D_FF + (j + 1) * FFN_TF])
        a = (_silu(g) * v).astype(BF16)
        acc = acc + _dot(a, wdn_ref[j * FFN_TF:(j + 1) * FFN_TF, :])
    y = x + 0.5 * _ada_rows(ada_ref, k0 + 2, per_row) * acc
    if final_norm:
        ms = jnp.mean(y * y, axis=-1, keepdims=True)
        y = y * lax.rsqrt(ms + EPS) * nf_ref[...]
    o_ref[...] = y


def _ada_spec(per_row, rows, tiles_per_seq):
    if per_row:
        return pl.BlockSpec((N_ADA, rows, D_MODEL), lambda i: (0, 0, 0))
    return pl.BlockSpec((None, N_ADA, D_MODEL), lambda i: (i // tiles_per_seq, 0, 0))


def _ffn_call(x, ada, nw, wup, wdn, nf, *, k0, per_row, final_norm, tm, seq_len):
    m = x.shape[0]
    kern = functools.partial(_ffn_kernel, k0=k0, per_row=per_row, final_norm=final_norm)
    return pl.pallas_call(
        kern,
        grid=(m // tm,),
        in_specs=[pl.BlockSpec((tm, D_MODEL), lambda i: (i, 0)),
                  _ada_spec(per_row, m, seq_len // tm),
                  _resident((1, D_MODEL)),
                  _resident((D_MODEL, 2 * D_FF)),
                  _resident((D_FF, D_MODEL)),
                  _resident((1, D_MODEL))],
        out_specs=pl.BlockSpec((tm, D_MODEL), lambda i: (i, 0)),
        out_shape=jax.ShapeDtypeStruct((m, D_MODEL), F32),
        compiler_params=_params(("arbitrary",)),
        name="ffn",
    )(x, ada, nw, wup, wdn, nf)


TP = 512
NCH = TP // DN_CHUNK


def _inproj_kernel(x_ref, ada_ref, nw_ref, w_ref, wab_ref,
                   xr_ref, ggr_ref, qkv_ref, szg_ref, sga_ref, sgb_ref, gb_ref, *, per_row):
    h = _norm_mod(x_ref[...], nw_ref[...], _ada_rows(ada_ref, 3, per_row),
                  _ada_rows(ada_ref, 4, per_row)).astype(BF16)

    def col(c):
        return _dot(h, w_ref[:, c * D_MODEL:(c + 1) * D_MODEL])

    xr_ref[...] = col(0)
    ggr_ref[...] = jax.nn.gelu(col(1)).astype(BF16)
    for c in range(3):
        qkv_ref[:, c * D_MODEL:(c + 1) * D_MODEL] = col(2 + c)
    szg_ref[...] = _silu(col(5)).astype(BF16)
    sga_ref[...] = _sigmoid(col(6)).astype(BF16)
    sgb_ref[...] = _sigmoid(col(7)).astype(BF16)
    gb_ref[...] = _dot(h, wab_ref[...])


def _inproj_call(x, ada, nw, w_main, w_ab, *, per_row, tm, seq_len):
    m = x.shape[0]
    row = lambda n: pl.BlockSpec((tm, n), lambda i: (i, 0))
    shp = lambda n, dt: jax.ShapeDtypeStruct((m, n), dt)
    return pl.pallas_call(
        functools.partial(_inproj_kernel, per_row=per_row),
        grid=(m // tm,),
        in_specs=[row(D_MODEL), _ada_spec(per_row, m, seq_len // tm),
                  _resident((1, D_MODEL)), _resident((D_MODEL, N_MAIN)), _resident((D_MODEL, LANES))],
        out_specs=[row(D_RNN), row(D_RNN), row(DN_QKV), row(DN_V), row(D_MODEL), row(D_MODEL), row(LANES)],
        out_shape=[shp(D_RNN, F32), shp(D_RNN, BF16), shp(DN_QKV, F32), shp(DN_V, BF16),
                   shp(D_MODEL, BF16), shp(D_MODEL, BF16), shp(LANES, F32)],
        compiler_params=_params(("arbitrary",)),
        name="inproj",
    )(x, ada, nw, w_main, w_ab)


def _causal_conv(x, carry, cw_ref, lanes):
    t, c = x.shape
    x3 = x.reshape(t // SUBLANES, SUBLANES, c)
    prev8 = carry[:, lanes]
    sub = lax.broadcasted_iota(jnp.int32, (1, SUBLANES, 1), 1)
    acc = cw_ref[CONV_W - 1:CONV_W, lanes][None] * x3
    for j in range(1, CONV_W):
        r = pltpu.roll(x3, j, axis=1)
        rp = jnp.concatenate([pltpu.roll(prev8, j, axis=0)[None], r[:-1]], axis=0)
        acc = acc + cw_ref[CONV_W - 1 - j:CONV_W - j, lanes][None] * jnp.where(sub >= j, r, rp)
    carry[:, lanes] = x[t - SUBLANES:, :]
    return acc.reshape(t, c)


def _rg_gates(xc, wa_ref, ba, wx_ref, bx, lam):
    ra, ix = [], []
    for n in range(RG_BLOCKS):
        xb = xc[:, n * RG_BLOCK_W:(n + 1) * RG_BLOCK_W].astype(BF16)
        ra.append(_dot(xb, wa_ref[n]))
        ix.append(_dot(xb, wx_ref[n]))
    r = _sigmoid(jnp.concatenate(ra, axis=1) + ba)
    i = _sigmoid(jnp.concatenate(ix, axis=1) + bx)
    log_a = (-RG_C) * r * _softplus(-lam)
    return jnp.exp(log_a), log_a, i


MIX_CONV_LANES = 512


def _mixin_kernel(x_ref, ada_ref, nw_ref, w_ref, wab_ref,
                  cw_ref, cb_ref, wa_ref, ba_ref, wx_ref, bx_ref, lam_ref, cqw_ref,
                  oa_ref, act_ref, szg_ref, sga_ref, sgb_ref, gb_ref, hl_ref, cs_ref, cq_ref,
                  xcar, qcar, hcar, acum_s, hloc_s, hin_s):
    t = pl.program_id(1)

    @pl.when(t == 0)
    def _():
        xcar[...] = jnp.zeros(xcar.shape, F32)
        qcar[...] = jnp.zeros(qcar.shape, F32)
        hcar[...] = jnp.zeros(hcar.shape, F32)

    h = _norm_mod(x_ref[...], nw_ref[...], ada_ref[3:4, :], ada_ref[4:5, :]).astype(BF16)

    def col(c):
        return _dot(h, w_ref[:, c * D_MODEL:(c + 1) * D_MODEL])

    groups = TP // SUBLANES
    sub = lax.broadcasted_iota(jnp.int32, (1, SUBLANES, 1), 1)
    row = lax.broadcasted_iota(jnp.int32, (TP, 1), 0)
    first = jnp.logical_and(row == 0, t == 0)

    def rglru_block(xr, n):
        ln = slice(n * RG_BLOCK_W, (n + 1) * RG_BLOCK_W)
        xc = _causal_conv(xr[:, ln], xcar, cw_ref, ln) + cb_ref[:, ln]
        xb = xc.astype(BF16)
        r = _sigmoid(_dot(xb, wa_ref[n]) + ba_ref[:, ln])
        i = _sigmoid(_dot(xb, wx_ref[n]) + bx_ref[:, ln])
        log_a = (-RG_C) * r * _softplus(-lam_ref[:, ln])
        a = jnp.exp(log_a)
        mult = jnp.where(first, 1.0, jnp.sqrt(_neg_expm1(2.0 * log_a)))
        a3 = a.reshape(groups, SUBLANES, RG_BLOCK_W)
        b3 = (mult * (i * xc)).reshape(groups, SUBLANES, RG_BLOCK_W)
        d = 1
        while d < SUBLANES:
            keep = sub >= d
            a_prev = jnp.where(keep, pltpu.roll(a3, d, axis=1), 1.0)
            b_prev = jnp.where(keep, pltpu.roll(b3, d, axis=1), 0.0)
            b3 = a3 * b_prev + b3
            a3 = a3 * a_prev
            d *= 2
        acum_s[:, ln] = a3.reshape(TP, RG_BLOCK_W)
        hloc_s[:, ln] = b3.reshape(TP, RG_BLOCK_W)

    def qkv_part(z, c):
        for l0 in range(0, D_MODEL, MIX_CONV_LANES):
            ln = slice(c * D_MODEL + l0, c * D_MODEL + l0 + MIX_CONV_LANES)
            act_ref[:, ln] = _silu(_causal_conv(z[:, l0:l0 + MIX_CONV_LANES], qcar, cqw_ref, ln))

    xr = col(0)
    for n in range(RG_BLOCKS):
        rglru_block(xr, n)
    cs_ref[0] = xcar[SUBLANES - (CONV_W - 1):, :]
    for c in range(DN_QKV // D_MODEL):
        qkv_part(col(2 + c), c)
    cq_ref[0] = qcar[SUBLANES - (CONV_W - 1):, :]

    hrow = hcar[0:1, :]
    for g in range(groups):
        hin_s[g * SUBLANES:(g + 1) * SUBLANES, :] = jnp.broadcast_to(hrow, (SUBLANES, D_RNN))
        e = (g + 1) * SUBLANES - 1
        hrow = acum_s[e:e + 1, :] * hrow + hloc_s[e:e + 1, :]
    hcar[0:1, :] = hrow
    hl_ref[0] = hrow

    y = hloc_s[...] + acum_s[...] * hin_s[...]
    oa_ref[...] = (y * jax.nn.gelu(col(1))).astype(BF16)
    szg_ref[...] = _silu(col(5)).astype(BF16)
    sga_ref[...] = _sigmoid(col(6)).astype(BF16)
    sgb_ref[...] = _sigmoid(col(7)).astype(BF16)
    gb_ref[...] = _dot(h, wab_ref[...])


def _mixin_call(x, ada, nw, w_main, w_ab, cw, cb, wa, ba, wx, bx, lam, cqw, *, batch, seq_len):
    nt = seq_len // TP
    m = batch * seq_len
    row = lambda n: pl.BlockSpec((TP, n), lambda b, t: (b * nt + t, 0))
    per_seq = lambda r, n: pl.BlockSpec((1, r, n), lambda b, t: (b, 0, 0))
    shp = lambda n, dt: jax.ShapeDtypeStruct((m, n), dt)
    blk = (RG_BLOCKS, RG_BLOCK_W, RG_BLOCK_W)
    return pl.pallas_call(
        _mixin_kernel,
        grid=(batch, nt),
        in_specs=[row(D_MODEL), pl.BlockSpec((None, N_ADA, D_MODEL), lambda b, t: (b, 0, 0)),
                  _resident((1, D_MODEL)), _resident((D_MODEL, N_MAIN)), _resident((D_MODEL, LANES)),
                  _resident((CONV_W, D_RNN)), _resident((1, D_RNN)), _resident(blk), _resident((1, D_RNN)),
                  _resident(blk), _resident((1, D_RNN)), _resident((1, D_RNN)), _resident((CONV_W, DN_QKV))],
        out_specs=[row(D_RNN), row(DN_QKV), row(DN_V), row(D_MODEL), row(D_MODEL), row(LANES),
                   per_seq(1, D_RNN), per_seq(CONV_W - 1, D_RNN), per_seq(CONV_W - 1, DN_QKV)],
        out_shape=[shp(D_RNN, BF16), shp(DN_QKV, F32), shp(DN_V, BF16), shp(D_MODEL, BF16), shp(D_MODEL, BF16),
                   shp(LANES, F32),
                   jax.ShapeDtypeStruct((batch, 1, D_RNN), F32),
                   jax.ShapeDtypeStruct((batch, CONV_W - 1, D_RNN), F32),
                   jax.ShapeDtypeStruct((batch, CONV_W - 1, DN_QKV), F32)],
        scratch_shapes=[pltpu.VMEM((SUBLANES, D_RNN), F32),
                        pltpu.VMEM((SUBLANES, DN_QKV), F32),
                        pltpu.VMEM((SUBLANES, D_RNN), F32),
                        pltpu.VMEM((TP, D_RNN), F32),
                        pltpu.VMEM((TP, D_RNN), F32),
                        pltpu.VMEM((TP, D_RNN), F32)],
        compiler_params=_params(("arbitrary", "arbitrary")),
        name="mixin",
    )(x, ada, nw, w_main, w_ab, cw, cb, wa, ba, wx, bx, lam, cqw)


def _split(x):
    hi = x.astype(BF16)
    return hi, (x - hi.astype(F32)).astype(BF16)


def _dot3(a_hi, a_lo, b):
    n = b.shape[1]
    b_hi, b_lo = _split(b)
    r = _dot(a_hi, jnp.concatenate([b_hi, b_lo], axis=1))
    return r[:, :n] + r[:, n:] + _dot(a_lo, b_hi)


def _pair_mask(ti, tj, lvl):
    same = (ti >> (lvl + 1)) == (tj >> (lvl + 1))
    return jnp.logical_and(same, jnp.logical_and(((ti >> lvl) & 1) == 1, ((tj >> lvl) & 1) == 0))


DELTA_CPI = 2


def _delta_kernel(act_s, gb_ref, szg_ref, alog_ref, dtb_ref, dnw_ref,
                  ob_ref, s_ref,
                  s_acc, o_s, u_s, wq_s, attn_s, kd_s, egl_s, *, nt):
    t = pl.program_id(1)
    c = DN_CHUNK
    heads = range(DN_HEADS)

    @pl.when(t == 0)
    def _():
        s_acc[...] = jnp.zeros(s_acc.shape, F32)

    ti = lax.broadcasted_iota(jnp.int32, (c, 2 * c), 0)
    tj = lax.broadcasted_iota(jnp.int32, (c, 2 * c), 1)
    left = tj < c
    incl = jnp.logical_and(ti >= tj, left)
    strict = jnp.logical_and(ti > tj, left)
    eye_l = (ti == tj).astype(F32)
    cum_l = incl[:, :c].astype(F32)
    cum_r = jnp.logical_and(ti <= tj, left).astype(F32)
    last = c - 1
    zpad = jnp.zeros((c, DN_DK), BF16)
    neg_a = -jnp.exp(alog_ref[...])
    dtb = dtb_ref[...]

    def chunk_rows(ref, j, lanes):
        return ref[pl.ds(pl.multiple_of(j * c, c), c), lanes]

    def prep(it, _):
        js = [it * DELTA_CPI + k for k in range(DELTA_CPI)]
        gcum, gcum_t, beta = [], [], []
        for j in js:
            gbv = chunk_rows(gb_ref, j, slice(None))
            g = neg_a * _softplus(gbv + dtb)
            beta.append(_sigmoid(gbv))
            gcum.append(jnp.dot(cum_l, g, precision=HI, preferred_element_type=F32))
            gcum_t.append(_dot_tn(g, cum_r, precision=HI))
        chains = [(k, h) for k in range(DELTA_CPI) for h in heads]
        gc, decay, qn, kn, kn16, kb, vb = {}, {}, {}, {}, {}, {}, {}
        for k, h in chains:
            j = js[k]
            gc[k, h] = gcum[k][:, h:h + 1]
            decay[k, h] = jnp.exp(jnp.where(incl, gc[k, h] - gcum_t[k][h:h + 1, :], -jnp.inf))
            bh = beta[k][:, DN_HEADS + h:DN_HEADS + h + 1]
            qh = chunk_rows(act_s, j, slice(h * DN_DK, (h + 1) * DN_DK))
            kh = chunk_rows(act_s, j, slice(DN_QK + h * DN_DK, DN_QK + (h + 1) * DN_DK))
            vh = chunk_rows(act_s, j, slice(2 * DN_QK + h * DN_DV, 2 * DN_QK + (h + 1) * DN_DV))
            qn[k, h] = qh * lax.rsqrt(jnp.sum(qh * qh, axis=-1, keepdims=True) + EPS) * (DN_DK ** -0.5)
            kn[k, h] = kh * lax.rsqrt(jnp.sum(kh * kh, axis=-1, keepdims=True) + EPS)
            kn16[k, h] = kn[k, h].astype(BF16)
            kb[k, h] = kn[k, h] * bh
            vb[k, h] = vh * bh
        lm = {}
        for ch in chains:
            kq = _dot_nt(jnp.concatenate([kb[ch], qn[ch]], axis=0).astype(BF16),
                         jnp.concatenate([kn16[ch], zpad], axis=0))
            lm[ch] = jnp.where(strict, kq[:c] * decay[ch], 0.0)
            attn_s[js[ch[0]], ch[1]] = (kq[c:] * decay[ch])[:, :c].astype(BF16)
        d = {ch: eye_l - jnp.where(_pair_mask(ti, tj, 0), lm[ch], 0.0) for ch in chains}
        for lvl in range(1, c.bit_length() - 1):
            pair = _pair_mask(ti, tj, lvl)
            d16 = {ch: d[ch].astype(BF16) for ch in chains}
            ed = {ch: _dot(jnp.where(pair, lm[ch], 0.0).astype(BF16)[:, :c], d16[ch]) for ch in chains}
            d = {ch: d[ch] - _dot(d16[ch][:, :c], ed[ch].astype(BF16)) for ch in chains}
        eg = {ch: jnp.exp(gc[ch]) for ch in chains}
        a16 = {ch: d[ch].astype(BF16)[:, :c] for ch in chains}
        rhs = {ch: jnp.concatenate([vb[ch], kb[ch] * eg[ch]], axis=1) for ch in chains}
        x0 = {ch: _dot(a16[ch], rhs[ch].astype(BF16)) for ch in chains}
        res = {}
        for ch in chains:
            l_hi, l_lo = _split(lm[ch])
            res[ch] = rhs[ch] - x0[ch] - _dot3(l_hi[:, :c], l_lo[:, :c], x0[ch])
        for ch in chains:
            k, h = ch
            x = x0[ch] + _dot(a16[ch], res[ch].astype(BF16))
            u_s[js[k], h] = x[:, :DN_DV]
            wq_s[js[k], h] = jnp.concatenate([x[:, DN_DV:], qn[ch] * eg[ch]], axis=0).astype(BF16)
            gl = gcum[k][last:last + 1, h:h + 1]
            kd_s[js[k], h] = (kn[ch] * jnp.exp(gl - gc[ch])).astype(BF16)
        for k, j in enumerate(js):
            egl_s[pl.ds(pl.multiple_of(j * SUBLANES, SUBLANES), SUBLANES), :] = jnp.broadcast_to(
                jnp.exp(gcum[k][last:last + 1, :]), (SUBLANES, LANES))
        return 0

    lax.fori_loop(0, NCH // DELTA_CPI, prep, 0)

    def recur(j, _):
        egl = egl_s[pl.ds(pl.multiple_of(j * SUBLANES, SUBLANES), 1), :]
        s_old = [s_acc[h] for h in heads]
        wq = [_dot(wq_s[j, h], s_old[h].astype(BF16)) for h in heads]
        vnew16 = [(u_s[j, h] - wq[h][:c]).astype(BF16) for h in heads]
        o = [wq[h][c:] + _dot(attn_s[j, h], vnew16[h]) for h in heads]
        for h in heads:
            s_acc[h] = s_old[h] * egl[:, h:h + 1] + _dot_tn(kd_s[j, h], vnew16[h])
        for h in heads:
            on = o[h] * lax.rsqrt(jnp.mean(o[h] * o[h], axis=-1, keepdims=True) + EPS) * dnw_ref[...]
            o_s[pl.ds(pl.multiple_of(j * c, c), c), h * DN_DV:(h + 1) * DN_DV] = on
        return 0

    lax.fori_loop(0, NCH, recur, 0)

    ob_ref[...] = (o_s[...] * szg_ref[...].astype(F32)).astype(BF16)

    @pl.when(t == nt - 1)
    def _():
        s_ref[0] = s_acc[...]


def _delta_call(act, gb, szg, alog, dtb, dnw, *, batch, seq_len):
    c = DN_CHUNK
    nt = seq_len // TP
    row = lambda n: pl.BlockSpec((TP, n), lambda b, t: (b * nt + t, 0))
    return pl.pallas_call(
        functools.partial(_delta_kernel, nt=nt),
        grid=(batch, nt),
        in_specs=[row(DN_QKV), row(LANES), row(DN_V),
                  _resident((1, LANES)), _resident((1, LANES)), _resident((1, DN_DV))],
        out_specs=[row(DN_V),
                   pl.BlockSpec((1, DN_HEADS, DN_DK, DN_DV), lambda b, t: (b, 0, 0, 0))],
        out_shape=[jax.ShapeDtypeStruct((batch * seq_len, DN_V), BF16),
                   jax.ShapeDtypeStruct((batch, DN_HEADS, DN_DK, DN_DV), F32)],
        scratch_shapes=[pltpu.VMEM((DN_HEADS, DN_DK, DN_DV), F32),
                        pltpu.VMEM((TP, DN_V), F32),
                        pltpu.VMEM((NCH, DN_HEADS, c, DN_DV), F32),
                        pltpu.VMEM((NCH, DN_HEADS, 2 * c, DN_DK), BF16),
                        pltpu.VMEM((NCH, DN_HEADS, c, c), BF16),
                        pltpu.VMEM((NCH, DN_HEADS, c, DN_DK), BF16),
                        pltpu.VMEM((NCH * SUBLANES, LANES), F32)],
        compiler_params=_params(("arbitrary", "arbitrary")),
        name="delta",
    )(act, gb, szg, alog, dtb, dnw)


def _sample_pre_kernel(xr_ref, ggr_ref, qkv_ref, gb_ref, crnn_ref, cqkv_ref, h0_ref,
                       cw_ref, cb_ref, wa_ref, ba_ref, wx_ref, bx_ref, lam_ref,
                       cqw_ref, alog_ref, dtb_ref,
                       oa_ref, hn_ref, qn_ref, kn_ref, v_ref, eg_ref, beta_ref):
    xc = cb_ref[...] + cw_ref[CONV_W - 1:CONV_W, :] * xr_ref[...]
    for j in range(CONV_W - 1):
        xc = xc + cw_ref[j:j + 1, :] * crnn_ref[j]
    a, log_a, i = _rg_gates(xc, wa_ref, ba_ref[...], wx_ref, bx_ref[...], lam_ref[...])
    mult = jnp.sqrt(_neg_expm1(2.0 * log_a))
    h = a * h0_ref[...] + mult * (i * xc)
    hn_ref[...] = h
    oa_ref[...] = (h * ggr_ref[...].astype(F32)).astype(BF16)

    qc = cqw_ref[CONV_W - 1:CONV_W, :] * qkv_ref[...]
    for j in range(CONV_W - 1):
        qc = qc + cqw_ref[j:j + 1, :] * cqkv_ref[j]
    act = _silu(qc)
    for h_ in range(DN_HEADS):
        qh = act[:, h_ * DN_DK:(h_ + 1) * DN_DK]
        kh = act[:, DN_QK + h_ * DN_DK:DN_QK + (h_ + 1) * DN_DK]
        qn_ref[:, h_ * DN_DK:(h_ + 1) * DN_DK] = (
            qh * lax.rsqrt(jnp.sum(qh * qh, axis=-1, keepdims=True) + EPS) * (DN_DK ** -0.5))
        kn_ref[:, h_ * DN_DK:(h_ + 1) * DN_DK] = (
            kh * lax.rsqrt(jnp.sum(kh * kh, axis=-1, keepdims=True) + EPS))
    v_ref[...] = act[:, 2 * DN_QK:]
    gbv = gb_ref[...]
    eg_ref[...] = jnp.exp(-jnp.exp(alog_ref[...]) * _softplus(gbv + dtb_ref[...]))
    beta_ref[...] = _sigmoid(gbv)


def _sample_pre_call(xr, ggr, qkv, gb, crnn, cqkv, h0, cw, cb, wa, ba, wx, bx, lam, cqw, alog, dtb):
    n = xr.shape[0]
    args = (xr, ggr, qkv, gb, crnn, cqkv, h0, cw, cb, wa, ba, wx, bx, lam, cqw, alog, dtb)
    shp = lambda w: jax.ShapeDtypeStruct((n, w), F32)
    outs = [jax.ShapeDtypeStruct((n, D_RNN), BF16), shp(D_RNN), shp(DN_QK), shp(DN_QK), shp(DN_V), shp(LANES), shp(LANES)]
    return pl.pallas_call(
        _sample_pre_kernel,
        grid=(1,),
        in_specs=[_resident(a.shape) for a in args],
        out_specs=[pl.BlockSpec(o.shape, lambda i: (0, 0)) for o in outs],
        out_shape=outs,
        compiler_params=_params(("arbitrary",)),
        name="sample_pre",
    )(*args)


def _sample_state_kernel(qn_ref, kn_ref, v_ref, eg_ref, beta_ref, szg_ref, dnw_ref, s_ref,
                         ob_ref, sn_ref):
    srow = lax.broadcasted_iota(jnp.int32, (SUBLANES, DN_DK), 0)
    rows = lambda r0, r1: jnp.where(srow == 0, r0, jnp.where(srow == 1, r1, 0.0))
    for h in range(DN_HEADS):
        sl = slice(h * DN_DK, (h + 1) * DN_DK)
        q = qn_ref[0, :, sl]
        k = kn_ref[0, :, sl]
        v = v_ref[0, :, sl]
        eg = eg_ref[0, :, h:h + 1]
        beta = beta_ref[0, :, DN_HEADS + h:DN_HEADS + h + 1]
        s_h = s_ref[0, h]
        ws_qs = _dot(rows((k * beta) * eg, q * eg).astype(BF16), s_h.astype(BF16))
        vnew = v * beta - ws_qs[0:1, :]
        o = ws_qs[1:2, :] + jnp.sum(q * k, axis=-1, keepdims=True) * vnew
        zero = jnp.zeros_like(k)
        sn_ref[0, h] = s_h * eg + _dot_tn(rows(k, zero).astype(BF16), rows(vnew, zero).astype(BF16))
        on = o * lax.rsqrt(jnp.mean(o * o, axis=-1, keepdims=True) + EPS) * dnw_ref[...]
        ob_ref[0, :, sl] = on * szg_ref[0, :, sl]


def _sample_state_call(qn, kn, v, eg, beta, szg, dnw, s0):
    n = qn.shape[0]
    r3 = lambda a: a.reshape(n, 1, a.shape[-1])
    vec = lambda w: pl.BlockSpec((1, 1, w), lambda b: (b, 0, 0))
    s_spec = pl.BlockSpec((1, DN_HEADS, DN_DK, DN_DV), lambda b: (b, 0, 0, 0))
    ob, sn = pl.pallas_call(
        _sample_state_kernel,
        grid=(n,),
        in_specs=[vec(DN_QK), vec(DN_QK), vec(DN_V), vec(LANES), vec(LANES), vec(DN_V),
                  _resident((1, DN_DV)), s_spec],
        out_specs=[vec(DN_V), s_spec],
        out_shape=[jax.ShapeDtypeStruct((n, 1, DN_V), F32),
                   jax.ShapeDtypeStruct((n, DN_HEADS, DN_DK, DN_DV), F32)],
        compiler_params=_params(("arbitrary",)),
        name="sample_state",
    )(r3(qn), r3(kn), r3(v), r3(eg), r3(beta), r3(szg), dnw, s0)
    return ob.reshape(n, DN_V), sn


def _merge_kernel(x_ref, oa_ref, ob_ref, sga_ref, sgb_ref, ada_ref, wb_ref, wo_ref, o_ref, *, per_row):
    ya = _dot(oa_ref[...], wb_ref[0])
    yb = _dot(ob_ref[...], wb_ref[1])
    merged = (sga_ref[...].astype(F32) * ya + sgb_ref[...].astype(F32) * yb).astype(BF16)
    o_ref[...] = x_ref[...] + _ada_rows(ada_ref, 5, per_row) * _dot(merged, wo_ref[...])


def _merge_call(x, oa, ob, sga, sgb, ada, wb, wo, *, per_row, tm, seq_len):
    m = x.shape[0]
    row = pl.BlockSpec((tm, D_MODEL), lambda i: (i, 0))
    return pl.pallas_call(
        functools.partial(_merge_kernel, per_row=per_row),
        grid=(m // tm,),
        in_specs=[row, row, row, row, row, _ada_spec(per_row, m, seq_len // tm),
                  _resident((2, D_RNN, D_MODEL)), _resident((D_MODEL, D_MODEL))],
        out_specs=row,
        out_shape=jax.ShapeDtypeStruct((m, D_MODEL), F32),
        compiler_params=_params(("arbitrary",)),
        name="merge",
    )(x, oa, ob, sga, sgb, ada, wb, wo)


def kernel(x_prompt, x_sample, c_prompt, c_sample, state_rglru_h, state_rglru_conv, state_delta_S, state_delta_conv, w_ada, b_ada, norm_ffn1, w_ffn1_up, w_ffn1_down, norm_mix, w_in, conv_rnn_w, conv_rnn_b, rg_w_a, rg_b_a, rg_w_x, rg_b_x, rg_lambda, conv_qkv_w, dn_a_log, dn_dt_bias, dn_norm, w_branch, w_out, norm_ffn2, w_ffn2_up, w_ffn2_down, norm_final):
    batch, seq_len, _ = x_prompt.shape
    n_dec = x_sample.shape[0]
    assert w_ada.shape[0] == 1 and x_sample.shape[1] == 1 and seq_len % TP == 0

    row = lambda a: a.reshape(1, -1).astype(F32)
    wup1, wdn1 = w_ffn1_up[0].astype(BF16), w_ffn1_down[0].astype(BF16)
    wup2, wdn2 = w_ffn2_up[0].astype(BF16), w_ffn2_down[0].astype(BF16)
    w_main = jnp.concatenate([w_in[0][:, :AB_OFF], w_in[0][:, AB_OFF + 2 * DN_HEADS:]], axis=1).astype(BF16)
    w_ab = jnp.pad(w_in[0][:, AB_OFF:AB_OFF + 2 * DN_HEADS], ((0, 0), (0, LANES - 2 * DN_HEADS))).astype(BF16)
    wb, wo = w_branch[0].astype(BF16), w_out[0].astype(BF16)
    wa, wx = rg_w_a[0].astype(BF16), rg_w_x[0].astype(BF16)
    lane_pad = lambda a: jnp.pad(a.reshape(1, -1).astype(F32), ((0, 0), (0, LANES - a.size)))
    alog = lane_pad(dn_a_log[0])
    dtb = lane_pad(dn_dt_bias[0])
    nf = row(norm_final)

    ada = _ada_call(jnp.concatenate([c_prompt, c_sample], axis=0), w_ada[0], row(b_ada[0]))
    ada = ada.reshape(batch + n_dec, N_ADA, D_MODEL)
    ada_p = ada[:batch]
    ada_s = jnp.transpose(ada[batch:], (1, 0, 2))

    mixer_w = (conv_rnn_w[0], row(conv_rnn_b[0]), wa, row(rg_b_a[0]), wx, row(rg_b_x[0]), row(rg_lambda[0]))

    kw = dict(per_row=False, seq_len=seq_len, tm=TP)
    xp = x_prompt.reshape(batch * seq_len, D_MODEL)
    xp = _ffn_call(xp, ada_p, row(norm_ffn1[0]), wup1, wdn1, nf, k0=0, final_norm=False, **kw)
    oa, act, szg, sga, sgb, gb, hp, cp, qp = _mixin_call(
        xp, ada_p, row(norm_mix[0]), w_main, w_ab, *mixer_w, conv_qkv_w[0], batch=batch, seq_len=seq_len)
    ob, sp = _delta_call(act, gb, szg, alog, dtb, row(dn_norm[0]), batch=batch, seq_len=seq_len)
    xp = _merge_call(xp, oa, ob, sga, sgb, ada_p, wb, wo, **kw)
    y_prompt = _ffn_call(xp, ada_p, row(norm_ffn2[0]), wup2, wdn2, nf, k0=6, final_norm=True, **kw)

    kw = dict(per_row=True, seq_len=n_dec, tm=n_dec)
    xs = x_sample.reshape(n_dec, D_MODEL)
    xs = _ffn_call(xs, ada_s, row(norm_ffn1[0]), wup1, wdn1, nf, k0=0, final_norm=False, **kw)
    xr_s, ggr_s, qkv_s, szg_s, sga_s, sgb_s, gb_s = _inproj_call(xs, ada_s, row(norm_mix[0]), w_main, w_ab, **kw)
    crnn = jnp.transpose(state_rglru_conv[0], (1, 0, 2))
    cqkv = jnp.transpose(state_delta_conv[0], (1, 0, 2))
    oa_s, hs, qn, kn, v, eg, beta = _sample_pre_call(
        xr_s, ggr_s, qkv_s, gb_s, crnn, cqkv, state_rglru_h[0], *mixer_w, conv_qkv_w[0], alog, dtb)
    ob_s, ss = _sample_state_call(qn, kn, v, eg, beta, szg_s.astype(F32), row(dn_norm[0]), state_delta_S[0])
    xs = _merge_call(xs, oa_s, ob_s.astype(BF16), sga_s, sgb_s, ada_s, wb, wo, **kw)
    y_sample = _ffn_call(xs, ada_s, row(norm_ffn2[0]), wup2, wdn2, nf, k0=6, final_norm=True, **kw)
    cs = jnp.concatenate([state_rglru_conv[0][:, 1:], xr_s[:, None, :]], axis=1)
    qs = jnp.concatenate([state_delta_conv[0][:, 1:], qkv_s[:, None, :]], axis=1)

    return (y_prompt.reshape(batch, seq_len, D_MODEL), y_sample.reshape(n_dec, 1, D_MODEL),
            hp.reshape(1, batch, D_RNN), cp[None], sp[None], qp[None],
            hs[None], cs[None], ss[None], qs[None])
```

```python
import functools

import jax
import jax.numpy as jnp
from jax import lax
from jax.experimental import pallas as pl
from jax.experimental.pallas import tpu as pltpu

D_MODEL = 1024
D_RNN = 1024
RG_BLOCKS = 8
RG_BLOCK_W = D_RNN // RG_BLOCKS
RG_C = 8.0
CONV_W = 4
DN_HEADS = 8
DN_DK = 128
DN_DV = 128
DN_QK = DN_HEADS * DN_DK
DN_V = DN_HEADS * DN_DV
DN_QKV = 2 * DN_QK + DN_V
DN_CHUNK = 64
D_FF = 2816
N_ADA = 9
EPS = 1e-6
LANES = 128
SUBLANES = 8
AB_OFF = D_RNN * 2 + DN_QKV
N_MAIN = 8 * D_MODEL

BF16 = jnp.bfloat16
F32 = jnp.float32
HI = lax.Precision.HIGHEST

VMEM_LIMIT = 56 * 1024 * 1024


def _params(sem):
    return pltpu.CompilerParams(dimension_semantics=sem, vmem_limit_bytes=VMEM_LIMIT)


def _resident(shape):
    nd = len(shape)
    return pl.BlockSpec(shape, lambda *_: (0,) * nd, pipeline_mode=pl.Buffered(1))


def _dot(a, b):
    return jnp.dot(a, b, preferred_element_type=F32)


def _dot_nt(a, b, precision=None):
    return lax.dot_general(a, b, (((1,), (1,)), ((), ())), precision=precision,
                           preferred_element_type=F32)


def _dot_tn(a, b, precision=None):
    return lax.dot_general(a, b, (((0,), (0,)), ((), ())), precision=precision,
                           preferred_element_type=F32)


def _sigmoid(x):
    return jax.nn.sigmoid(x)


def _silu(x):
    return x * jax.nn.sigmoid(x)


def _softplus(x):
    return jnp.maximum(x, 0.0) + jnp.log1p(jnp.exp(-jnp.abs(x)))


def _neg_expm1(x):
    return -jnp.tanh(0.5 * x) * (jnp.exp(x) + 1.0)


def _ada_rows(ada_ref, k, per_row):
    if per_row:
        return ada_ref[k]
    return ada_ref[k:k + 1, :]


def _norm_mod(x, nw, shift, scale):
    ms = jnp.mean(x * x, axis=-1, keepdims=True)
    hn = x * lax.rsqrt(ms + EPS) * nw
    return hn * (1.0 + scale) + shift


def _ada_kernel(c_ref, w_ref, b_ref, o_ref):
    o_ref[...] = _dot(c_ref[...].astype(BF16), w_ref[...].astype(BF16)) + b_ref[...]


def _ada_call(c_all, w_ada, b_ada):
    n = c_all.shape[0]
    tn = D_MODEL
    return pl.pallas_call(
        _ada_kernel,
        grid=(N_ADA * D_MODEL // tn,),
        in_specs=[pl.BlockSpec((n, D_MODEL), lambda j: (0, 0)),
                  pl.BlockSpec((D_MODEL, tn), lambda j: (0, j)),
                  pl.BlockSpec((1, tn), lambda j: (0, j))],
        out_specs=pl.BlockSpec((n, tn), lambda j: (0, j)),
        out_shape=jax.ShapeDtypeStruct((n, N_ADA * D_MODEL), F32),
        compiler_params=_params(("arbitrary",)),
        name="ada",
    )(c_all, w_ada, b_ada)


FFN_TF = 256


def _ffn_kernel(x_ref, ada_ref, nw_ref, wup_ref, wdn_ref, nf_ref, o_ref, *, k0, per_row, final_norm):
    x = x_ref[...]
    h = _norm_mod(x, nw_ref[...], _ada_rows(ada_ref, k0, per_row),
                  _ada_rows(ada_ref, k0 + 1, per_row)).astype(BF16)
    acc = jnp.zeros(x.shape, F32)
    for j in range(D_FF // FFN_TF):
        g = _dot(h, wup_ref[:, j * FFN_TF:(j + 1) * FFN_TF])
        v = _dot(h, wup_ref[:, D_FF + j * FFN_TF:D_FF + (j + 1) * FFN_TF])
        a = (_silu(g) * v).astype(BF16)
        acc = acc + _dot(a, wdn_ref[j * FFN_TF:(j + 1) * FFN_TF, :])
    y = x + 0.5 * _ada_rows(ada_ref, k0 + 2, per_row) * acc
    if final_norm:
        ms = jnp.mean(y * y, axis=-1, keepdims=True)
        y = y * lax.rsqrt(ms + EPS) * nf_ref[...]
    o_ref[...] = y


def _ada_spec(per_row, rows, tiles_per_seq):
    if per_row:
        return pl.BlockSpec((N_ADA, rows, D_MODEL), lambda i: (0, 0, 0))
    return pl.BlockSpec((None, N_ADA, D_MODEL), lambda i: (i // tiles_per_seq, 0, 0))


def _ffn_call(x, ada, nw, wup, wdn, nf, *, k0, per_row, final_norm, tm, seq_len):
    m = x.shape[0]
    kern = functools.partial(_ffn_kernel, k0=k0, per_row=per_row, final_norm=final_norm)
    return pl.pallas_call(
        kern,
        grid=(m // tm,),
        in_specs=[pl.BlockSpec((tm, D_MODEL), lambda i: (i, 0)),
                  _ada_spec(per_row, m, seq_len // tm),
                  _resident((1, D_MODEL)),
                  _resident((D_MODEL, 2 * D_FF)),
                  _resident((D_FF, D_MODEL)),
                  _resident((1, D_MODEL))],
        out_specs=pl.BlockSpec((tm, D_MODEL), lambda i: (i, 0)),
        out_shape=jax.ShapeDtypeStruct((m, D_MODEL), F32),
        compiler_params=_params(("arbitrary",)),
        name="ffn",
    )(x, ada, nw, wup, wdn, nf)


TP = 512
NCH = TP // DN_CHUNK


def _inproj_kernel(x_ref, ada_ref, nw_ref, w_ref, wab_ref,
                   xr_ref, ggr_ref, qkv_ref, szg_ref, sga_ref, sgb_ref, gb_ref, *, per_row):
    h = _norm_mod(x_ref[...], nw_ref[...], _ada_rows(ada_ref, 3, per_row),
                  _ada_rows(ada_ref, 4, per_row)).astype(BF16)

    def col(c):
        return _dot(h, w_ref[:, c * D_MODEL:(c + 1) * D_MODEL])

    xr_ref[...] = col(0)
    ggr_ref[...] = jax.nn.gelu(col(1)).astype(BF16)
    for c in range(3):
        qkv_ref[:, c * D_MODEL:(c + 1) * D_MODEL] = col(2 + c)
    szg_ref[...] = _silu(col(5)).astype(BF16)
    sga_ref[...] = _sigmoid(col(6)).astype(BF16)
    sgb_ref[...] = _sigmoid(col(7)).astype(BF16)
    gb_ref[...] = _dot(h, wab_ref[...])


def _inproj_call(x, ada, nw, w_main, w_ab, *, per_row, tm, seq_len):
    m = x.shape[0]
    row = lambda n: pl.BlockSpec((tm, n), lambda i: (i, 0))
    shp = lambda n, dt: jax.ShapeDtypeStruct((m, n), dt)
    return pl.pallas_call(
        functools.partial(_inproj_kernel, per_row=per_row),
        grid=(m // tm,),
        in_specs=[row(D_MODEL), _ada_spec(per_row, m, seq_len // tm),
                  _resident((1, D_MODEL)), _resident((D_MODEL, N_MAIN)), _resident((D_MODEL, LANES))],
        out_specs=[row(D_RNN), row(D_RNN), row(DN_QKV), row(DN_V), row(D_MODEL), row(D_MODEL), row(LANES)],
        out_shape=[shp(D_RNN, F32), shp(D_RNN, BF16), shp(DN_QKV, F32), shp(DN_V, BF16),
                   shp(D_MODEL, BF16), shp(D_MODEL, BF16), shp(LANES, F32)],
        compiler_params=_params(("arbitrary",)),
        name="inproj",
    )(x, ada, nw, w_main, w_ab)


def _causal_conv(x, carry, cw_ref, lanes):
    t, c = x.shape
    x3 = x.reshape(t // SUBLANES, SUBLANES, c)
    prev8 = carry[:, lanes]
    sub = lax.broadcasted_iota(jnp.int32, (1, SUBLANES, 1), 1)
    acc = cw_ref[CONV_W - 1:CONV_W, lanes][None] * x3
    for j in range(1, CONV_W):
        r = pltpu.roll(x3, j, axis=1)
        rp = jnp.concatenate([pltpu.roll(prev8, j, axis=0)[None], r[:-1]], axis=0)
        acc = acc + cw_ref[CONV_W - 1 - j:CONV_W - j, lanes][None] * jnp.where(sub >= j, r, rp)
    carry[:, lanes] = x[t - SUBLANES:, :]
    return acc.reshape(t, c)


def _rg_gates(xc, wa_ref, ba, wx_ref, bx, lam):
    ra, ix = [], []
    for n in range(RG_BLOCKS):
        xb = xc[:, n * RG_BLOCK_W:(n + 1) * RG_BLOCK_W].astype(BF16)
        ra.append(_dot(xb, wa_ref[n]))
        ix.append(_dot(xb, wx_ref[n]))
    r = _sigmoid(jnp.concatenate(ra, axis=1) + ba)
    i = _sigmoid(jnp.concatenate(ix, axis=1) + bx)
    log_a = (-RG_C) * r * _softplus(-lam)
    return jnp.exp(log_a), log_a, i


MIX_CONV_LANES = 512


def _mixin_kernel(x_ref, ada_ref, nw_ref, w_ref, wab_ref,
                  cw_ref, cb_ref, wa_ref, ba_ref, wx_ref, bx_ref, lam_ref, cqw_ref,
                  oa_ref, act_ref, szg_ref, sga_ref, sgb_ref, gb_ref, hl_ref, cs_ref, cq_ref,
                  xcar, qcar, hcar, acum_s, hloc_s, hin_s):
    t = pl.program_id(1)

    @pl.when(t == 0)
    def _():
        xcar[...] = jnp.zeros(xcar.shape, F32)
        qcar[...] = jnp.zeros(qcar.shape, F32)
        hcar[...] = jnp.zeros(hcar.shape, F32)

    h = _norm_mod(x_ref[...], nw_ref[...], ada_ref[3:4, :], ada_ref[4:5, :]).astype(BF16)

    def col(c):
        return _dot(h, w_ref[:, c * D_MODEL:(c + 1) * D_MODEL])

    groups = TP // SUBLANES
    sub = lax.broadcasted_iota(jnp.int32, (1, SUBLANES, 1), 1)
    row = lax.broadcasted_iota(jnp.int32, (TP, 1), 0)
    first = jnp.logical_and(row == 0, t == 0)

    def rglru_block(xr, n):
        ln = slice(n * RG_BLOCK_W, (n + 1) * RG_BLOCK_W)
        xc = _causal_conv(xr[:, ln], xcar, cw_ref, ln) + cb_ref[:, ln]
        xb = xc.astype(BF16)
        r = _sigmoid(_dot(xb, wa_ref[n]) + ba_ref[:, ln])
        i = _sigmoid(_dot(xb, wx_ref[n]) + bx_ref[:, ln])
        log_a = (-RG_C) * r * _softplus(-lam_ref[:, ln])
        a = jnp.exp(log_a)
        mult = jnp.where(first, 1.0, jnp.sqrt(_neg_expm1(2.0 * log_a)))
        a3 = a.reshape(groups, SUBLANES, RG_BLOCK_W)
        b3 = (mult * (i * xc)).reshape(groups, SUBLANES, RG_BLOCK_W)
        d = 1
        while d < SUBLANES:
            keep = sub >= d
            a_prev = jnp.where(keep, pltpu.roll(a3, d, axis=1), 1.0)
            b_prev = jnp.where(keep, pltpu.roll(b3, d, axis=1), 0.0)
            b3 = a3 * b_prev + b3
            a3 = a3 * a_prev
            d *= 2
        acum_s[:, ln] = a3.reshape(TP, RG_BLOCK_W)
        hloc_s[:, ln] = b3.reshape(TP, RG_BLOCK_W)

    def qkv_part(z, c):
        for l0 in range(0, D_MODEL, MIX_CONV_LANES):
            ln = slice(c * D_MODEL + l0, c * D_MODEL + l0 + MIX_CONV_LANES)
            act_ref[:, ln] = _silu(_causal_conv(z[:, l0:l0 + MIX_CONV_LANES], qcar, cqw_ref, ln))

    xr = col(0)
    for n in range(RG_BLOCKS):
        rglru_block(xr, n)
    cs_ref[0] = xcar[SUBLANES - (CONV_W - 1):, :]
    for c in range(DN_QKV // D_MODEL):
        qkv_part(col(2 + c), c)
    cq_ref[0] = qcar[SUBLANES - (CONV_W - 1):, :]

    hrow = hcar[0:1, :]
    for g in range(groups):
        hin_s[g * SUBLANES:(g + 1) * SUBLANES, :] = jnp.broadcast_to(hrow, (SUBLANES, D_RNN))
        e = (g + 1) * SUBLANES - 1
        hrow = acum_s[e:e + 1, :] * hrow + hloc_s[e:e + 1, :]
    hcar[0:1, :] = hrow
    hl_ref[0] = hrow

    y = hloc_s[...] + acum_s[...] * hin_s[...]
    oa_ref[...] = (y * jax.nn.gelu(col(1))).astype(BF16)
    szg_ref[...] = _silu(col(5)).astype(BF16)
    sga_ref[...] = _sigmoid(col(6)).astype(BF16)
    sgb_ref[...] = _sigmoid(col(7)).astype(BF16)
    gb_ref[...] = _dot(h, wab_ref[...])


def _mixin_call(x, ada, nw, w_main, w_ab, cw, cb, wa, ba, wx, bx, lam, cqw, *, batch, seq_len):
    nt = seq_len // TP
    m = batch * seq_len
    row = lambda n: pl.BlockSpec((TP, n), lambda b, t: (b * nt + t, 0))
    per_seq = lambda r, n: pl.BlockSpec((1, r, n), lambda b, t: (b, 0, 0))
    shp = lambda n, dt: jax.ShapeDtypeStruct((m, n), dt)
    blk = (RG_BLOCKS, RG_BLOCK_W, RG_BLOCK_W)
    return pl.pallas_call(
        _mixin_kernel,
        grid=(batch, nt),
        in_specs=[row(D_MODEL), pl.BlockSpec((None, N_ADA, D_MODEL), lambda b, t: (b, 0, 0)),
                  _resident((1, D_MODEL)), _resident((D_MODEL, N_MAIN)), _resident((D_MODEL, LANES)),
                  _resident((CONV_W, D_RNN)), _resident((1, D_RNN)), _resident(blk), _resident((1, D_RNN)),
                  _resident(blk), _resident((1, D_RNN)), _resident((1, D_RNN)), _resident((CONV_W, DN_QKV))],
        out_specs=[row(D_RNN), row(DN_QKV), row(DN_V), row(D_MODEL), row(D_MODEL), row(LANES),
                   per_seq(1, D_RNN), per_seq(CONV_W - 1, D_RNN), per_seq(CONV_W - 1, DN_QKV)],
        out_shape=[shp(D_RNN, BF16), shp(DN_QKV, F32), shp(DN_V, BF16), shp(D_MODEL, BF16), shp(D_MODEL, BF16),
                   shp(LANES, F32),
                   jax.ShapeDtypeStruct((batch, 1, D_RNN), F32),
                   jax.ShapeDtypeStruct((batch, CONV_W - 1, D_RNN), F32),
                   jax.ShapeDtypeStruct((batch, CONV_W - 1, DN_QKV), F32)],
        scratch_shapes=[pltpu.VMEM((SUBLANES, D_RNN), F32),
                        pltpu.VMEM((SUBLANES, DN_QKV), F32),
                        pltpu.VMEM((SUBLANES, D_RNN), F32),
                        pltpu.VMEM((TP, D_RNN), F32),
                        pltpu.VMEM((TP, D_RNN), F32),
                        pltpu.VMEM((TP, D_RNN), F32)],
        compiler_params=_params(("arbitrary", "arbitrary")),
        name="mixin",
    )(x, ada, nw, w_main, w_ab, cw, cb, wa, ba, wx, bx, lam, cqw)


def _split(x):
    hi = x.astype(BF16)
    return hi, (x - hi.astype(F32)).astype(BF16)


def _dot3(a_hi, a_lo, b):
    n = b.shape[1]
    b_hi, b_lo = _split(b)
    r = _dot(a_hi, jnp.concatenate([b_hi, b_lo], axis=1))
    return r[:, :n] + r[:, n:] + _dot(a_lo, b_hi)


def _pair_mask(ti, tj, lvl):
    same = (ti >> (lvl + 1)) == (tj >> (lvl + 1))
    return jnp.logical_and(same, jnp.logical_and(((ti >> lvl) & 1) == 1, ((tj >> lvl) & 1) == 0))


DELTA_CPI = 4


def _delta_kernel(act_s, gb_ref, szg_ref, alog_ref, dtb_ref, dnw_ref,
                  ob_ref, s_ref,
                  s_acc, o_s, u_s, wq_s, attn_s, kd_s, egl_s, *, nt):
    t = pl.program_id(1)
    c = DN_CHUNK
    heads = range(DN_HEADS)

    @pl.when(t == 0)
    def _():
        s_acc[...] = jnp.zeros(s_acc.shape, F32)

    ti = lax.broadcasted_iota(jnp.int32, (c, 2 * c), 0)
    tj = lax.broadcasted_iota(jnp.int32, (c, 2 * c), 1)
    left = tj < c
    incl = jnp.logical_and(ti >= tj, left)
    strict = jnp.logical_and(ti > tj, left)
    eye_l = (ti == tj).astype(F32)
    cum_l = incl[:, :c].astype(F32)
    cum_r = jnp.logical_and(ti <= tj, left).astype(F32)
    last = c - 1
    zpad = jnp.zeros((c, DN_DK), BF16)
    neg_a = -jnp.exp(alog_ref[...])
    dtb = dtb_ref[...]

    def chunk_rows(ref, j, lanes):
        return ref[pl.ds(pl.multiple_of(j * c, c), c), lanes]

    def prep(it, _):
        js = [it * DELTA_CPI + k for k in range(DELTA_CPI)]
        gcum, gcum_t, beta = [], [], []
        for j in js:
            gbv = chunk_rows(gb_ref, j, slice(None))
            g = neg_a * _softplus(gbv + dtb)
            beta.append(_sigmoid(gbv))
            gcum.append(jnp.dot(cum_l, g, precision=HI, preferred_element_type=F32))
            gcum_t.append(_dot_tn(g, cum_r, precision=HI))
        chains = [(k, h) for k in range(DELTA_CPI) for h in heads]
        gc, decay, qn, kn, kn16, kb, vb = {}, {}, {}, {}, {}, {}, {}
        for k, h in chains:
            j = js[k]
            gc[k, h] = gcum[k][:, h:h + 1]
            decay[k, h] = jnp.exp(jnp.where(incl, gc[k, h] - gcum_t[k][h:h + 1, :], -jnp.inf))
            bh = beta[k][:, DN_HEADS + h:DN_HEADS + h + 1]
            qh = chunk_rows(act_s, j, slice(h * DN_DK, (h + 1) * DN_DK))
            kh = chunk_rows(act_s, j, slice(DN_QK + h * DN_DK, DN_QK + (h + 1) * DN_DK))
            vh = chunk_rows(act_s, j, slice(2 * DN_QK + h * DN_DV, 2 * DN_QK + (h + 1) * DN_DV))
            qn[k, h] = qh * lax.rsqrt(jnp.sum(qh * qh, axis=-1, keepdims=True) + EPS) * (DN_DK ** -0.5)
            kn[k, h] = kh * lax.rsqrt(jnp.sum(kh * kh, axis=-1, keepdims=True) + EPS)
            kn16[k, h] = kn[k, h].astype(BF16)
            kb[k, h] = kn[k, h] * bh
            vb[k, h] = vh * bh
        lm = {}
        for ch in chains:
            kq = _dot_nt(jnp.concatenate([kb[ch], qn[ch]], axis=0).astype(BF16),
                         jnp.concatenate([kn16[ch], zpad], axis=0))
            lm[ch] = jnp.where(strict, kq[:c] * decay[ch], 0.0)
            attn_s[js[ch[0]], ch[1]] = (kq[c:] * decay[ch])[:, :c].astype(BF16)
        d = {ch: eye_l - jnp.where(_pair_mask(ti, tj, 0), lm[ch], 0.0) for ch in chains}
        for lvl in range(1, c.bit_length() - 1):
            pair = _pair_mask(ti, tj, lvl)
            d16 = {ch: d[ch].astype(BF16) for ch in chains}
            ed = {ch: _dot(jnp.where(pair, lm[ch], 0.0).astype(BF16)[:, :c], d16[ch]) for ch in chains}
            d = {ch: d[ch] - _dot(d16[ch][:, :c], ed[ch].astype(BF16)) for ch in chains}
        eg = {ch: jnp.exp(gc[ch]) for ch in chains}
        a16 = {ch: d[ch].astype(BF16)[:, :c] for ch in chains}
        rhs = {ch: jnp.concatenate([vb[ch], kb[ch] * eg[ch]], axis=1) for ch in chains}
        x0 = {ch: _dot(a16[ch], rhs[ch].astype(BF16)) for ch in chains}
        res = {}
        for ch in chains:
            l_hi, l_lo = _split(lm[ch])
            res[ch] = rhs[ch] - x0[ch] - _dot3(l_hi[:, :c], l_lo[:, :c], x0[ch])
        for ch in chains:
            k, h = ch
            x = x0[ch] + _dot(a16[ch], res[ch].astype(BF16))
            u_s[js[k], h] = x[:, :DN_DV]
            wq_s[js[k], h] = jnp.concatenate([x[:, DN_DV:], qn[ch] * eg[ch]], axis=0).astype(BF16)
            gl = gcum[k][last:last + 1, h:h + 1]
            kd_s[js[k], h] = (kn[ch] * jnp.exp(gl - gc[ch])).astype(BF16)
        for k, j in enumerate(js):
            egl_s[pl.ds(pl.multiple_of(j * SUBLANES, SUBLANES), SUBLANES), :] = jnp.broadcast_to(
                jnp.exp(gcum[k][last:last + 1, :]), (SUBLANES, LANES))
        return 0

    lax.fori_loop(0, NCH // DELTA_CPI, prep, 0)

    def recur(j, _):
        egl = egl_s[pl.ds(pl.multiple_of(j * SUBLANES, SUBLANES), 1), :]
        s_old = [s_acc[h] for h in heads]
        wq = [_dot(wq_s[j, h], s_old[h].astype(BF16)) for h in heads]
        vnew16 = [(u_s[j, h] - wq[h][:c]).astype(BF16) for h in heads]
        o = [wq[h][c:] + _dot(attn_s[j, h], vnew16[h]) for h in heads]
        for h in heads:
            s_acc[h] = s_old[h] * egl[:, h:h + 1] + _dot_tn(kd_s[j, h], vnew16[h])
        for h in heads:
            on = o[h] * lax.rsqrt(jnp.mean(o[h] * o[h], axis=-1, keepdims=True) + EPS) * dnw_ref[...]
            o_s[pl.ds(pl.multiple_of(j * c, c), c), h * DN_DV:(h + 1) * DN_DV] = on
        return 0

    lax.fori_loop(0, NCH, recur, 0)

    ob_ref[...] = (o_s[...] * szg_ref[...].astype(F32)).astype(BF16)

    @pl.when(t == nt - 1)
    def _():
        s_ref[0] = s_acc[...]


def _delta_call(act, gb, szg, alog, dtb, dnw, *, batch, seq_len):
    c = DN_CHUNK
    nt = seq_len // TP
    row = lambda n: pl.BlockSpec((TP, n), lambda b, t: (b * nt + t, 0))
    return pl.pallas_call(
        functools.partial(_delta_kernel, nt=nt),
        grid=(batch, nt),
        in_specs=[row(DN_QKV), row(LANES), row(DN_V),
                  _resident((1, LANES)), _resident((1, LANES)), _resident((1, DN_DV))],
        out_specs=[row(DN_V),
                   pl.BlockSpec((1, DN_HEADS, DN_DK, DN_DV), lambda b, t: (b, 0, 0, 0))],
        out_shape=[jax.ShapeDtypeStruct((batch * seq_len, DN_V), BF16),
                   jax.ShapeDtypeStruct((batch, DN_HEADS, DN_DK, DN_DV), F32)],
        scratch_shapes=[pltpu.VMEM((DN_HEADS, DN_DK, DN_DV), F32),
                        pltpu.VMEM((TP, DN_V), F32),
                        pltpu.VMEM((NCH, DN_HEADS, c, DN_DV), F32),
                        pltpu.VMEM((NCH, DN_HEADS, 2 * c, DN_DK), BF16),
                        pltpu.VMEM((NCH, DN_HEADS, c, c), BF16),
                        pltpu.VMEM((NCH, DN_HEADS, c, DN_DK), BF16),
                        pltpu.VMEM((NCH * SUBLANES, LANES), F32)],
        compiler_params=_params(("arbitrary", "arbitrary")),
        name="delta",
    )(act, gb, szg, alog, dtb, dnw)


def _sample_pre_kernel(xr_ref, ggr_ref, qkv_ref, gb_ref, crnn_ref, cqkv_ref, h0_ref,
                       cw_ref, cb_ref, wa_ref, ba_ref, wx_ref, bx_ref, lam_ref,
                       cqw_ref, alog_ref, dtb_ref,
                       oa_ref, hn_ref, qn_ref, kn_ref, v_ref, eg_ref, beta_ref):
    xc = cb_ref[...] + cw_ref[CONV_W - 1:CONV_W, :] * xr_ref[...]
    for j in range(CONV_W - 1):
        xc = xc + cw_ref[j:j + 1, :] * crnn_ref[j]
    a, log_a, i = _rg_gates(xc, wa_ref, ba_ref[...], wx_ref, bx_ref[...], lam_ref[...])
    mult = jnp.sqrt(_neg_expm1(2.0 * log_a))
    h = a * h0_ref[...] + mult * (i * xc)
    hn_ref[...] = h
    oa_ref[...] = (h * ggr_ref[...].astype(F32)).astype(BF16)

    qc = cqw_ref[CONV_W - 1:CONV_W, :] * qkv_ref[...]
    for j in range(CONV_W - 1):
        qc = qc + cqw_ref[j:j + 1, :] * cqkv_ref[j]
    act = _silu(qc)
    for h_ in range(DN_HEADS):
        qh = act[:, h_ * DN_DK:(h_ + 1) * DN_DK]
        kh = act[:, DN_QK + h_ * DN_DK:DN_QK + (h_ + 1) * DN_DK]
        qn_ref[:, h_ * DN_DK:(h_ + 1) * DN_DK] = (
            qh * lax.rsqrt(jnp.sum(qh * qh, axis=-1, keepdims=True) + EPS) * (DN_DK ** -0.5))
        kn_ref[:, h_ * DN_DK:(h_ + 1) * DN_DK] = (
            kh * lax.rsqrt(jnp.sum(kh * kh, axis=-1, keepdims=True) + EPS))
    v_ref[...] = act[:, 2 * DN_QK:]
    gbv = gb_ref[...]
    eg_ref[...] = jnp.exp(-jnp.exp(alog_ref[...]) * _softplus(gbv + dtb_ref[...]))
    beta_ref[...] = _sigmoid(gbv)


def _sample_pre_call(xr, ggr, qkv, gb, crnn, cqkv, h0, cw, cb, wa, ba, wx, bx, lam, cqw, alog, dtb):
    n = xr.shape[0]
    args = (xr, ggr, qkv, gb, crnn, cqkv, h0, cw, cb, wa, ba, wx, bx, lam, cqw, alog, dtb)
    shp = lambda w: jax.ShapeDtypeStruct((n, w), F32)
    outs = [jax.ShapeDtypeStruct((n, D_RNN), BF16), shp(D_RNN), shp(DN_QK), shp(DN_QK), shp(DN_V), shp(LANES), shp(LANES)]
    return pl.pallas_call(
        _sample_pre_kernel,
        grid=(1,),
        in_specs=[_resident(a.shape) for a in args],
        out_specs=[pl.BlockSpec(o.shape, lambda i: (0, 0)) for o in outs],
        out_shape=outs,
        compiler_params=_params(("arbitrary",)),
        name="sample_pre",
    )(*args)


def _sample_state_kernel(qn_ref, kn_ref, v_ref, eg_ref, beta_ref, szg_ref, dnw_ref, s_ref,
                         ob_ref, sn_ref):
    srow = lax.broadcasted_iota(jnp.int32, (SUBLANES, DN_DK), 0)
    rows = lambda r0, r1: jnp.where(srow == 0, r0, jnp.where(srow == 1, r1, 0.0))
    chains = [(b, h) for b in range(SAMPLE_BB) for h in range(DN_HEADS)]
    sl = lambda h: slice(h * DN_DK, (h + 1) * DN_DK)
    q = {(b, h): qn_ref[0, b:b + 1, sl(h)] for b, h in chains}
    k = {(b, h): kn_ref[0, b:b + 1, sl(h)] for b, h in chains}
    eg = {(b, h): eg_ref[0, b:b + 1, h:h + 1] for b, h in chains}
    beta = {(b, h): beta_ref[0, b:b + 1, DN_HEADS + h:DN_HEADS + h + 1] for b, h in chains}
    ws_qs = {ch: _dot(rows((k[ch] * beta[ch]) * eg[ch], q[ch] * eg[ch]).astype(BF16),
                      s_ref[ch[0], ch[1]].astype(BF16)) for ch in chains}
    vnew = {(b, h): v_ref[0, b:b + 1, sl(h)] * beta[b, h] - ws_qs[b, h][0:1, :] for b, h in chains}
    for ch in chains:
        b, h = ch
        zero = jnp.zeros_like(k[ch])
        sn_ref[b, h] = s_ref[b, h] * eg[ch] + _dot_tn(rows(k[ch], zero).astype(BF16),
                                                     rows(vnew[ch], zero).astype(BF16))
    for ch in chains:
        b, h = ch
        o = ws_qs[ch][1:2, :] + jnp.sum(q[ch] * k[ch], axis=-1, keepdims=True) * vnew[ch]
        on = o * lax.rsqrt(jnp.mean(o * o, axis=-1, keepdims=True) + EPS) * dnw_ref[...]
        ob_ref[0, b:b + 1, sl(h)] = on * szg_ref[0, b:b + 1, sl(h)]


SAMPLE_BB = 4


def _sample_state_call(qn, kn, v, eg, beta, szg, dnw, s0):
    n = qn.shape[0]
    bb = SAMPLE_BB
    r3 = lambda a: a.reshape(n // bb, bb, a.shape[-1])
    vec = lambda w: pl.BlockSpec((1, bb, w), lambda i: (i, 0, 0))
    s_spec = pl.BlockSpec((bb, DN_HEADS, DN_DK, DN_DV), lambda i: (i, 0, 0, 0))
    ob, sn = pl.pallas_call(
        _sample_state_kernel,
        grid=(n // bb,),
        in_specs=[vec(DN_QK), vec(DN_QK), vec(DN_V), vec(LANES), vec(LANES), vec(DN_V),
                  _resident((1, DN_DV)), s_spec],
        out_specs=[vec(DN_V), s_spec],
        out_shape=[jax.ShapeDtypeStruct((n // bb, bb, DN_V), F32),
                   jax.ShapeDtypeStruct((n, DN_HEADS, DN_DK, DN_DV), F32)],
        compiler_params=_params(("arbitrary",)),
        name="sample_state",
    )(r3(qn), r3(kn), r3(v), r3(eg), r3(beta), r3(szg), dnw, s0)
    return ob.reshape(n, DN_V), sn


def _merge_kernel(x_ref, oa_ref, ob_ref, sga_ref, sgb_ref, ada_ref, wb_ref, wo_ref, o_ref, *, per_row):
    ya = _dot(oa_ref[...], wb_ref[0])
    yb = _dot(ob_ref[...], wb_ref[1])
    merged = (sga_ref[...].astype(F32) * ya + sgb_ref[...].astype(F32) * yb).astype(BF16)
    o_ref[...] = x_ref[...] + _ada_rows(ada_ref, 5, per_row) * _dot(merged, wo_ref[...])


def _merge_call(x, oa, ob, sga, sgb, ada, wb, wo, *, per_row, tm, seq_len):
    m = x.shape[0]
    row = pl.BlockSpec((tm, D_MODEL), lambda i: (i, 0))
    return pl.pallas_call(
        functools.partial(_merge_kernel, per_row=per_row),
        grid=(m // tm,),
        in_specs=[row, row, row, row, row, _ada_spec(per_row, m, seq_len // tm),
                  _resident((2, D_RNN, D_MODEL)), _resident((D_MODEL, D_MODEL))],
        out_specs=row,
        out_shape=jax.ShapeDtypeStruct((m, D_MODEL), F32),
        compiler_params=_params(("arbitrary",)),
        name="merge",
    )(x, oa, ob, sga, sgb, ada, wb, wo)


def kernel(x_prompt, x_sample, c_prompt, c_sample, state_rglru_h, state_rglru_conv, state_delta_S, state_delta_conv, w_ada, b_ada, norm_ffn1, w_ffn1_up, w_ffn1_down, norm_mix, w_in, conv_rnn_w, conv_rnn_b, rg_w_a, rg_b_a, rg_w_x, rg_b_x, rg_lambda, conv_qkv_w, dn_a_log, dn_dt_bias, dn_norm, w_branch, w_out, norm_ffn2, w_ffn2_up, w_ffn2_down, norm_final):
    batch, seq_len, _ = x_prompt.shape
    n_dec = x_sample.shape[0]
    assert w_ada.shape[0] == 1 and x_sample.shape[1] == 1 and seq_len % TP == 0

    row = lambda a: a.reshape(1, -1).astype(F32)
    wup1, wdn1 = w_ffn1_up[0].astype(BF16), w_ffn1_down[0].astype(BF16)
    wup2, wdn2 = w_ffn2_up[0].astype(BF16), w_ffn2_down[0].astype(BF16)
    w_main = jnp.concatenate([w_in[0][:, :AB_OFF], w_in[0][:, AB_OFF + 2 * DN_HEADS:]], axis=1).astype(BF16)
    w_ab = jnp.pad(w_in[0][:, AB_OFF:AB_OFF + 2 * DN_HEADS], ((0, 0), (0, LANES - 2 * DN_HEADS))).astype(BF16)
    wb, wo = w_branch[0].astype(BF16), w_out[0].astype(BF16)
    wa, wx = rg_w_a[0].astype(BF16), rg_w_x[0].astype(BF16)
    lane_pad = lambda a: jnp.pad(a.reshape(1, -1).astype(F32), ((0, 0), (0, LANES - a.size)))
    alog = lane_pad(dn_a_log[0])
    dtb = lane_pad(dn_dt_bias[0])
    nf = row(norm_final)

    ada = _ada_call(jnp.concatenate([c_prompt, c_sample], axis=0), w_ada[0], row(b_ada[0]))
    ada = ada.reshape(batch + n_dec, N_ADA, D_MODEL)
    ada_p = ada[:batch]
    ada_s = jnp.transpose(ada[batch:], (1, 0, 2))

    mixer_w = (conv_rnn_w[0], row(conv_rnn_b[0]), wa, row(rg_b_a[0]), wx, row(rg_b_x[0]), row(rg_lambda[0]))

    kw = dict(per_row=False, seq_len=seq_len, tm=TP)
    xp = x_prompt.reshape(batch * seq_len, D_MODEL)
    xp = _ffn_call(xp, ada_p, row(norm_ffn1[0]), wup1, wdn1, nf, k0=0, final_norm=False, **kw)
    oa, act, szg, sga, sgb, gb, hp, cp, qp = _mixin_call(
        xp, ada_p, row(norm_mix[0]), w_main, w_ab, *mixer_w, conv_qkv_w[0], batch=batch, seq_len=seq_len)
    ob, sp = _delta_call(act, gb, szg, alog, dtb, row(dn_norm[0]), batch=batch, seq_len=seq_len)
    xp = _merge_call(xp, oa, ob, sga, sgb, ada_p, wb, wo, **kw)
    y_prompt = _ffn_call(xp, ada_p, row(norm_ffn2[0]), wup2, wdn2, nf, k0=6, final_norm=True, **kw)

    kw = dict(per_row=True, seq_len=n_dec, tm=n_dec)
    xs = x_sample.reshape(n_dec, D_MODEL)
    xs = _ffn_call(xs, ada_s, row(norm_ffn1[0]), wup1, wdn1, nf, k0=0, final_norm=False, **kw)
    xr_s, ggr_s, qkv_s, szg_s, sga_s, sgb_s, gb_s = _inproj_call(xs, ada_s, row(norm_mix[0]), w_main, w_ab, **kw)
    crnn = jnp.transpose(state_rglru_conv[0], (1, 0, 2))
    cqkv = jnp.transpose(state_delta_conv[0], (1, 0, 2))
    oa_s, hs, qn, kn, v, eg, beta = _sample_pre_call(
        xr_s, ggr_s, qkv_s, gb_s, crnn, cqkv, state_rglru_h[0], *mixer_w, conv_qkv_w[0], alog, dtb)
    ob_s, ss = _sample_state_call(qn, kn, v, eg, beta, szg_s.astype(F32), row(dn_norm[0]), state_delta_S[0])
    xs = _merge_call(xs, oa_s, ob_s.astype(BF16), sga_s, sgb_s, ada_s, wb, wo, **kw)
    y_sample = _ffn_call(xs, ada_s, row(norm_ffn2[0]), wup2, wdn2, nf, k0=6, final_norm=True, **kw)
    cs = jnp.concatenate([state_rglru_conv[0][:, 1:], xr_s[:, None, :]], axis=1)
    qs = jnp.concatenate([state_delta_conv[0][:, 1:], qkv_s[:, None, :]], axis=1)

    return (y_prompt.reshape(batch, seq_len, D_MODEL), y_sample.reshape(n_dec, 1, D_MODEL),
            hp.reshape(1, batch, D_RNN), cp[None], sp[None], qp[None],
            hs[None], cs[None], ss[None], qs[None])
```

```python
import functools

import jax
import jax.numpy as jnp
from jax import lax
from jax.experimental import pallas as pl
from jax.experimental.pallas import tpu as pltpu

D_MODEL = 1024
D_RNN = 1024
RG_BLOCKS = 8
RG_BLOCK_W = D_RNN // RG_BLOCKS
RG_C = 8.0
CONV_W = 4
DN_HEADS = 8
DN_DK = 128
DN_DV = 128
DN_QK = DN_HEADS * DN_DK
DN_V = DN_HEADS * DN_DV
DN_QKV = 2 * DN_QK + DN_V
DN_CHUNK = 64
D_FF = 2816
N_ADA = 9
EPS = 1e-6
LANES = 128
SUBLANES = 8
AB_OFF = D_RNN * 2 + DN_QKV
N_HI = 3 * D_MODEL

BF16 = jnp.bfloat16
F32 = jnp.float32
HI = lax.Precision.HIGHEST

VMEM_LIMIT = 56 * 1024 * 1024


def _params(sem):
    return pltpu.CompilerParams(dimension_semantics=sem, vmem_limit_bytes=VMEM_LIMIT)


def _resident(shape):
    nd = len(shape)
    return pl.BlockSpec(shape, lambda *_: (0,) * nd, pipeline_mode=pl.Buffered(1))


def _dot(a, b):
    return jnp.dot(a, b, preferred_element_type=F32)


def _dot_nt(a, b, precision=None):
    return lax.dot_general(a, b, (((1,), (1,)), ((), ())), precision=precision,
                           preferred_element_type=F32)


def _dot_tn(a, b, precision=None):
    return lax.dot_general(a, b, (((0,), (0,)), ((), ())), precision=precision,
                           preferred_element_type=F32)


def _sigmoid(x):
    return 0.5 * jnp.tanh(0.5 * x) + 0.5


def _silu(x):
    h = 0.5 * x
    return h + h * jnp.tanh(h)


def _softplus(x):
    return jnp.maximum(x, 0.0) + jnp.log1p(jnp.exp(-jnp.abs(x)))


def _neg_expm1(x):
    return -jnp.tanh(0.5 * x) * (jnp.exp(x) + 1.0)


def _ada_rows(ada_ref, k, per_row):
    if per_row:
        return ada_ref[k]
    return ada_ref[k:k + 1, :]


def _norm_mod(x, nw, shift, scale):
    ms = jnp.mean(x * x, axis=-1, keepdims=True)
    hn = x * lax.rsqrt(ms + EPS) * nw
    return hn * (1.0 + scale) + shift


def _ada_kernel(c_ref, w_ref, b_ref, o_ref):
    o_ref[...] = _dot(c_ref[...].astype(BF16), w_ref[...].astype(BF16)) + b_ref[...]


def _ada_call(c_all, w_ada, b_ada):
    n = c_all.shape[0]
    tn = D_MODEL
    return pl.pallas_call(
        _ada_kernel,
        grid=(N_ADA * D_MODEL // tn,),
        in_specs=[pl.BlockSpec((n, D_MODEL), lambda j: (0, 0)),
                  pl.BlockSpec((D_MODEL, tn), lambda j: (0, j)),
                  pl.BlockSpec((1, tn), lambda j: (0, j))],
        out_specs=pl.BlockSpec((n, tn), lambda j: (0, j)),
        out_shape=jax.ShapeDtypeStruct((n, N_ADA * D_MODEL), F32),
        compiler_params=_params(("arbitrary",)),
        name="ada",
    )(c_all, w_ada, b_ada)


FFN_TF = 256


def _ffn_kernel(x_ref, ada_ref, nw_ref, wup_ref, wdn_ref, nf_ref, o_ref, *, k0, per_row, final_norm):
    x = x_ref[...]
    h = _norm_mod(x, nw_ref[...], _ada_rows(ada_ref, k0, per_row),
                  _ada_rows(ada_ref, k0 + 1, per_row)).astype(BF16)
    acc = jnp.zeros(x.shape, F32)
    for j in range(D_FF // FFN_TF):
        g = _dot(h, wup_ref[:, j * FFN_TF:(j + 1) * FFN_TF])
        v = _dot(h, wup_ref[:, D_FF + j * FFN_TF:D_FF + (j + 1) * FFN_TF])
        a = (_silu(g) * v).astype(BF16)
        acc = acc + _dot(a, wdn_ref[j * FFN_TF:(j + 1) * FFN_TF, :])
    y = x + 0.5 * _ada_rows(ada_ref, k0 + 2, per_row) * acc
    if final_norm:
        ms = jnp.mean(y * y, axis=-1, keepdims=True)
        y = y * lax.rsqrt(ms + EPS) * nf_ref[...]
    o_ref[...] = y


def _ada_spec(per_row, rows, tiles_per_seq):
    if per_row:
        return pl.BlockSpec((N_ADA, rows, D_MODEL), lambda i: (0, 0, 0))
    return pl.BlockSpec((None, N_ADA, D_MODEL), lambda i: (i // tiles_per_seq, 0, 0))


def _ffn_call(x, ada, nw, wup, wdn, nf, *, k0, per_row, final_norm, tm, seq_len):
    m = x.shape[0]
    kern = functools.partial(_ffn_kernel, k0=k0, per_row=per_row, final_norm=final_norm)
    return pl.pallas_call(
        kern,
        grid=(m // tm,),
        in_specs=[pl.BlockSpec((tm, D_MODEL), lambda i: (i, 0)),
                  _ada_spec(per_row, m, seq_len // tm),
                  _resident((1, D_MODEL)),
                  _resident((D_MODEL, 2 * D_FF)),
                  _resident((D_FF, D_MODEL)),
                  _resident((1, D_MODEL))],
        out_specs=pl.BlockSpec((tm, D_MODEL), lambda i: (i, 0)),
        out_shape=jax.ShapeDtypeStruct((m, D_MODEL), F32),
        compiler_params=_params(("arbitrary",)),
        name="ffn",
    )(x, ada, nw, wup, wdn, nf)


TP = 512
NCH = TP // DN_CHUNK


def _proj_col(h, wlo_ref, whi_ref, c):
    n_lo = AB_OFF // D_MODEL
    w_ref, c = (wlo_ref, c) if c < n_lo else (whi_ref, c - n_lo)
    return _dot(h, w_ref[:, c * D_MODEL:(c + 1) * D_MODEL])


def _inproj_kernel(x_ref, ada_ref, nw_ref, wlo_ref, whi_ref, wab_ref,
                   xr_ref, ggr_ref, qkv_ref, szg_ref, sga_ref, sgb_ref, gb_ref, *, per_row):
    h = _norm_mod(x_ref[...], nw_ref[...], _ada_rows(ada_ref, 3, per_row),
                  _ada_rows(ada_ref, 4, per_row)).astype(BF16)
    col = functools.partial(_proj_col, h, wlo_ref, whi_ref)
    xr_ref[...] = col(0)
    ggr_ref[...] = jax.nn.gelu(col(1)).astype(BF16)
    for c in range(3):
        qkv_ref[:, c * D_MODEL:(c + 1) * D_MODEL] = col(2 + c)
    szg_ref[...] = _silu(col(5)).astype(BF16)
    sga_ref[...] = _sigmoid(col(6)).astype(BF16)
    sgb_ref[...] = _sigmoid(col(7)).astype(BF16)
    gb_ref[...] = _dot(h, wab_ref[...])


def _inproj_call(x, ada, nw, w_lo, w_hi, w_ab, *, per_row, tm, seq_len):
    m = x.shape[0]
    row = lambda n: pl.BlockSpec((tm, n), lambda i: (i, 0))
    shp = lambda n, dt: jax.ShapeDtypeStruct((m, n), dt)
    return pl.pallas_call(
        functools.partial(_inproj_kernel, per_row=per_row),
        grid=(m // tm,),
        in_specs=[row(D_MODEL), _ada_spec(per_row, m, seq_len // tm),
                  _resident((1, D_MODEL)), _resident((D_MODEL, AB_OFF)), _resident((D_MODEL, N_HI)),
                  _resident((D_MODEL, LANES))],
        out_specs=[row(D_RNN), row(D_RNN), row(DN_QKV), row(DN_V), row(D_MODEL), row(D_MODEL), row(LANES)],
        out_shape=[shp(D_RNN, F32), shp(D_RNN, BF16), shp(DN_QKV, F32), shp(DN_V, BF16),
                   shp(D_MODEL, BF16), shp(D_MODEL, BF16), shp(LANES, F32)],
        compiler_params=_params(("arbitrary",)),
        name="inproj",
    )(x, ada, nw, w_lo, w_hi, w_ab)


def _causal_conv(x, carry, cw_ref, lanes):
    t, c = x.shape
    x3 = x.reshape(t // SUBLANES, SUBLANES, c)
    prev8 = carry[:, lanes]
    sub = lax.broadcasted_iota(jnp.int32, (1, SUBLANES, 1), 1)
    acc = cw_ref[CONV_W - 1:CONV_W, lanes][None] * x3
    for j in range(1, CONV_W):
        r = pltpu.roll(x3, j, axis=1)
        rp = jnp.concatenate([pltpu.roll(prev8, j, axis=0)[None], r[:-1]], axis=0)
        acc = acc + cw_ref[CONV_W - 1 - j:CONV_W - j, lanes][None] * jnp.where(sub >= j, r, rp)
    carry[:, lanes] = x[t - SUBLANES:, :]
    return acc.reshape(t, c)


def _rg_gates(xc, wa_ref, ba, wx_ref, bx, lam):
    ra, ix = [], []
    for n in range(RG_BLOCKS):
        xb = xc[:, n * RG_BLOCK_W:(n + 1) * RG_BLOCK_W].astype(BF16)
        ra.append(_dot(xb, wa_ref[n]))
        ix.append(_dot(xb, wx_ref[n]))
    r = _sigmoid(jnp.concatenate(ra, axis=1) + ba)
    i = _sigmoid(jnp.concatenate(ix, axis=1) + bx)
    log_a = (-RG_C) * r * _softplus(-lam)
    return jnp.exp(log_a), log_a, i


MIX_CONV_LANES = 512


def _mixin_kernel(x_ref, ada_ref, nw_ref, wlo_ref, whi_ref, wab_ref,
                  cw_ref, cb_ref, wa_ref, ba_ref, wx_ref, bx_ref, lam_ref, cqw_ref,
                  oa_ref, act_ref, szg_ref, sga_ref, sgb_ref, gb_ref, hl_ref, cs_ref, cq_ref,
                  xcar, qcar, hcar, acum_s, hloc_s, hin_s):
    t = pl.program_id(1)

    @pl.when(t == 0)
    def _():
        xcar[...] = jnp.zeros(xcar.shape, F32)
        qcar[...] = jnp.zeros(qcar.shape, F32)
        hcar[...] = jnp.zeros(hcar.shape, F32)

    h = _norm_mod(x_ref[...], nw_ref[...], ada_ref[3:4, :], ada_ref[4:5, :]).astype(BF16)

    col = functools.partial(_proj_col, h, wlo_ref, whi_ref)
    groups = TP // SUBLANES
    sub = lax.broadcasted_iota(jnp.int32, (1, SUBLANES, 1), 1)
    row = lax.broadcasted_iota(jnp.int32, (TP, 1), 0)
    first = jnp.logical_and(row == 0, t == 0)

    def rglru_block(xr, n):
        ln = slice(n * RG_BLOCK_W, (n + 1) * RG_BLOCK_W)
        xc = _causal_conv(xr[:, ln], xcar, cw_ref, ln) + cb_ref[:, ln]
        xb = xc.astype(BF16)
        r = _sigmoid(_dot(xb, wa_ref[n]) + ba_ref[:, ln])
        i = _sigmoid(_dot(xb, wx_ref[n]) + bx_ref[:, ln])
        log_a = (-RG_C) * r * _softplus(-lam_ref[:, ln])
        a = jnp.exp(log_a)
        mult = jnp.where(first, 1.0, jnp.sqrt(_neg_expm1(2.0 * log_a)))
        a3 = a.reshape(groups, SUBLANES, RG_BLOCK_W)
        b3 = (mult * (i * xc)).reshape(groups, SUBLANES, RG_BLOCK_W)
        d = 1
        while d < SUBLANES:
            keep = sub >= d
            a_prev = jnp.where(keep, pltpu.roll(a3, d, axis=1), 1.0)
            b_prev = jnp.where(keep, pltpu.roll(b3, d, axis=1), 0.0)
            b3 = a3 * b_prev + b3
            a3 = a3 * a_prev
            d *= 2
        acum_s[:, ln] = a3.reshape(TP, RG_BLOCK_W)
        hloc_s[:, ln] = b3.reshape(TP, RG_BLOCK_W)

    def qkv_part(z, c):
        for l0 in range(0, D_MODEL, MIX_CONV_LANES):
            ln = slice(c * D_MODEL + l0, c * D_MODEL + l0 + MIX_CONV_LANES)
            act_ref[:, ln] = _silu(_causal_conv(z[:, l0:l0 + MIX_CONV_LANES], qcar, cqw_ref, ln))

    xr = col(0)
    for n in range(RG_BLOCKS):
        rglru_block(xr, n)
    cs_ref[0] = xcar[SUBLANES - (CONV_W - 1):, :]
    for c in range(DN_QKV // D_MODEL):
        qkv_part(col(2 + c), c)
    cq_ref[0] = qcar[SUBLANES - (CONV_W - 1):, :]

    hrow = hcar[0:1, :]
    for g in range(groups):
        hin_s[g * SUBLANES:(g + 1) * SUBLANES, :] = jnp.broadcast_to(hrow, (SUBLANES, D_RNN))
        e = (g + 1) * SUBLANES - 1
        hrow = acum_s[e:e + 1, :] * hrow + hloc_s[e:e + 1, :]
    hcar[0:1, :] = hrow
    hl_ref[0] = hrow

    y = hloc_s[...] + acum_s[...] * hin_s[...]
    oa_ref[...] = (y * jax.nn.gelu(col(1))).astype(BF16)
    szg_ref[...] = _silu(col(5)).astype(BF16)
    sga_ref[...] = _sigmoid(col(6)).astype(BF16)
    sgb_ref[...] = _sigmoid(col(7)).astype(BF16)
    gb_ref[...] = _dot(h, wab_ref[...])


def _mixin_call(x, ada, nw, w_lo, w_hi, w_ab, cw, cb, wa, ba, wx, bx, lam, cqw, *, batch, seq_len):
    nt = seq_len // TP
    m = batch * seq_len
    row = lambda n: pl.BlockSpec((TP, n), lambda b, t: (b * nt + t, 0))
    per_seq = lambda r, n: pl.BlockSpec((1, r, n), lambda b, t: (b, 0, 0))
    shp = lambda n, dt: jax.ShapeDtypeStruct((m, n), dt)
    blk = (RG_BLOCKS, RG_BLOCK_W, RG_BLOCK_W)
    return pl.pallas_call(
        _mixin_kernel,
        grid=(batch, nt),
        in_specs=[row(D_MODEL), pl.BlockSpec((None, N_ADA, D_MODEL), lambda b, t: (b, 0, 0)),
                  _resident((1, D_MODEL)), _resident((D_MODEL, AB_OFF)), _resident((D_MODEL, N_HI)),
                  _resident((D_MODEL, LANES)),
                  _resident((CONV_W, D_RNN)), _resident((1, D_RNN)), _resident(blk), _resident((1, D_RNN)),
                  _resident(blk), _resident((1, D_RNN)), _resident((1, D_RNN)), _resident((CONV_W, DN_QKV))],
        out_specs=[row(D_RNN), row(DN_QKV), row(DN_V), row(D_MODEL), row(D_MODEL), row(LANES),
                   per_seq(1, D_RNN), per_seq(CONV_W - 1, D_RNN), per_seq(CONV_W - 1, DN_QKV)],
        out_shape=[shp(D_RNN, BF16), shp(DN_QKV, F32), shp(DN_V, BF16), shp(D_MODEL, BF16), shp(D_MODEL, BF16),
                   shp(LANES, F32),
                   jax.ShapeDtypeStruct((batch, 1, D_RNN), F32),
                   jax.ShapeDtypeStruct((batch, CONV_W - 1, D_RNN), F32),
                   jax.ShapeDtypeStruct((batch, CONV_W - 1, DN_QKV), F32)],
        scratch_shapes=[pltpu.VMEM((SUBLANES, D_RNN), F32),
                        pltpu.VMEM((SUBLANES, DN_QKV), F32),
                        pltpu.VMEM((SUBLANES, D_RNN), F32),
                        pltpu.VMEM((TP, D_RNN), F32),
                        pltpu.VMEM((TP, D_RNN), F32),
                        pltpu.VMEM((TP, D_RNN), F32)],
        compiler_params=_params(("arbitrary", "arbitrary")),
        name="mixin",
    )(x, ada, nw, w_lo, w_hi, w_ab, cw, cb, wa, ba, wx, bx, lam, cqw)


def _split(x):
    hi = x.astype(BF16)
    return hi, (x - hi.astype(F32)).astype(BF16)


def _dot3(a_hi, a_lo, b):
    n = b.shape[1]
    b_hi, b_lo = _split(b)
    r = _dot(a_hi, jnp.concatenate([b_hi, b_lo], axis=1))
    return r[:, :n] + r[:, n:] + _dot(a_lo, b_hi)


def _pair_mask(ti, tj, lvl):
    same = (ti >> (lvl + 1)) == (tj >> (lvl + 1))
    return jnp.logical_and(same, jnp.logical_and(((ti >> lvl) & 1) == 1, ((tj >> lvl) & 1) == 0))


DELTA_CPI = 4


def _delta_kernel(act_s, gb_ref, szg_ref, alog_ref, dtb_ref, dnw_ref,
                  ob_ref, s_ref,
                  s_acc, o_s, u_s, wq_s, attn_s, kd_s, egl_s, *, nt):
    t = pl.program_id(1)
    c = DN_CHUNK
    heads = range(DN_HEADS)

    @pl.when(t == 0)
    def _():
        s_acc[...] = jnp.zeros(s_acc.shape, F32)

    ti = lax.broadcasted_iota(jnp.int32, (c, 2 * c), 0)
    tj = lax.broadcasted_iota(jnp.int32, (c, 2 * c), 1)
    left = tj < c
    incl = jnp.logical_and(ti >= tj, left)
    strict = jnp.logical_and(ti > tj, left)
    eye_l = (ti == tj).astype(F32)
    cum_l = incl[:, :c].astype(F32)
    cum_r = jnp.logical_and(ti <= tj, left).astype(F32)
    last = c - 1
    zpad = jnp.zeros((c, DN_DK), BF16)
    neg_a = -jnp.exp(alog_ref[...])
    dtb = dtb_ref[...]

    def chunk_rows(ref, j, lanes):
        return ref[pl.ds(pl.multiple_of(j * c, c), c), lanes]

    def prep(it, _):
        js = [it * DELTA_CPI + k for k in range(DELTA_CPI)]
        gcum, gcum_t, beta = [], [], []
        for j in js:
            gbv = chunk_rows(gb_ref, j, slice(None))
            g = neg_a * _softplus(gbv + dtb)
            beta.append(_sigmoid(gbv))
            gcum.append(jnp.dot(cum_l, g, precision=HI, preferred_element_type=F32))
            gcum_t.append(_dot_tn(g, cum_r, precision=HI))
        chains = [(k, h) for k in range(DELTA_CPI) for h in heads]
        gc, decay, qn, kn, kn16, kb, vb = {}, {}, {}, {}, {}, {}, {}
        for k, h in chains:
            j = js[k]
            gc[k, h] = gcum[k][:, h:h + 1]
            decay[k, h] = jnp.exp(jnp.where(incl, gc[k, h] - gcum_t[k][h:h + 1, :], -jnp.inf))
            bh = beta[k][:, DN_HEADS + h:DN_HEADS + h + 1]
            qh = chunk_rows(act_s, j, slice(h * DN_DK, (h + 1) * DN_DK))
            kh = chunk_rows(act_s, j, slice(DN_QK + h * DN_DK, DN_QK + (h + 1) * DN_DK))
            vh = chunk_rows(act_s, j, slice(2 * DN_QK + h * DN_DV, 2 * DN_QK + (h + 1) * DN_DV))
            qn[k, h] = qh * lax.rsqrt(jnp.sum(qh * qh, axis=-1, keepdims=True) + EPS) * (DN_DK ** -0.5)
            kn[k, h] = kh * lax.rsqrt(jnp.sum(kh * kh, axis=-1, keepdims=True) + EPS)
            kn16[k, h] = kn[k, h].astype(BF16)
            kb[k, h] = kn[k, h] * bh
            vb[k, h] = vh * bh
        lm = {}
        for ch in chains:
            kq = _dot_nt(jnp.concatenate([kb[ch], qn[ch]], axis=0).astype(BF16),
                         jnp.concatenate([kn16[ch], zpad], axis=0))
            lm[ch] = jnp.where(strict, kq[:c] * decay[ch], 0.0)
            attn_s[js[ch[0]], ch[1]] = (kq[c:] * decay[ch])[:, :c].astype(BF16)
        d = {ch: eye_l - jnp.where(_pair_mask(ti, tj, 0), lm[ch], 0.0) for ch in chains}
        for lvl in range(1, c.bit_length() - 1):
            pair = _pair_mask(ti, tj, lvl)
            d16 = {ch: d[ch].astype(BF16) for ch in chains}
            ed = {ch: _dot(jnp.where(pair, lm[ch], 0.0).astype(BF16)[:, :c], d16[ch]) for ch in chains}
            d = {ch: d[ch] - _dot(d16[ch][:, :c], ed[ch].astype(BF16)) for ch in chains}
        eg = {ch: jnp.exp(gc[ch]) for ch in chains}
        a16 = {ch: d[ch].astype(BF16)[:, :c] for ch in chains}
        rhs = {ch: jnp.concatenate([vb[ch], kb[ch] * eg[ch]], axis=1) for ch in chains}
        x0 = {ch: _dot(a16[ch], rhs[ch].astype(BF16)) for ch in chains}
        res = {}
        for ch in chains:
            l_hi, l_lo = _split(lm[ch])
            res[ch] = rhs[ch] - x0[ch] - _dot3(l_hi[:, :c], l_lo[:, :c], x0[ch])
        for ch in chains:
            k, h = ch
            x = x0[ch] + _dot(a16[ch], res[ch].astype(BF16))
            u_s[js[k], h] = x[:, :DN_DV]
            wq_s[js[k], h] = jnp.concatenate([x[:, DN_DV:], qn[ch] * eg[ch]], axis=0).astype(BF16)
            gl = gcum[k][last:last + 1, h:h + 1]
            kd_s[js[k], h] = (kn[ch] * jnp.exp(gl - gc[ch])).astype(BF16)
        for k, j in enumerate(js):
            egl_s[pl.ds(pl.multiple_of(j * SUBLANES, SUBLANES), SUBLANES), :] = jnp.broadcast_to(
                jnp.exp(gcum[k][last:last + 1, :]), (SUBLANES, LANES))
        return 0

    lax.fori_loop(0, NCH // DELTA_CPI, prep, 0)

    def recur(j, _):
        egl = egl_s[pl.ds(pl.multiple_of(j * SUBLANES, SUBLANES), 1), :]
        s_old = [s_acc[h] for h in heads]
        wq = [_dot(wq_s[j, h], s_old[h].astype(BF16)) for h in heads]
        vnew16 = [(u_s[j, h] - wq[h][:c]).astype(BF16) for h in heads]
        o = [wq[h][c:] + _dot(attn_s[j, h], vnew16[h]) for h in heads]
        for h in heads:
            s_acc[h] = s_old[h] * egl[:, h:h + 1] + _dot_tn(kd_s[j, h], vnew16[h])
        for h in heads:
            on = o[h] * lax.rsqrt(jnp.mean(o[h] * o[h], axis=-1, keepdims=True) + EPS) * dnw_ref[...]
            o_s[pl.ds(pl.multiple_of(j * c, c), c), h * DN_DV:(h + 1) * DN_DV] = on
        return 0

    lax.fori_loop(0, NCH, recur, 0)

    ob_ref[...] = (o_s[...] * szg_ref[...].astype(F32)).astype(BF16)

    @pl.when(t == nt - 1)
    def _():
        s_ref[0] = s_acc[...]


def _delta_call(act, gb, szg, alog, dtb, dnw, *, batch, seq_len):
    c = DN_CHUNK
    nt = seq_len // TP
    row = lambda n: pl.BlockSpec((TP, n), lambda b, t: (b * nt + t, 0))
    return pl.pallas_call(
        functools.partial(_delta_kernel, nt=nt),
        grid=(batch, nt),
        in_specs=[row(DN_QKV), row(LANES), row(DN_V),
                  _resident((1, LANES)), _resident((1, LANES)), _resident((1, DN_DV))],
        out_specs=[row(DN_V),
                   pl.BlockSpec((1, DN_HEADS, DN_DK, DN_DV), lambda b, t: (b, 0, 0, 0))],
        out_shape=[jax.ShapeDtypeStruct((batch * seq_len, DN_V), BF16),
                   jax.ShapeDtypeStruct((batch, DN_HEADS, DN_DK, DN_DV), F32)],
        scratch_shapes=[pltpu.VMEM((DN_HEADS, DN_DK, DN_DV), F32),
                        pltpu.VMEM((TP, DN_V), F32),
                        pltpu.VMEM((NCH, DN_HEADS, c, DN_DV), F32),
                        pltpu.VMEM((NCH, DN_HEADS, 2 * c, DN_DK), BF16),
                        pltpu.VMEM((NCH, DN_HEADS, c, c), BF16),
                        pltpu.VMEM((NCH, DN_HEADS, c, DN_DK), BF16),
                        pltpu.VMEM((NCH * SUBLANES, LANES), F32)],
        compiler_params=_params(("arbitrary", "arbitrary")),
        name="delta",
    )(act, gb, szg, alog, dtb, dnw)


def _sample_pre_kernel(xr_ref, ggr_ref, qkv_ref, gb_ref, crnn_ref, cqkv_ref, h0_ref,
                       cw_ref, cb_ref, wa_ref, ba_ref, wx_ref, bx_ref, lam_ref,
                       cqw_ref, alog_ref, dtb_ref,
                       oa_ref, hn_ref, qn_ref, kn_ref, v_ref, eg_ref, beta_ref):
    xc = cb_ref[...] + cw_ref[CONV_W - 1:CONV_W, :] * xr_ref[...]
    for j in range(CONV_W - 1):
        xc = xc + cw_ref[j:j + 1, :] * crnn_ref[j]
    a, log_a, i = _rg_gates(xc, wa_ref, ba_ref[...], wx_ref, bx_ref[...], lam_ref[...])
    mult = jnp.sqrt(_neg_expm1(2.0 * log_a))
    h = a * h0_ref[...] + mult * (i * xc)
    hn_ref[...] = h
    oa_ref[...] = (h * ggr_ref[...].astype(F32)).astype(BF16)

    qc = cqw_ref[CONV_W - 1:CONV_W, :] * qkv_ref[...]
    for j in range(CONV_W - 1):
        qc = qc + cqw_ref[j:j + 1, :] * cqkv_ref[j]
    act = _silu(qc)
    for h_ in range(DN_HEADS):
        qh = act[:, h_ * DN_DK:(h_ + 1) * DN_DK]
        kh = act[:, DN_QK + h_ * DN_DK:DN_QK + (h_ + 1) * DN_DK]
        qn_ref[:, h_ * DN_DK:(h_ + 1) * DN_DK] = (
            qh * lax.rsqrt(jnp.sum(qh * qh, axis=-1, keepdims=True) + EPS) * (DN_DK ** -0.5))
        kn_ref[:, h_ * DN_DK:(h_ + 1) * DN_DK] = (
            kh * lax.rsqrt(jnp.sum(kh * kh, axis=-1, keepdims=True) + EPS))
    v_ref[...] = act[:, 2 * DN_QK:]
    gbv = gb_ref[...]
    eg_ref[...] = jnp.exp(-jnp.exp(alog_ref[...]) * _softplus(gbv + dtb_ref[...]))
    beta_ref[...] = _sigmoid(gbv)


def _sample_pre_call(xr, ggr, qkv, gb, crnn, cqkv, h0, cw, cb, wa, ba, wx, bx, lam, cqw, alog, dtb):
    n = xr.shape[0]
    args = (xr, ggr, qkv, gb, crnn, cqkv, h0, cw, cb, wa, ba, wx, bx, lam, cqw, alog, dtb)
    shp = lambda w: jax.ShapeDtypeStruct((n, w), F32)
    outs = [jax.ShapeDtypeStruct((n, D_RNN), BF16), shp(D_RNN), shp(DN_QK), shp(DN_QK), shp(DN_V), shp(LANES), shp(LANES)]
    return pl.pallas_call(
        _sample_pre_kernel,
        grid=(1,),
        in_specs=[_resident(a.shape) for a in args],
        out_specs=[pl.BlockSpec(o.shape, lambda i: (0, 0)) for o in outs],
        out_shape=outs,
        compiler_params=_params(("arbitrary",)),
        name="sample_pre",
    )(*args)


def _sample_state_kernel(qn_ref, kn_ref, v_ref, eg_ref, beta_ref, szg_ref, dnw_ref, s_ref,
                         ob_ref, sn_ref):
    srow = lax.broadcasted_iota(jnp.int32, (SUBLANES, DN_DK), 0)
    rows = lambda r0, r1: jnp.where(srow == 0, r0, jnp.where(srow == 1, r1, 0.0))
    chains = [(b, h) for b in range(SAMPLE_BB) for h in range(DN_HEADS)]
    sl = lambda h: slice(h * DN_DK, (h + 1) * DN_DK)
    q = {(b, h): qn_ref[0, b:b + 1, sl(h)] for b, h in chains}
    k = {(b, h): kn_ref[0, b:b + 1, sl(h)] for b, h in chains}
    eg = {(b, h): eg_ref[0, b:b + 1, h:h + 1] for b, h in chains}
    beta = {(b, h): beta_ref[0, b:b + 1, DN_HEADS + h:DN_HEADS + h + 1] for b, h in chains}
    ws_qs = {ch: _dot(rows((k[ch] * beta[ch]) * eg[ch], q[ch] * eg[ch]).astype(BF16),
                      s_ref[ch[0], ch[1]].astype(BF16)) for ch in chains}
    vnew = {(b, h): v_ref[0, b:b + 1, sl(h)] * beta[b, h] - ws_qs[b, h][0:1, :] for b, h in chains}
    for ch in chains:
        b, h = ch
        zero = jnp.zeros_like(k[ch])
        sn_ref[b, h] = s_ref[b, h] * eg[ch] + _dot_tn(rows(k[ch], zero).astype(BF16),
                                                     rows(vnew[ch], zero).astype(BF16))
    for ch in chains:
        b, h = ch
        o = ws_qs[ch][1:2, :] + jnp.sum(q[ch] * k[ch], axis=-1, keepdims=True) * vnew[ch]
        on = o * lax.rsqrt(jnp.mean(o * o, axis=-1, keepdims=True) + EPS) * dnw_ref[...]
        ob_ref[0, b:b + 1, sl(h)] = on * szg_ref[0, b:b + 1, sl(h)]


SAMPLE_BB = 4


def _sample_state_call(qn, kn, v, eg, beta, szg, dnw, s0):
    n = qn.shape[0]
    bb = SAMPLE_BB
    r3 = lambda a: a.reshape(n // bb, bb, a.shape[-1])
    vec = lambda w: pl.BlockSpec((1, bb, w), lambda i: (i, 0, 0))
    s_spec = pl.BlockSpec((bb, DN_HEADS, DN_DK, DN_DV), lambda i: (i, 0, 0, 0))
    ob, sn = pl.pallas_call(
        _sample_state_kernel,
        grid=(n // bb,),
        in_specs=[vec(DN_QK), vec(DN_QK), vec(DN_V), vec(LANES), vec(LANES), vec(DN_V),
                  _resident((1, DN_DV)), s_spec],
        out_specs=[vec(DN_V), s_spec],
        out_shape=[jax.ShapeDtypeStruct((n // bb, bb, DN_V), F32),
                   jax.ShapeDtypeStruct((n, DN_HEADS, DN_DK, DN_DV), F32)],
        compiler_params=_params(("arbitrary",)),
        name="sample_state",
    )(r3(qn), r3(kn), r3(v), r3(eg), r3(beta), r3(szg), dnw, s0)
    return ob.reshape(n, DN_V), sn


def _merge_kernel(x_ref, oa_ref, ob_ref, sga_ref, sgb_ref, ada_ref, wb_ref, wo_ref, o_ref, *, per_row):
    ya = _dot(oa_ref[...], wb_ref[0])
    yb = _dot(ob_ref[...], wb_ref[1])
    merged = (sga_ref[...].astype(F32) * ya + sgb_ref[...].astype(F32) * yb).astype(BF16)
    o_ref[...] = x_ref[...] + _ada_rows(ada_ref, 5, per_row) * _dot(merged, wo_ref[...])


def _merge_call(x, oa, ob, sga, sgb, ada, wb, wo, *, per_row, tm, seq_len):
    m = x.shape[0]
    row = pl.BlockSpec((tm, D_MODEL), lambda i: (i, 0))
    return pl.pallas_call(
        functools.partial(_merge_kernel, per_row=per_row),
        grid=(m // tm,),
        in_specs=[row, row, row, row, row, _ada_spec(per_row, m, seq_len // tm),
                  _resident((2, D_RNN, D_MODEL)), _resident((D_MODEL, D_MODEL))],
        out_specs=row,
        out_shape=jax.ShapeDtypeStruct((m, D_MODEL), F32),
        compiler_params=_params(("arbitrary",)),
        name="merge",
    )(x, oa, ob, sga, sgb, ada, wb, wo)


def kernel(x_prompt, x_sample, c_prompt, c_sample, state_rglru_h, state_rglru_conv, state_delta_S, state_delta_conv, w_ada, b_ada, norm_ffn1, w_ffn1_up, w_ffn1_down, norm_mix, w_in, conv_rnn_w, conv_rnn_b, rg_w_a, rg_b_a, rg_w_x, rg_b_x, rg_lambda, conv_qkv_w, dn_a_log, dn_dt_bias, dn_norm, w_branch, w_out, norm_ffn2, w_ffn2_up, w_ffn2_down, norm_final):
    batch, seq_len, _ = x_prompt.shape
    n_dec = x_sample.shape[0]
    assert w_ada.shape[0] == 1 and x_sample.shape[1] == 1 and seq_len % TP == 0

    row = lambda a: a.reshape(1, -1).astype(F32)
    wup1, wdn1 = w_ffn1_up[0].astype(BF16), w_ffn1_down[0].astype(BF16)
    wup2, wdn2 = w_ffn2_up[0].astype(BF16), w_ffn2_down[0].astype(BF16)
    w_lo = w_in[0][:, :AB_OFF].astype(BF16)
    w_hi = w_in[0][:, AB_OFF + 2 * DN_HEADS:].astype(BF16)
    w_ab = jnp.pad(w_in[0][:, AB_OFF:AB_OFF + 2 * DN_HEADS], ((0, 0), (0, LANES - 2 * DN_HEADS))).astype(BF16)
    wb, wo = w_branch[0].astype(BF16), w_out[0].astype(BF16)
    wa, wx = rg_w_a[0].astype(BF16), rg_w_x[0].astype(BF16)
    lane_pad = lambda a: jnp.pad(a.reshape(1, -1).astype(F32), ((0, 0), (0, LANES - a.size)))
    alog = lane_pad(dn_a_log[0])
    dtb = lane_pad(dn_dt_bias[0])
    nf = row(norm_final)

    ada = _ada_call(jnp.concatenate([c_prompt, c_sample], axis=0), w_ada[0], row(b_ada[0]))
    ada = ada.reshape(batch + n_dec, N_ADA, D_MODEL)
    ada_p = ada[:batch]
    ada_s = jnp.transpose(ada[batch:], (1, 0, 2))

    mixer_w = (conv_rnn_w[0], row(conv_rnn_b[0]), wa, row(rg_b_a[0]), wx, row(rg_b_x[0]), row(rg_lambda[0]))

    kw = dict(per_row=False, seq_len=seq_len, tm=TP)
    xp = x_prompt.reshape(batch * seq_len, D_MODEL)
    xp = _ffn_call(xp, ada_p, row(norm_ffn1[0]), wup1, wdn1, nf, k0=0, final_norm=False, **kw)
    oa, act, szg, sga, sgb, gb, hp, cp, qp = _mixin_call(
        xp, ada_p, row(norm_mix[0]), w_lo, w_hi, w_ab, *mixer_w, conv_qkv_w[0], batch=batch, seq_len=seq_len)
    ob, sp = _delta_call(act, gb, szg, alog, dtb, row(dn_norm[0]), batch=batch, seq_len=seq_len)
    xp = _merge_call(xp, oa, ob, sga, sgb, ada_p, wb, wo, **kw)
    y_prompt = _ffn_call(xp, ada_p, row(norm_ffn2[0]), wup2, wdn2, nf, k0=6, final_norm=True, **kw)

    kw = dict(per_row=True, seq_len=n_dec, tm=n_dec)
    xs = x_sample.reshape(n_dec, D_MODEL)
    xs = _ffn_call(xs, ada_s, row(norm_ffn1[0]), wup1, wdn1, nf, k0=0, final_norm=False, **kw)
    xr_s, ggr_s, qkv_s, szg_s, sga_s, sgb_s, gb_s = _inproj_call(xs, ada_s, row(norm_mix[0]), w_lo, w_hi, w_ab, **kw)
    crnn = jnp.transpose(state_rglru_conv[0], (1, 0, 2))
    cqkv = jnp.transpose(state_delta_conv[0], (1, 0, 2))
    oa_s, hs, qn, kn, v, eg, beta = _sample_pre_call(
        xr_s, ggr_s, qkv_s, gb_s, crnn, cqkv, state_rglru_h[0], *mixer_w, conv_qkv_w[0], alog, dtb)
    ob_s, ss = _sample_state_call(qn, kn, v, eg, beta, szg_s.astype(F32), row(dn_norm[0]), state_delta_S[0])
    xs = _merge_call(xs, oa_s, ob_s.astype(BF16), sga_s, sgb_s, ada_s, wb, wo, **kw)
    y_sample = _ffn_call(xs, ada_s, row(norm_ffn2[0]), wup2, wdn2, nf, k0=6, final_norm=True, **kw)
    cs = jnp.concatenate([state_rglru_conv[0][:, 1:], xr_s[:, None, :]], axis=1)
    qs = jnp.concatenate([state_delta_conv[0][:, 1:], qkv_s[:, None, :]], axis=1)

    return (y_prompt.reshape(batch, seq_len, D_MODEL), y_sample.reshape(n_dec, 1, D_MODEL),
            hp.reshape(1, batch, D_RNN), cp[None], sp[None], qp[None],
            hs[None], cs[None], ss[None], qs[None])
```

```python
import functools

import jax
import jax.numpy as jnp
from jax import lax
from jax.experimental import pallas as pl
from jax.experimental.pallas import tpu as pltpu

D_MODEL = 1024
D_RNN = 1024
RG_BLOCKS = 8
RG_BLOCK_W = D_RNN // RG_BLOCKS
RG_C = 8.0
CONV_W = 4
DN_HEADS = 8
DN_DK = 128
DN_DV = 128
DN_QK = DN_HEADS * DN_DK
DN_V = DN_HEADS * DN_DV
DN_QKV = 2 * DN_QK + DN_V
DN_CHUNK = 64
D_FF = 2816
N_ADA = 9
EPS = 1e-6
LANES = 128
SUBLANES = 8
AB_OFF = D_RNN * 2 + DN_QKV
N_HI = 3 * D_MODEL

BF16 = jnp.bfloat16
F32 = jnp.float32
HI = lax.Precision.HIGHEST

VMEM_LIMIT = 58 * 1024 * 1024


def _params(sem):
    return pltpu.CompilerParams(dimension_semantics=sem, vmem_limit_bytes=VMEM_LIMIT)


def _resident(shape):
    nd = len(shape)
    return pl.BlockSpec(shape, lambda *_: (0,) * nd, pipeline_mode=pl.Buffered(1))


def _dot(a, b):
    return jnp.dot(a, b, preferred_element_type=F32)


def _dot_nt(a, b, precision=None):
    return lax.dot_general(a, b, (((1,), (1,)), ((), ())), precision=precision,
                           preferred_element_type=F32)


def _dot_tn(a, b, precision=None):
    return lax.dot_general(a, b, (((0,), (0,)), ((), ())), precision=precision,
                           preferred_element_type=F32)


def _sigmoid(x):
    return 0.5 * jnp.tanh(0.5 * x) + 0.5


def _silu(x):
    h = 0.5 * x
    return h + h * jnp.tanh(h)


def _softplus(x):
    return jnp.maximum(x, 0.0) + jnp.log1p(jnp.exp(-jnp.abs(x)))


def _neg_expm1(x):
    return -jnp.tanh(0.5 * x) * (jnp.exp(x) + 1.0)


def _ada_rows(ada_ref, k, per_row):
    if per_row:
        return ada_ref[k]
    return ada_ref[k:k + 1, :]


def _norm_mod(x, nw, shift, scale):
    ms = jnp.mean(x * x, axis=-1, keepdims=True)
    hn = x * lax.rsqrt(ms + EPS) * nw
    return hn * (1.0 + scale) + shift


def _ada_kernel(c_ref, w_ref, b_ref, o_ref):
    o_ref[...] = _dot(c_ref[...].astype(BF16), w_ref[...].astype(BF16)) + b_ref[...]


def _ada_call(c_all, w_ada, b_ada):
    n = c_all.shape[0]
    tn = D_MODEL
    return pl.pallas_call(
        _ada_kernel,
        grid=(N_ADA * D_MODEL // tn,),
        in_specs=[pl.BlockSpec((n, D_MODEL), lambda j: (0, 0)),
                  pl.BlockSpec((D_MODEL, tn), lambda j: (0, j)),
                  pl.BlockSpec((1, tn), lambda j: (0, j))],
        out_specs=pl.BlockSpec((n, tn), lambda j: (0, j)),
        out_shape=jax.ShapeDtypeStruct((n, N_ADA * D_MODEL), F32),
        compiler_params=_params(("arbitrary",)),
        name="ada",
    )(c_all, w_ada, b_ada)


FFN_TF = 256


def _ffn_kernel(x_ref, ada_ref, nw_ref, wup_ref, wdn_ref, nf_ref, o_ref, *, k0, per_row, final_norm):
    x = x_ref[...]
    h = _norm_mod(x, nw_ref[...], _ada_rows(ada_ref, k0, per_row),
                  _ada_rows(ada_ref, k0 + 1, per_row)).astype(BF16)
    acc = jnp.zeros(x.shape, F32)
    for j in range(D_FF // FFN_TF):
        g = _dot(h, wup_ref[:, j * FFN_TF:(j + 1) * FFN_TF].astype(BF16))
        v = _dot(h, wup_ref[:, D_FF + j * FFN_TF:D_FF + (j + 1) * FFN_TF].astype(BF16))
        a = (_silu(g) * v).astype(BF16)
        acc = acc + _dot(a, wdn_ref[j * FFN_TF:(j + 1) * FFN_TF, :].astype(BF16))
    y = x + 0.5 * _ada_rows(ada_ref, k0 + 2, per_row) * acc
    if final_norm:
        ms = jnp.mean(y * y, axis=-1, keepdims=True)
        y = y * lax.rsqrt(ms + EPS) * nf_ref[...]
    o_ref[...] = y


def _ada_spec(per_row, rows, tiles_per_seq):
    if per_row:
        return pl.BlockSpec((N_ADA, rows, D_MODEL), lambda i: (0, 0, 0))
    return pl.BlockSpec((None, N_ADA, D_MODEL), lambda i: (i // tiles_per_seq, 0, 0))


def _ffn_call(x, ada, nw, wup, wdn, nf, *, k0, per_row, final_norm, tm, seq_len):
    m = x.shape[0]
    kern = functools.partial(_ffn_kernel, k0=k0, per_row=per_row, final_norm=final_norm)
    return pl.pallas_call(
        kern,
        grid=(m // tm,),
        in_specs=[pl.BlockSpec((tm, D_MODEL), lambda i: (i, 0)),
                  _ada_spec(per_row, m, seq_len // tm),
                  _resident((1, D_MODEL)),
                  _resident((D_MODEL, 2 * D_FF)),
                  _resident((D_FF, D_MODEL)),
                  _resident((1, D_MODEL))],
        out_specs=pl.BlockSpec((tm, D_MODEL), lambda i: (i, 0)),
        out_shape=jax.ShapeDtypeStruct((m, D_MODEL), F32),
        compiler_params=_params(("arbitrary",)),
        name="ffn",
    )(x, ada, nw, wup, wdn, nf)


TP = 512
NCH = TP // DN_CHUNK


def _proj_col(h, wlo_ref, whi_ref, c):
    n_lo = AB_OFF // D_MODEL
    w_ref, c = (wlo_ref, c) if c < n_lo else (whi_ref, c - n_lo)
    return _dot(h, w_ref[:, c * D_MODEL:(c + 1) * D_MODEL])


def _inproj_kernel(x_ref, ada_ref, nw_ref, wlo_ref, whi_ref, wab_ref,
                   xr_ref, ggr_ref, qkv_ref, szg_ref, sga_ref, sgb_ref, gb_ref, *, per_row):
    h = _norm_mod(x_ref[...], nw_ref[...], _ada_rows(ada_ref, 3, per_row),
                  _ada_rows(ada_ref, 4, per_row)).astype(BF16)
    col = functools.partial(_proj_col, h, wlo_ref, whi_ref)
    xr_ref[...] = col(0)
    ggr_ref[...] = jax.nn.gelu(col(1)).astype(BF16)
    for c in range(3):
        qkv_ref[:, c * D_MODEL:(c + 1) * D_MODEL] = col(2 + c)
    szg_ref[...] = _silu(col(5)).astype(BF16)
    sga_ref[...] = _sigmoid(col(6)).astype(BF16)
    sgb_ref[...] = _sigmoid(col(7)).astype(BF16)
    gb_ref[...] = _dot(h, wab_ref[...])


def _inproj_call(x, ada, nw, w_lo, w_hi, w_ab, *, per_row, tm, seq_len):
    m = x.shape[0]
    row = lambda n: pl.BlockSpec((tm, n), lambda i: (i, 0))
    shp = lambda n, dt: jax.ShapeDtypeStruct((m, n), dt)
    return pl.pallas_call(
        functools.partial(_inproj_kernel, per_row=per_row),
        grid=(m // tm,),
        in_specs=[row(D_MODEL), _ada_spec(per_row, m, seq_len // tm),
                  _resident((1, D_MODEL)), _resident((D_MODEL, AB_OFF)), _resident((D_MODEL, N_HI)),
                  _resident((D_MODEL, LANES))],
        out_specs=[row(D_RNN), row(D_RNN), row(DN_QKV), row(DN_V), row(D_MODEL), row(D_MODEL), row(LANES)],
        out_shape=[shp(D_RNN, F32), shp(D_RNN, BF16), shp(DN_QKV, F32), shp(DN_V, BF16),
                   shp(D_MODEL, BF16), shp(D_MODEL, BF16), shp(LANES, F32)],
        compiler_params=_params(("arbitrary",)),
        name="inproj",
    )(x, ada, nw, w_lo, w_hi, w_ab)


def _causal_conv(x, carry, cw_ref, lanes):
    t, c = x.shape
    x3 = x.reshape(t // SUBLANES, SUBLANES, c)
    prev8 = carry[:, lanes]
    sub = lax.broadcasted_iota(jnp.int32, (1, SUBLANES, 1), 1)
    acc = cw_ref[CONV_W - 1:CONV_W, lanes][None] * x3
    for j in range(1, CONV_W):
        r = pltpu.roll(x3, j, axis=1)
        rp = jnp.concatenate([pltpu.roll(prev8, j, axis=0)[None], r[:-1]], axis=0)
        acc = acc + cw_ref[CONV_W - 1 - j:CONV_W - j, lanes][None] * jnp.where(sub >= j, r, rp)
    carry[:, lanes] = x[t - SUBLANES:, :]
    return acc.reshape(t, c)


def _rg_gates(xc, wa_ref, ba, wx_ref, bx, lam):
    ra, ix = [], []
    for n in range(RG_BLOCKS):
        xb = xc[:, n * RG_BLOCK_W:(n + 1) * RG_BLOCK_W].astype(BF16)
        ra.append(_dot(xb, wa_ref[n]))
        ix.append(_dot(xb, wx_ref[n]))
    r = _sigmoid(jnp.concatenate(ra, axis=1) + ba)
    i = _sigmoid(jnp.concatenate(ix, axis=1) + bx)
    log_a = (-RG_C) * r * _softplus(-lam)
    return jnp.exp(log_a), log_a, i


MIX_CONV_LANES = 512


def _mixin_kernel(x_ref, ada_ref, nw_ref, wlo_ref, whi_ref, wab_ref,
                  cw_ref, cb_ref, wa_ref, ba_ref, wx_ref, bx_ref, lam_ref, cqw_ref,
                  oa_ref, act_ref, szg_ref, sga_ref, sgb_ref, gb_ref, hl_ref, cs_ref, cq_ref,
                  xcar, qcar, hcar, acum_s, hloc_s, hin_s):
    t = pl.program_id(1)

    @pl.when(t == 0)
    def _():
        xcar[...] = jnp.zeros(xcar.shape, F32)
        qcar[...] = jnp.zeros(qcar.shape, F32)
        hcar[...] = jnp.zeros(hcar.shape, F32)

    h = _norm_mod(x_ref[...], nw_ref[...], ada_ref[3:4, :], ada_ref[4:5, :]).astype(BF16)

    col = functools.partial(_proj_col, h, wlo_ref, whi_ref)
    groups = TP // SUBLANES
    sub = lax.broadcasted_iota(jnp.int32, (1, SUBLANES, 1), 1)
    row = lax.broadcasted_iota(jnp.int32, (TP, 1), 0)
    first = jnp.logical_and(row == 0, t == 0)

    def rglru_block(xr, n):
        ln = slice(n * RG_BLOCK_W, (n + 1) * RG_BLOCK_W)
        xc = _causal_conv(xr[:, ln], xcar, cw_ref, ln) + cb_ref[:, ln]
        xb = xc.astype(BF16)
        r = _sigmoid(_dot(xb, wa_ref[n]) + ba_ref[:, ln])
        i = _sigmoid(_dot(xb, wx_ref[n]) + bx_ref[:, ln])
        log_a = (-RG_C) * r * _softplus(-lam_ref[:, ln])
        a = jnp.exp(log_a)
        mult = jnp.where(first, 1.0, jnp.sqrt(_neg_expm1(2.0 * log_a)))
        a3 = a.reshape(groups, SUBLANES, RG_BLOCK_W)
        b3 = (mult * (i * xc)).reshape(groups, SUBLANES, RG_BLOCK_W)
        d = 1
        while d < SUBLANES:
            keep = sub >= d
            a_prev = jnp.where(keep, pltpu.roll(a3, d, axis=1), 1.0)
            b_prev = jnp.where(keep, pltpu.roll(b3, d, axis=1), 0.0)
            b3 = a3 * b_prev + b3
            a3 = a3 * a_prev
            d *= 2
        acum_s[:, ln] = a3.reshape(TP, RG_BLOCK_W)
        hloc_s[:, ln] = b3.reshape(TP, RG_BLOCK_W)

    def qkv_part(z, c):
        for l0 in range(0, D_MODEL, MIX_CONV_LANES):
            ln = slice(c * D_MODEL + l0, c * D_MODEL + l0 + MIX_CONV_LANES)
            act_ref[:, ln] = _silu(_causal_conv(z[:, l0:l0 + MIX_CONV_LANES], qcar, cqw_ref, ln))

    xr = col(0)
    for n in range(RG_BLOCKS):
        rglru_block(xr, n)
    cs_ref[0] = xcar[SUBLANES - (CONV_W - 1):, :]
    for c in range(DN_QKV // D_MODEL):
        qkv_part(col(2 + c), c)
    cq_ref[0] = qcar[SUBLANES - (CONV_W - 1):, :]

    hrow = hcar[0:1, :]
    for g in range(groups):
        hin_s[g * SUBLANES:(g + 1) * SUBLANES, :] = jnp.broadcast_to(hrow, (SUBLANES, D_RNN))
        e = (g + 1) * SUBLANES - 1
        hrow = acum_s[e:e + 1, :] * hrow + hloc_s[e:e + 1, :]
    hcar[0:1, :] = hrow
    hl_ref[0] = hrow

    y = hloc_s[...] + acum_s[...] * hin_s[...]
    oa_ref[...] = (y * jax.nn.gelu(col(1))).astype(BF16)
    szg_ref[...] = _silu(col(5)).astype(BF16)
    sga_ref[...] = _sigmoid(col(6)).astype(BF16)
    sgb_ref[...] = _sigmoid(col(7)).astype(BF16)
    gb_ref[...] = _dot(h, wab_ref[...])


def _mixin_call(x, ada, nw, w_lo, w_hi, w_ab, cw, cb, wa, ba, wx, bx, lam, cqw, *, batch, seq_len):
    nt = seq_len // TP
    m = batch * seq_len
    row = lambda n: pl.BlockSpec((TP, n), lambda b, t: (b * nt + t, 0))
    per_seq = lambda r, n: pl.BlockSpec((1, r, n), lambda b, t: (b, 0, 0))
    shp = lambda n, dt: jax.ShapeDtypeStruct((m, n), dt)
    blk = (RG_BLOCKS, RG_BLOCK_W, RG_BLOCK_W)
    return pl.pallas_call(
        _mixin_kernel,
        grid=(batch, nt),
        in_specs=[row(D_MODEL), pl.BlockSpec((None, N_ADA, D_MODEL), lambda b, t: (b, 0, 0)),
                  _resident((1, D_MODEL)), _resident((D_MODEL, AB_OFF)), _resident((D_MODEL, N_HI)),
                  _resident((D_MODEL, LANES)),
                  _resident((CONV_W, D_RNN)), _resident((1, D_RNN)), _resident(blk), _resident((1, D_RNN)),
                  _resident(blk), _resident((1, D_RNN)), _resident((1, D_RNN)), _resident((CONV_W, DN_QKV))],
        out_specs=[row(D_RNN), row(DN_QKV), row(DN_V), row(D_MODEL), row(D_MODEL), row(LANES),
                   per_seq(1, D_RNN), per_seq(CONV_W - 1, D_RNN), per_seq(CONV_W - 1, DN_QKV)],
        out_shape=[shp(D_RNN, BF16), shp(DN_QKV, F32), shp(DN_V, BF16), shp(D_MODEL, BF16), shp(D_MODEL, BF16),
                   shp(LANES, F32),
                   jax.ShapeDtypeStruct((batch, 1, D_RNN), F32),
                   jax.ShapeDtypeStruct((batch, CONV_W - 1, D_RNN), F32),
                   jax.ShapeDtypeStruct((batch, CONV_W - 1, DN_QKV), F32)],
        scratch_shapes=[pltpu.VMEM((SUBLANES, D_RNN), F32),
                        pltpu.VMEM((SUBLANES, DN_QKV), F32),
                        pltpu.VMEM((SUBLANES, D_RNN), F32),
                        pltpu.VMEM((TP, D_RNN), F32),
                        pltpu.VMEM((TP, D_RNN), F32),
                        pltpu.VMEM((TP, D_RNN), F32)],
        compiler_params=_params(("arbitrary", "arbitrary")),
        name="mixin",
    )(x, ada, nw, w_lo, w_hi, w_ab, cw, cb, wa, ba, wx, bx, lam, cqw)


def _split(x):
    hi = x.astype(BF16)
    return hi, (x - hi.astype(F32)).astype(BF16)


def _dot3(a_hi, a_lo, b):
    n = b.shape[1]
    b_hi, b_lo = _split(b)
    r = _dot(a_hi, jnp.concatenate([b_hi, b_lo], axis=1))
    return r[:, :n] + r[:, n:] + _dot(a_lo, b_hi)


def _pair_mask(ti, tj, lvl):
    same = (ti >> (lvl + 1)) == (tj >> (lvl + 1))
    return jnp.logical_and(same, jnp.logical_and(((ti >> lvl) & 1) == 1, ((tj >> lvl) & 1) == 0))


DELTA_CPI = 4


def _delta_kernel(act_s, gb_ref, szg_ref, alog_ref, dtb_ref, dnw_ref,
                  ob_ref, s_ref,
                  s_acc, o_s, u_s, wq_s, attn_s, kd_s, egl_s, *, nt):
    t = pl.program_id(1)
    c = DN_CHUNK
    heads = range(DN_HEADS)

    @pl.when(t == 0)
    def _():
        s_acc[...] = jnp.zeros(s_acc.shape, F32)

    ti = lax.broadcasted_iota(jnp.int32, (c, 2 * c), 0)
    tj = lax.broadcasted_iota(jnp.int32, (c, 2 * c), 1)
    left = tj < c
    incl = jnp.logical_and(ti >= tj, left)
    strict = jnp.logical_and(ti > tj, left)
    eye_l = (ti == tj).astype(F32)
    cum_l = incl[:, :c].astype(F32)
    cum_r = jnp.logical_and(ti <= tj, left).astype(F32)
    last = c - 1
    zpad = jnp.zeros((c, DN_DK), BF16)
    neg_a = -jnp.exp(alog_ref[...])
    dtb = dtb_ref[...]

    def chunk_rows(ref, j, lanes):
        return ref[pl.ds(pl.multiple_of(j * c, c), c), lanes]

    def prep(it, _):
        js = [it * DELTA_CPI + k for k in range(DELTA_CPI)]
        gcum, gcum_t, beta = [], [], []
        for j in js:
            gbv = chunk_rows(gb_ref, j, slice(None))
            g = neg_a * _softplus(gbv + dtb)
            beta.append(_sigmoid(gbv))
            gcum.append(jnp.dot(cum_l, g, precision=HI, preferred_element_type=F32))
            gcum_t.append(_dot_tn(g, cum_r, precision=HI))
        chains = [(k, h) for k in range(DELTA_CPI) for h in heads]
        gc, decay, qn, kn, kn16, kb, vb = {}, {}, {}, {}, {}, {}, {}
        for k, h in chains:
            j = js[k]
            gc[k, h] = gcum[k][:, h:h + 1]
            decay[k, h] = jnp.exp(jnp.where(incl, gc[k, h] - gcum_t[k][h:h + 1, :], -jnp.inf))
            bh = beta[k][:, DN_HEADS + h:DN_HEADS + h + 1]
            qh = chunk_rows(act_s, j, slice(h * DN_DK, (h + 1) * DN_DK))
            kh = chunk_rows(act_s, j, slice(DN_QK + h * DN_DK, DN_QK + (h + 1) * DN_DK))
            vh = chunk_rows(act_s, j, slice(2 * DN_QK + h * DN_DV, 2 * DN_QK + (h + 1) * DN_DV))
            qn[k, h] = qh * lax.rsqrt(jnp.sum(qh * qh, axis=-1, keepdims=True) + EPS) * (DN_DK ** -0.5)
            kn[k, h] = kh * lax.rsqrt(jnp.sum(kh * kh, axis=-1, keepdims=True) + EPS)
            kn16[k, h] = kn[k, h].astype(BF16)
            kb[k, h] = kn[k, h] * bh
            vb[k, h] = vh * bh
        lm = {}
        for ch in chains:
            kq = _dot_nt(jnp.concatenate([kb[ch], qn[ch]], axis=0).astype(BF16),
                         jnp.concatenate([kn16[ch], zpad], axis=0))
            lm[ch] = jnp.where(strict, kq[:c] * decay[ch], 0.0)
            attn_s[js[ch[0]], ch[1]] = (kq[c:] * decay[ch])[:, :c].astype(BF16)
        d = {ch: eye_l - jnp.where(_pair_mask(ti, tj, 0), lm[ch], 0.0) for ch in chains}
        for lvl in range(1, c.bit_length() - 1):
            pair = _pair_mask(ti, tj, lvl)
            d16 = {ch: d[ch].astype(BF16) for ch in chains}
            ed = {ch: _dot(jnp.where(pair, lm[ch], 0.0).astype(BF16)[:, :c], d16[ch]) for ch in chains}
            d = {ch: d[ch] - _dot(d16[ch][:, :c], ed[ch].astype(BF16)) for ch in chains}
        eg = {ch: jnp.exp(gc[ch]) for ch in chains}
        a16 = {ch: d[ch].astype(BF16)[:, :c] for ch in chains}
        rhs = {ch: jnp.concatenate([vb[ch], kb[ch] * eg[ch]], axis=1) for ch in chains}
        x0 = {ch: _dot(a16[ch], rhs[ch].astype(BF16)) for ch in chains}
        res = {}
        for ch in chains:
            l_hi, l_lo = _split(lm[ch])
            res[ch] = rhs[ch] - x0[ch] - _dot3(l_hi[:, :c], l_lo[:, :c], x0[ch])
        for ch in chains:
            k, h = ch
            x = x0[ch] + _dot(a16[ch], res[ch].astype(BF16))
            u_s[js[k], h] = x[:, :DN_DV]
            wq_s[js[k], h] = jnp.concatenate([x[:, DN_DV:], qn[ch] * eg[ch]], axis=0).astype(BF16)
            gl = gcum[k][last:last + 1, h:h + 1]
            kd_s[js[k], h] = (kn[ch] * jnp.exp(gl - gc[ch])).astype(BF16)
        for k, j in enumerate(js):
            egl_s[pl.ds(pl.multiple_of(j * SUBLANES, SUBLANES), SUBLANES), :] = jnp.broadcast_to(
                jnp.exp(gcum[k][last:last + 1, :]), (SUBLANES, LANES))
        return 0

    lax.fori_loop(0, NCH // DELTA_CPI, prep, 0)

    def recur(j, _):
        egl = egl_s[pl.ds(pl.multiple_of(j * SUBLANES, SUBLANES), 1), :]
        s_old = [s_acc[h] for h in heads]
        wq = [_dot(wq_s[j, h], s_old[h].astype(BF16)) for h in heads]
        vnew16 = [(u_s[j, h] - wq[h][:c]).astype(BF16) for h in heads]
        o = [wq[h][c:] + _dot(attn_s[j, h], vnew16[h]) for h in heads]
        for h in heads:
            s_acc[h] = s_old[h] * egl[:, h:h + 1] + _dot_tn(kd_s[j, h], vnew16[h])
        for h in heads:
            on = o[h] * lax.rsqrt(jnp.mean(o[h] * o[h], axis=-1, keepdims=True) + EPS) * dnw_ref[...]
            o_s[pl.ds(pl.multiple_of(j * c, c), c), h * DN_DV:(h + 1) * DN_DV] = on
        return 0

    lax.fori_loop(0, NCH, recur, 0)

    ob_ref[...] = (o_s[...] * szg_ref[...].astype(F32)).astype(BF16)

    @pl.when(t == nt - 1)
    def _():
        s_ref[0] = s_acc[...]


def _delta_call(act, gb, szg, alog, dtb, dnw, *, batch, seq_len):
    c = DN_CHUNK
    nt = seq_len // TP
    row = lambda n: pl.BlockSpec((TP, n), lambda b, t: (b * nt + t, 0))
    return pl.pallas_call(
        functools.partial(_delta_kernel, nt=nt),
        grid=(batch, nt),
        in_specs=[row(DN_QKV), row(LANES), row(DN_V),
                  _resident((1, LANES)), _resident((1, LANES)), _resident((1, DN_DV))],
        out_specs=[row(DN_V),
                   pl.BlockSpec((1, DN_HEADS, DN_DK, DN_DV), lambda b, t: (b, 0, 0, 0))],
        out_shape=[jax.ShapeDtypeStruct((batch * seq_len, DN_V), BF16),
                   jax.ShapeDtypeStruct((batch, DN_HEADS, DN_DK, DN_DV), F32)],
        scratch_shapes=[pltpu.VMEM((DN_HEADS, DN_DK, DN_DV), F32),
                        pltpu.VMEM((TP, DN_V), F32),
                        pltpu.VMEM((NCH, DN_HEADS, c, DN_DV), F32),
                        pltpu.VMEM((NCH, DN_HEADS, 2 * c, DN_DK), BF16),
                        pltpu.VMEM((NCH, DN_HEADS, c, c), BF16),
                        pltpu.VMEM((NCH, DN_HEADS, c, DN_DK), BF16),
                        pltpu.VMEM((NCH * SUBLANES, LANES), F32)],
        compiler_params=_params(("arbitrary", "arbitrary")),
        name="delta",
    )(act, gb, szg, alog, dtb, dnw)


def _sample_pre_kernel(xr_ref, ggr_ref, qkv_ref, gb_ref, crnn_ref, cqkv_ref, h0_ref,
                       cw_ref, cb_ref, wa_ref, ba_ref, wx_ref, bx_ref, lam_ref,
                       cqw_ref, alog_ref, dtb_ref,
                       oa_ref, hn_ref, qn_ref, kn_ref, v_ref, eg_ref, beta_ref):
    xc = cb_ref[...] + cw_ref[CONV_W - 1:CONV_W, :] * xr_ref[...]
    for j in range(CONV_W - 1):
        xc = xc + cw_ref[j:j + 1, :] * crnn_ref[j]
    a, log_a, i = _rg_gates(xc, wa_ref, ba_ref[...], wx_ref, bx_ref[...], lam_ref[...])
    mult = jnp.sqrt(_neg_expm1(2.0 * log_a))
    h = a * h0_ref[...] + mult * (i * xc)
    hn_ref[...] = h
    oa_ref[...] = (h * ggr_ref[...].astype(F32)).astype(BF16)

    qc = cqw_ref[CONV_W - 1:CONV_W, :] * qkv_ref[...]
    for j in range(CONV_W - 1):
        qc = qc + cqw_ref[j:j + 1, :] * cqkv_ref[j]
    act = _silu(qc)
    for h_ in range(DN_HEADS):
        qh = act[:, h_ * DN_DK:(h_ + 1) * DN_DK]
        kh = act[:, DN_QK + h_ * DN_DK:DN_QK + (h_ + 1) * DN_DK]
        qn_ref[:, h_ * DN_DK:(h_ + 1) * DN_DK] = (
            qh * lax.rsqrt(jnp.sum(qh * qh, axis=-1, keepdims=True) + EPS) * (DN_DK ** -0.5))
        kn_ref[:, h_ * DN_DK:(h_ + 1) * DN_DK] = (
            kh * lax.rsqrt(jnp.sum(kh * kh, axis=-1, keepdims=True) + EPS))
    v_ref[...] = act[:, 2 * DN_QK:]
    gbv = gb_ref[...]
    eg_ref[...] = jnp.exp(-jnp.exp(alog_ref[...]) * _softplus(gbv + dtb_ref[...]))
    beta_ref[...] = _sigmoid(gbv)


def _sample_pre_call(xr, ggr, qkv, gb, crnn, cqkv, h0, cw, cb, wa, ba, wx, bx, lam, cqw, alog, dtb):
    n = xr.shape[0]
    args = (xr, ggr, qkv, gb, crnn, cqkv, h0, cw, cb, wa, ba, wx, bx, lam, cqw, alog, dtb)
    shp = lambda w: jax.ShapeDtypeStruct((n, w), F32)
    outs = [jax.ShapeDtypeStruct((n, D_RNN), BF16), shp(D_RNN), shp(DN_QK), shp(DN_QK), shp(DN_V), shp(LANES), shp(LANES)]
    return pl.pallas_call(
        _sample_pre_kernel,
        grid=(1,),
        in_specs=[_resident(a.shape) for a in args],
        out_specs=[pl.BlockSpec(o.shape, lambda i: (0, 0)) for o in outs],
        out_shape=outs,
        compiler_params=_params(("arbitrary",)),
        name="sample_pre",
    )(*args)


def _sample_state_kernel(qn_ref, kn_ref, v_ref, eg_ref, beta_ref, szg_ref, dnw_ref, s_ref,
                         ob_ref, sn_ref):
    srow = lax.broadcasted_iota(jnp.int32, (SUBLANES, DN_DK), 0)
    rows = lambda r0, r1: jnp.where(srow == 0, r0, jnp.where(srow == 1, r1, 0.0))
    chains = [(b, h) for b in range(SAMPLE_BB) for h in range(DN_HEADS)]
    sl = lambda h: slice(h * DN_DK, (h + 1) * DN_DK)
    q = {(b, h): qn_ref[0, b:b + 1, sl(h)] for b, h in chains}
    k = {(b, h): kn_ref[0, b:b + 1, sl(h)] for b, h in chains}
    eg = {(b, h): eg_ref[0, b:b + 1, h:h + 1] for b, h in chains}
    beta = {(b, h): beta_ref[0, b:b + 1, DN_HEADS + h:DN_HEADS + h + 1] for b, h in chains}
    ws_qs = {ch: _dot(rows((k[ch] * beta[ch]) * eg[ch], q[ch] * eg[ch]).astype(BF16),
                      s_ref[ch[0], ch[1]].astype(BF16)) for ch in chains}
    vnew = {(b, h): v_ref[0, b:b + 1, sl(h)] * beta[b, h] - ws_qs[b, h][0:1, :] for b, h in chains}
    for ch in chains:
        b, h = ch
        zero = jnp.zeros_like(k[ch])
        sn_ref[b, h] = s_ref[b, h] * eg[ch] + _dot_tn(rows(k[ch], zero).astype(BF16),
                                                     rows(vnew[ch], zero).astype(BF16))
    for ch in chains:
        b, h = ch
        o = ws_qs[ch][1:2, :] + jnp.sum(q[ch] * k[ch], axis=-1, keepdims=True) * vnew[ch]
        on = o * lax.rsqrt(jnp.mean(o * o, axis=-1, keepdims=True) + EPS) * dnw_ref[...]
        ob_ref[0, b:b + 1, sl(h)] = on * szg_ref[0, b:b + 1, sl(h)]


SAMPLE_BB = 8


def _sample_state_call(qn, kn, v, eg, beta, szg, dnw, s0):
    n = qn.shape[0]
    bb = SAMPLE_BB
    r3 = lambda a: a.reshape(n // bb, bb, a.shape[-1])
    vec = lambda w: pl.BlockSpec((1, bb, w), lambda i: (i, 0, 0))
    s_spec = pl.BlockSpec((bb, DN_HEADS, DN_DK, DN_DV), lambda i: (i, 0, 0, 0))
    ob, sn = pl.pallas_call(
        _sample_state_kernel,
        grid=(n // bb,),
        in_specs=[vec(DN_QK), vec(DN_QK), vec(DN_V), vec(LANES), vec(LANES), vec(DN_V),
                  _resident((1, DN_DV)), s_spec],
        out_specs=[vec(DN_V), s_spec],
        out_shape=[jax.ShapeDtypeStruct((n // bb, bb, DN_V), F32),
                   jax.ShapeDtypeStruct((n, DN_HEADS, DN_DK, DN_DV), F32)],
        compiler_params=_params(("arbitrary",)),
        name="sample_state",
    )(r3(qn), r3(kn), r3(v), r3(eg), r3(beta), r3(szg), dnw, s0)
    return ob.reshape(n, DN_V), sn


def _merge_kernel(x_ref, oa_ref, ob_ref, sga_ref, sgb_ref, ada_ref, wb_ref, wo_ref, o_ref, *, per_row):
    ya = _dot(oa_ref[...], wb_ref[0].astype(BF16))
    yb = _dot(ob_ref[...], wb_ref[1].astype(BF16))
    merged = (sga_ref[...].astype(F32) * ya + sgb_ref[...].astype(F32) * yb).astype(BF16)
    o_ref[...] = x_ref[...] + _ada_rows(ada_ref, 5, per_row) * _dot(merged, wo_ref[...].astype(BF16))


def _merge_call(x, oa, ob, sga, sgb, ada, wb, wo, *, per_row, tm, seq_len):
    m = x.shape[0]
    row = pl.BlockSpec((tm, D_MODEL), lambda i: (i, 0))
    return pl.pallas_call(
        functools.partial(_merge_kernel, per_row=per_row),
        grid=(m // tm,),
        in_specs=[row, row, row, row, row, _ada_spec(per_row, m, seq_len // tm),
                  _resident((2, D_RNN, D_MODEL)), _resident((D_MODEL, D_MODEL))],
        out_specs=row,
        out_shape=jax.ShapeDtypeStruct((m, D_MODEL), F32),
        compiler_params=_params(("arbitrary",)),
        name="merge",
    )(x, oa, ob, sga, sgb, ada, wb, wo)


def kernel(x_prompt, x_sample, c_prompt, c_sample, state_rglru_h, state_rglru_conv, state_delta_S, state_delta_conv, w_ada, b_ada, norm_ffn1, w_ffn1_up, w_ffn1_down, norm_mix, w_in, conv_rnn_w, conv_rnn_b, rg_w_a, rg_b_a, rg_w_x, rg_b_x, rg_lambda, conv_qkv_w, dn_a_log, dn_dt_bias, dn_norm, w_branch, w_out, norm_ffn2, w_ffn2_up, w_ffn2_down, norm_final):
    batch, seq_len, _ = x_prompt.shape
    n_dec = x_sample.shape[0]
    assert w_ada.shape[0] == 1 and x_sample.shape[1] == 1 and seq_len % TP == 0

    row = lambda a: a.reshape(1, -1).astype(F32)
    wup1, wdn1 = w_ffn1_up[0], w_ffn1_down[0]
    wup2, wdn2 = w_ffn2_up[0], w_ffn2_down[0]
    w_lo = w_in[0][:, :AB_OFF].astype(BF16)
    w_hi = w_in[0][:, AB_OFF + 2 * DN_HEADS:].astype(BF16)
    w_ab = jnp.pad(w_in[0][:, AB_OFF:AB_OFF + 2 * DN_HEADS], ((0, 0), (0, LANES - 2 * DN_HEADS))).astype(BF16)
    wb, wo = w_branch[0], w_out[0]
    wa, wx = rg_w_a[0].astype(BF16), rg_w_x[0].astype(BF16)
    lane_pad = lambda a: jnp.pad(a.reshape(1, -1).astype(F32), ((0, 0), (0, LANES - a.size)))
    alog = lane_pad(dn_a_log[0])
    dtb = lane_pad(dn_dt_bias[0])
    nf = row(norm_final)

    ada = _ada_call(jnp.concatenate([c_prompt, c_sample], axis=0), w_ada[0], row(b_ada[0]))
    ada = ada.reshape(batch + n_dec, N_ADA, D_MODEL)
    ada_p = ada[:batch]
    ada_s = jnp.transpose(ada[batch:], (1, 0, 2))

    mixer_w = (conv_rnn_w[0], row(conv_rnn_b[0]), wa, row(rg_b_a[0]), wx, row(rg_b_x[0]), row(rg_lambda[0]))

    kw = dict(per_row=False, seq_len=seq_len, tm=TP)
    xp = x_prompt.reshape(batch * seq_len, D_MODEL)
    xp = _ffn_call(xp, ada_p, row(norm_ffn1[0]), wup1, wdn1, nf, k0=0, final_norm=False, **kw)
    oa, act, szg, sga, sgb, gb, hp, cp, qp = _mixin_call(
        xp, ada_p, row(norm_mix[0]), w_lo, w_hi, w_ab, *mixer_w, conv_qkv_w[0], batch=batch, seq_len=seq_len)
    ob, sp = _delta_call(act, gb, szg, alog, dtb, row(dn_norm[0]), batch=batch, seq_len=seq_len)
    xp = _merge_call(xp, oa, ob, sga, sgb, ada_p, wb, wo, **kw)
    y_prompt = _ffn_call(xp, ada_p, row(norm_ffn2[0]), wup2, wdn2, nf, k0=6, final_norm=True, **kw)

    kw = dict(per_row=True, seq_len=n_dec, tm=n_dec)
    xs = x_sample.reshape(n_dec, D_MODEL)
    xs = _ffn_call(xs, ada_s, row(norm_ffn1[0]), wup1, wdn1, nf, k0=0, final_norm=False, **kw)
    xr_s, ggr_s, qkv_s, szg_s, sga_s, sgb_s, gb_s = _inproj_call(xs, ada_s, row(norm_mix[0]), w_lo, w_hi, w_ab, **kw)
    crnn = jnp.transpose(state_rglru_conv[0], (1, 0, 2))
    cqkv = jnp.transpose(state_delta_conv[0], (1, 0, 2))
    oa_s, hs, qn, kn, v, eg, beta = _sample_pre_call(
        xr_s, ggr_s, qkv_s, gb_s, crnn, cqkv, state_rglru_h[0], *mixer_w, conv_qkv_w[0], alog, dtb)
    ob_s, ss = _sample_state_call(qn, kn, v, eg, beta, szg_s.astype(F32), row(dn_norm[0]), state_delta_S[0])
    xs = _merge_call(xs, oa_s, ob_s.astype(BF16), sga_s, sgb_s, ada_s, wb, wo, **kw)
    y_sample = _ffn_call(xs, ada_s, row(norm_ffn2[0]), wup2, wdn2, nf, k0=6, final_norm=True, **kw)
    cs = jnp.concatenate([state_rglru_conv[0][:, 1:], xr_s[:, None, :]], axis=1)
    qs = jnp.concatenate([state_delta_conv[0][:, 1:], qkv_s[:, None, :]], axis=1)

    return (y_prompt.reshape(batch, seq_len, D_MODEL), y_sample.reshape(n_dec, 1, D_MODEL),
            hp.reshape(1, batch, D_RNN), cp[None], sp[None], qp[None],
            hs[None], cs[None], ss[None], qs[None])
```

```python
import functools

import jax
import jax.numpy as jnp
from jax import lax
from jax.experimental import pallas as pl
from jax.experimental.pallas import tpu as pltpu

D_MODEL = 1024
D_RNN = 1024
RG_BLOCKS = 8
RG_BLOCK_W = D_RNN // RG_BLOCKS
RG_C = 8.0
CONV_W = 4
DN_HEADS = 8
DN_DK = 128
DN_DV = 128
DN_QK = DN_HEADS * DN_DK
DN_V = DN_HEADS * DN_DV
DN_QKV = 2 * DN_QK + DN_V
DN_CHUNK = 64
D_FF = 2816
N_ADA = 9
EPS = 1e-6
LANES = 128
SUBLANES = 8
AB_OFF = D_RNN * 2 + DN_QKV
N_HI = 3 * D_MODEL

BF16 = jnp.bfloat16
F32 = jnp.float32
HI = lax.Precision.HIGHEST

VMEM_LIMIT = 58 * 1024 * 1024


def _params(sem):
    return pltpu.CompilerParams(dimension_semantics=sem, vmem_limit_bytes=VMEM_LIMIT)


def _resident(shape):
    nd = len(shape)
    return pl.BlockSpec(shape, lambda *_: (0,) * nd, pipeline_mode=pl.Buffered(1))


def _dot(a, b):
    return jnp.dot(a, b, preferred_element_type=F32)


def _dot_nt(a, b, precision=None):
    return lax.dot_general(a, b, (((1,), (1,)), ((), ())), precision=precision,
                           preferred_element_type=F32)


def _dot_tn(a, b, precision=None):
    return lax.dot_general(a, b, (((0,), (0,)), ((), ())), precision=precision,
                           preferred_element_type=F32)


def _sigmoid(x):
    return 0.5 * jnp.tanh(0.5 * x) + 0.5


def _silu(x):
    h = 0.5 * x
    return h + h * jnp.tanh(h)


def _softplus(x):
    return jnp.maximum(x, 0.0) + jnp.log1p(jnp.exp(-jnp.abs(x)))


def _neg_expm1(x):
    return -jnp.tanh(0.5 * x) * (jnp.exp(x) + 1.0)


def _ada_rows(ada_ref, k, per_row):
    if per_row:
        return ada_ref[k]
    return ada_ref[k:k + 1, :]


def _norm_mod(x, nw, shift, scale):
    ms = jnp.mean(x * x, axis=-1, keepdims=True)
    hn = x * lax.rsqrt(ms + EPS) * nw
    return hn * (1.0 + scale) + shift


def _ada_kernel(c_ref, w_ref, b_ref, o_ref):
    o_ref[...] = _dot(c_ref[...].astype(BF16), w_ref[...].astype(BF16)) + b_ref[...]


def _ada_call(c_all, w_ada, b_ada):
    n = c_all.shape[0]
    tn = D_MODEL
    return pl.pallas_call(
        _ada_kernel,
        grid=(N_ADA * D_MODEL // tn,),
        in_specs=[pl.BlockSpec((n, D_MODEL), lambda j: (0, 0)),
                  pl.BlockSpec((D_MODEL, tn), lambda j: (0, j)),
                  pl.BlockSpec((1, tn), lambda j: (0, j))],
        out_specs=pl.BlockSpec((n, tn), lambda j: (0, j)),
        out_shape=jax.ShapeDtypeStruct((n, N_ADA * D_MODEL), F32),
        compiler_params=_params(("arbitrary",)),
        name="ada",
    )(c_all, w_ada, b_ada)


FFN_TF = 256


def _ffn_kernel(x_ref, ada_ref, nw_ref, wup_ref, wdn_ref, nf_ref, o_ref, *, k0, per_row, final_norm):
    x = x_ref[...]
    h = _norm_mod(x, nw_ref[...], _ada_rows(ada_ref, k0, per_row),
                  _ada_rows(ada_ref, k0 + 1, per_row)).astype(BF16)
    acc = jnp.zeros(x.shape, F32)
    for j in range(D_FF // FFN_TF):
        g = _dot(h, wup_ref[:, j * FFN_TF:(j + 1) * FFN_TF].astype(BF16))
        v = _dot(h, wup_ref[:, D_FF + j * FFN_TF:D_FF + (j + 1) * FFN_TF].astype(BF16))
        a = (_silu(g) * v).astype(BF16)
        acc = acc + _dot(a, wdn_ref[j * FFN_TF:(j + 1) * FFN_TF, :].astype(BF16))
    y = x + 0.5 * _ada_rows(ada_ref, k0 + 2, per_row) * acc
    if final_norm:
        ms = jnp.mean(y * y, axis=-1, keepdims=True)
        y = y * lax.rsqrt(ms + EPS) * nf_ref[...]
    o_ref[...] = y


def _ada_spec(per_row, rows, tiles_per_seq):
    if per_row:
        return pl.BlockSpec((N_ADA, rows, D_MODEL), lambda i: (0, 0, 0))
    return pl.BlockSpec((None, N_ADA, D_MODEL), lambda i: (i // tiles_per_seq, 0, 0))


def _ffn_call(x, ada, nw, wup, wdn, nf, *, k0, per_row, final_norm, tm, seq_len):
    m = x.shape[0]
    kern = functools.partial(_ffn_kernel, k0=k0, per_row=per_row, final_norm=final_norm)
    return pl.pallas_call(
        kern,
        grid=(m // tm,),
        in_specs=[pl.BlockSpec((tm, D_MODEL), lambda i: (i, 0)),
                  _ada_spec(per_row, m, seq_len // tm),
                  _resident((1, D_MODEL)),
                  _resident((D_MODEL, 2 * D_FF)),
                  _resident((D_FF, D_MODEL)),
                  _resident((1, D_MODEL))],
        out_specs=pl.BlockSpec((tm, D_MODEL), lambda i: (i, 0)),
        out_shape=jax.ShapeDtypeStruct((m, D_MODEL), F32),
        compiler_params=_params(("arbitrary",)),
        name="ffn",
    )(x, ada, nw, wup, wdn, nf)


TP = 512
NCH = TP // DN_CHUNK


WCAST_ROWS = 256


def _win_cast_kernel(w_ref, lo_ref, ab_ref):
    w = w_ref[...]
    lo_ref[...] = w[:, :AB_OFF].astype(BF16)
    lane = lax.broadcasted_iota(jnp.int32, (1, LANES), 1)
    ab_ref[...] = jnp.where(lane < 2 * DN_HEADS, w[:, AB_OFF:], 0.0).astype(BF16)


def _win_cast_call(w_in):
    k = w_in.shape[0]
    return pl.pallas_call(
        _win_cast_kernel,
        grid=(k // WCAST_ROWS,),
        in_specs=[pl.BlockSpec((WCAST_ROWS, AB_OFF + LANES), lambda i: (i, 0))],
        out_specs=[pl.BlockSpec((WCAST_ROWS, AB_OFF), lambda i: (i, 0)),
                   pl.BlockSpec((WCAST_ROWS, LANES), lambda i: (i, 0))],
        out_shape=[jax.ShapeDtypeStruct((k, AB_OFF), BF16), jax.ShapeDtypeStruct((k, LANES), BF16)],
        compiler_params=_params(("arbitrary",)),
        name="win_cast",
    )(w_in)


def _proj_col(h, wlo_ref, whi_ref, c):
    n_lo = AB_OFF // D_MODEL
    w_ref, c = (wlo_ref, c) if c < n_lo else (whi_ref, c - n_lo)
    return _dot(h, w_ref[:, c * D_MODEL:(c + 1) * D_MODEL])


def _inproj_kernel(x_ref, ada_ref, nw_ref, wlo_ref, whi_ref, wab_ref,
                   xr_ref, ggr_ref, qkv_ref, szg_ref, sga_ref, sgb_ref, gb_ref, *, per_row):
    h = _norm_mod(x_ref[...], nw_ref[...], _ada_rows(ada_ref, 3, per_row),
                  _ada_rows(ada_ref, 4, per_row)).astype(BF16)
    col = functools.partial(_proj_col, h, wlo_ref, whi_ref)
    xr_ref[...] = col(0)
    ggr_ref[...] = jax.nn.gelu(col(1)).astype(BF16)
    for c in range(3):
        qkv_ref[:, c * D_MODEL:(c + 1) * D_MODEL] = col(2 + c)
    szg_ref[...] = _silu(col(5)).astype(BF16)
    sga_ref[...] = _sigmoid(col(6)).astype(BF16)
    sgb_ref[...] = _sigmoid(col(7)).astype(BF16)
    gb_ref[...] = _dot(h, wab_ref[...])


def _inproj_call(x, ada, nw, w_lo, w_hi, w_ab, *, per_row, tm, seq_len):
    m = x.shape[0]
    row = lambda n: pl.BlockSpec((tm, n), lambda i: (i, 0))
    shp = lambda n, dt: jax.ShapeDtypeStruct((m, n), dt)
    return pl.pallas_call(
        functools.partial(_inproj_kernel, per_row=per_row),
        grid=(m // tm,),
        in_specs=[row(D_MODEL), _ada_spec(per_row, m, seq_len // tm),
                  _resident((1, D_MODEL)), _resident((D_MODEL, AB_OFF)), _resident((D_MODEL, N_HI)),
                  _resident((D_MODEL, LANES))],
        out_specs=[row(D_RNN), row(D_RNN), row(DN_QKV), row(DN_V), row(D_MODEL), row(D_MODEL), row(LANES)],
        out_shape=[shp(D_RNN, F32), shp(D_RNN, BF16), shp(DN_QKV, F32), shp(DN_V, BF16),
                   shp(D_MODEL, BF16), shp(D_MODEL, BF16), shp(LANES, F32)],
        compiler_params=_params(("arbitrary",)),
        name="inproj",
    )(x, ada, nw, w_lo, w_hi, w_ab)


def _causal_conv(x, carry, cw_ref, lanes):
    t, c = x.shape
    x3 = x.reshape(t // SUBLANES, SUBLANES, c)
    prev8 = carry[:, lanes]
    sub = lax.broadcasted_iota(jnp.int32, (1, SUBLANES, 1), 1)
    acc = cw_ref[CONV_W - 1:CONV_W, lanes][None] * x3
    for j in range(1, CONV_W):
        r = pltpu.roll(x3, j, axis=1)
        rp = jnp.concatenate([pltpu.roll(prev8, j, axis=0)[None], r[:-1]], axis=0)
        acc = acc + cw_ref[CONV_W - 1 - j:CONV_W - j, lanes][None] * jnp.where(sub >= j, r, rp)
    carry[:, lanes] = x[t - SUBLANES:, :]
    return acc.reshape(t, c)


def _rg_gates(xc, wa_ref, ba, wx_ref, bx, lam):
    ra, ix = [], []
    for n in range(RG_BLOCKS):
        xb = xc[:, n * RG_BLOCK_W:(n + 1) * RG_BLOCK_W].astype(BF16)
        ra.append(_dot(xb, wa_ref[n]))
        ix.append(_dot(xb, wx_ref[n]))
    r = _sigmoid(jnp.concatenate(ra, axis=1) + ba)
    i = _sigmoid(jnp.concatenate(ix, axis=1) + bx)
    log_a = (-RG_C) * r * _softplus(-lam)
    return jnp.exp(log_a), log_a, i


MIX_CONV_LANES = 512


def _mixin_kernel(x_ref, ada_ref, nw_ref, wlo_ref, whi_ref, wab_ref,
                  cw_ref, cb_ref, wa_ref, ba_ref, wx_ref, bx_ref, lam_ref, cqw_ref,
                  oa_ref, act_ref, szg_ref, sga_ref, sgb_ref, gb_ref, hl_ref, cs_ref, cq_ref,
                  xcar, qcar, hcar, acum_s, hloc_s, hin_s):
    t = pl.program_id(1)

    @pl.when(t == 0)
    def _():
        xcar[...] = jnp.zeros(xcar.shape, F32)
        qcar[...] = jnp.zeros(qcar.shape, F32)
        hcar[...] = jnp.zeros(hcar.shape, F32)

    h = _norm_mod(x_ref[...], nw_ref[...], ada_ref[3:4, :], ada_ref[4:5, :]).astype(BF16)

    col = functools.partial(_proj_col, h, wlo_ref, whi_ref)
    groups = TP // SUBLANES
    sub = lax.broadcasted_iota(jnp.int32, (1, SUBLANES, 1), 1)
    row = lax.broadcasted_iota(jnp.int32, (TP, 1), 0)
    first = jnp.logical_and(row == 0, t == 0)

    def rglru_block(xr, n):
        ln = slice(n * RG_BLOCK_W, (n + 1) * RG_BLOCK_W)
        xc = _causal_conv(xr[:, ln], xcar, cw_ref, ln) + cb_ref[:, ln]
        xb = xc.astype(BF16)
        r = _sigmoid(_dot(xb, wa_ref[n]) + ba_ref[:, ln])
        i = _sigmoid(_dot(xb, wx_ref[n]) + bx_ref[:, ln])
        log_a = (-RG_C) * r * _softplus(-lam_ref[:, ln])
        a = jnp.exp(log_a)
        mult = jnp.where(first, 1.0, jnp.sqrt(_neg_expm1(2.0 * log_a)))
        a3 = a.reshape(groups, SUBLANES, RG_BLOCK_W)
        b3 = (mult * (i * xc)).reshape(groups, SUBLANES, RG_BLOCK_W)
        d = 1
        while d < SUBLANES:
            keep = sub >= d
            a_prev = jnp.where(keep, pltpu.roll(a3, d, axis=1), 1.0)
            b_prev = jnp.where(keep, pltpu.roll(b3, d, axis=1), 0.0)
            b3 = a3 * b_prev + b3
            a3 = a3 * a_prev
            d *= 2
        acum_s[:, ln] = a3.reshape(TP, RG_BLOCK_W)
        hloc_s[:, ln] = b3.reshape(TP, RG_BLOCK_W)

    def qkv_part(z, c):
        for l0 in range(0, D_MODEL, MIX_CONV_LANES):
            ln = slice(c * D_MODEL + l0, c * D_MODEL + l0 + MIX_CONV_LANES)
            act_ref[:, ln] = _silu(_causal_conv(z[:, l0:l0 + MIX_CONV_LANES], qcar, cqw_ref, ln))

    xr = col(0)
    for n in range(RG_BLOCKS):
        rglru_block(xr, n)
    cs_ref[0] = xcar[SUBLANES - (CONV_W - 1):, :]
    for c in range(DN_QKV // D_MODEL):
        qkv_part(col(2 + c), c)
    cq_ref[0] = qcar[SUBLANES - (CONV_W - 1):, :]

    hrow = hcar[0:1, :]
    for g in range(groups):
        hin_s[g * SUBLANES:(g + 1) * SUBLANES, :] = jnp.broadcast_to(hrow, (SUBLANES, D_RNN))
        e = (g + 1) * SUBLANES - 1
        hrow = acum_s[e:e + 1, :] * hrow + hloc_s[e:e + 1, :]
    hcar[0:1, :] = hrow
    hl_ref[0] = hrow

    y = hloc_s[...] + acum_s[...] * hin_s[...]
    oa_ref[...] = (y * jax.nn.gelu(col(1))).astype(BF16)
    szg_ref[...] = _silu(col(5)).astype(BF16)
    sga_ref[...] = _sigmoid(col(6)).astype(BF16)
    sgb_ref[...] = _sigmoid(col(7)).astype(BF16)
    gb_ref[...] = _dot(h, wab_ref[...])


def _mixin_call(x, ada, nw, w_lo, w_hi, w_ab, cw, cb, wa, ba, wx, bx, lam, cqw, *, batch, seq_len):
    nt = seq_len // TP
    m = batch * seq_len
    row = lambda n: pl.BlockSpec((TP, n), lambda b, t: (b * nt + t, 0))
    per_seq = lambda r, n: pl.BlockSpec((1, r, n), lambda b, t: (b, 0, 0))
    shp = lambda n, dt: jax.ShapeDtypeStruct((m, n), dt)
    blk = (RG_BLOCKS, RG_BLOCK_W, RG_BLOCK_W)
    return pl.pallas_call(
        _mixin_kernel,
        grid=(batch, nt),
        in_specs=[row(D_MODEL), pl.BlockSpec((None, N_ADA, D_MODEL), lambda b, t: (b, 0, 0)),
                  _resident((1, D_MODEL)), _resident((D_MODEL, AB_OFF)), _resident((D_MODEL, N_HI)),
                  _resident((D_MODEL, LANES)),
                  _resident((CONV_W, D_RNN)), _resident((1, D_RNN)), _resident(blk), _resident((1, D_RNN)),
                  _resident(blk), _resident((1, D_RNN)), _resident((1, D_RNN)), _resident((CONV_W, DN_QKV))],
        out_specs=[row(D_RNN), row(DN_QKV), row(DN_V), row(D_MODEL), row(D_MODEL), row(LANES),
                   per_seq(1, D_RNN), per_seq(CONV_W - 1, D_RNN), per_seq(CONV_W - 1, DN_QKV)],
        out_shape=[shp(D_RNN, BF16), shp(DN_QKV, F32), shp(DN_V, BF16), shp(D_MODEL, BF16), shp(D_MODEL, BF16),
                   shp(LANES, F32),
                   jax.ShapeDtypeStruct((batch, 1, D_RNN), F32),
                   jax.ShapeDtypeStruct((batch, CONV_W - 1, D_RNN), F32),
                   jax.ShapeDtypeStruct((batch, CONV_W - 1, DN_QKV), F32)],
        scratch_shapes=[pltpu.VMEM((SUBLANES, D_RNN), F32),
                        pltpu.VMEM((SUBLANES, DN_QKV), F32),
                        pltpu.VMEM((SUBLANES, D_RNN), F32),
                        pltpu.VMEM((TP, D_RNN), F32),
                        pltpu.VMEM((TP, D_RNN), F32),
                        pltpu.VMEM((TP, D_RNN), F32)],
        compiler_params=_params(("arbitrary", "arbitrary")),
        name="mixin",
    )(x, ada, nw, w_lo, w_hi, w_ab, cw, cb, wa, ba, wx, bx, lam, cqw)


def _split(x):
    hi = x.astype(BF16)
    return hi, (x - hi.astype(F32)).astype(BF16)


def _dot3(a_hi, a_lo, b):
    n = b.shape[1]
    b_hi, b_lo = _split(b)
    r = _dot(a_hi, jnp.concatenate([b_hi, b_lo], axis=1))
    return r[:, :n] + r[:, n:] + _dot(a_lo, b_hi)


def _pair_mask(ti, tj, lvl):
    same = (ti >> (lvl + 1)) == (tj >> (lvl + 1))
    return jnp.logical_and(same, jnp.logical_and(((ti >> lvl) & 1) == 1, ((tj >> lvl) & 1) == 0))


DELTA_CPI = 4


def _delta_kernel(act_s, gb_ref, szg_ref, alog_ref, dtb_ref, dnw_ref,
                  ob_ref, s_ref,
                  s_acc, o_s, u_s, wq_s, attn_s, kd_s, egl_s, *, nt):
    t = pl.program_id(1)
    c = DN_CHUNK
    heads = range(DN_HEADS)

    @pl.when(t == 0)
    def _():
        s_acc[...] = jnp.zeros(s_acc.shape, F32)

    ti = lax.broadcasted_iota(jnp.int32, (c, 2 * c), 0)
    tj = lax.broadcasted_iota(jnp.int32, (c, 2 * c), 1)
    left = tj < c
    incl = jnp.logical_and(ti >= tj, left)
    strict = jnp.logical_and(ti > tj, left)
    eye_l = (ti == tj).astype(F32)
    cum_l = incl[:, :c].astype(F32)
    cum_r = jnp.logical_and(ti <= tj, left).astype(F32)
    last = c - 1
    zpad = jnp.zeros((c, DN_DK), BF16)
    neg_a = -jnp.exp(alog_ref[...])
    dtb = dtb_ref[...]

    def chunk_rows(ref, j, lanes):
        return ref[pl.ds(pl.multiple_of(j * c, c), c), lanes]

    def prep(it, _):
        js = [it * DELTA_CPI + k for k in range(DELTA_CPI)]
        gcum, gcum_t, beta = [], [], []
        for j in js:
            gbv = chunk_rows(gb_ref, j, slice(None))
            g = neg_a * _softplus(gbv + dtb)
            beta.append(_sigmoid(gbv))
            gcum.append(jnp.dot(cum_l, g, precision=HI, preferred_element_type=F32))
            gcum_t.append(_dot_tn(g, cum_r, precision=HI))
        chains = [(k, h) for k in range(DELTA_CPI) for h in heads]
        gc, decay, qn, kn, kn16, kb, vb = {}, {}, {}, {}, {}, {}, {}
        for k, h in chains:
            j = js[k]
            gc[k, h] = gcum[k][:, h:h + 1]
            decay[k, h] = jnp.exp(jnp.where(incl, gc[k, h] - gcum_t[k][h:h + 1, :], -jnp.inf))
            bh = beta[k][:, DN_HEADS + h:DN_HEADS + h + 1]
            qh = chunk_rows(act_s, j, slice(h * DN_DK, (h + 1) * DN_DK))
            kh = chunk_rows(act_s, j, slice(DN_QK + h * DN_DK, DN_QK + (h + 1) * DN_DK))
            vh = chunk_rows(act_s, j, slice(2 * DN_QK + h * DN_DV, 2 * DN_QK + (h + 1) * DN_DV))
            qn[k, h] = qh * lax.rsqrt(jnp.sum(qh * qh, axis=-1, keepdims=True) + EPS) * (DN_DK ** -0.5)
            kn[k, h] = kh * lax.rsqrt(jnp.sum(kh * kh, axis=-1, keepdims=True) + EPS)
            kn16[k, h] = kn[k, h].astype(BF16)
            kb[k, h] = kn[k, h] * bh
            vb[k, h] = vh * bh
        lm = {}
        for ch in chains:
            kq = _dot_nt(jnp.concatenate([kb[ch], qn[ch]], axis=0).astype(BF16),
                         jnp.concatenate([kn16[ch], zpad], axis=0))
            lm[ch] = jnp.where(strict, kq[:c] * decay[ch], 0.0)
            attn_s[js[ch[0]], ch[1]] = (kq[c:] * decay[ch])[:, :c].astype(BF16)
        d = {ch: eye_l - jnp.where(_pair_mask(ti, tj, 0), lm[ch], 0.0) for ch in chains}
        for lvl in range(1, c.bit_length() - 1):
            pair = _pair_mask(ti, tj, lvl)
            d16 = {ch: d[ch].astype(BF16) for ch in chains}
            ed = {ch: _dot(jnp.where(pair, lm[ch], 0.0).astype(BF16)[:, :c], d16[ch]) for ch in chains}
            d = {ch: d[ch] - _dot(d16[ch][:, :c], ed[ch].astype(BF16)) for ch in chains}
        eg = {ch: jnp.exp(gc[ch]) for ch in chains}
        a16 = {ch: d[ch].astype(BF16)[:, :c] for ch in chains}
        rhs = {ch: jnp.concatenate([vb[ch], kb[ch] * eg[ch]], axis=1) for ch in chains}
        x0 = {ch: _dot(a16[ch], rhs[ch].astype(BF16)) for ch in chains}
        res = {}
        for ch in chains:
            l_hi, l_lo = _split(lm[ch])
            res[ch] = rhs[ch] - x0[ch] - _dot3(l_hi[:, :c], l_lo[:, :c], x0[ch])
        for ch in chains:
            k, h = ch
            x = x0[ch] + _dot(a16[ch], res[ch].astype(BF16))
            u_s[js[k], h] = x[:, :DN_DV]
            wq_s[js[k], h] = jnp.concatenate([x[:, DN_DV:], qn[ch] * eg[ch]], axis=0).astype(BF16)
            gl = gcum[k][last:last + 1, h:h + 1]
            kd_s[js[k], h] = (kn[ch] * jnp.exp(gl - gc[ch])).astype(BF16)
        for k, j in enumerate(js):
            egl_s[pl.ds(pl.multiple_of(j * SUBLANES, SUBLANES), SUBLANES), :] = jnp.broadcast_to(
                jnp.exp(gcum[k][last:last + 1, :]), (SUBLANES, LANES))
        return 0

    lax.fori_loop(0, NCH // DELTA_CPI, prep, 0)

    def recur(j, _):
        egl = egl_s[pl.ds(pl.multiple_of(j * SUBLANES, SUBLANES), 1), :]
        s_old = [s_acc[h] for h in heads]
        wq = [_dot(wq_s[j, h], s_old[h].astype(BF16)) for h in heads]
        vnew16 = [(u_s[j, h] - wq[h][:c]).astype(BF16) for h in heads]
        o = [wq[h][c:] + _dot(attn_s[j, h], vnew16[h]) for h in heads]
        for h in heads:
            s_acc[h] = s_old[h] * egl[:, h:h + 1] + _dot_tn(kd_s[j, h], vnew16[h])
        for h in heads:
            on = o[h] * lax.rsqrt(jnp.mean(o[h] * o[h], axis=-1, keepdims=True) + EPS) * dnw_ref[...]
            o_s[pl.ds(pl.multiple_of(j * c, c), c), h * DN_DV:(h + 1) * DN_DV] = on
        return 0

    lax.fori_loop(0, NCH, recur, 0)

    ob_ref[...] = (o_s[...] * szg_ref[...].astype(F32)).astype(BF16)

    @pl.when(t == nt - 1)
    def _():
        s_ref[0] = s_acc[...]


def _delta_call(act, gb, szg, alog, dtb, dnw, *, batch, seq_len):
    c = DN_CHUNK
    nt = seq_len // TP
    row = lambda n: pl.BlockSpec((TP, n), lambda b, t: (b * nt + t, 0))
    return pl.pallas_call(
        functools.partial(_delta_kernel, nt=nt),
        grid=(batch, nt),
        in_specs=[row(DN_QKV), row(LANES), row(DN_V),
                  _resident((1, LANES)), _resident((1, LANES)), _resident((1, DN_DV))],
        out_specs=[row(DN_V),
                   pl.BlockSpec((1, DN_HEADS, DN_DK, DN_DV), lambda b, t: (b, 0, 0, 0))],
        out_shape=[jax.ShapeDtypeStruct((batch * seq_len, DN_V), BF16),
                   jax.ShapeDtypeStruct((batch, DN_HEADS, DN_DK, DN_DV), F32)],
        scratch_shapes=[pltpu.VMEM((DN_HEADS, DN_DK, DN_DV), F32),
                        pltpu.VMEM((TP, DN_V), F32),
                        pltpu.VMEM((NCH, DN_HEADS, c, DN_DV), F32),
                        pltpu.VMEM((NCH, DN_HEADS, 2 * c, DN_DK), BF16),
                        pltpu.VMEM((NCH, DN_HEADS, c, c), BF16),
                        pltpu.VMEM((NCH, DN_HEADS, c, DN_DK), BF16),
                        pltpu.VMEM((NCH * SUBLANES, LANES), F32)],
        compiler_params=_params(("arbitrary", "arbitrary")),
        name="delta",
    )(act, gb, szg, alog, dtb, dnw)


def _sample_pre_kernel(xr_ref, ggr_ref, qkv_ref, gb_ref, crnn_ref, cqkv_ref, h0_ref,
                       cw_ref, cb_ref, wa_ref, ba_ref, wx_ref, bx_ref, lam_ref,
                       cqw_ref, alog_ref, dtb_ref,
                       oa_ref, hn_ref, qn_ref, kn_ref, v_ref, eg_ref, beta_ref):
    xc = cb_ref[...] + cw_ref[CONV_W - 1:CONV_W, :] * xr_ref[...]
    for j in range(CONV_W - 1):
        xc = xc + cw_ref[j:j + 1, :] * crnn_ref[j]
    a, log_a, i = _rg_gates(xc, wa_ref, ba_ref[...], wx_ref, bx_ref[...], lam_ref[...])
    mult = jnp.sqrt(_neg_expm1(2.0 * log_a))
    h = a * h0_ref[...] + mult * (i * xc)
    hn_ref[...] = h
    oa_ref[...] = (h * ggr_ref[...].astype(F32)).astype(BF16)

    qc = cqw_ref[CONV_W - 1:CONV_W, :] * qkv_ref[...]
    for j in range(CONV_W - 1):
        qc = qc + cqw_ref[j:j + 1, :] * cqkv_ref[j]
    act = _silu(qc)
    for h_ in range(DN_HEADS):
        qh = act[:, h_ * DN_DK:(h_ + 1) * DN_DK]
        kh = act[:, DN_QK + h_ * DN_DK:DN_QK + (h_ + 1) * DN_DK]
        qn_ref[:, h_ * DN_DK:(h_ + 1) * DN_DK] = (
            qh * lax.rsqrt(jnp.sum(qh * qh, axis=-1, keepdims=True) + EPS) * (DN_DK ** -0.5))
        kn_ref[:, h_ * DN_DK:(h_ + 1) * DN_DK] = (
            kh * lax.rsqrt(jnp.sum(kh * kh, axis=-1, keepdims=True) + EPS))
    v_ref[...] = act[:, 2 * DN_QK:]
    gbv = gb_ref[...]
    eg_ref[...] = jnp.exp(-jnp.exp(alog_ref[...]) * _softplus(gbv + dtb_ref[...]))
    beta_ref[...] = _sigmoid(gbv)


def _sample_pre_call(xr, ggr, qkv, gb, crnn, cqkv, h0, cw, cb, wa, ba, wx, bx, lam, cqw, alog, dtb):
    n = xr.shape[0]
    args = (xr, ggr, qkv, gb, crnn, cqkv, h0, cw, cb, wa, ba, wx, bx, lam, cqw, alog, dtb)
    shp = lambda w: jax.ShapeDtypeStruct((n, w), F32)
    outs = [jax.ShapeDtypeStruct((n, D_RNN), BF16), shp(D_RNN), shp(DN_QK), shp(DN_QK), shp(DN_V), shp(LANES), shp(LANES)]
    return pl.pallas_call(
        _sample_pre_kernel,
        grid=(1,),
        in_specs=[_resident(a.shape) for a in args],
        out_specs=[pl.BlockSpec(o.shape, lambda i: (0, 0)) for o in outs],
        out_shape=outs,
        compiler_params=_params(("arbitrary",)),
        name="sample_pre",
    )(*args)


def _sample_state_kernel(qn_ref, kn_ref, v_ref, eg_ref, beta_ref, szg_ref, dnw_ref, s_ref,
                         ob_ref, sn_ref):
    srow = lax.broadcasted_iota(jnp.int32, (SUBLANES, DN_DK), 0)
    rows = lambda r0, r1: jnp.where(srow == 0, r0, jnp.where(srow == 1, r1, 0.0))
    chains = [(b, h) for b in range(SAMPLE_BB) for h in range(DN_HEADS)]
    sl = lambda h: slice(h * DN_DK, (h + 1) * DN_DK)
    q = {(b, h): qn_ref[0, b:b + 1, sl(h)] for b, h in chains}
    k = {(b, h): kn_ref[0, b:b + 1, sl(h)] for b, h in chains}
    eg = {(b, h): eg_ref[0, b:b + 1, h:h + 1] for b, h in chains}
    beta = {(b, h): beta_ref[0, b:b + 1, DN_HEADS + h:DN_HEADS + h + 1] for b, h in chains}
    ws_qs = {ch: _dot(rows((k[ch] * beta[ch]) * eg[ch], q[ch] * eg[ch]).astype(BF16),
                      s_ref[ch[0], ch[1]].astype(BF16)) for ch in chains}
    vnew = {(b, h): v_ref[0, b:b + 1, sl(h)] * beta[b, h] - ws_qs[b, h][0:1, :] for b, h in chains}
    for ch in chains:
        b, h = ch
        zero = jnp.zeros_like(k[ch])
        sn_ref[b, h] = s_ref[b, h] * eg[ch] + _dot_tn(rows(k[ch], zero).astype(BF16),
                                                     rows(vnew[ch], zero).astype(BF16))
    for ch in chains:
        b, h = ch
        o = ws_qs[ch][1:2, :] + jnp.sum(q[ch] * k[ch], axis=-1, keepdims=True) * vnew[ch]
        on = o * lax.rsqrt(jnp.mean(o * o, axis=-1, keepdims=True) + EPS) * dnw_ref[...]
        ob_ref[0, b:b + 1, sl(h)] = on * szg_ref[0, b:b + 1, sl(h)]


SAMPLE_BB = 8


def _sample_state_call(qn, kn, v, eg, beta, szg, dnw, s0):
    n = qn.shape[0]
    bb = SAMPLE_BB
    r3 = lambda a: a.reshape(n // bb, bb, a.shape[-1])
    vec = lambda w: pl.BlockSpec((1, bb, w), lambda i: (i, 0, 0))
    s_spec = pl.BlockSpec((bb, DN_HEADS, DN_DK, DN_DV), lambda i: (i, 0, 0, 0))
    ob, sn = pl.pallas_call(
        _sample_state_kernel,
        grid=(n // bb,),
        in_specs=[vec(DN_QK), vec(DN_QK), vec(DN_V), vec(LANES), vec(LANES), vec(DN_V),
                  _resident((1, DN_DV)), s_spec],
        out_specs=[vec(DN_V), s_spec],
        out_shape=[jax.ShapeDtypeStruct((n // bb, bb, DN_V), F32),
                   jax.ShapeDtypeStruct((n, DN_HEADS, DN_DK, DN_DV), F32)],
        compiler_params=_params(("arbitrary",)),
        name="sample_state",
    )(r3(qn), r3(kn), r3(v), r3(eg), r3(beta), r3(szg), dnw, s0)
    return ob.reshape(n, DN_V), sn


def _merge_kernel(x_ref, oa_ref, ob_ref, sga_ref, sgb_ref, ada_ref, wb_ref, wo_ref, o_ref, *, per_row):
    ya = _dot(oa_ref[...], wb_ref[0].astype(BF16))
    yb = _dot(ob_ref[...], wb_ref[1].astype(BF16))
    merged = (sga_ref[...].astype(F32) * ya + sgb_ref[...].astype(F32) * yb).astype(BF16)
    o_ref[...] = x_ref[...] + _ada_rows(ada_ref, 5, per_row) * _dot(merged, wo_ref[...].astype(BF16))


def _merge_call(x, oa, ob, sga, sgb, ada, wb, wo, *, per_row, tm, seq_len):
    m = x.shape[0]
    row = pl.BlockSpec((tm, D_MODEL), lambda i: (i, 0))
    return pl.pallas_call(
        functools.partial(_merge_kernel, per_row=per_row),
        grid=(m // tm,),
        in_specs=[row, row, row, row, row, _ada_spec(per_row, m, seq_len // tm),
                  _resident((2, D_RNN, D_MODEL)), _resident((D_MODEL, D_MODEL))],
        out_specs=row,
        out_shape=jax.ShapeDtypeStruct((m, D_MODEL), F32),
        compiler_params=_params(("arbitrary",)),
        name="merge",
    )(x, oa, ob, sga, sgb, ada, wb, wo)


def kernel(x_prompt, x_sample, c_prompt, c_sample, state_rglru_h, state_rglru_conv, state_delta_S, state_delta_conv, w_ada, b_ada, norm_ffn1, w_ffn1_up, w_ffn1_down, norm_mix, w_in, conv_rnn_w, conv_rnn_b, rg_w_a, rg_b_a, rg_w_x, rg_b_x, rg_lambda, conv_qkv_w, dn_a_log, dn_dt_bias, dn_norm, w_branch, w_out, norm_ffn2, w_ffn2_up, w_ffn2_down, norm_final):
    batch, seq_len, _ = x_prompt.shape
    n_dec = x_sample.shape[0]
    assert w_ada.shape[0] == 1 and x_sample.shape[1] == 1 and seq_len % TP == 0

    row = lambda a: a.reshape(1, -1).astype(F32)
    wup1, wdn1 = w_ffn1_up[0], w_ffn1_down[0]
    wup2, wdn2 = w_ffn2_up[0], w_ffn2_down[0]
    w_lo, w_ab = _win_cast_call(w_in[0])
    w_hi = w_in[0][:, AB_OFF + 2 * DN_HEADS:].astype(BF16)
    wb, wo = w_branch[0], w_out[0]
    wa, wx = rg_w_a[0].astype(BF16), rg_w_x[0].astype(BF16)
    lane_pad = lambda a: jnp.pad(a.reshape(1, -1).astype(F32), ((0, 0), (0, LANES - a.size)))
    alog = lane_pad(dn_a_log[0])
    dtb = lane_pad(dn_dt_bias[0])
    nf = row(norm_final)

    ada = _ada_call(jnp.concatenate([c_prompt, c_sample], axis=0), w_ada[0], row(b_ada[0]))
    ada = ada.reshape(batch + n_dec, N_ADA, D_MODEL)
    ada_p = ada[:batch]
    ada_s = jnp.transpose(ada[batch:], (1, 0, 2))

    mixer_w = (conv_rnn_w[0], row(conv_rnn_b[0]), wa, row(rg_b_a[0]), wx, row(rg_b_x[0]), row(rg_lambda[0]))

    kw = dict(per_row=False, seq_len=seq_len, tm=TP)
    xp = x_prompt.reshape(batch * seq_len, D_MODEL)
    xp = _ffn_call(xp, ada_p, row(norm_ffn1[0]), wup1, wdn1, nf, k0=0, final_norm=False, **kw)
    oa, act, szg, sga, sgb, gb, hp, cp, qp = _mixin_call(
        xp, ada_p, row(norm_mix[0]), w_lo, w_hi, w_ab, *mixer_w, conv_qkv_w[0], batch=batch, seq_len=seq_len)
    ob, sp = _delta_call(act, gb, szg, alog, dtb, row(dn_norm[0]), batch=batch, seq_len=seq_len)
    xp = _merge_call(xp, oa, ob, sga, sgb, ada_p, wb, wo, **kw)
    y_prompt = _ffn_call(xp, ada_p, row(norm_ffn2[0]), wup2, wdn2, nf, k0=6, final_norm=True, **kw)

    kw = dict(per_row=True, seq_len=n_dec, tm=n_dec)
    xs = x_sample.reshape(n_dec, D_MODEL)
    xs = _ffn_call(xs, ada_s, row(norm_ffn1[0]), wup1, wdn1, nf, k0=0, final_norm=False, **kw)
    xr_s, ggr_s, qkv_s, szg_s, sga_s, sgb_s, gb_s = _inproj_call(xs, ada_s, row(norm_mix[0]), w_lo, w_hi, w_ab, **kw)
    crnn = jnp.transpose(state_rglru_conv[0], (1, 0, 2))
    cqkv = jnp.transpose(state_delta_conv[0], (1, 0, 2))
    oa_s, hs, qn, kn, v, eg, beta = _sample_pre_call(
        xr_s, ggr_s, qkv_s, gb_s, crnn, cqkv, state_rglru_h[0], *mixer_w, conv_qkv_w[0], alog, dtb)
    ob_s, ss = _sample_state_call(qn, kn, v, eg, beta, szg_s.astype(F32), row(dn_norm[0]), state_delta_S[0])
    xs = _merge_call(xs, oa_s, ob_s.astype(BF16), sga_s, sgb_s, ada_s, wb, wo, **kw)
    y_sample = _ffn_call(xs, ada_s, row(norm_ffn2[0]), wup2, wdn2, nf, k0=6, final_norm=True, **kw)
    cs = jnp.concatenate([state_rglru_conv[0][:, 1:], xr_s[:, None, :]], axis=1)
    qs = jnp.concatenate([state_delta_conv[0][:, 1:], qkv_s[:, None, :]], axis=1)

    return (y_prompt.reshape(batch, seq_len, D_MODEL), y_sample.reshape(n_dec, 1, D_MODEL),
            hp.reshape(1, batch, D_RNN), cp[None], sp[None], qp[None],
            hs[None], cs[None], ss[None], qs[None])
```

```python
import functools

import jax
import jax.numpy as jnp
from jax import lax
from jax.experimental import pallas as pl
from jax.experimental.pallas import tpu as pltpu

D_MODEL = 1024
D_RNN = 1024
RG_BLOCKS = 8
RG_BLOCK_W = D_RNN // RG_BLOCKS
RG_C = 8.0
CONV_W = 4
DN_HEADS = 8
DN_DK = 128
DN_DV = 128
DN_QK = DN_HEADS * DN_DK
DN_V = DN_HEADS * DN_DV
DN_QKV = 2 * DN_QK + DN_V
DN_CHUNK = 64
D_FF = 2816
N_ADA = 9
EPS = 1e-6
LANES = 128
SUBLANES = 8
AB_OFF = D_RNN * 2 + DN_QKV
N_HI = 3 * D_MODEL

BF16 = jnp.bfloat16
F32 = jnp.float32
HI = lax.Precision.HIGHEST

VMEM_LIMIT = 58 * 1024 * 1024


def _params(sem):
    return pltpu.CompilerParams(dimension_semantics=sem, vmem_limit_bytes=VMEM_LIMIT)


def _resident(shape):
    nd = len(shape)
    return pl.BlockSpec(shape, lambda *_: (0,) * nd, pipeline_mode=pl.Buffered(1))


def _dot(a, b):
    return jnp.dot(a, b, preferred_element_type=F32)


def _dot_nt(a, b, precision=None):
    return lax.dot_general(a, b, (((1,), (1,)), ((), ())), precision=precision,
                           preferred_element_type=F32)


def _dot_tn(a, b, precision=None):
    return lax.dot_general(a, b, (((0,), (0,)), ((), ())), precision=precision,
                           preferred_element_type=F32)


def _sigmoid(x):
    return 0.5 * jnp.tanh(0.5 * x) + 0.5


def _silu(x):
    h = 0.5 * x
    return h + h * jnp.tanh(h)


def _softplus(x):
    return jnp.maximum(x, 0.0) + jnp.log1p(jnp.exp(-jnp.abs(x)))


def _neg_expm1(x):
    return -jnp.tanh(0.5 * x) * (jnp.exp(x) + 1.0)


def _ada_rows(ada_ref, k, per_row):
    if per_row:
        return ada_ref[k]
    return ada_ref[k:k + 1, :]


def _norm_mod(x, nw, shift, scale):
    ms = jnp.mean(x * x, axis=-1, keepdims=True)
    hn = x * lax.rsqrt(ms + EPS) * nw
    return hn * (1.0 + scale) + shift


def _ada_kernel(c_ref, w_ref, b_ref, o_ref):
    o_ref[...] = _dot(c_ref[...].astype(BF16), w_ref[...].astype(BF16)) + b_ref[...]


def _ada_call(c_all, w_ada, b_ada):
    n = c_all.shape[0]
    tn = D_MODEL
    return pl.pallas_call(
        _ada_kernel,
        grid=(N_ADA * D_MODEL // tn,),
        in_specs=[pl.BlockSpec((n, D_MODEL), lambda j: (0, 0)),
                  pl.BlockSpec((D_MODEL, tn), lambda j: (0, j)),
                  pl.BlockSpec((1, tn), lambda j: (0, j))],
        out_specs=pl.BlockSpec((n, tn), lambda j: (0, j)),
        out_shape=jax.ShapeDtypeStruct((n, N_ADA * D_MODEL), F32),
        compiler_params=_params(("arbitrary",)),
        name="ada",
    )(c_all, w_ada, b_ada)


FFN_TF = 256


def _ffn_kernel(x_ref, ada_ref, nw_ref, wup_ref, wdn_ref, nf_ref, o_ref, *, k0, per_row, final_norm):
    x = x_ref[...]
    h = _norm_mod(x, nw_ref[...], _ada_rows(ada_ref, k0, per_row),
                  _ada_rows(ada_ref, k0 + 1, per_row)).astype(BF16)
    acc = jnp.zeros(x.shape, F32)
    for j in range(D_FF // FFN_TF):
        g = _dot(h, wup_ref[:, j * FFN_TF:(j + 1) * FFN_TF].astype(BF16))
        v = _dot(h, wup_ref[:, D_FF + j * FFN_TF:D_FF + (j + 1) * FFN_TF].astype(BF16))
        a = (_silu(g) * v).astype(BF16)
        acc = acc + _dot(a, wdn_ref[j * FFN_TF:(j + 1) * FFN_TF, :].astype(BF16))
    y = x + 0.5 * _ada_rows(ada_ref, k0 + 2, per_row) * acc
    if final_norm:
        ms = jnp.mean(y * y, axis=-1, keepdims=True)
        y = y * lax.rsqrt(ms + EPS) * nf_ref[...]
    o_ref[...] = y


def _ada_spec(per_row, rows, tiles_per_seq):
    if per_row:
        return pl.BlockSpec((N_ADA, rows, D_MODEL), lambda i: (0, 0, 0))
    return pl.BlockSpec((None, N_ADA, D_MODEL), lambda i: (i // tiles_per_seq, 0, 0))


def _ffn_call(x, ada, nw, wup, wdn, nf, *, k0, per_row, final_norm, tm, seq_len):
    m = x.shape[0]
    kern = functools.partial(_ffn_kernel, k0=k0, per_row=per_row, final_norm=final_norm)
    return pl.pallas_call(
        kern,
        grid=(m // tm,),
        in_specs=[pl.BlockSpec((tm, D_MODEL), lambda i: (i, 0)),
                  _ada_spec(per_row, m, seq_len // tm),
                  _resident((1, D_MODEL)),
                  _resident((D_MODEL, 2 * D_FF)),
                  _resident((D_FF, D_MODEL)),
                  _resident((1, D_MODEL))],
        out_specs=pl.BlockSpec((tm, D_MODEL), lambda i: (i, 0)),
        out_shape=jax.ShapeDtypeStruct((m, D_MODEL), F32),
        compiler_params=_params(("arbitrary",)),
        name="ffn",
    )(x, ada, nw, wup, wdn, nf)


TP = 512
NCH = TP // DN_CHUNK


WCAST_ROWS = 256


def _win_cast_kernel(wa_ref, wb_ref, lo_ref, hi_ref, ab_ref):
    lo_ref[...] = wa_ref[...].astype(BF16)
    wb = wb_ref[...]
    n_ab = 2 * DN_HEADS
    lane = lax.broadcasted_iota(jnp.int32, (1, LANES), 1)
    ab_ref[...] = jnp.where(lane < n_ab, wb[:, :LANES], 0.0).astype(BF16)
    hi_ref[...] = pltpu.roll(wb, AB_OFF - n_ab, axis=1)[:, :N_HI].astype(BF16)


def _win_cast_call(w_in):
    k, d_in = w_in.shape
    assert AB_OFF < d_in <= 2 * AB_OFF and d_in - AB_OFF - 2 * DN_HEADS == N_HI
    return pl.pallas_call(
        _win_cast_kernel,
        grid=(k // WCAST_ROWS,),
        in_specs=[pl.BlockSpec((WCAST_ROWS, AB_OFF), lambda i: (i, 0)),
                  pl.BlockSpec((WCAST_ROWS, AB_OFF), lambda i: (i, 1))],
        out_specs=[pl.BlockSpec((WCAST_ROWS, AB_OFF), lambda i: (i, 0)),
                   pl.BlockSpec((WCAST_ROWS, N_HI), lambda i: (i, 0)),
                   pl.BlockSpec((WCAST_ROWS, LANES), lambda i: (i, 0))],
        out_shape=[jax.ShapeDtypeStruct((k, AB_OFF), BF16), jax.ShapeDtypeStruct((k, N_HI), BF16),
                   jax.ShapeDtypeStruct((k, LANES), BF16)],
        compiler_params=_params(("arbitrary",)),
        name="win_cast",
    )(w_in, w_in)


def _proj_col(h, wlo_ref, whi_ref, c):
    n_lo = AB_OFF // D_MODEL
    w_ref, c = (wlo_ref, c) if c < n_lo else (whi_ref, c - n_lo)
    return _dot(h, w_ref[:, c * D_MODEL:(c + 1) * D_MODEL])


def _inproj_kernel(x_ref, ada_ref, nw_ref, wlo_ref, whi_ref, wab_ref,
                   xr_ref, ggr_ref, qkv_ref, szg_ref, sga_ref, sgb_ref, gb_ref, *, per_row):
    h = _norm_mod(x_ref[...], nw_ref[...], _ada_rows(ada_ref, 3, per_row),
                  _ada_rows(ada_ref, 4, per_row)).astype(BF16)
    col = functools.partial(_proj_col, h, wlo_ref, whi_ref)
    xr_ref[...] = col(0)
    ggr_ref[...] = jax.nn.gelu(col(1)).astype(BF16)
    for c in range(3):
        qkv_ref[:, c * D_MODEL:(c + 1) * D_MODEL] = col(2 + c)
    szg_ref[...] = _silu(col(5)).astype(BF16)
    sga_ref[...] = _sigmoid(col(6)).astype(BF16)
    sgb_ref[...] = _sigmoid(col(7)).astype(BF16)
    gb_ref[...] = _dot(h, wab_ref[...])


def _inproj_call(x, ada, nw, w_lo, w_hi, w_ab, *, per_row, tm, seq_len):
    m = x.shape[0]
    row = lambda n: pl.BlockSpec((tm, n), lambda i: (i, 0))
    shp = lambda n, dt: jax.ShapeDtypeStruct((m, n), dt)
    return pl.pallas_call(
        functools.partial(_inproj_kernel, per_row=per_row),
        grid=(m // tm,),
        in_specs=[row(D_MODEL), _ada_spec(per_row, m, seq_len // tm),
                  _resident((1, D_MODEL)), _resident((D_MODEL, AB_OFF)), _resident((D_MODEL, N_HI)),
                  _resident((D_MODEL, LANES))],
        out_specs=[row(D_RNN), row(D_RNN), row(DN_QKV), row(DN_V), row(D_MODEL), row(D_MODEL), row(LANES)],
        out_shape=[shp(D_RNN, F32), shp(D_RNN, BF16), shp(DN_QKV, F32), shp(DN_V, BF16),
                   shp(D_MODEL, BF16), shp(D_MODEL, BF16), shp(LANES, F32)],
        compiler_params=_params(("arbitrary",)),
        name="inproj",
    )(x, ada, nw, w_lo, w_hi, w_ab)


def _causal_conv(x, carry, cw_ref, lanes):
    t, c = x.shape
    x3 = x.reshape(t // SUBLANES, SUBLANES, c)
    prev8 = carry[:, lanes]
    sub = lax.broadcasted_iota(jnp.int32, (1, SUBLANES, 1), 1)
    acc = cw_ref[CONV_W - 1:CONV_W, lanes][None] * x3
    for j in range(1, CONV_W):
        r = pltpu.roll(x3, j, axis=1)
        rp = jnp.concatenate([pltpu.roll(prev8, j, axis=0)[None], r[:-1]], axis=0)
        acc = acc + cw_ref[CONV_W - 1 - j:CONV_W - j, lanes][None] * jnp.where(sub >= j, r, rp)
    carry[:, lanes] = x[t - SUBLANES:, :]
    return acc.reshape(t, c)


def _rg_gates(xc, wa_ref, ba, wx_ref, bx, lam):
    ra, ix = [], []
    for n in range(RG_BLOCKS):
        xb = xc[:, n * RG_BLOCK_W:(n + 1) * RG_BLOCK_W].astype(BF16)
        ra.append(_dot(xb, wa_ref[n]))
        ix.append(_dot(xb, wx_ref[n]))
    r = _sigmoid(jnp.concatenate(ra, axis=1) + ba)
    i = _sigmoid(jnp.concatenate(ix, axis=1) + bx)
    log_a = (-RG_C) * r * _softplus(-lam)
    return jnp.exp(log_a), log_a, i


MIX_CONV_LANES = 512


def _mixin_kernel(x_ref, ada_ref, nw_ref, wlo_ref, whi_ref, wab_ref,
                  cw_ref, cb_ref, wa_ref, ba_ref, wx_ref, bx_ref, lam_ref, cqw_ref,
                  oa_ref, act_ref, szg_ref, sga_ref, sgb_ref, gb_ref, hl_ref, cs_ref, cq_ref,
                  xcar, qcar, hcar, acum_s, hloc_s, hin_s):
    t = pl.program_id(1)

    @pl.when(t == 0)
    def _():
        xcar[...] = jnp.zeros(xcar.shape, F32)
        qcar[...] = jnp.zeros(qcar.shape, F32)
        hcar[...] = jnp.zeros(hcar.shape, F32)

    h = _norm_mod(x_ref[...], nw_ref[...], ada_ref[3:4, :], ada_ref[4:5, :]).astype(BF16)

    col = functools.partial(_proj_col, h, wlo_ref, whi_ref)
    groups = TP // SUBLANES
    sub = lax.broadcasted_iota(jnp.int32, (1, SUBLANES, 1), 1)
    row = lax.broadcasted_iota(jnp.int32, (TP, 1), 0)
    first = jnp.logical_and(row == 0, t == 0)

    def rglru_block(xr, n):
        ln = slice(n * RG_BLOCK_W, (n + 1) * RG_BLOCK_W)
        xc = _causal_conv(xr[:, ln], xcar, cw_ref, ln) + cb_ref[:, ln]
        xb = xc.astype(BF16)
        r = _sigmoid(_dot(xb, wa_ref[n]) + ba_ref[:, ln])
        i = _sigmoid(_dot(xb, wx_ref[n]) + bx_ref[:, ln])
        log_a = (-RG_C) * r * _softplus(-lam_ref[:, ln])
        a = jnp.exp(log_a)
        mult = jnp.where(first, 1.0, jnp.sqrt(_neg_expm1(2.0 * log_a)))
        a3 = a.reshape(groups, SUBLANES, RG_BLOCK_W)
        b3 = (mult * (i * xc)).reshape(groups, SUBLANES, RG_BLOCK_W)
        d = 1
        while d < SUBLANES:
            keep = sub >= d
            a_prev = jnp.where(keep, pltpu.roll(a3, d, axis=1), 1.0)
            b_prev = jnp.where(keep, pltpu.roll(b3, d, axis=1), 0.0)
            b3 = a3 * b_prev + b3
            a3 = a3 * a_prev
            d *= 2
        acum_s[:, ln] = a3.reshape(TP, RG_BLOCK_W)
        hloc_s[:, ln] = b3.reshape(TP, RG_BLOCK_W)

    def qkv_part(z, c):
        for l0 in range(0, D_MODEL, MIX_CONV_LANES):
            ln = slice(c * D_MODEL + l0, c * D_MODEL + l0 + MIX_CONV_LANES)
            act_ref[:, ln] = _silu(_causal_conv(z[:, l0:l0 + MIX_CONV_LANES], qcar, cqw_ref, ln))

    xr = col(0)
    for n in range(RG_BLOCKS):
        rglru_block(xr, n)
    cs_ref[0] = xcar[SUBLANES - (CONV_W - 1):, :]
    for c in range(DN_QKV // D_MODEL):
        qkv_part(col(2 + c), c)
    cq_ref[0] = qcar[SUBLANES - (CONV_W - 1):, :]

    hrow = hcar[0:1, :]
    for g in range(groups):
        hin_s[g * SUBLANES:(g + 1) * SUBLANES, :] = jnp.broadcast_to(hrow, (SUBLANES, D_RNN))
        e = (g + 1) * SUBLANES - 1
        hrow = acum_s[e:e + 1, :] * hrow + hloc_s[e:e + 1, :]
    hcar[0:1, :] = hrow
    hl_ref[0] = hrow

    y = hloc_s[...] + acum_s[...] * hin_s[...]
    oa_ref[...] = (y * jax.nn.gelu(col(1))).astype(BF16)
    szg_ref[...] = _silu(col(5)).astype(BF16)
    sga_ref[...] = _sigmoid(col(6)).astype(BF16)
    sgb_ref[...] = _sigmoid(col(7)).astype(BF16)
    gb_ref[...] = _dot(h, wab_ref[...])


def _mixin_call(x, ada, nw, w_lo, w_hi, w_ab, cw, cb, wa, ba, wx, bx, lam, cqw, *, batch, seq_len):
    nt = seq_len // TP
    m = batch * seq_len
    row = lambda n: pl.BlockSpec((TP, n), lambda b, t: (b * nt + t, 0))
    per_seq = lambda r, n: pl.BlockSpec((1, r, n), lambda b, t: (b, 0, 0))
    shp = lambda n, dt: jax.ShapeDtypeStruct((m, n), dt)
    blk = (RG_BLOCKS, RG_BLOCK_W, RG_BLOCK_W)
    return pl.pallas_call(
        _mixin_kernel,
        grid=(batch, nt),
        in_specs=[row(D_MODEL), pl.BlockSpec((None, N_ADA, D_MODEL), lambda b, t: (b, 0, 0)),
                  _resident((1, D_MODEL)), _resident((D_MODEL, AB_OFF)), _resident((D_MODEL, N_HI)),
                  _resident((D_MODEL, LANES)),
                  _resident((CONV_W, D_RNN)), _resident((1, D_RNN)), _resident(blk), _resident((1, D_RNN)),
                  _resident(blk), _resident((1, D_RNN)), _resident((1, D_RNN)), _resident((CONV_W, DN_QKV))],
        out_specs=[row(D_RNN), row(DN_QKV), row(DN_V), row(D_MODEL), row(D_MODEL), row(LANES),
                   per_seq(1, D_RNN), per_seq(CONV_W - 1, D_RNN), per_seq(CONV_W - 1, DN_QKV)],
        out_shape=[shp(D_RNN, BF16), shp(DN_QKV, F32), shp(DN_V, BF16), shp(D_MODEL, BF16), shp(D_MODEL, BF16),
                   shp(LANES, F32),
                   jax.ShapeDtypeStruct((batch, 1, D_RNN), F32),
                   jax.ShapeDtypeStruct((batch, CONV_W - 1, D_RNN), F32),
                   jax.ShapeDtypeStruct((batch, CONV_W - 1, DN_QKV), F32)],
        scratch_shapes=[pltpu.VMEM((SUBLANES, D_RNN), F32),
                        pltpu.VMEM((SUBLANES, DN_QKV), F32),
                        pltpu.VMEM((SUBLANES, D_RNN), F32),
                        pltpu.VMEM((TP, D_RNN), F32),
                        pltpu.VMEM((TP, D_RNN), F32),
                        pltpu.VMEM((TP, D_RNN), F32)],
        compiler_params=_params(("arbitrary", "arbitrary")),
        name="mixin",
    )(x, ada, nw, w_lo, w_hi, w_ab, cw, cb, wa, ba, wx, bx, lam, cqw)


def _split(x):
    hi = x.astype(BF16)
    return hi, (x - hi.astype(F32)).astype(BF16)


def _dot3(a_hi, a_lo, b):
    n = b.shape[1]
    b_hi, b_lo = _split(b)
    r = _dot(a_hi, jnp.concatenate([b_hi, b_lo], axis=1))
    return r[:, :n] + r[:, n:] + _dot(a_lo, b_hi)


def _pair_mask(ti, tj, lvl):
    same = (ti >> (lvl + 1)) == (tj >> (lvl + 1))
    return jnp.logical_and(same, jnp.logical_and(((ti >> lvl) & 1) == 1, ((tj >> lvl) & 1) == 0))


DELTA_CPI = 4


def _delta_kernel(act_s, gb_ref, szg_ref, alog_ref, dtb_ref, dnw_ref,
                  ob_ref, s_ref,
                  s_acc, o_s, u_s, wq_s, attn_s, kd_s, egl_s, *, nt):
    t = pl.program_id(1)
    c = DN_CHUNK
    heads = range(DN_HEADS)

    @pl.when(t == 0)
    def _():
        s_acc[...] = jnp.zeros(s_acc.shape, F32)

    ti = lax.broadcasted_iota(jnp.int32, (c, 2 * c), 0)
    tj = lax.broadcasted_iota(jnp.int32, (c, 2 * c), 1)
    left = tj < c
    incl = jnp.logical_and(ti >= tj, left)
    strict = jnp.logical_and(ti > tj, left)
    eye_l = (ti == tj).astype(F32)
    cum_l = incl[:, :c].astype(F32)
    cum_r = jnp.logical_and(ti <= tj, left).astype(F32)
    last = c - 1
    zpad = jnp.zeros((c, DN_DK), BF16)
    neg_a = -jnp.exp(alog_ref[...])
    dtb = dtb_ref[...]

    def chunk_rows(ref, j, lanes):
        return ref[pl.ds(pl.multiple_of(j * c, c), c), lanes]

    def prep(it, _):
        js = [it * DELTA_CPI + k for k in range(DELTA_CPI)]
        gcum, gcum_t, beta = [], [], []
        for j in js:
            gbv = chunk_rows(gb_ref, j, slice(None))
            g = neg_a * _softplus(gbv + dtb)
            beta.append(_sigmoid(gbv))
            gcum.append(jnp.dot(cum_l, g, precision=HI, preferred_element_type=F32))
            gcum_t.append(_dot_tn(g, cum_r, precision=HI))
        chains = [(k, h) for k in range(DELTA_CPI) for h in heads]
        gc, decay, qn, kn, kn16, kb, vb = {}, {}, {}, {}, {}, {}, {}
        for k, h in chains:
            j = js[k]
            gc[k, h] = gcum[k][:, h:h + 1]
            decay[k, h] = jnp.exp(jnp.where(incl, gc[k, h] - gcum_t[k][h:h + 1, :], -jnp.inf))
            bh = beta[k][:, DN_HEADS + h:DN_HEADS + h + 1]
            qh = chunk_rows(act_s, j, slice(h * DN_DK, (h + 1) * DN_DK))
            kh = chunk_rows(act_s, j, slice(DN_QK + h * DN_DK, DN_QK + (h + 1) * DN_DK))
            vh = chunk_rows(act_s, j, slice(2 * DN_QK + h * DN_DV, 2 * DN_QK + (h + 1) * DN_DV))
            qn[k, h] = qh * lax.rsqrt(jnp.sum(qh * qh, axis=-1, keepdims=True) + EPS) * (DN_DK ** -0.5)
            kn[k, h] = kh * lax.rsqrt(jnp.sum(kh * kh, axis=-1, keepdims=True) + EPS)
            kn16[k, h] = kn[k, h].astype(BF16)
            kb[k, h] = kn[k, h] * bh
            vb[k, h] = vh * bh
        lm = {}
        for ch in chains:
            kq = _dot_nt(jnp.concatenate([kb[ch], qn[ch]], axis=0).astype(BF16),
                         jnp.concatenate([kn16[ch], zpad], axis=0))
            lm[ch] = jnp.where(strict, kq[:c] * decay[ch], 0.0)
            attn_s[js[ch[0]], ch[1]] = (kq[c:] * decay[ch])[:, :c].astype(BF16)
        d = {ch: eye_l - jnp.where(_pair_mask(ti, tj, 0), lm[ch], 0.0) for ch in chains}
        for lvl in range(1, c.bit_length() - 1):
            pair = _pair_mask(ti, tj, lvl)
            d16 = {ch: d[ch].astype(BF16) for ch in chains}
            ed = {ch: _dot(jnp.where(pair, lm[ch], 0.0).astype(BF16)[:, :c], d16[ch]) for ch in chains}
            d = {ch: d[ch] - _dot(d16[ch][:, :c], ed[ch].astype(BF16)) for ch in chains}
        eg = {ch: jnp.exp(gc[ch]) for ch in chains}
        a16 = {ch: d[ch].astype(BF16)[:, :c] for ch in chains}
        rhs = {ch: jnp.concatenate([vb[ch], kb[ch] * eg[ch]], axis=1) for ch in chains}
        x0 = {ch: _dot(a16[ch], rhs[ch].astype(BF16)) for ch in chains}
        res = {}
        for ch in chains:
            l_hi, l_lo = _split(lm[ch])
            res[ch] = rhs[ch] - x0[ch] - _dot3(l_hi[:, :c], l_lo[:, :c], x0[ch])
        for ch in chains:
            k, h = ch
            x = x0[ch] + _dot(a16[ch], res[ch].astype(BF16))
            u_s[js[k], h] = x[:, :DN_DV]
            wq_s[js[k], h] = jnp.concatenate([x[:, DN_DV:], qn[ch] * eg[ch]], axis=0).astype(BF16)
            gl = gcum[k][last:last + 1, h:h + 1]
            kd_s[js[k], h] = (kn[ch] * jnp.exp(gl - gc[ch])).astype(BF16)
        for k, j in enumerate(js):
            egl_s[pl.ds(pl.multiple_of(j * SUBLANES, SUBLANES), SUBLANES), :] = jnp.broadcast_to(
                jnp.exp(gcum[k][last:last + 1, :]), (SUBLANES, LANES))
        return 0

    lax.fori_loop(0, NCH // DELTA_CPI, prep, 0)

    def recur(j, _):
        egl = egl_s[pl.ds(pl.multiple_of(j * SUBLANES, SUBLANES), 1), :]
        s_old = [s_acc[h] for h in heads]
        wq = [_dot(wq_s[j, h], s_old[h].astype(BF16)) for h in heads]
        vnew16 = [(u_s[j, h] - wq[h][:c]).astype(BF16) for h in heads]
        o = [wq[h][c:] + _dot(attn_s[j, h], vnew16[h]) for h in heads]
        for h in heads:
            s_acc[h] = s_old[h] * egl[:, h:h + 1] + _dot_tn(kd_s[j, h], vnew16[h])
        for h in heads:
            on = o[h] * lax.rsqrt(jnp.mean(o[h] * o[h], axis=-1, keepdims=True) + EPS) * dnw_ref[...]
            o_s[pl.ds(pl.multiple_of(j * c, c), c), h * DN_DV:(h + 1) * DN_DV] = on
        return 0

    lax.fori_loop(0, NCH, recur, 0)

    ob_ref[...] = (o_s[...] * szg_ref[...].astype(F32)).astype(BF16)

    @pl.when(t == nt - 1)
    def _():
        s_ref[0] = s_acc[...]


def _delta_call(act, gb, szg, alog, dtb, dnw, *, batch, seq_len):
    c = DN_CHUNK
    nt = seq_len // TP
    row = lambda n: pl.BlockSpec((TP, n), lambda b, t: (b * nt + t, 0))
    return pl.pallas_call(
        functools.partial(_delta_kernel, nt=nt),
        grid=(batch, nt),
        in_specs=[row(DN_QKV), row(LANES), row(DN_V),
                  _resident((1, LANES)), _resident((1, LANES)), _resident((1, DN_DV))],
        out_specs=[row(DN_V),
                   pl.BlockSpec((1, DN_HEADS, DN_DK, DN_DV), lambda b, t: (b, 0, 0, 0))],
        out_shape=[jax.ShapeDtypeStruct((batch * seq_len, DN_V), BF16),
                   jax.ShapeDtypeStruct((batch, DN_HEADS, DN_DK, DN_DV), F32)],
        scratch_shapes=[pltpu.VMEM((DN_HEADS, DN_DK, DN_DV), F32),
                        pltpu.VMEM((TP, DN_V), F32),
                        pltpu.VMEM((NCH, DN_HEADS, c, DN_DV), F32),
                        pltpu.VMEM((NCH, DN_HEADS, 2 * c, DN_DK), BF16),
                        pltpu.VMEM((NCH, DN_HEADS, c, c), BF16),
                        pltpu.VMEM((NCH, DN_HEADS, c, DN_DK), BF16),
                        pltpu.VMEM((NCH * SUBLANES, LANES), F32)],
        compiler_params=_params(("arbitrary", "arbitrary")),
        name="delta",
    )(act, gb, szg, alog, dtb, dnw)


def _sample_pre_kernel(xr_ref, ggr_ref, qkv_ref, gb_ref, crnn_ref, cqkv_ref, h0_ref,
                       cw_ref, cb_ref, wa_ref, ba_ref, wx_ref, bx_ref, lam_ref,
                       cqw_ref, alog_ref, dtb_ref,
                       oa_ref, hn_ref, qn_ref, kn_ref, v_ref, eg_ref, beta_ref):
    xc = cb_ref[...] + cw_ref[CONV_W - 1:CONV_W, :] * xr_ref[...]
    for j in range(CONV_W - 1):
        xc = xc + cw_ref[j:j + 1, :] * crnn_ref[j]
    a, log_a, i = _rg_gates(xc, wa_ref, ba_ref[...], wx_ref, bx_ref[...], lam_ref[...])
    mult = jnp.sqrt(_neg_expm1(2.0 * log_a))
    h = a * h0_ref[...] + mult * (i * xc)
    hn_ref[...] = h
    oa_ref[...] = (h * ggr_ref[...].astype(F32)).astype(BF16)

    qc = cqw_ref[CONV_W - 1:CONV_W, :] * qkv_ref[...]
    for j in range(CONV_W - 1):
        qc = qc + cqw_ref[j:j + 1, :] * cqkv_ref[j]
    act = _silu(qc)
    for h_ in range(DN_HEADS):
        qh = act[:, h_ * DN_DK:(h_ + 1) * DN_DK]
        kh = act[:, DN_QK + h_ * DN_DK:DN_QK + (h_ + 1) * DN_DK]
        qn_ref[:, h_ * DN_DK:(h_ + 1) * DN_DK] = (
            qh * lax.rsqrt(jnp.sum(qh * qh, axis=-1, keepdims=True) + EPS) * (DN_DK ** -0.5))
        kn_ref[:, h_ * DN_DK:(h_ + 1) * DN_DK] = (
            kh * lax.rsqrt(jnp.sum(kh * kh, axis=-1, keepdims=True) + EPS))
    v_ref[...] = act[:, 2 * DN_QK:]
    gbv = gb_ref[...]
    eg_ref[...] = jnp.exp(-jnp.exp(alog_ref[...]) * _softplus(gbv + dtb_ref[...]))
    beta_ref[...] = _sigmoid(gbv)


def _sample_pre_call(xr, ggr, qkv, gb, crnn, cqkv, h0, cw, cb, wa, ba, wx, bx, lam, cqw, alog, dtb):
    n = xr.shape[0]
    args = (xr, ggr, qkv, gb, crnn, cqkv, h0, cw, cb, wa, ba, wx, bx, lam, cqw, alog, dtb)
    shp = lambda w: jax.ShapeDtypeStruct((n, w), F32)
    outs = [jax.ShapeDtypeStruct((n, D_RNN), BF16), shp(D_RNN), shp(DN_QK), shp(DN_QK), shp(DN_V), shp(LANES), shp(LANES)]
    return pl.pallas_call(
        _sample_pre_kernel,
        grid=(1,),
        in_specs=[_resident(a.shape) for a in args],
        out_specs=[pl.BlockSpec(o.shape, lambda i: (0, 0)) for o in outs],
        out_shape=outs,
        compiler_params=_params(("arbitrary",)),
        name="sample_pre",
    )(*args)


def _sample_state_kernel(qn_ref, kn_ref, v_ref, eg_ref, beta_ref, szg_ref, dnw_ref, s_ref,
                         ob_ref, sn_ref):
    srow = lax.broadcasted_iota(jnp.int32, (SUBLANES, DN_DK), 0)
    rows = lambda r0, r1: jnp.where(srow == 0, r0, jnp.where(srow == 1, r1, 0.0))
    chains = [(b, h) for b in range(SAMPLE_BB) for h in range(DN_HEADS)]
    sl = lambda h: slice(h * DN_DK, (h + 1) * DN_DK)
    q = {(b, h): qn_ref[0, b:b + 1, sl(h)] for b, h in chains}
    k = {(b, h): kn_ref[0, b:b + 1, sl(h)] for b, h in chains}
    eg = {(b, h): eg_ref[0, b:b + 1, h:h + 1] for b, h in chains}
    beta = {(b, h): beta_ref[0, b:b + 1, DN_HEADS + h:DN_HEADS + h + 1] for b, h in chains}
    ws_qs = {ch: _dot(rows((k[ch] * beta[ch]) * eg[ch], q[ch] * eg[ch]).astype(BF16),
                      s_ref[ch[0], ch[1]].astype(BF16)) for ch in chains}
    vnew = {(b, h): v_ref[0, b:b + 1, sl(h)] * beta[b, h] - ws_qs[b, h][0:1, :] for b, h in chains}
    for ch in chains:
        b, h = ch
        zero = jnp.zeros_like(k[ch])
        sn_ref[b, h] = s_ref[b, h] * eg[ch] + _dot_tn(rows(k[ch], zero).astype(BF16),
                                                     rows(vnew[ch], zero).astype(BF16))
    for ch in chains:
        b, h = ch
        o = ws_qs[ch][1:2, :] + jnp.sum(q[ch] * k[ch], axis=-1, keepdims=True) * vnew[ch]
        on = o * lax.rsqrt(jnp.mean(o * o, axis=-1, keepdims=True) + EPS) * dnw_ref[...]
        ob_ref[0, b:b + 1, sl(h)] = on * szg_ref[0, b:b + 1, sl(h)]


SAMPLE_BB = 8


def _sample_state_call(qn, kn, v, eg, beta, szg, dnw, s0):
    n = qn.shape[0]
    bb = SAMPLE_BB
    r3 = lambda a: a.reshape(n // bb, bb, a.shape[-1])
    vec = lambda w: pl.BlockSpec((1, bb, w), lambda i: (i, 0, 0))
    s_spec = pl.BlockSpec((bb, DN_HEADS, DN_DK, DN_DV), lambda i: (i, 0, 0, 0))
    ob, sn = pl.pallas_call(
        _sample_state_kernel,
        grid=(n // bb,),
        in_specs=[vec(DN_QK), vec(DN_QK), vec(DN_V), vec(LANES), vec(LANES), vec(DN_V),
                  _resident((1, DN_DV)), s_spec],
        out_specs=[vec(DN_V), s_spec],
        out_shape=[jax.ShapeDtypeStruct((n // bb, bb, DN_V), F32),
                   jax.ShapeDtypeStruct((n, DN_HEADS, DN_DK, DN_DV), F32)],
        compiler_params=_params(("arbitrary",)),
        name="sample_state",
    )(r3(qn), r3(kn), r3(v), r3(eg), r3(beta), r3(szg), dnw, s0)
    return ob.reshape(n, DN_V), sn


def _merge_kernel(x_ref, oa_ref, ob_ref, sga_ref, sgb_ref, ada_ref, wb_ref, wo_ref, o_ref, *, per_row):
    ya = _dot(oa_ref[...], wb_ref[0].astype(BF16))
    yb = _dot(ob_ref[...], wb_ref[1].astype(BF16))
    merged = (sga_ref[...].astype(F32) * ya + sgb_ref[...].astype(F32) * yb).astype(BF16)
    o_ref[...] = x_ref[...] + _ada_rows(ada_ref, 5, per_row) * _dot(merged, wo_ref[...].astype(BF16))


def _merge_call(x, oa, ob, sga, sgb, ada, wb, wo, *, per_row, tm, seq_len):
    m = x.shape[0]
    row = pl.BlockSpec((tm, D_MODEL), lambda i: (i, 0))
    return pl.pallas_call(
        functools.partial(_merge_kernel, per_row=per_row),
        grid=(m // tm,),
        in_specs=[row, row, row, row, row, _ada_spec(per_row, m, seq_len // tm),
                  _resident((2, D_RNN, D_MODEL)), _resident((D_MODEL, D_MODEL))],
        out_specs=row,
        out_shape=jax.ShapeDtypeStruct((m, D_MODEL), F32),
        compiler_params=_params(("arbitrary",)),
        name="merge",
    )(x, oa, ob, sga, sgb, ada, wb, wo)


def kernel(x_prompt, x_sample, c_prompt, c_sample, state_rglru_h, state_rglru_conv, state_delta_S, state_delta_conv, w_ada, b_ada, norm_ffn1, w_ffn1_up, w_ffn1_down, norm_mix, w_in, conv_rnn_w, conv_rnn_b, rg_w_a, rg_b_a, rg_w_x, rg_b_x, rg_lambda, conv_qkv_w, dn_a_log, dn_dt_bias, dn_norm, w_branch, w_out, norm_ffn2, w_ffn2_up, w_ffn2_down, norm_final):
    batch, seq_len, _ = x_prompt.shape
    n_dec = x_sample.shape[0]
    assert w_ada.shape[0] == 1 and x_sample.shape[1] == 1 and seq_len % TP == 0

    row = lambda a: a.reshape(1, -1).astype(F32)
    wup1, wdn1 = w_ffn1_up[0], w_ffn1_down[0]
    wup2, wdn2 = w_ffn2_up[0], w_ffn2_down[0]
    w_lo, w_hi, w_ab = _win_cast_call(w_in[0])
    wb, wo = w_branch[0], w_out[0]
    wa, wx = rg_w_a[0].astype(BF16), rg_w_x[0].astype(BF16)
    lane_pad = lambda a: jnp.pad(a.reshape(1, -1).astype(F32), ((0, 0), (0, LANES - a.size)))
    alog = lane_pad(dn_a_log[0])
    dtb = lane_pad(dn_dt_bias[0])
    nf = row(norm_final)

    ada = _ada_call(jnp.concatenate([c_prompt, c_sample], axis=0), w_ada[0], row(b_ada[0]))
    ada = ada.reshape(batch + n_dec, N_ADA, D_MODEL)
    ada_p = ada[:batch]
    ada_s = jnp.transpose(ada[batch:], (1, 0, 2))

    mixer_w = (conv_rnn_w[0], row(conv_rnn_b[0]), wa, row(rg_b_a[0]), wx, row(rg_b_x[0]), row(rg_lambda[0]))

    kw = dict(per_row=False, seq_len=seq_len, tm=TP)
    xp = x_prompt.reshape(batch * seq_len, D_MODEL)
    xp = _ffn_call(xp, ada_p, row(norm_ffn1[0]), wup1, wdn1, nf, k0=0, final_norm=False, **kw)
    oa, act, szg, sga, sgb, gb, hp, cp, qp = _mixin_call(
        xp, ada_p, row(norm_mix[0]), w_lo, w_hi, w_ab, *mixer_w, conv_qkv_w[0], batch=batch, seq_len=seq_len)
    ob, sp = _delta_call(act, gb, szg, alog, dtb, row(dn_norm[0]), batch=batch, seq_len=seq_len)
    xp = _merge_call(xp, oa, ob, sga, sgb, ada_p, wb, wo, **kw)
    y_prompt = _ffn_call(xp, ada_p, row(norm_ffn2[0]), wup2, wdn2, nf, k0=6, final_norm=True, **kw)

    kw = dict(per_row=True, seq_len=n_dec, tm=n_dec)
    xs = x_sample.reshape(n_dec, D_MODEL)
    xs = _ffn_call(xs, ada_s, row(norm_ffn1[0]), wup1, wdn1, nf, k0=0, final_norm=False, **kw)
    xr_s, ggr_s, qkv_s, szg_s, sga_s, sgb_s, gb_s = _inproj_call(xs, ada_s, row(norm_mix[0]), w_lo, w_hi, w_ab, **kw)
    crnn = jnp.transpose(state_rglru_conv[0], (1, 0, 2))
    cqkv = jnp.transpose(state_delta_conv[0], (1, 0, 2))
    oa_s, hs, qn, kn, v, eg, beta = _sample_pre_call(
        xr_s, ggr_s, qkv_s, gb_s, crnn, cqkv, state_rglru_h[0], *mixer_w, conv_qkv_w[0], alog, dtb)
    ob_s, ss = _sample_state_call(qn, kn, v, eg, beta, szg_s.astype(F32), row(dn_norm[0]), state_delta_S[0])
    xs = _merge_call(xs, oa_s, ob_s.astype(BF16), sga_s, sgb_s, ada_s, wb, wo, **kw)
    y_sample = _ffn_call(xs, ada_s, row(norm_ffn2[0]), wup2, wdn2, nf, k0=6, final_norm=True, **kw)
    cs = jnp.concatenate([state_rglru_conv[0][:, 1:], xr_s[:, None, :]], axis=1)
    qs = jnp.concatenate([state_delta_conv[0][:, 1:], qkv_s[:, None, :]], axis=1)

    return (y_prompt.reshape(batch, seq_len, D_MODEL), y_sample.reshape(n_dec, 1, D_MODEL),
            hp.reshape(1, batch, D_RNN), cp[None], sp[None], qp[None],
            hs[None], cs[None], ss[None], qs[None])
```

```python
import functools

import jax
import jax.numpy as jnp
from jax import lax
from jax.experimental import pallas as pl
from jax.experimental.pallas import tpu as pltpu

D_MODEL = 1024
D_RNN = 1024
RG_BLOCKS = 8
RG_BLOCK_W = D_RNN // RG_BLOCKS
RG_C = 8.0
CONV_W = 4
DN_HEADS = 8
DN_DK = 128
DN_DV = 128
DN_QK = DN_HEADS * DN_DK
DN_V = DN_HEADS * DN_DV
DN_QKV = 2 * DN_QK + DN_V
DN_CHUNK = 64
D_FF = 2816
N_ADA = 9
EPS = 1e-6
LANES = 128
SUBLANES = 8
AB_OFF = D_RNN * 2 + DN_QKV
N_HI = 3 * D_MODEL

BF16 = jnp.bfloat16
F32 = jnp.float32
HI = lax.Precision.HIGHEST

VMEM_LIMIT = 58 * 1024 * 1024


def _params(sem):
    return pltpu.CompilerParams(dimension_semantics=sem, vmem_limit_bytes=VMEM_LIMIT)


def _resident(shape):
    nd = len(shape)
    return pl.BlockSpec(shape, lambda *_: (0,) * nd, pipeline_mode=pl.Buffered(1))


def _dot(a, b):
    return jnp.dot(a, b, preferred_element_type=F32)


def _dot_nt(a, b, precision=None):
    return lax.dot_general(a, b, (((1,), (1,)), ((), ())), precision=precision,
                           preferred_element_type=F32)


def _dot_tn(a, b, precision=None):
    return lax.dot_general(a, b, (((0,), (0,)), ((), ())), precision=precision,
                           preferred_element_type=F32)


def _sigmoid(x):
    return 0.5 * jnp.tanh(0.5 * x) + 0.5


def _silu(x):
    h = 0.5 * x
    return h + h * jnp.tanh(h)


def _softplus(x):
    return jnp.maximum(x, 0.0) + jnp.log1p(jnp.exp(-jnp.abs(x)))


def _neg_expm1(x):
    return -jnp.tanh(0.5 * x) * (jnp.exp(x) + 1.0)


def _ada_rows(ada_ref, k, per_row):
    if per_row:
        return ada_ref[k]
    return ada_ref[k:k + 1, :]


def _norm_mod(x, nw, shift, scale):
    ms = jnp.mean(x * x, axis=-1, keepdims=True)
    hn = x * lax.rsqrt(ms + EPS) * nw
    return hn * (1.0 + scale) + shift


def _ada_kernel(c_ref, w_ref, b_ref, o_ref):
    o_ref[...] = _dot(c_ref[...].astype(BF16), w_ref[...].astype(BF16)) + b_ref[...]


def _ada_call(c_all, w_ada, b_ada):
    n = c_all.shape[0]
    tn = D_MODEL
    return pl.pallas_call(
        _ada_kernel,
        grid=(N_ADA * D_MODEL // tn,),
        in_specs=[pl.BlockSpec((n, D_MODEL), lambda j: (0, 0)),
                  pl.BlockSpec((D_MODEL, tn), lambda j: (0, j)),
                  pl.BlockSpec((1, tn), lambda j: (0, j))],
        out_specs=pl.BlockSpec((n, tn), lambda j: (0, j)),
        out_shape=jax.ShapeDtypeStruct((n, N_ADA * D_MODEL), F32),
        compiler_params=_params(("arbitrary",)),
        name="ada",
    )(c_all, w_ada, b_ada)


FFN_TF = 256


def _ffn_kernel(x_ref, ada_ref, nw_ref, wup_ref, wdn_ref, nf_ref, o_ref, *, k0, per_row, final_norm):
    x = x_ref[...]
    h = _norm_mod(x, nw_ref[...], _ada_rows(ada_ref, k0, per_row),
                  _ada_rows(ada_ref, k0 + 1, per_row)).astype(BF16)
    acc = jnp.zeros(x.shape, F32)
    for j in range(D_FF // FFN_TF):
        g = _dot(h, wup_ref[:, j * FFN_TF:(j + 1) * FFN_TF].astype(BF16))
        v = _dot(h, wup_ref[:, D_FF + j * FFN_TF:D_FF + (j + 1) * FFN_TF].astype(BF16))
        a = (_silu(g) * v).astype(BF16)
        acc = acc + _dot(a, wdn_ref[j * FFN_TF:(j + 1) * FFN_TF, :].astype(BF16))
    y = x + 0.5 * _ada_rows(ada_ref, k0 + 2, per_row) * acc
    if final_norm:
        ms = jnp.mean(y * y, axis=-1, keepdims=True)
        y = y * lax.rsqrt(ms + EPS) * nf_ref[...]
    o_ref[...] = y


def _ada_spec(per_row, rows, tiles_per_seq):
    if per_row:
        return pl.BlockSpec((N_ADA, rows, D_MODEL), lambda i: (0, 0, 0))
    return pl.BlockSpec((None, N_ADA, D_MODEL), lambda i: (i // tiles_per_seq, 0, 0))


def _ffn_call(x, ada, nw, wup, wdn, nf, *, k0, per_row, final_norm, tm, seq_len):
    m = x.shape[0]
    kern = functools.partial(_ffn_kernel, k0=k0, per_row=per_row, final_norm=final_norm)
    return pl.pallas_call(
        kern,
        grid=(m // tm,),
        in_specs=[pl.BlockSpec((tm, D_MODEL), lambda i: (i, 0)),
                  _ada_spec(per_row, m, seq_len // tm),
                  _resident((1, D_MODEL)),
                  _resident((D_MODEL, 2 * D_FF)),
                  _resident((D_FF, D_MODEL)),
                  _resident((1, D_MODEL))],
        out_specs=pl.BlockSpec((tm, D_MODEL), lambda i: (i, 0)),
        out_shape=jax.ShapeDtypeStruct((m, D_MODEL), F32),
        compiler_params=_params(("arbitrary",)),
        name="ffn",
    )(x, ada, nw, wup, wdn, nf)


TP = 512
NCH = TP // DN_CHUNK


WCAST_ROWS = 512


def _wt_cast_kernel(wt_ref, o_ref, *, n_valid):
    o = wt_ref[...].T
    if n_valid < o.shape[1]:
        o = jnp.where(lax.broadcasted_iota(jnp.int32, (1, o.shape[1]), 1) < n_valid, o, 0.0)
    o_ref[...] = o.astype(BF16)


def _wt_cast_call(wt, row0, n_valid, n_out):
    k = wt.shape[1]
    blk = min(WCAST_ROWS, n_out)
    return pl.pallas_call(
        functools.partial(_wt_cast_kernel, n_valid=min(n_valid, blk)),
        grid=(n_out // blk,),
        in_specs=[pl.BlockSpec((pl.Element(blk), pl.Element(k)),
                               lambda i: (pl.multiple_of(row0 + i * blk, SUBLANES), 0))],
        out_specs=pl.BlockSpec((k, blk), lambda i: (0, i)),
        out_shape=jax.ShapeDtypeStruct((k, n_out), BF16),
        compiler_params=_params(("arbitrary",)),
        name="wt_cast",
    )(wt)


def _proj_col(h, wlo_ref, whi_ref, c):
    n_lo = AB_OFF // D_MODEL
    w_ref, c = (wlo_ref, c) if c < n_lo else (whi_ref, c - n_lo)
    return _dot(h, w_ref[:, c * D_MODEL:(c + 1) * D_MODEL])


def _inproj_kernel(x_ref, ada_ref, nw_ref, wlo_ref, whi_ref, wab_ref,
                   xr_ref, ggr_ref, qkv_ref, szg_ref, sga_ref, sgb_ref, gb_ref, *, per_row):
    h = _norm_mod(x_ref[...], nw_ref[...], _ada_rows(ada_ref, 3, per_row),
                  _ada_rows(ada_ref, 4, per_row)).astype(BF16)
    col = functools.partial(_proj_col, h, wlo_ref, whi_ref)
    xr_ref[...] = col(0)
    ggr_ref[...] = jax.nn.gelu(col(1)).astype(BF16)
    for c in range(3):
        qkv_ref[:, c * D_MODEL:(c + 1) * D_MODEL] = col(2 + c)
    szg_ref[...] = _silu(col(5)).astype(BF16)
    sga_ref[...] = _sigmoid(col(6)).astype(BF16)
    sgb_ref[...] = _sigmoid(col(7)).astype(BF16)
    gb_ref[...] = _dot(h, wab_ref[...])


def _inproj_call(x, ada, nw, w_lo, w_hi, w_ab, *, per_row, tm, seq_len):
    m = x.shape[0]
    row = lambda n: pl.BlockSpec((tm, n), lambda i: (i, 0))
    shp = lambda n, dt: jax.ShapeDtypeStruct((m, n), dt)
    return pl.pallas_call(
        functools.partial(_inproj_kernel, per_row=per_row),
        grid=(m // tm,),
        in_specs=[row(D_MODEL), _ada_spec(per_row, m, seq_len // tm),
                  _resident((1, D_MODEL)), _resident((D_MODEL, AB_OFF)), _resident((D_MODEL, N_HI)),
                  _resident((D_MODEL, LANES))],
        out_specs=[row(D_RNN), row(D_RNN), row(DN_QKV), row(DN_V), row(D_MODEL), row(D_MODEL), row(LANES)],
        out_shape=[shp(D_RNN, F32), shp(D_RNN, BF16), shp(DN_QKV, F32), shp(DN_V, BF16),
                   shp(D_MODEL, BF16), shp(D_MODEL, BF16), shp(LANES, F32)],
        compiler_params=_params(("arbitrary",)),
        name="inproj",
    )(x, ada, nw, w_lo, w_hi, w_ab)


def _causal_conv(x, carry, cw_ref, lanes):
    t, c = x.shape
    x3 = x.reshape(t // SUBLANES, SUBLANES, c)
    prev8 = carry[:, lanes]
    sub = lax.broadcasted_iota(jnp.int32, (1, SUBLANES, 1), 1)
    acc = cw_ref[CONV_W - 1:CONV_W, lanes][None] * x3
    for j in range(1, CONV_W):
        r = pltpu.roll(x3, j, axis=1)
        rp = jnp.concatenate([pltpu.roll(prev8, j, axis=0)[None], r[:-1]], axis=0)
        acc = acc + cw_ref[CONV_W - 1 - j:CONV_W - j, lanes][None] * jnp.where(sub >= j, r, rp)
    carry[:, lanes] = x[t - SUBLANES:, :]
    return acc.reshape(t, c)


def _rg_gates(xc, wa_ref, ba, wx_ref, bx, lam):
    ra, ix = [], []
    for n in range(RG_BLOCKS):
        xb = xc[:, n * RG_BLOCK_W:(n + 1) * RG_BLOCK_W].astype(BF16)
        ra.append(_dot(xb, wa_ref[n]))
        ix.append(_dot(xb, wx_ref[n]))
    r = _sigmoid(jnp.concatenate(ra, axis=1) + ba)
    i = _sigmoid(jnp.concatenate(ix, axis=1) + bx)
    log_a = (-RG_C) * r * _softplus(-lam)
    return jnp.exp(log_a), log_a, i


MIX_CONV_LANES = 512


def _mixin_kernel(x_ref, ada_ref, nw_ref, wlo_ref, whi_ref, wab_ref,
                  cw_ref, cb_ref, wa_ref, ba_ref, wx_ref, bx_ref, lam_ref, cqw_ref,
                  oa_ref, act_ref, szg_ref, sga_ref, sgb_ref, gb_ref, hl_ref, cs_ref, cq_ref,
                  xcar, qcar, hcar, acum_s, hloc_s, hin_s):
    t = pl.program_id(1)

    @pl.when(t == 0)
    def _():
        xcar[...] = jnp.zeros(xcar.shape, F32)
        qcar[...] = jnp.zeros(qcar.shape, F32)
        hcar[...] = jnp.zeros(hcar.shape, F32)

    h = _norm_mod(x_ref[...], nw_ref[...], ada_ref[3:4, :], ada_ref[4:5, :]).astype(BF16)

    col = functools.partial(_proj_col, h, wlo_ref, whi_ref)
    groups = TP // SUBLANES
    sub = lax.broadcasted_iota(jnp.int32, (1, SUBLANES, 1), 1)
    row = lax.broadcasted_iota(jnp.int32, (TP, 1), 0)
    first = jnp.logical_and(row == 0, t == 0)

    def rglru_block(xr, n):
        ln = slice(n * RG_BLOCK_W, (n + 1) * RG_BLOCK_W)
        xc = _causal_conv(xr[:, ln], xcar, cw_ref, ln) + cb_ref[:, ln]
        xb = xc.astype(BF16)
        r = _sigmoid(_dot(xb, wa_ref[n]) + ba_ref[:, ln])
        i = _sigmoid(_dot(xb, wx_ref[n]) + bx_ref[:, ln])
        log_a = (-RG_C) * r * _softplus(-lam_ref[:, ln])
        a = jnp.exp(log_a)
        mult = jnp.where(first, 1.0, jnp.sqrt(_neg_expm1(2.0 * log_a)))
        a3 = a.reshape(groups, SUBLANES, RG_BLOCK_W)
        b3 = (mult * (i * xc)).reshape(groups, SUBLANES, RG_BLOCK_W)
        d = 1
        while d < SUBLANES:
            keep = sub >= d
            a_prev = jnp.where(keep, pltpu.roll(a3, d, axis=1), 1.0)
            b_prev = jnp.where(keep, pltpu.roll(b3, d, axis=1), 0.0)
            b3 = a3 * b_prev + b3
            a3 = a3 * a_prev
            d *= 2
        acum_s[:, ln] = a3.reshape(TP, RG_BLOCK_W)
        hloc_s[:, ln] = b3.reshape(TP, RG_BLOCK_W)

    def qkv_part(z, c):
        for l0 in range(0, D_MODEL, MIX_CONV_LANES):
            ln = slice(c * D_MODEL + l0, c * D_MODEL + l0 + MIX_CONV_LANES)
            act_ref[:, ln] = _silu(_causal_conv(z[:, l0:l0 + MIX_CONV_LANES], qcar, cqw_ref, ln))

    xr = col(0)
    for n in range(RG_BLOCKS):
        rglru_block(xr, n)
    cs_ref[0] = xcar[SUBLANES - (CONV_W - 1):, :]
    for c in range(DN_QKV // D_MODEL):
        qkv_part(col(2 + c), c)
    cq_ref[0] = qcar[SUBLANES - (CONV_W - 1):, :]

    hrow = hcar[0:1, :]
    for g in range(groups):
        hin_s[g * SUBLANES:(g + 1) * SUBLANES, :] = jnp.broadcast_to(hrow, (SUBLANES, D_RNN))
        e = (g + 1) * SUBLANES - 1
        hrow = acum_s[e:e + 1, :] * hrow + hloc_s[e:e + 1, :]
    hcar[0:1, :] = hrow
    hl_ref[0] = hrow

    y = hloc_s[...] + acum_s[...] * hin_s[...]
    oa_ref[...] = (y * jax.nn.gelu(col(1))).astype(BF16)
    szg_ref[...] = _silu(col(5)).astype(BF16)
    sga_ref[...] = _sigmoid(col(6)).astype(BF16)
    sgb_ref[...] = _sigmoid(col(7)).astype(BF16)
    gb_ref[...] = _dot(h, wab_ref[...])


def _mixin_call(x, ada, nw, w_lo, w_hi, w_ab, cw, cb, wa, ba, wx, bx, lam, cqw, *, batch, seq_len):
    nt = seq_len // TP
    m = batch * seq_len
    row = lambda n: pl.BlockSpec((TP, n), lambda b, t: (b * nt + t, 0))
    per_seq = lambda r, n: pl.BlockSpec((1, r, n), lambda b, t: (b, 0, 0))
    shp = lambda n, dt: jax.ShapeDtypeStruct((m, n), dt)
    blk = (RG_BLOCKS, RG_BLOCK_W, RG_BLOCK_W)
    return pl.pallas_call(
        _mixin_kernel,
        grid=(batch, nt),
        in_specs=[row(D_MODEL), pl.BlockSpec((None, N_ADA, D_MODEL), lambda b, t: (b, 0, 0)),
                  _resident((1, D_MODEL)), _resident((D_MODEL, AB_OFF)), _resident((D_MODEL, N_HI)),
                  _resident((D_MODEL, LANES)),
                  _resident((CONV_W, D_RNN)), _resident((1, D_RNN)), _resident(blk), _resident((1, D_RNN)),
                  _resident(blk), _resident((1, D_RNN)), _resident((1, D_RNN)), _resident((CONV_W, DN_QKV))],
        out_specs=[row(D_RNN), row(DN_QKV), row(DN_V), row(D_MODEL), row(D_MODEL), row(LANES),
                   per_seq(1, D_RNN), per_seq(CONV_W - 1, D_RNN), per_seq(CONV_W - 1, DN_QKV)],
        out_shape=[shp(D_RNN, BF16), shp(DN_QKV, F32), shp(DN_V, BF16), shp(D_MODEL, BF16), shp(D_MODEL, BF16),
                   shp(LANES, F32),
                   jax.ShapeDtypeStruct((batch, 1, D_RNN), F32),
                   jax.ShapeDtypeStruct((batch, CONV_W - 1, D_RNN), F32),
                   jax.ShapeDtypeStruct((batch, CONV_W - 1, DN_QKV), F32)],
        scratch_shapes=[pltpu.VMEM((SUBLANES, D_RNN), F32),
                        pltpu.VMEM((SUBLANES, DN_QKV), F32),
                        pltpu.VMEM((SUBLANES, D_RNN), F32),
                        pltpu.VMEM((TP, D_RNN), F32),
                        pltpu.VMEM((TP, D_RNN), F32),
                        pltpu.VMEM((TP, D_RNN), F32)],
        compiler_params=_params(("arbitrary", "arbitrary")),
        name="mixin",
    )(x, ada, nw, w_lo, w_hi, w_ab, cw, cb, wa, ba, wx, bx, lam, cqw)


def _split(x):
    hi = x.astype(BF16)
    return hi, (x - hi.astype(F32)).astype(BF16)


def _dot3(a_hi, a_lo, b):
    n = b.shape[1]
    b_hi, b_lo = _split(b)
    r = _dot(a_hi, jnp.concatenate([b_hi, b_lo], axis=1))
    return r[:, :n] + r[:, n:] + _dot(a_lo, b_hi)


def _pair_mask(ti, tj, lvl):
    same = (ti >> (lvl + 1)) == (tj >> (lvl + 1))
    return jnp.logical_and(same, jnp.logical_and(((ti >> lvl) & 1) == 1, ((tj >> lvl) & 1) == 0))


DELTA_CPI = 4


def _delta_kernel(act_s, gb_ref, szg_ref, alog_ref, dtb_ref, dnw_ref,
                  ob_ref, s_ref,
                  s_acc, o_s, u_s, wq_s, attn_s, kd_s, egl_s, *, nt):
    t = pl.program_id(1)
    c = DN_CHUNK
    heads = range(DN_HEADS)

    @pl.when(t == 0)
    def _():
        s_acc[...] = jnp.zeros(s_acc.shape, F32)

    ti = lax.broadcasted_iota(jnp.int32, (c, 2 * c), 0)
    tj = lax.broadcasted_iota(jnp.int32, (c, 2 * c), 1)
    left = tj < c
    incl = jnp.logical_and(ti >= tj, left)
    strict = jnp.logical_and(ti > tj, left)
    eye_l = (ti == tj).astype(F32)
    cum_l = incl[:, :c].astype(F32)
    cum_r = jnp.logical_and(ti <= tj, left).astype(F32)
    last = c - 1
    zpad = jnp.zeros((c, DN_DK), BF16)
    neg_a = -jnp.exp(alog_ref[...])
    dtb = dtb_ref[...]

    def chunk_rows(ref, j, lanes):
        return ref[pl.ds(pl.multiple_of(j * c, c), c), lanes]

    def prep(it, _):
        js = [it * DELTA_CPI + k for k in range(DELTA_CPI)]
        gcum, gcum_t, beta = [], [], []
        for j in js:
            gbv = chunk_rows(gb_ref, j, slice(None))
            g = neg_a * _softplus(gbv + dtb)
            beta.append(_sigmoid(gbv))
            gcum.append(jnp.dot(cum_l, g, precision=HI, preferred_element_type=F32))
            gcum_t.append(_dot_tn(g, cum_r, precision=HI))
        chains = [(k, h) for k in range(DELTA_CPI) for h in heads]
        gc, decay, qn, kn, kn16, kb, vb = {}, {}, {}, {}, {}, {}, {}
        for k, h in chains:
            j = js[k]
            gc[k, h] = gcum[k][:, h:h + 1]
            decay[k, h] = jnp.exp(jnp.where(incl, gc[k, h] - gcum_t[k][h:h + 1, :], -jnp.inf))
            bh = beta[k][:, DN_HEADS + h:DN_HEADS + h + 1]
            qh = chunk_rows(act_s, j, slice(h * DN_DK, (h + 1) * DN_DK))
            kh = chunk_rows(act_s, j, slice(DN_QK + h * DN_DK, DN_QK + (h + 1) * DN_DK))
            vh = chunk_rows(act_s, j, slice(2 * DN_QK + h * DN_DV, 2 * DN_QK + (h + 1) * DN_DV))
            qn[k, h] = qh * lax.rsqrt(jnp.sum(qh * qh, axis=-1, keepdims=True) + EPS) * (DN_DK ** -0.5)
            kn[k, h] = kh * lax.rsqrt(jnp.sum(kh * kh, axis=-1, keepdims=True) + EPS)
            kn16[k, h] = kn[k, h].astype(BF16)
            kb[k, h] = kn[k, h] * bh
            vb[k, h] = vh * bh
        lm = {}
        for ch in chains:
            kq = _dot_nt(jnp.concatenate([kb[ch], qn[ch]], axis=0).astype(BF16),
                         jnp.concatenate([kn16[ch], zpad], axis=0))
            lm[ch] = jnp.where(strict, kq[:c] * decay[ch], 0.0)
            attn_s[js[ch[0]], ch[1]] = (kq[c:] * decay[ch])[:, :c].astype(BF16)
        d = {ch: eye_l - jnp.where(_pair_mask(ti, tj, 0), lm[ch], 0.0) for ch in chains}
        for lvl in range(1, c.bit_length() - 1):
            pair = _pair_mask(ti, tj, lvl)
            d16 = {ch: d[ch].astype(BF16) for ch in chains}
            ed = {ch: _dot(jnp.where(pair, lm[ch], 0.0).astype(BF16)[:, :c], d16[ch]) for ch in chains}
            d = {ch: d[ch] - _dot(d16[ch][:, :c], ed[ch].astype(BF16)) for ch in chains}
        eg = {ch: jnp.exp(gc[ch]) for ch in chains}
        a16 = {ch: d[ch].astype(BF16)[:, :c] for ch in chains}
        rhs = {ch: jnp.concatenate([vb[ch], kb[ch] * eg[ch]], axis=1) for ch in chains}
        x0 = {ch: _dot(a16[ch], rhs[ch].astype(BF16)) for ch in chains}
        res = {}
        for ch in chains:
            l_hi, l_lo = _split(lm[ch])
            res[ch] = rhs[ch] - x0[ch] - _dot3(l_hi[:, :c], l_lo[:, :c], x0[ch])
        for ch in chains:
            k, h = ch
            x = x0[ch] + _dot(a16[ch], res[ch].astype(BF16))
            u_s[js[k], h] = x[:, :DN_DV]
            wq_s[js[k], h] = jnp.concatenate([x[:, DN_DV:], qn[ch] * eg[ch]], axis=0).astype(BF16)
            gl = gcum[k][last:last + 1, h:h + 1]
            kd_s[js[k], h] = (kn[ch] * jnp.exp(gl - gc[ch])).astype(BF16)
        for k, j in enumerate(js):
            egl_s[pl.ds(pl.multiple_of(j * SUBLANES, SUBLANES), SUBLANES), :] = jnp.broadcast_to(
                jnp.exp(gcum[k][last:last + 1, :]), (SUBLANES, LANES))
        return 0

    lax.fori_loop(0, NCH // DELTA_CPI, prep, 0)

    def recur(j, _):
        egl = egl_s[pl.ds(pl.multiple_of(j * SUBLANES, SUBLANES), 1), :]
        s_old = [s_acc[h] for h in heads]
        wq = [_dot(wq_s[j, h], s_old[h].astype(BF16)) for h in heads]
        vnew16 = [(u_s[j, h] - wq[h][:c]).astype(BF16) for h in heads]
        o = [wq[h][c:] + _dot(attn_s[j, h], vnew16[h]) for h in heads]
        for h in heads:
            s_acc[h] = s_old[h] * egl[:, h:h + 1] + _dot_tn(kd_s[j, h], vnew16[h])
        for h in heads:
            on = o[h] * lax.rsqrt(jnp.mean(o[h] * o[h], axis=-1, keepdims=True) + EPS) * dnw_ref[...]
            o_s[pl.ds(pl.multiple_of(j * c, c), c), h * DN_DV:(h + 1) * DN_DV] = on
        return 0

    lax.fori_loop(0, NCH, recur, 0)

    ob_ref[...] = (o_s[...] * szg_ref[...].astype(F32)).astype(BF16)

    @pl.when(t == nt - 1)
    def _():
        s_ref[0] = s_acc[...]


def _delta_call(act, gb, szg, alog, dtb, dnw, *, batch, seq_len):
    c = DN_CHUNK
    nt = seq_len // TP
    row = lambda n: pl.BlockSpec((TP, n), lambda b, t: (b * nt + t, 0))
    return pl.pallas_call(
        functools.partial(_delta_kernel, nt=nt),
        grid=(batch, nt),
        in_specs=[row(DN_QKV), row(LANES), row(DN_V),
                  _resident((1, LANES)), _resident((1, LANES)), _resident((1, DN_DV))],
        out_specs=[row(DN_V),
                   pl.BlockSpec((1, DN_HEADS, DN_DK, DN_DV), lambda b, t: (b, 0, 0, 0))],
        out_shape=[jax.ShapeDtypeStruct((batch * seq_len, DN_V), BF16),
                   jax.ShapeDtypeStruct((batch, DN_HEADS, DN_DK, DN_DV), F32)],
        scratch_shapes=[pltpu.VMEM((DN_HEADS, DN_DK, DN_DV), F32),
                        pltpu.VMEM((TP, DN_V), F32),
                        pltpu.VMEM((NCH, DN_HEADS, c, DN_DV), F32),
                        pltpu.VMEM((NCH, DN_HEADS, 2 * c, DN_DK), BF16),
                        pltpu.VMEM((NCH, DN_HEADS, c, c), BF16),
                        pltpu.VMEM((NCH, DN_HEADS, c, DN_DK), BF16),
                        pltpu.VMEM((NCH * SUBLANES, LANES), F32)],
        compiler_params=_params(("arbitrary", "arbitrary")),
        name="delta",
    )(act, gb, szg, alog, dtb, dnw)


def _sample_pre_kernel(xr_ref, ggr_ref, qkv_ref, gb_ref, crnn_ref, cqkv_ref, h0_ref,
                       cw_ref, cb_ref, wa_ref, ba_ref, wx_ref, bx_ref, lam_ref,
                       cqw_ref, alog_ref, dtb_ref,
                       oa_ref, hn_ref, qn_ref, kn_ref, v_ref, eg_ref, beta_ref):
    xc = cb_ref[...] + cw_ref[CONV_W - 1:CONV_W, :] * xr_ref[...]
    for j in range(CONV_W - 1):
        xc = xc + cw_ref[j:j + 1, :] * crnn_ref[j]
    a, log_a, i = _rg_gates(xc, wa_ref, ba_ref[...], wx_ref, bx_ref[...], lam_ref[...])
    mult = jnp.sqrt(_neg_expm1(2.0 * log_a))
    h = a * h0_ref[...] + mult * (i * xc)
    hn_ref[...] = h
    oa_ref[...] = (h * ggr_ref[...].astype(F32)).astype(BF16)

    qc = cqw_ref[CONV_W - 1:CONV_W, :] * qkv_ref[...]
    for j in range(CONV_W - 1):
        qc = qc + cqw_ref[j:j + 1, :] * cqkv_ref[j]
    act = _silu(qc)
    for h_ in range(DN_HEADS):
        qh = act[:, h_ * DN_DK:(h_ + 1) * DN_DK]
        kh = act[:, DN_QK + h_ * DN_DK:DN_QK + (h_ + 1) * DN_DK]
        qn_ref[:, h_ * DN_DK:(h_ + 1) * DN_DK] = (
            qh * lax.rsqrt(jnp.sum(qh * qh, axis=-1, keepdims=True) + EPS) * (DN_DK ** -0.5))
        kn_ref[:, h_ * DN_DK:(h_ + 1) * DN_DK] = (
            kh * lax.rsqrt(jnp.sum(kh * kh, axis=-1, keepdims=True) + EPS))
    v_ref[...] = act[:, 2 * DN_QK:]
    gbv = gb_ref[...]
    eg_ref[...] = jnp.exp(-jnp.exp(alog_ref[...]) * _softplus(gbv + dtb_ref[...]))
    beta_ref[...] = _sigmoid(gbv)


def _sample_pre_call(xr, ggr, qkv, gb, crnn, cqkv, h0, cw, cb, wa, ba, wx, bx, lam, cqw, alog, dtb):
    n = xr.shape[0]
    args = (xr, ggr, qkv, gb, crnn, cqkv, h0, cw, cb, wa, ba, wx, bx, lam, cqw, alog, dtb)
    shp = lambda w: jax.ShapeDtypeStruct((n, w), F32)
    outs = [jax.ShapeDtypeStruct((n, D_RNN), BF16), shp(D_RNN), shp(DN_QK), shp(DN_QK), shp(DN_V), shp(LANES), shp(LANES)]
    return pl.pallas_call(
        _sample_pre_kernel,
        grid=(1,),
        in_specs=[_resident(a.shape) for a in args],
        out_specs=[pl.BlockSpec(o.shape, lambda i: (0, 0)) for o in outs],
        out_shape=outs,
        compiler_params=_params(("arbitrary",)),
        name="sample_pre",
    )(*args)


def _sample_state_kernel(qn_ref, kn_ref, v_ref, eg_ref, beta_ref, szg_ref, dnw_ref, s_ref,
                         ob_ref, sn_ref):
    srow = lax.broadcasted_iota(jnp.int32, (SUBLANES, DN_DK), 0)
    rows = lambda r0, r1: jnp.where(srow == 0, r0, jnp.where(srow == 1, r1, 0.0))
    chains = [(b, h) for b in range(SAMPLE_BB) for h in range(DN_HEADS)]
    sl = lambda h: slice(h * DN_DK, (h + 1) * DN_DK)
    q = {(b, h): qn_ref[0, b:b + 1, sl(h)] for b, h in chains}
    k = {(b, h): kn_ref[0, b:b + 1, sl(h)] for b, h in chains}
    eg = {(b, h): eg_ref[0, b:b + 1, h:h + 1] for b, h in chains}
    beta = {(b, h): beta_ref[0, b:b + 1, DN_HEADS + h:DN_HEADS + h + 1] for b, h in chains}
    ws_qs = {ch: _dot(rows((k[ch] * beta[ch]) * eg[ch], q[ch] * eg[ch]).astype(BF16),
                      s_ref[ch[0], ch[1]].astype(BF16)) for ch in chains}
    vnew = {(b, h): v_ref[0, b:b + 1, sl(h)] * beta[b, h] - ws_qs[b, h][0:1, :] for b, h in chains}
    for ch in chains:
        b, h = ch
        zero = jnp.zeros_like(k[ch])
        sn_ref[b, h] = s_ref[b, h] * eg[ch] + _dot_tn(rows(k[ch], zero).astype(BF16),
                                                     rows(vnew[ch], zero).astype(BF16))
    for ch in chains:
        b, h = ch
        o = ws_qs[ch][1:2, :] + jnp.sum(q[ch] * k[ch], axis=-1, keepdims=True) * vnew[ch]
        on = o * lax.rsqrt(jnp.mean(o * o, axis=-1, keepdims=True) + EPS) * dnw_ref[...]
        ob_ref[0, b:b + 1, sl(h)] = on * szg_ref[0, b:b + 1, sl(h)]


SAMPLE_BB = 8


def _sample_state_call(qn, kn, v, eg, beta, szg, dnw, s0):
    n = qn.shape[0]
    bb = SAMPLE_BB
    r3 = lambda a: a.reshape(n // bb, bb, a.shape[-1])
    vec = lambda w: pl.BlockSpec((1, bb, w), lambda i: (i, 0, 0))
    s_spec = pl.BlockSpec((bb, DN_HEADS, DN_DK, DN_DV), lambda i: (i, 0, 0, 0))
    ob, sn = pl.pallas_call(
        _sample_state_kernel,
        grid=(n // bb,),
        in_specs=[vec(DN_QK), vec(DN_QK), vec(DN_V), vec(LANES), vec(LANES), vec(DN_V),
                  _resident((1, DN_DV)), s_spec],
        out_specs=[vec(DN_V), s_spec],
        out_shape=[jax.ShapeDtypeStruct((n // bb, bb, DN_V), F32),
                   jax.ShapeDtypeStruct((n, DN_HEADS, DN_DK, DN_DV), F32)],
        compiler_params=_params(("arbitrary",)),
        name="sample_state",
    )(r3(qn), r3(kn), r3(v), r3(eg), r3(beta), r3(szg), dnw, s0)
    return ob.reshape(n, DN_V), sn


def _merge_kernel(x_ref, oa_ref, ob_ref, sga_ref, sgb_ref, ada_ref, wb_ref, wo_ref, o_ref, *, per_row):
    ya = _dot(oa_ref[...], wb_ref[0].astype(BF16))
    yb = _dot(ob_ref[...], wb_ref[1].astype(BF16))
    merged = (sga_ref[...].astype(F32) * ya + sgb_ref[...].astype(F32) * yb).astype(BF16)
    o_ref[...] = x_ref[...] + _ada_rows(ada_ref, 5, per_row) * _dot(merged, wo_ref[...].astype(BF16))


def _merge_call(x, oa, ob, sga, sgb, ada, wb, wo, *, per_row, tm, seq_len):
    m = x.shape[0]
    row = pl.BlockSpec((tm, D_MODEL), lambda i: (i, 0))
    return pl.pallas_call(
        functools.partial(_merge_kernel, per_row=per_row),
        grid=(m // tm,),
        in_specs=[row, row, row, row, row, _ada_spec(per_row, m, seq_len // tm),
                  _resident((2, D_RNN, D_MODEL)), _resident((D_MODEL, D_MODEL))],
        out_specs=row,
        out_shape=jax.ShapeDtypeStruct((m, D_MODEL), F32),
        compiler_params=_params(("arbitrary",)),
        name="merge",
    )(x, oa, ob, sga, sgb, ada, wb, wo)


def kernel(x_prompt, x_sample, c_prompt, c_sample, state_rglru_h, state_rglru_conv, state_delta_S, state_delta_conv, w_ada, b_ada, norm_ffn1, w_ffn1_up, w_ffn1_down, norm_mix, w_in, conv_rnn_w, conv_rnn_b, rg_w_a, rg_b_a, rg_w_x, rg_b_x, rg_lambda, conv_qkv_w, dn_a_log, dn_dt_bias, dn_norm, w_branch, w_out, norm_ffn2, w_ffn2_up, w_ffn2_down, norm_final):
    batch, seq_len, _ = x_prompt.shape
    n_dec = x_sample.shape[0]
    assert w_ada.shape[0] == 1 and x_sample.shape[1] == 1 and seq_len % TP == 0

    row = lambda a: a.reshape(1, -1).astype(F32)
    wup1, wdn1 = w_ffn1_up[0], w_ffn1_down[0]
    wup2, wdn2 = w_ffn2_up[0], w_ffn2_down[0]
    w_t = jnp.transpose(w_in[0])
    n_ab = 2 * DN_HEADS
    assert w_t.shape[0] == AB_OFF + n_ab + N_HI
    w_lo = _wt_cast_call(w_t, 0, AB_OFF, AB_OFF)
    w_ab = _wt_cast_call(w_t, AB_OFF, n_ab, LANES)
    w_hi = _wt_cast_call(w_t, AB_OFF + n_ab, N_HI, N_HI)
    wb, wo = w_branch[0], w_out[0]
    wa, wx = rg_w_a[0].astype(BF16), rg_w_x[0].astype(BF16)
    lane_pad = lambda a: jnp.pad(a.reshape(1, -1).astype(F32), ((0, 0), (0, LANES - a.size)))
    alog = lane_pad(dn_a_log[0])
    dtb = lane_pad(dn_dt_bias[0])
    nf = row(norm_final)

    ada = _ada_call(jnp.concatenate([c_prompt, c_sample], axis=0), w_ada[0], row(b_ada[0]))
    ada = ada.reshape(batch + n_dec, N_ADA, D_MODEL)
    ada_p = ada[:batch]
    ada_s = jnp.transpose(ada[batch:], (1, 0, 2))

    mixer_w = (conv_rnn_w[0], row(conv_rnn_b[0]), wa, row(rg_b_a[0]), wx, row(rg_b_x[0]), row(rg_lambda[0]))

    kw = dict(per_row=False, seq_len=seq_len, tm=TP)
    xp = x_prompt.reshape(batch * seq_len, D_MODEL)
    xp = _ffn_call(xp, ada_p, row(norm_ffn1[0]), wup1, wdn1, nf, k0=0, final_norm=False, **kw)
    oa, act, szg, sga, sgb, gb, hp, cp, qp = _mixin_call(
        xp, ada_p, row(norm_mix[0]), w_lo, w_hi, w_ab, *mixer_w, conv_qkv_w[0], batch=batch, seq_len=seq_len)
    ob, sp = _delta_call(act, gb, szg, alog, dtb, row(dn_norm[0]), batch=batch, seq_len=seq_len)
    xp = _merge_call(xp, oa, ob, sga, sgb, ada_p, wb, wo, **kw)
    y_prompt = _ffn_call(xp, ada_p, row(norm_ffn2[0]), wup2, wdn2, nf, k0=6, final_norm=True, **kw)

    kw = dict(per_row=True, seq_len=n_dec, tm=n_dec)
    xs = x_sample.reshape(n_dec, D_MODEL)
    xs = _ffn_call(xs, ada_s, row(norm_ffn1[0]), wup1, wdn1, nf, k0=0, final_norm=False, **kw)
    xr_s, ggr_s, qkv_s, szg_s, sga_s, sgb_s, gb_s = _inproj_call(xs, ada_s, row(norm_mix[0]), w_lo, w_hi, w_ab, **kw)
    crnn = jnp.transpose(state_rglru_conv[0], (1, 0, 2))
    cqkv = jnp.transpose(state_delta_conv[0], (1, 0, 2))
    oa_s, hs, qn, kn, v, eg, beta = _sample_pre_call(
        xr_s, ggr_s, qkv_s, gb_s, crnn, cqkv, state_rglru_h[0], *mixer_w, conv_qkv_w[0], alog, dtb)
    ob_s, ss = _sample_state_call(qn, kn, v, eg, beta, szg_s.astype(F32), row(dn_norm[0]), state_delta_S[0])
    xs = _merge_call(xs, oa_s, ob_s.astype(BF16), sga_s, sgb_s, ada_s, wb, wo, **kw)
    y_sample = _ffn_call(xs, ada_s, row(norm_ffn2[0]), wup2, wdn2, nf, k0=6, final_norm=True, **kw)
    cs = jnp.concatenate([state_rglru_conv[0][:, 1:], xr_s[:, None, :]], axis=1)
    qs = jnp.concatenate([state_delta_conv[0][:, 1:], qkv_s[:, None, :]], axis=1)

    return (y_prompt.reshape(batch, seq_len, D_MODEL), y_sample.reshape(n_dec, 1, D_MODEL),
            hp.reshape(1, batch, D_RNN), cp[None], sp[None], qp[None],
            hs[None], cs[None], ss[None], qs[None])
```

```python
import functools

import jax
import jax.numpy as jnp
from jax import lax
from jax.experimental import pallas as pl
from jax.experimental.pallas import tpu as pltpu

D_MODEL = 1024
D_RNN = 1024
RG_BLOCKS = 8
RG_BLOCK_W = D_RNN // RG_BLOCKS
RG_C = 8.0
CONV_W = 4
DN_HEADS = 8
DN_DK = 128
DN_DV = 128
DN_QK = DN_HEADS * DN_DK
DN_V = DN_HEADS * DN_DV
DN_QKV = 2 * DN_QK + DN_V
DN_CHUNK = 64
D_FF = 2816
N_ADA = 9
EPS = 1e-6
LANES = 128
SUBLANES = 8
AB_OFF = D_RNN * 2 + DN_QKV
N_HI = 3 * D_MODEL

BF16 = jnp.bfloat16
F32 = jnp.float32
HI = lax.Precision.HIGHEST

VMEM_LIMIT = 58 * 1024 * 1024


def _params(sem):
    return pltpu.CompilerParams(dimension_semantics=sem, vmem_limit_bytes=VMEM_LIMIT)


def _resident(shape):
    nd = len(shape)
    return pl.BlockSpec(shape, lambda *_: (0,) * nd, pipeline_mode=pl.Buffered(1))


def _dot(a, b):
    return jnp.dot(a, b, preferred_element_type=F32)


def _dot_nt(a, b, precision=None):
    return lax.dot_general(a, b, (((1,), (1,)), ((), ())), precision=precision,
                           preferred_element_type=F32)


def _dot_tn(a, b, precision=None):
    return lax.dot_general(a, b, (((0,), (0,)), ((), ())), precision=precision,
                           preferred_element_type=F32)


def _sigmoid(x):
    return 0.5 * jnp.tanh(0.5 * x) + 0.5


def _silu(x):
    h = 0.5 * x
    return h + h * jnp.tanh(h)


def _softplus(x):
    return jnp.maximum(x, 0.0) + jnp.log1p(jnp.exp(-jnp.abs(x)))


def _neg_expm1(x):
    return -jnp.tanh(0.5 * x) * (jnp.exp(x) + 1.0)


def _ada_rows(ada_ref, k, per_row):
    if per_row:
        return ada_ref[k]
    return ada_ref[k:k + 1, :]


def _norm_mod(x, nw, shift, scale):
    ms = jnp.mean(x * x, axis=-1, keepdims=True)
    hn = x * lax.rsqrt(ms + EPS) * nw
    return hn * (1.0 + scale) + shift


def _ada_kernel(c_ref, w_ref, b_ref, o_ref):
    o_ref[...] = _dot(c_ref[...].astype(BF16), w_ref[...].astype(BF16)) + b_ref[...]


def _ada_call(c_all, w_ada, b_ada):
    n = c_all.shape[0]
    tn = D_MODEL
    return pl.pallas_call(
        _ada_kernel,
        grid=(N_ADA * D_MODEL // tn,),
        in_specs=[pl.BlockSpec((n, D_MODEL), lambda j: (0, 0)),
                  pl.BlockSpec((D_MODEL, tn), lambda j: (0, j)),
                  pl.BlockSpec((1, tn), lambda j: (0, j))],
        out_specs=pl.BlockSpec((n, tn), lambda j: (0, j)),
        out_shape=jax.ShapeDtypeStruct((n, N_ADA * D_MODEL), F32),
        compiler_params=_params(("arbitrary",)),
        name="ada",
    )(c_all, w_ada, b_ada)


FFN_TF = 256


def _ffn_kernel(x_ref, ada_ref, nw_ref, wup_ref, wdn_ref, nf_ref, o_ref, *h_refs, k0, per_row, post):
    x = x_ref[...]
    h = _norm_mod(x, nw_ref[...], _ada_rows(ada_ref, k0, per_row),
                  _ada_rows(ada_ref, k0 + 1, per_row)).astype(BF16)
    acc = jnp.zeros(x.shape, F32)
    for j in range(D_FF // FFN_TF):
        g = _dot(h, wup_ref[:, j * FFN_TF:(j + 1) * FFN_TF].astype(BF16))
        v = _dot(h, wup_ref[:, D_FF + j * FFN_TF:D_FF + (j + 1) * FFN_TF].astype(BF16))
        a = (_silu(g) * v).astype(BF16)
        acc = acc + _dot(a, wdn_ref[j * FFN_TF:(j + 1) * FFN_TF, :].astype(BF16))
    y = x + 0.5 * _ada_rows(ada_ref, k0 + 2, per_row) * acc
    if post == "final":
        ms = jnp.mean(y * y, axis=-1, keepdims=True)
        y = y * lax.rsqrt(ms + EPS) * nf_ref[...]
    else:
        h_refs[0][...] = _norm_mod(y, nf_ref[...], _ada_rows(ada_ref, 3, per_row),
                                   _ada_rows(ada_ref, 4, per_row)).astype(BF16)
    o_ref[...] = y


def _ada_spec(per_row, rows, tiles_per_seq):
    if per_row:
        return pl.BlockSpec((N_ADA, rows, D_MODEL), lambda i: (0, 0, 0))
    return pl.BlockSpec((None, N_ADA, D_MODEL), lambda i: (i // tiles_per_seq, 0, 0))


def _ffn_call(x, ada, nw, wup, wdn, nf, *, k0, per_row, post, tm, seq_len):
    m = x.shape[0]
    kern = functools.partial(_ffn_kernel, k0=k0, per_row=per_row, post=post)
    row = pl.BlockSpec((tm, D_MODEL), lambda i: (i, 0))
    out_specs, out_shape = [row], [jax.ShapeDtypeStruct((m, D_MODEL), F32)]
    if post == "mix":
        out_specs, out_shape = out_specs + [row], out_shape + [jax.ShapeDtypeStruct((m, D_MODEL), BF16)]
    return pl.pallas_call(
        kern,
        grid=(m // tm,),
        in_specs=[pl.BlockSpec((tm, D_MODEL), lambda i: (i, 0)),
                  _ada_spec(per_row, m, seq_len // tm),
                  _resident((1, D_MODEL)),
                  _resident((D_MODEL, 2 * D_FF)),
                  _resident((D_FF, D_MODEL)),
                  _resident((1, D_MODEL))],
        out_specs=out_specs,
        out_shape=out_shape,
        compiler_params=_params(("arbitrary",)),
        name="ffn",
    )(x, ada, nw, wup, wdn, nf)


TP = 512
NCH = TP // DN_CHUNK


WCAST_ROWS = 512


def _wt_cast_kernel(wt_ref, o_ref, *, n_valid):
    o = wt_ref[...].T
    if n_valid < o.shape[1]:
        o = jnp.where(lax.broadcasted_iota(jnp.int32, (1, o.shape[1]), 1) < n_valid, o, 0.0)
    o_ref[...] = o.astype(BF16)


def _wt_cast_call(wt, row0, n_valid, n_out):
    k = wt.shape[1]
    blk = min(WCAST_ROWS, n_out)
    return pl.pallas_call(
        functools.partial(_wt_cast_kernel, n_valid=min(n_valid, blk)),
        grid=(n_out // blk,),
        in_specs=[pl.BlockSpec((pl.Element(blk), pl.Element(k)),
                               lambda i: (pl.multiple_of(row0 + i * blk, SUBLANES), 0))],
        out_specs=pl.BlockSpec((k, blk), lambda i: (0, i)),
        out_shape=jax.ShapeDtypeStruct((k, n_out), BF16),
        compiler_params=_params(("arbitrary",)),
        name="wt_cast",
    )(wt)


def _proj_col(h, wlo_ref, whi_ref, c):
    n_lo = AB_OFF // D_MODEL
    w_ref, c = (wlo_ref, c) if c < n_lo else (whi_ref, c - n_lo)
    return _dot(h, w_ref[:, c * D_MODEL:(c + 1) * D_MODEL])


def _inproj_kernel(h_ref, wlo_ref, whi_ref, wab_ref,
                   xr_ref, ggr_ref, qkv_ref, szg_ref, sga_ref, sgb_ref, gb_ref):
    h = h_ref[...]
    col = functools.partial(_proj_col, h, wlo_ref, whi_ref)
    xr_ref[...] = col(0)
    ggr_ref[...] = jax.nn.gelu(col(1)).astype(BF16)
    for c in range(3):
        qkv_ref[:, c * D_MODEL:(c + 1) * D_MODEL] = col(2 + c)
    szg_ref[...] = _silu(col(5)).astype(BF16)
    sga_ref[...] = _sigmoid(col(6)).astype(BF16)
    sgb_ref[...] = _sigmoid(col(7)).astype(BF16)
    gb_ref[...] = _dot(h, wab_ref[...])


def _inproj_call(h, w_lo, w_hi, w_ab, *, tm):
    m = h.shape[0]
    row = lambda n: pl.BlockSpec((tm, n), lambda i: (i, 0))
    shp = lambda n, dt: jax.ShapeDtypeStruct((m, n), dt)
    return pl.pallas_call(
        _inproj_kernel,
        grid=(m // tm,),
        in_specs=[row(D_MODEL), _resident((D_MODEL, AB_OFF)), _resident((D_MODEL, N_HI)),
                  _resident((D_MODEL, LANES))],
        out_specs=[row(D_RNN), row(D_RNN), row(DN_QKV), row(DN_V), row(D_MODEL), row(D_MODEL), row(LANES)],
        out_shape=[shp(D_RNN, F32), shp(D_RNN, BF16), shp(DN_QKV, F32), shp(DN_V, BF16),
                   shp(D_MODEL, BF16), shp(D_MODEL, BF16), shp(LANES, F32)],
        compiler_params=_params(("arbitrary",)),
        name="inproj",
    )(h, w_lo, w_hi, w_ab)


def _causal_conv(x, carry, cw_ref, lanes):
    t, c = x.shape
    x3 = x.reshape(t // SUBLANES, SUBLANES, c)
    prev8 = carry[:, lanes]
    sub = lax.broadcasted_iota(jnp.int32, (1, SUBLANES, 1), 1)
    acc = cw_ref[CONV_W - 1:CONV_W, lanes][None] * x3
    for j in range(1, CONV_W):
        r = pltpu.roll(x3, j, axis=1)
        rp = jnp.concatenate([pltpu.roll(prev8, j, axis=0)[None], r[:-1]], axis=0)
        acc = acc + cw_ref[CONV_W - 1 - j:CONV_W - j, lanes][None] * jnp.where(sub >= j, r, rp)
    carry[:, lanes] = x[t - SUBLANES:, :]
    return acc.reshape(t, c)


def _rg_gates(xc, wa_ref, ba, wx_ref, bx, lam):
    ra, ix = [], []
    for n in range(RG_BLOCKS):
        xb = xc[:, n * RG_BLOCK_W:(n + 1) * RG_BLOCK_W].astype(BF16)
        ra.append(_dot(xb, wa_ref[n]))
        ix.append(_dot(xb, wx_ref[n]))
    r = _sigmoid(jnp.concatenate(ra, axis=1) + ba)
    i = _sigmoid(jnp.concatenate(ix, axis=1) + bx)
    log_a = (-RG_C) * r * _softplus(-lam)
    return jnp.exp(log_a), log_a, i


MIX_CONV_LANES = 512


def _mixin_kernel(h_ref, wlo_ref, whi_ref, wab_ref,
                  cw_ref, cb_ref, wa_ref, ba_ref, wx_ref, bx_ref, lam_ref, cqw_ref,
                  oa_ref, act_ref, szg_ref, sga_ref, sgb_ref, gb_ref, hl_ref, cs_ref, cq_ref,
                  xcar, qcar, hcar, acum_s, hloc_s, hin_s):
    t = pl.program_id(1)

    @pl.when(t == 0)
    def _():
        xcar[...] = jnp.zeros(xcar.shape, F32)
        qcar[...] = jnp.zeros(qcar.shape, F32)
        hcar[...] = jnp.zeros(hcar.shape, F32)

    h = h_ref[...]
    col = functools.partial(_proj_col, h, wlo_ref, whi_ref)
    groups = TP // SUBLANES
    sub = lax.broadcasted_iota(jnp.int32, (1, SUBLANES, 1), 1)
    row = lax.broadcasted_iota(jnp.int32, (TP, 1), 0)
    first = jnp.logical_and(row == 0, t == 0)

    def rglru_block(xr, n):
        ln = slice(n * RG_BLOCK_W, (n + 1) * RG_BLOCK_W)
        xc = _causal_conv(xr[:, ln], xcar, cw_ref, ln) + cb_ref[:, ln]
        xb = xc.astype(BF16)
        r = _sigmoid(_dot(xb, wa_ref[n]) + ba_ref[:, ln])
        i = _sigmoid(_dot(xb, wx_ref[n]) + bx_ref[:, ln])
        log_a = (-RG_C) * r * _softplus(-lam_ref[:, ln])
        a = jnp.exp(log_a)
        mult = jnp.where(first, 1.0, jnp.sqrt(_neg_expm1(2.0 * log_a)))
        a3 = a.reshape(groups, SUBLANES, RG_BLOCK_W)
        b3 = (mult * (i * xc)).reshape(groups, SUBLANES, RG_BLOCK_W)
        d = 1
        while d < SUBLANES:
            keep = sub >= d
            a_prev = jnp.where(keep, pltpu.roll(a3, d, axis=1), 1.0)
            b_prev = jnp.where(keep, pltpu.roll(b3, d, axis=1), 0.0)
            b3 = a3 * b_prev + b3
            a3 = a3 * a_prev
            d *= 2
        acum_s[:, ln] = a3.reshape(TP, RG_BLOCK_W)
        hloc_s[:, ln] = b3.reshape(TP, RG_BLOCK_W)

    def qkv_part(z, c):
        for l0 in range(0, D_MODEL, MIX_CONV_LANES):
            ln = slice(c * D_MODEL + l0, c * D_MODEL + l0 + MIX_CONV_LANES)
            act_ref[:, ln] = _silu(_causal_conv(z[:, l0:l0 + MIX_CONV_LANES], qcar, cqw_ref, ln))

    xr = col(0)
    for n in range(RG_BLOCKS):
        rglru_block(xr, n)
    cs_ref[0] = xcar[SUBLANES - (CONV_W - 1):, :]
    for c in range(DN_QKV // D_MODEL):
        qkv_part(col(2 + c), c)
    cq_ref[0] = qcar[SUBLANES - (CONV_W - 1):, :]

    hrow = hcar[0:1, :]
    for g in range(groups):
        hin_s[g * SUBLANES:(g + 1) * SUBLANES, :] = jnp.broadcast_to(hrow, (SUBLANES, D_RNN))
        e = (g + 1) * SUBLANES - 1
        hrow = acum_s[e:e + 1, :] * hrow + hloc_s[e:e + 1, :]
    hcar[0:1, :] = hrow
    hl_ref[0] = hrow

    y = hloc_s[...] + acum_s[...] * hin_s[...]
    oa_ref[...] = (y * jax.nn.gelu(col(1))).astype(BF16)
    szg_ref[...] = _silu(col(5)).astype(BF16)
    sga_ref[...] = _sigmoid(col(6)).astype(BF16)
    sgb_ref[...] = _sigmoid(col(7)).astype(BF16)
    gb_ref[...] = _dot(h, wab_ref[...])


def _mixin_call(h, w_lo, w_hi, w_ab, cw, cb, wa, ba, wx, bx, lam, cqw, *, batch, seq_len):
    nt = seq_len // TP
    m = batch * seq_len
    row = lambda n: pl.BlockSpec((TP, n), lambda b, t: (b * nt + t, 0))
    per_seq = lambda r, n: pl.BlockSpec((1, r, n), lambda b, t: (b, 0, 0))
    shp = lambda n, dt: jax.ShapeDtypeStruct((m, n), dt)
    blk = (RG_BLOCKS, RG_BLOCK_W, RG_BLOCK_W)
    return pl.pallas_call(
        _mixin_kernel,
        grid=(batch, nt),
        in_specs=[row(D_MODEL), _resident((D_MODEL, AB_OFF)), _resident((D_MODEL, N_HI)),
                  _resident((D_MODEL, LANES)),
                  _resident((CONV_W, D_RNN)), _resident((1, D_RNN)), _resident(blk), _resident((1, D_RNN)),
                  _resident(blk), _resident((1, D_RNN)), _resident((1, D_RNN)), _resident((CONV_W, DN_QKV))],
        out_specs=[row(D_RNN), row(DN_QKV), row(DN_V), row(D_MODEL), row(D_MODEL), row(LANES),
                   per_seq(1, D_RNN), per_seq(CONV_W - 1, D_RNN), per_seq(CONV_W - 1, DN_QKV)],
        out_shape=[shp(D_RNN, BF16), shp(DN_QKV, F32), shp(DN_V, BF16), shp(D_MODEL, BF16), shp(D_MODEL, BF16),
                   shp(LANES, F32),
                   jax.ShapeDtypeStruct((batch, 1, D_RNN), F32),
                   jax.ShapeDtypeStruct((batch, CONV_W - 1, D_RNN), F32),
                   jax.ShapeDtypeStruct((batch, CONV_W - 1, DN_QKV), F32)],
        scratch_shapes=[pltpu.VMEM((SUBLANES, D_RNN), F32),
                        pltpu.VMEM((SUBLANES, DN_QKV), F32),
                        pltpu.VMEM((SUBLANES, D_RNN), F32),
                        pltpu.VMEM((TP, D_RNN), F32),
                        pltpu.VMEM((TP, D_RNN), F32),
                        pltpu.VMEM((TP, D_RNN), F32)],
        compiler_params=_params(("arbitrary", "arbitrary")),
        name="mixin",
    )(h, w_lo, w_hi, w_ab, cw, cb, wa, ba, wx, bx, lam, cqw)


def _split(x):
    hi = x.astype(BF16)
    return hi, (x - hi.astype(F32)).astype(BF16)


def _dot3(a_hi, a_lo, b):
    n = b.shape[1]
    b_hi, b_lo = _split(b)
    r = _dot(a_hi, jnp.concatenate([b_hi, b_lo], axis=1))
    return r[:, :n] + r[:, n:] + _dot(a_lo, b_hi)


def _pair_mask(ti, tj, lvl):
    same = (ti >> (lvl + 1)) == (tj >> (lvl + 1))
    return jnp.logical_and(same, jnp.logical_and(((ti >> lvl) & 1) == 1, ((tj >> lvl) & 1) == 0))


DELTA_CPI = 4


def _delta_kernel(act_s, gb_ref, szg_ref, alog_ref, dtb_ref, dnw_ref,
                  ob_ref, s_ref,
                  s_acc, o_s, u_s, wq_s, attn_s, kd_s, egl_s, *, nt):
    t = pl.program_id(1)
    c = DN_CHUNK
    heads = range(DN_HEADS)

    @pl.when(t == 0)
    def _():
        s_acc[...] = jnp.zeros(s_acc.shape, F32)

    ti = lax.broadcasted_iota(jnp.int32, (c, 2 * c), 0)
    tj = lax.broadcasted_iota(jnp.int32, (c, 2 * c), 1)
    left = tj < c
    incl = jnp.logical_and(ti >= tj, left)
    strict = jnp.logical_and(ti > tj, left)
    eye_l = (ti == tj).astype(F32)
    cum_l = incl[:, :c].astype(F32)
    cum_r = jnp.logical_and(ti <= tj, left).astype(F32)
    last = c - 1
    zpad = jnp.zeros((c, DN_DK), BF16)
    neg_a = -jnp.exp(alog_ref[...])
    dtb = dtb_ref[...]

    def chunk_rows(ref, j, lanes):
        return ref[pl.ds(pl.multiple_of(j * c, c), c), lanes]

    def prep(it, _):
        js = [it * DELTA_CPI + k for k in range(DELTA_CPI)]
        gcum, gcum_t, beta = [], [], []
        for j in js:
            gbv = chunk_rows(gb_ref, j, slice(None))
            g = neg_a * _softplus(gbv + dtb)
            beta.append(_sigmoid(gbv))
            gcum.append(jnp.dot(cum_l, g, precision=HI, preferred_element_type=F32))
            gcum_t.append(_dot_tn(g, cum_r, precision=HI))
        chains = [(k, h) for k in range(DELTA_CPI) for h in heads]
        gc, decay, qn, kn, kn16, kb, vb = {}, {}, {}, {}, {}, {}, {}
        for k, h in chains:
            j = js[k]
            gc[k, h] = gcum[k][:, h:h + 1]
            decay[k, h] = jnp.exp(jnp.where(incl, gc[k, h] - gcum_t[k][h:h + 1, :], -jnp.inf))
            bh = beta[k][:, DN_HEADS + h:DN_HEADS + h + 1]
            qh = chunk_rows(act_s, j, slice(h * DN_DK, (h + 1) * DN_DK))
            kh = chunk_rows(act_s, j, slice(DN_QK + h * DN_DK, DN_QK + (h + 1) * DN_DK))
            vh = chunk_rows(act_s, j, slice(2 * DN_QK + h * DN_DV, 2 * DN_QK + (h + 1) * DN_DV))
            qn[k, h] = qh * lax.rsqrt(jnp.sum(qh * qh, axis=-1, keepdims=True) + EPS) * (DN_DK ** -0.5)
            kn[k, h] = kh * lax.rsqrt(jnp.sum(kh * kh, axis=-1, keepdims=True) + EPS)
            kn16[k, h] = kn[k, h].astype(BF16)
            kb[k, h] = kn[k, h] * bh
            vb[k, h] = vh * bh
        lm = {}
        for ch in chains:
            kq = _dot_nt(jnp.concatenate([kb[ch], qn[ch]], axis=0).astype(BF16),
                         jnp.concatenate([kn16[ch], zpad], axis=0))
            lm[ch] = jnp.where(strict, kq[:c] * decay[ch], 0.0)
            attn_s[js[ch[0]], ch[1]] = (kq[c:] * decay[ch])[:, :c].astype(BF16)
        d = {ch: eye_l - jnp.where(_pair_mask(ti, tj, 0), lm[ch], 0.0) for ch in chains}
        for lvl in range(1, c.bit_length() - 1):
            pair = _pair_mask(ti, tj, lvl)
            d16 = {ch: d[ch].astype(BF16) for ch in chains}
            ed = {ch: _dot(jnp.where(pair, lm[ch], 0.0).astype(BF16)[:, :c], d16[ch]) for ch in chains}
            d = {ch: d[ch] - _dot(d16[ch][:, :c], ed[ch].astype(BF16)) for ch in chains}
        eg = {ch: jnp.exp(gc[ch]) for ch in chains}
        a16 = {ch: d[ch].astype(BF16)[:, :c] for ch in chains}
        rhs = {ch: jnp.concatenate([vb[ch], kb[ch] * eg[ch]], axis=1) for ch in chains}
        x0 = {ch: _dot(a16[ch], rhs[ch].astype(BF16)) for ch in chains}
        res = {}
        for ch in chains:
            l_hi, l_lo = _split(lm[ch])
            res[ch] = rhs[ch] - x0[ch] - _dot3(l_hi[:, :c], l_lo[:, :c], x0[ch])
        for ch in chains:
            k, h = ch
            x = x0[ch] + _dot(a16[ch], res[ch].astype(BF16))
            u_s[js[k], h] = x[:, :DN_DV]
            wq_s[js[k], h] = jnp.concatenate([x[:, DN_DV:], qn[ch] * eg[ch]], axis=0).astype(BF16)
            gl = gcum[k][last:last + 1, h:h + 1]
            kd_s[js[k], h] = (kn[ch] * jnp.exp(gl - gc[ch])).astype(BF16)
        for k, j in enumerate(js):
            egl_s[pl.ds(pl.multiple_of(j * SUBLANES, SUBLANES), SUBLANES), :] = jnp.broadcast_to(
                jnp.exp(gcum[k][last:last + 1, :]), (SUBLANES, LANES))
        return 0

    lax.fori_loop(0, NCH // DELTA_CPI, prep, 0)

    def recur(j, _):
        egl = egl_s[pl.ds(pl.multiple_of(j * SUBLANES, SUBLANES), 1), :]
        s_old = [s_acc[h] for h in heads]
        wq = [_dot(wq_s[j, h], s_old[h].astype(BF16)) for h in heads]
        vnew16 = [(u_s[j, h] - wq[h][:c]).astype(BF16) for h in heads]
        o = [wq[h][c:] + _dot(attn_s[j, h], vnew16[h]) for h in heads]
        for h in heads:
            s_acc[h] = s_old[h] * egl[:, h:h + 1] + _dot_tn(kd_s[j, h], vnew16[h])
        for h in heads:
            on = o[h] * lax.rsqrt(jnp.mean(o[h] * o[h], axis=-1, keepdims=True) + EPS) * dnw_ref[...]
            o_s[pl.ds(pl.multiple_of(j * c, c), c), h * DN_DV:(h + 1) * DN_DV] = on
        return 0

    lax.fori_loop(0, NCH, recur, 0)

    ob_ref[...] = (o_s[...] * szg_ref[...].astype(F32)).astype(BF16)

    @pl.when(t == nt - 1)
    def _():
        s_ref[0] = s_acc[...]


def _delta_call(act, gb, szg, alog, dtb, dnw, *, batch, seq_len):
    c = DN_CHUNK
    nt = seq_len // TP
    row = lambda n: pl.BlockSpec((TP, n), lambda b, t: (b * nt + t, 0))
    return pl.pallas_call(
        functools.partial(_delta_kernel, nt=nt),
        grid=(batch, nt),
        in_specs=[row(DN_QKV), row(LANES), row(DN_V),
                  _resident((1, LANES)), _resident((1, LANES)), _resident((1, DN_DV))],
        out_specs=[row(DN_V),
                   pl.BlockSpec((1, DN_HEADS, DN_DK, DN_DV), lambda b, t: (b, 0, 0, 0))],
        out_shape=[jax.ShapeDtypeStruct((batch * seq_len, DN_V), BF16),
                   jax.ShapeDtypeStruct((batch, DN_HEADS, DN_DK, DN_DV), F32)],
        scratch_shapes=[pltpu.VMEM((DN_HEADS, DN_DK, DN_DV), F32),
                        pltpu.VMEM((TP, DN_V), F32),
                        pltpu.VMEM((NCH, DN_HEADS, c, DN_DV), F32),
                        pltpu.VMEM((NCH, DN_HEADS, 2 * c, DN_DK), BF16),
                        pltpu.VMEM((NCH, DN_HEADS, c, c), BF16),
                        pltpu.VMEM((NCH, DN_HEADS, c, DN_DK), BF16),
                        pltpu.VMEM((NCH * SUBLANES, LANES), F32)],
        compiler_params=_params(("arbitrary", "arbitrary")),
        name="delta",
    )(act, gb, szg, alog, dtb, dnw)


def _sample_pre_kernel(xr_ref, ggr_ref, qkv_ref, gb_ref, crnn_ref, cqkv_ref, h0_ref,
                       cw_ref, cb_ref, wa_ref, ba_ref, wx_ref, bx_ref, lam_ref,
                       cqw_ref, alog_ref, dtb_ref,
                       oa_ref, hn_ref, qn_ref, kn_ref, v_ref, eg_ref, beta_ref):
    xc = cb_ref[...] + cw_ref[CONV_W - 1:CONV_W, :] * xr_ref[...]
    for j in range(CONV_W - 1):
        xc = xc + cw_ref[j:j + 1, :] * crnn_ref[j]
    a, log_a, i = _rg_gates(xc, wa_ref, ba_ref[...], wx_ref, bx_ref[...], lam_ref[...])
    mult = jnp.sqrt(_neg_expm1(2.0 * log_a))
    h = a * h0_ref[...] + mult * (i * xc)
    hn_ref[...] = h
    oa_ref[...] = (h * ggr_ref[...].astype(F32)).astype(BF16)

    qc = cqw_ref[CONV_W - 1:CONV_W, :] * qkv_ref[...]
    for j in range(CONV_W - 1):
        qc = qc + cqw_ref[j:j + 1, :] * cqkv_ref[j]
    act = _silu(qc)
    for h_ in range(DN_HEADS):
        qh = act[:, h_ * DN_DK:(h_ + 1) * DN_DK]
        kh = act[:, DN_QK + h_ * DN_DK:DN_QK + (h_ + 1) * DN_DK]
        qn_ref[:, h_ * DN_DK:(h_ + 1) * DN_DK] = (
            qh * lax.rsqrt(jnp.sum(qh * qh, axis=-1, keepdims=True) + EPS) * (DN_DK ** -0.5))
        kn_ref[:, h_ * DN_DK:(h_ + 1) * DN_DK] = (
            kh * lax.rsqrt(jnp.sum(kh * kh, axis=-1, keepdims=True) + EPS))
    v_ref[...] = act[:, 2 * DN_QK:]
    gbv = gb_ref[...]
    eg_ref[...] = jnp.exp(-jnp.exp(alog_ref[...]) * _softplus(gbv + dtb_ref[...]))
    beta_ref[...] = _sigmoid(gbv)


def _sample_pre_call(xr, ggr, qkv, gb, crnn, cqkv, h0, cw, cb, wa, ba, wx, bx, lam, cqw, alog, dtb):
    n = xr.shape[0]
    args = (xr, ggr, qkv, gb, crnn, cqkv, h0, cw, cb, wa, ba, wx, bx, lam, cqw, alog, dtb)
    shp = lambda w: jax.ShapeDtypeStruct((n, w), F32)
    outs = [jax.ShapeDtypeStruct((n, D_RNN), BF16), shp(D_RNN), shp(DN_QK), shp(DN_QK), shp(DN_V), shp(LANES), shp(LANES)]
    return pl.pallas_call(
        _sample_pre_kernel,
        grid=(1,),
        in_specs=[_resident(a.shape) for a in args],
        out_specs=[pl.BlockSpec(o.shape, lambda i: (0, 0)) for o in outs],
        out_shape=outs,
        compiler_params=_params(("arbitrary",)),
        name="sample_pre",
    )(*args)


def _sample_state_kernel(qn_ref, kn_ref, v_ref, eg_ref, beta_ref, szg_ref, dnw_ref, s_ref,
                         ob_ref, sn_ref):
    srow = lax.broadcasted_iota(jnp.int32, (SUBLANES, DN_DK), 0)
    rows = lambda r0, r1: jnp.where(srow == 0, r0, jnp.where(srow == 1, r1, 0.0))
    chains = [(b, h) for b in range(SAMPLE_BB) for h in range(DN_HEADS)]
    sl = lambda h: slice(h * DN_DK, (h + 1) * DN_DK)
    q = {(b, h): qn_ref[0, b:b + 1, sl(h)] for b, h in chains}
    k = {(b, h): kn_ref[0, b:b + 1, sl(h)] for b, h in chains}
    eg = {(b, h): eg_ref[0, b:b + 1, h:h + 1] for b, h in chains}
    beta = {(b, h): beta_ref[0, b:b + 1, DN_HEADS + h:DN_HEADS + h + 1] for b, h in chains}
    ws_qs = {ch: _dot(rows((k[ch] * beta[ch]) * eg[ch], q[ch] * eg[ch]).astype(BF16),
                      s_ref[ch[0], ch[1]].astype(BF16)) for ch in chains}
    vnew = {(b, h): v_ref[0, b:b + 1, sl(h)] * beta[b, h] - ws_qs[b, h][0:1, :] for b, h in chains}
    for ch in chains:
        b, h = ch
        zero = jnp.zeros_like(k[ch])
        sn_ref[b, h] = s_ref[b, h] * eg[ch] + _dot_tn(rows(k[ch], zero).astype(BF16),
                                                     rows(vnew[ch], zero).astype(BF16))
    for ch in chains:
        b, h = ch
        o = ws_qs[ch][1:2, :] + jnp.sum(q[ch] * k[ch], axis=-1, keepdims=True) * vnew[ch]
        on = o * lax.rsqrt(jnp.mean(o * o, axis=-1, keepdims=True) + EPS) * dnw_ref[...]
        ob_ref[0, b:b + 1, sl(h)] = on * szg_ref[0, b:b + 1, sl(h)]


SAMPLE_BB = 8


def _sample_state_call(qn, kn, v, eg, beta, szg, dnw, s0):
    n = qn.shape[0]
    bb = SAMPLE_BB
    r3 = lambda a: a.reshape(n // bb, bb, a.shape[-1])
    vec = lambda w: pl.BlockSpec((1, bb, w), lambda i: (i, 0, 0))
    s_spec = pl.BlockSpec((bb, DN_HEADS, DN_DK, DN_DV), lambda i: (i, 0, 0, 0))
    ob, sn = pl.pallas_call(
        _sample_state_kernel,
        grid=(n // bb,),
        in_specs=[vec(DN_QK), vec(DN_QK), vec(DN_V), vec(LANES), vec(LANES), vec(DN_V),
                  _resident((1, DN_DV)), s_spec],
        out_specs=[vec(DN_V), s_spec],
        out_shape=[jax.ShapeDtypeStruct((n // bb, bb, DN_V), F32),
                   jax.ShapeDtypeStruct((n, DN_HEADS, DN_DK, DN_DV), F32)],
        compiler_params=_params(("arbitrary",)),
        name="sample_state",
    )(r3(qn), r3(kn), r3(v), r3(eg), r3(beta), r3(szg), dnw, s0)
    return ob.reshape(n, DN_V), sn


def _merge_kernel(x_ref, oa_ref, ob_ref, sga_ref, sgb_ref, ada_ref, wb_ref, wo_ref, o_ref, *, per_row):
    ya = _dot(oa_ref[...], wb_ref[0].astype(BF16))
    yb = _dot(ob_ref[...], wb_ref[1].astype(BF16))
    merged = (sga_ref[...].astype(F32) * ya + sgb_ref[...].astype(F32) * yb).astype(BF16)
    o_ref[...] = x_ref[...] + _ada_rows(ada_ref, 5, per_row) * _dot(merged, wo_ref[...].astype(BF16))


def _merge_call(x, oa, ob, sga, sgb, ada, wb, wo, *, per_row, tm, seq_len):
    m = x.shape[0]
    row = pl.BlockSpec((tm, D_MODEL), lambda i: (i, 0))
    return pl.pallas_call(
        functools.partial(_merge_kernel, per_row=per_row),
        grid=(m // tm,),
        in_specs=[row, row, row, row, row, _ada_spec(per_row, m, seq_len // tm),
                  _resident((2, D_RNN, D_MODEL)), _resident((D_MODEL, D_MODEL))],
        out_specs=row,
        out_shape=jax.ShapeDtypeStruct((m, D_MODEL), F32),
        compiler_params=_params(("arbitrary",)),
        name="merge",
    )(x, oa, ob, sga, sgb, ada, wb, wo)


def kernel(x_prompt, x_sample, c_prompt, c_sample, state_rglru_h, state_rglru_conv, state_delta_S, state_delta_conv, w_ada, b_ada, norm_ffn1, w_ffn1_up, w_ffn1_down, norm_mix, w_in, conv_rnn_w, conv_rnn_b, rg_w_a, rg_b_a, rg_w_x, rg_b_x, rg_lambda, conv_qkv_w, dn_a_log, dn_dt_bias, dn_norm, w_branch, w_out, norm_ffn2, w_ffn2_up, w_ffn2_down, norm_final):
    batch, seq_len, _ = x_prompt.shape
    n_dec = x_sample.shape[0]
    assert w_ada.shape[0] == 1 and x_sample.shape[1] == 1 and seq_len % TP == 0

    row = lambda a: a.reshape(1, -1).astype(F32)
    wup1, wdn1 = w_ffn1_up[0], w_ffn1_down[0]
    wup2, wdn2 = w_ffn2_up[0], w_ffn2_down[0]
    w_t = jnp.transpose(w_in[0])
    n_ab = 2 * DN_HEADS
    assert w_t.shape[0] == AB_OFF + n_ab + N_HI
    w_lo = _wt_cast_call(w_t, 0, AB_OFF, AB_OFF)
    w_ab = _wt_cast_call(w_t, AB_OFF, n_ab, LANES)
    w_hi = _wt_cast_call(w_t, AB_OFF + n_ab, N_HI, N_HI)
    wb, wo = w_branch[0], w_out[0]
    wa, wx = rg_w_a[0].astype(BF16), rg_w_x[0].astype(BF16)
    lane_pad = lambda a: jnp.pad(a.reshape(1, -1).astype(F32), ((0, 0), (0, LANES - a.size)))
    alog = lane_pad(dn_a_log[0])
    dtb = lane_pad(dn_dt_bias[0])
    nf = row(norm_final)

    ada = _ada_call(jnp.concatenate([c_prompt, c_sample], axis=0), w_ada[0], row(b_ada[0]))
    ada = ada.reshape(batch + n_dec, N_ADA, D_MODEL)
    ada_p = ada[:batch]
    ada_s = jnp.transpose(ada[batch:], (1, 0, 2))

    mixer_w = (conv_rnn_w[0], row(conv_rnn_b[0]), wa, row(rg_b_a[0]), wx, row(rg_b_x[0]), row(rg_lambda[0]))

    kw = dict(per_row=False, seq_len=seq_len, tm=TP)
    xp = x_prompt.reshape(batch * seq_len, D_MODEL)
    xp, hmix = _ffn_call(xp, ada_p, row(norm_ffn1[0]), wup1, wdn1, row(norm_mix[0]), k0=0, post="mix", **kw)
    oa, act, szg, sga, sgb, gb, hp, cp, qp = _mixin_call(
        hmix, w_lo, w_hi, w_ab, *mixer_w, conv_qkv_w[0], batch=batch, seq_len=seq_len)
    ob, sp = _delta_call(act, gb, szg, alog, dtb, row(dn_norm[0]), batch=batch, seq_len=seq_len)
    xp = _merge_call(xp, oa, ob, sga, sgb, ada_p, wb, wo, **kw)
    y_prompt, = _ffn_call(xp, ada_p, row(norm_ffn2[0]), wup2, wdn2, nf, k0=6, post="final", **kw)

    kw = dict(per_row=True, seq_len=n_dec, tm=n_dec)
    xs = x_sample.reshape(n_dec, D_MODEL)
    xs, hmix_s = _ffn_call(xs, ada_s, row(norm_ffn1[0]), wup1, wdn1, row(norm_mix[0]), k0=0, post="mix", **kw)
    xr_s, ggr_s, qkv_s, szg_s, sga_s, sgb_s, gb_s = _inproj_call(hmix_s, w_lo, w_hi, w_ab, tm=n_dec)
    crnn = jnp.transpose(state_rglru_conv[0], (1, 0, 2))
    cqkv = jnp.transpose(state_delta_conv[0], (1, 0, 2))
    oa_s, hs, qn, kn, v, eg, beta = _sample_pre_call(
        xr_s, ggr_s, qkv_s, gb_s, crnn, cqkv, state_rglru_h[0], *mixer_w, conv_qkv_w[0], alog, dtb)
    ob_s, ss = _sample_state_call(qn, kn, v, eg, beta, szg_s.astype(F32), row(dn_norm[0]), state_delta_S[0])
    xs = _merge_call(xs, oa_s, ob_s.astype(BF16), sga_s, sgb_s, ada_s, wb, wo, **kw)
    y_sample, = _ffn_call(xs, ada_s, row(norm_ffn2[0]), wup2, wdn2, nf, k0=6, post="final", **kw)
    cs = jnp.concatenate([state_rglru_conv[0][:, 1:], xr_s[:, None, :]], axis=1)
    qs = jnp.concatenate([state_delta_conv[0][:, 1:], qkv_s[:, None, :]], axis=1)

    return (y_prompt.reshape(batch, seq_len, D_MODEL), y_sample.reshape(n_dec, 1, D_MODEL),
            hp.reshape(1, batch, D_RNN), cp[None], sp[None], qp[None],
            hs[None], cs[None], ss[None], qs[None])
```

```python
import functools

import jax
import jax.numpy as jnp
from jax import lax
from jax.experimental import pallas as pl
from jax.experimental.pallas import tpu as pltpu

D_MODEL = 1024
D_RNN = 1024
RG_BLOCKS = 8
RG_BLOCK_W = D_RNN // RG_BLOCKS
RG_C = 8.0
CONV_W = 4
DN_HEADS = 8
DN_DK = 128
DN_DV = 128
DN_QK = DN_HEADS * DN_DK
DN_V = DN_HEADS * DN_DV
DN_QKV = 2 * DN_QK + DN_V
DN_CHUNK = 64
D_FF = 2816
N_ADA = 9
EPS = 1e-6
LANES = 128
SUBLANES = 8
AB_OFF = D_RNN * 2 + DN_QKV
N_HI = 3 * D_MODEL

BF16 = jnp.bfloat16
F32 = jnp.float32
HI = lax.Precision.HIGHEST

VMEM_LIMIT = 58 * 1024 * 1024


def _params(sem):
    return pltpu.CompilerParams(dimension_semantics=sem, vmem_limit_bytes=VMEM_LIMIT)


def _resident(shape):
    nd = len(shape)
    return pl.BlockSpec(shape, lambda *_: (0,) * nd, pipeline_mode=pl.Buffered(1))


def _dot(a, b):
    return jnp.dot(a, b, preferred_element_type=F32)


def _dot_nt(a, b, precision=None):
    return lax.dot_general(a, b, (((1,), (1,)), ((), ())), precision=precision,
                           preferred_element_type=F32)


def _dot_tn(a, b, precision=None):
    return lax.dot_general(a, b, (((0,), (0,)), ((), ())), precision=precision,
                           preferred_element_type=F32)


def _sigmoid(x):
    return 0.5 * jnp.tanh(0.5 * x) + 0.5


def _silu(x):
    h = 0.5 * x
    return h + h * jnp.tanh(h)


def _softplus(x):
    return jnp.maximum(x, 0.0) + jnp.log1p(jnp.exp(-jnp.abs(x)))


def _neg_expm1(x):
    return -jnp.tanh(0.5 * x) * (jnp.exp(x) + 1.0)


def _ada_rows(ada_ref, k, seq):
    if seq is None:
        return ada_ref[k]
    return ada_ref[k, pl.ds(seq, 1), :]


def _tile_seq(per_row, tiles_per_seq):
    return None if per_row else pl.program_id(0) // tiles_per_seq


def _norm_mod(x, nw, shift, scale):
    ms = jnp.mean(x * x, axis=-1, keepdims=True)
    hn = x * lax.rsqrt(ms + EPS) * nw
    return hn * (1.0 + scale) + shift


def _ada_kernel(c_ref, w_ref, b_ref, o_ref):
    o_ref[...] = _dot(c_ref[...].astype(BF16), w_ref[...].astype(BF16)) + b_ref[...]


def _ada_call(c_all, w_ada, b_ada):
    n = c_all.shape[0]
    return pl.pallas_call(
        _ada_kernel,
        grid=(N_ADA,),
        in_specs=[pl.BlockSpec((n, D_MODEL), lambda j: (0, 0)),
                  pl.BlockSpec((D_MODEL, D_MODEL), lambda j: (0, j)),
                  pl.BlockSpec((1, D_MODEL), lambda j: (0, j))],
        out_specs=pl.BlockSpec((None, n, D_MODEL), lambda j: (j, 0, 0)),
        out_shape=jax.ShapeDtypeStruct((N_ADA, n, D_MODEL), F32),
        compiler_params=_params(("arbitrary",)),
        name="ada",
    )(c_all, w_ada, b_ada)


FFN_TF = 256


def _ffn_kernel(x_ref, ada_ref, nw_ref, wup_ref, wdn_ref, nf_ref, o_ref, *h_refs, k0, per_row, tps, post):
    seq = _tile_seq(per_row, tps)
    x = x_ref[...]
    h = _norm_mod(x, nw_ref[...], _ada_rows(ada_ref, k0, seq), _ada_rows(ada_ref, k0 + 1, seq)).astype(BF16)
    acc = jnp.zeros(x.shape, F32)
    for j in range(D_FF // FFN_TF):
        g = _dot(h, wup_ref[:, j * FFN_TF:(j + 1) * FFN_TF].astype(BF16))
        v = _dot(h, wup_ref[:, D_FF + j * FFN_TF:D_FF + (j + 1) * FFN_TF].astype(BF16))
        a = (_silu(g) * v).astype(BF16)
        acc = acc + _dot(a, wdn_ref[j * FFN_TF:(j + 1) * FFN_TF, :].astype(BF16))
    y = x + 0.5 * _ada_rows(ada_ref, k0 + 2, seq) * acc
    if post == "final":
        ms = jnp.mean(y * y, axis=-1, keepdims=True)
        y = y * lax.rsqrt(ms + EPS) * nf_ref[...]
    else:
        h_refs[0][...] = _norm_mod(y, nf_ref[...], _ada_rows(ada_ref, 3, seq), _ada_rows(ada_ref, 4, seq)).astype(BF16)
    o_ref[...] = y


def _ada_spec(per_row, n_dec):
    if per_row:
        return pl.BlockSpec((N_ADA, n_dec, D_MODEL), lambda i: (0, 0, 0))
    return pl.BlockSpec((N_ADA, SUBLANES, D_MODEL), lambda i: (0, n_dec // SUBLANES, 0))


def _ffn_call(x, ada, nw, wup, wdn, nf, *, k0, per_row, post, tm, seq_len, n_dec):
    m = x.shape[0]
    kern = functools.partial(_ffn_kernel, k0=k0, per_row=per_row, tps=seq_len // tm, post=post)
    row = pl.BlockSpec((tm, D_MODEL), lambda i: (i, 0))
    out_specs, out_shape = [row], [jax.ShapeDtypeStruct((m, D_MODEL), F32)]
    if post == "mix":
        out_specs, out_shape = out_specs + [row], out_shape + [jax.ShapeDtypeStruct((m, D_MODEL), BF16)]
    return pl.pallas_call(
        kern,
        grid=(m // tm,),
        in_specs=[pl.BlockSpec((tm, D_MODEL), lambda i: (i, 0)),
                  _ada_spec(per_row, n_dec),
                  _resident((1, D_MODEL)),
                  _resident((D_MODEL, 2 * D_FF)),
                  _resident((D_FF, D_MODEL)),
                  _resident((1, D_MODEL))],
        out_specs=out_specs,
        out_shape=out_shape,
        compiler_params=_params(("arbitrary",)),
        name="ffn",
    )(x, ada, nw, wup, wdn, nf)


TP = 512
NCH = TP // DN_CHUNK


WCAST_ROWS = 512


def _wt_cast_kernel(wt_ref, o_ref, *, n_valid):
    o = wt_ref[...].T
    if n_valid < o.shape[1]:
        o = jnp.where(lax.broadcasted_iota(jnp.int32, (1, o.shape[1]), 1) < n_valid, o, 0.0)
    o_ref[...] = o.astype(BF16)


def _wt_cast_call(wt, row0, n_valid, n_out):
    k = wt.shape[1]
    blk = min(WCAST_ROWS, n_out)
    return pl.pallas_call(
        functools.partial(_wt_cast_kernel, n_valid=min(n_valid, blk)),
        grid=(n_out // blk,),
        in_specs=[pl.BlockSpec((pl.Element(blk), pl.Element(k)),
                               lambda i: (pl.multiple_of(row0 + i * blk, SUBLANES), 0))],
        out_specs=pl.BlockSpec((k, blk), lambda i: (0, i)),
        out_shape=jax.ShapeDtypeStruct((k, n_out), BF16),
        compiler_params=_params(("arbitrary",)),
        name="wt_cast",
    )(wt)


def _proj_col(h, wlo_ref, whi_ref, c):
    n_lo = AB_OFF // D_MODEL
    w_ref, c = (wlo_ref, c) if c < n_lo else (whi_ref, c - n_lo)
    return _dot(h, w_ref[:, c * D_MODEL:(c + 1) * D_MODEL])


def _inproj_kernel(h_ref, wlo_ref, whi_ref, wab_ref,
                   xr_ref, ggr_ref, qkv_ref, szg_ref, sga_ref, sgb_ref, gb_ref):
    h = h_ref[...]
    col = functools.partial(_proj_col, h, wlo_ref, whi_ref)
    xr_ref[...] = col(0)
    ggr_ref[...] = jax.nn.gelu(col(1)).astype(BF16)
    for c in range(3):
        qkv_ref[:, c * D_MODEL:(c + 1) * D_MODEL] = col(2 + c)
    szg_ref[...] = _silu(col(5)).astype(BF16)
    sga_ref[...] = _sigmoid(col(6)).astype(BF16)
    sgb_ref[...] = _sigmoid(col(7)).astype(BF16)
    gb_ref[...] = _dot(h, wab_ref[...])


def _inproj_call(h, w_lo, w_hi, w_ab, *, tm):
    m = h.shape[0]
    row = lambda n: pl.BlockSpec((tm, n), lambda i: (i, 0))
    shp = lambda n, dt: jax.ShapeDtypeStruct((m, n), dt)
    return pl.pallas_call(
        _inproj_kernel,
        grid=(m // tm,),
        in_specs=[row(D_MODEL), _resident((D_MODEL, AB_OFF)), _resident((D_MODEL, N_HI)),
                  _resident((D_MODEL, LANES))],
        out_specs=[row(D_RNN), row(D_RNN), row(DN_QKV), row(DN_V), row(D_MODEL), row(D_MODEL), row(LANES)],
        out_shape=[shp(D_RNN, F32), shp(D_RNN, BF16), shp(DN_QKV, F32), shp(DN_V, BF16),
                   shp(D_MODEL, BF16), shp(D_MODEL, BF16), shp(LANES, F32)],
        compiler_params=_params(("arbitrary",)),
        name="inproj",
    )(h, w_lo, w_hi, w_ab)


def _causal_conv(x, carry, cw_ref, lanes):
    t, c = x.shape
    x3 = x.reshape(t // SUBLANES, SUBLANES, c)
    prev8 = carry[:, lanes]
    sub = lax.broadcasted_iota(jnp.int32, (1, SUBLANES, 1), 1)
    acc = cw_ref[CONV_W - 1:CONV_W, lanes][None] * x3
    for j in range(1, CONV_W):
        r = pltpu.roll(x3, j, axis=1)
        rp = jnp.concatenate([pltpu.roll(prev8, j, axis=0)[None], r[:-1]], axis=0)
        acc = acc + cw_ref[CONV_W - 1 - j:CONV_W - j, lanes][None] * jnp.where(sub >= j, r, rp)
    carry[:, lanes] = x[t - SUBLANES:, :]
    return acc.reshape(t, c)


def _rg_gates(xc, wa_ref, ba, wx_ref, bx, lam):
    ra, ix = [], []
    for n in range(RG_BLOCKS):
        xb = xc[:, n * RG_BLOCK_W:(n + 1) * RG_BLOCK_W].astype(BF16)
        ra.append(_dot(xb, wa_ref[n]))
        ix.append(_dot(xb, wx_ref[n]))
    r = _sigmoid(jnp.concatenate(ra, axis=1) + ba)
    i = _sigmoid(jnp.concatenate(ix, axis=1) + bx)
    log_a = (-RG_C) * r * _softplus(-lam)
    return jnp.exp(log_a), log_a, i


MIX_CONV_LANES = 512


def _mixin_kernel(h_ref, wlo_ref, whi_ref, wab_ref,
                  cw_ref, cb_ref, wa_ref, ba_ref, wx_ref, bx_ref, lam_ref, cqw_ref,
                  oa_ref, act_ref, szg_ref, sga_ref, sgb_ref, gb_ref, hl_ref, cs_ref, cq_ref,
                  xcar, qcar, hcar, acum_s, hloc_s, hin_s):
    t = pl.program_id(1)

    @pl.when(t == 0)
    def _():
        xcar[...] = jnp.zeros(xcar.shape, F32)
        qcar[...] = jnp.zeros(qcar.shape, F32)
        hcar[...] = jnp.zeros(hcar.shape, F32)

    h = h_ref[...]
    col = functools.partial(_proj_col, h, wlo_ref, whi_ref)
    groups = TP // SUBLANES
    sub = lax.broadcasted_iota(jnp.int32, (1, SUBLANES, 1), 1)
    row = lax.broadcasted_iota(jnp.int32, (TP, 1), 0)
    first = jnp.logical_and(row == 0, t == 0)

    def rglru_block(xr, n):
        ln = slice(n * RG_BLOCK_W, (n + 1) * RG_BLOCK_W)
        xc = _causal_conv(xr[:, ln], xcar, cw_ref, ln) + cb_ref[:, ln]
        xb = xc.astype(BF16)
        r = _sigmoid(_dot(xb, wa_ref[n]) + ba_ref[:, ln])
        i = _sigmoid(_dot(xb, wx_ref[n]) + bx_ref[:, ln])
        log_a = (-RG_C) * r * _softplus(-lam_ref[:, ln])
        a = jnp.exp(log_a)
        mult = jnp.where(first, 1.0, jnp.sqrt(_neg_expm1(2.0 * log_a)))
        a3 = a.reshape(groups, SUBLANES, RG_BLOCK_W)
        b3 = (mult * (i * xc)).reshape(groups, SUBLANES, RG_BLOCK_W)
        d = 1
        while d < SUBLANES:
            keep = sub >= d
            a_prev = jnp.where(keep, pltpu.roll(a3, d, axis=1), 1.0)
            b_prev = jnp.where(keep, pltpu.roll(b3, d, axis=1), 0.0)
            b3 = a3 * b_prev + b3
            a3 = a3 * a_prev
            d *= 2
        acum_s[:, ln] = a3.reshape(TP, RG_BLOCK_W)
        hloc_s[:, ln] = b3.reshape(TP, RG_BLOCK_W)

    def qkv_part(z, c):
        for l0 in range(0, D_MODEL, MIX_CONV_LANES):
            ln = slice(c * D_MODEL + l0, c * D_MODEL + l0 + MIX_CONV_LANES)
            act_ref[:, ln] = _silu(_causal_conv(z[:, l0:l0 + MIX_CONV_LANES], qcar, cqw_ref, ln))

    xr = col(0)
    for n in range(RG_BLOCKS):
        rglru_block(xr, n)
    cs_ref[0] = xcar[SUBLANES - (CONV_W - 1):, :]
    for c in range(DN_QKV // D_MODEL):
        qkv_part(col(2 + c), c)
    cq_ref[0] = qcar[SUBLANES - (CONV_W - 1):, :]

    hrow = hcar[0:1, :]
    for g in range(groups):
        hin_s[g * SUBLANES:(g + 1) * SUBLANES, :] = jnp.broadcast_to(hrow, (SUBLANES, D_RNN))
        e = (g + 1) * SUBLANES - 1
        hrow = acum_s[e:e + 1, :] * hrow + hloc_s[e:e + 1, :]
    hcar[0:1, :] = hrow
    hl_ref[0] = hrow

    y = hloc_s[...] + acum_s[...] * hin_s[...]
    oa_ref[...] = (y * jax.nn.gelu(col(1))).astype(BF16)
    szg_ref[...] = _silu(col(5)).astype(BF16)
    sga_ref[...] = _sigmoid(col(6)).astype(BF16)
    sgb_ref[...] = _sigmoid(col(7)).astype(BF16)
    gb_ref[...] = _dot(h, wab_ref[...])


def _mixin_call(h, w_lo, w_hi, w_ab, cw, cb, wa, ba, wx, bx, lam, cqw, *, batch, seq_len):
    nt = seq_len // TP
    m = batch * seq_len
    row = lambda n: pl.BlockSpec((TP, n), lambda b, t: (b * nt + t, 0))
    per_seq = lambda r, n: pl.BlockSpec((1, r, n), lambda b, t: (b, 0, 0))
    shp = lambda n, dt: jax.ShapeDtypeStruct((m, n), dt)
    blk = (RG_BLOCKS, RG_BLOCK_W, RG_BLOCK_W)
    return pl.pallas_call(
        _mixin_kernel,
        grid=(batch, nt),
        in_specs=[row(D_MODEL), _resident((D_MODEL, AB_OFF)), _resident((D_MODEL, N_HI)),
                  _resident((D_MODEL, LANES)),
                  _resident((CONV_W, D_RNN)), _resident((1, D_RNN)), _resident(blk), _resident((1, D_RNN)),
                  _resident(blk), _resident((1, D_RNN)), _resident((1, D_RNN)), _resident((CONV_W, DN_QKV))],
        out_specs=[row(D_RNN), row(DN_QKV), row(DN_V), row(D_MODEL), row(D_MODEL), row(LANES),
                   per_seq(1, D_RNN), per_seq(CONV_W - 1, D_RNN), per_seq(CONV_W - 1, DN_QKV)],
        out_shape=[shp(D_RNN, BF16), shp(DN_QKV, F32), shp(DN_V, BF16), shp(D_MODEL, BF16), shp(D_MODEL, BF16),
                   shp(LANES, F32),
                   jax.ShapeDtypeStruct((batch, 1, D_RNN), F32),
                   jax.ShapeDtypeStruct((batch, CONV_W - 1, D_RNN), F32),
                   jax.ShapeDtypeStruct((batch, CONV_W - 1, DN_QKV), F32)],
        scratch_shapes=[pltpu.VMEM((SUBLANES, D_RNN), F32),
                        pltpu.VMEM((SUBLANES, DN_QKV), F32),
                        pltpu.VMEM((SUBLANES, D_RNN), F32),
                        pltpu.VMEM((TP, D_RNN), F32),
                        pltpu.VMEM((TP, D_RNN), F32),
                        pltpu.VMEM((TP, D_RNN), F32)],
        compiler_params=_params(("arbitrary", "arbitrary")),
        name="mixin",
    )(h, w_lo, w_hi, w_ab, cw, cb, wa, ba, wx, bx, lam, cqw)


def _split(x):
    hi = x.astype(BF16)
    return hi, (x - hi.astype(F32)).astype(BF16)


def _dot3(a_hi, a_lo, b):
    n = b.shape[1]
    b_hi, b_lo = _split(b)
    r = _dot(a_hi, jnp.concatenate([b_hi, b_lo], axis=1))
    return r[:, :n] + r[:, n:] + _dot(a_lo, b_hi)


def _pair_mask(ti, tj, lvl):
    same = (ti >> (lvl + 1)) == (tj >> (lvl + 1))
    return jnp.logical_and(same, jnp.logical_and(((ti >> lvl) & 1) == 1, ((tj >> lvl) & 1) == 0))


DELTA_CPI = 4


def _delta_kernel(act_s, gb_ref, szg_ref, alog_ref, dtb_ref, dnw_ref,
                  ob_ref, s_ref,
                  s_acc, o_s, u_s, wq_s, attn_s, kd_s, egl_s, *, nt):
    t = pl.program_id(1)
    c = DN_CHUNK
    heads = range(DN_HEADS)

    @pl.when(t == 0)
    def _():
        s_acc[...] = jnp.zeros(s_acc.shape, F32)

    ti = lax.broadcasted_iota(jnp.int32, (c, 2 * c), 0)
    tj = lax.broadcasted_iota(jnp.int32, (c, 2 * c), 1)
    left = tj < c
    incl = jnp.logical_and(ti >= tj, left)
    strict = jnp.logical_and(ti > tj, left)
    eye_l = (ti == tj).astype(F32)
    cum_l = incl[:, :c].astype(F32)
    cum_r = jnp.logical_and(ti <= tj, left).astype(F32)
    last = c - 1
    zpad = jnp.zeros((c, DN_DK), BF16)
    neg_a = -jnp.exp(alog_ref[...])
    dtb = dtb_ref[...]

    def chunk_rows(ref, j, lanes):
        return ref[pl.ds(pl.multiple_of(j * c, c), c), lanes]

    def prep(it, _):
        js = [it * DELTA_CPI + k for k in range(DELTA_CPI)]
        gcum, gcum_t, beta = [], [], []
        for j in js:
            gbv = chunk_rows(gb_ref, j, slice(None))
            g = neg_a * _softplus(gbv + dtb)
            beta.append(_sigmoid(gbv))
            gcum.append(jnp.dot(cum_l, g, precision=HI, preferred_element_type=F32))
            gcum_t.append(_dot_tn(g, cum_r, precision=HI))
        chains = [(k, h) for k in range(DELTA_CPI) for h in heads]
        gc, decay, qn, kn, kn16, kb, vb = {}, {}, {}, {}, {}, {}, {}
        for k, h in chains:
            j = js[k]
            gc[k, h] = gcum[k][:, h:h + 1]
            decay[k, h] = jnp.exp(jnp.where(incl, gc[k, h] - gcum_t[k][h:h + 1, :], -jnp.inf))
            bh = beta[k][:, DN_HEADS + h:DN_HEADS + h + 1]
            qh = chunk_rows(act_s, j, slice(h * DN_DK, (h + 1) * DN_DK))
            kh = chunk_rows(act_s, j, slice(DN_QK + h * DN_DK, DN_QK + (h + 1) * DN_DK))
            vh = chunk_rows(act_s, j, slice(2 * DN_QK + h * DN_DV, 2 * DN_QK + (h + 1) * DN_DV))
            qn[k, h] = qh * lax.rsqrt(jnp.sum(qh * qh, axis=-1, keepdims=True) + EPS) * (DN_DK ** -0.5)
            kn[k, h] = kh * lax.rsqrt(jnp.sum(kh * kh, axis=-1, keepdims=True) + EPS)
            kn16[k, h] = kn[k, h].astype(BF16)
            kb[k, h] = kn[k, h] * bh
            vb[k, h] = vh * bh
        lm = {}
        for ch in chains:
            kq = _dot_nt(jnp.concatenate([kb[ch], qn[ch]], axis=0).astype(BF16),
                         jnp.concatenate([kn16[ch], zpad], axis=0))
            lm[ch] = jnp.where(strict, kq[:c] * decay[ch], 0.0)
            attn_s[js[ch[0]], ch[1]] = (kq[c:] * decay[ch])[:, :c].astype(BF16)
        d = {ch: eye_l - jnp.where(_pair_mask(ti, tj, 0), lm[ch], 0.0) for ch in chains}
        for lvl in range(1, c.bit_length() - 1):
            pair = _pair_mask(ti, tj, lvl)
            d16 = {ch: d[ch].astype(BF16) for ch in chains}
            ed = {ch: _dot(jnp.where(pair, lm[ch], 0.0).astype(BF16)[:, :c], d16[ch]) for ch in chains}
            d = {ch: d[ch] - _dot(d16[ch][:, :c], ed[ch].astype(BF16)) for ch in chains}
        eg = {ch: jnp.exp(gc[ch]) for ch in chains}
        a16 = {ch: d[ch].astype(BF16)[:, :c] for ch in chains}
        rhs = {ch: jnp.concatenate([vb[ch], kb[ch] * eg[ch]], axis=1) for ch in chains}
        x0 = {ch: _dot(a16[ch], rhs[ch].astype(BF16)) for ch in chains}
        res = {}
        for ch in chains:
            l_hi, l_lo = _split(lm[ch])
            res[ch] = rhs[ch] - x0[ch] - _dot3(l_hi[:, :c], l_lo[:, :c], x0[ch])
        for ch in chains:
            k, h = ch
            x = x0[ch] + _dot(a16[ch], res[ch].astype(BF16))
            u_s[js[k], h] = x[:, :DN_DV]
            wq_s[js[k], h] = jnp.concatenate([x[:, DN_DV:], qn[ch] * eg[ch]], axis=0).astype(BF16)
            gl = gcum[k][last:last + 1, h:h + 1]
            kd_s[js[k], h] = (kn[ch] * jnp.exp(gl - gc[ch])).astype(BF16)
        for k, j in enumerate(js):
            egl_s[pl.ds(pl.multiple_of(j * SUBLANES, SUBLANES), SUBLANES), :] = jnp.broadcast_to(
                jnp.exp(gcum[k][last:last + 1, :]), (SUBLANES, LANES))
        return 0

    lax.fori_loop(0, NCH // DELTA_CPI, prep, 0)

    def recur(j, _):
        egl = egl_s[pl.ds(pl.multiple_of(j * SUBLANES, SUBLANES), 1), :]
        s_old = [s_acc[h] for h in heads]
        wq = [_dot(wq_s[j, h], s_old[h].astype(BF16)) for h in heads]
        vnew16 = [(u_s[j, h] - wq[h][:c]).astype(BF16) for h in heads]
        o = [wq[h][c:] + _dot(attn_s[j, h], vnew16[h]) for h in heads]
        for h in heads:
            s_acc[h] = s_old[h] * egl[:, h:h + 1] + _dot_tn(kd_s[j, h], vnew16[h])
        for h in heads:
            on = o[h] * lax.rsqrt(jnp.mean(o[h] * o[h], axis=-1, keepdims=True) + EPS) * dnw_ref[...]
            o_s[pl.ds(pl.multiple_of(j * c, c), c), h * DN_DV:(h + 1) * DN_DV] = on
        return 0

    lax.fori_loop(0, NCH, recur, 0)

    ob_ref[...] = (o_s[...] * szg_ref[...].astype(F32)).astype(BF16)

    @pl.when(t == nt - 1)
    def _():
        s_ref[0] = s_acc[...]


def _delta_call(act, gb, szg, alog, dtb, dnw, *, batch, seq_len):
    c = DN_CHUNK
    nt = seq_len // TP
    row = lambda n: pl.BlockSpec((TP, n), lambda b, t: (b * nt + t, 0))
    return pl.pallas_call(
        functools.partial(_delta_kernel, nt=nt),
        grid=(batch, nt),
        in_specs=[row(DN_QKV), row(LANES), row(DN_V),
                  _resident((1, LANES)), _resident((1, LANES)), _resident((1, DN_DV))],
        out_specs=[row(DN_V),
                   pl.BlockSpec((1, DN_HEADS, DN_DK, DN_DV), lambda b, t: (b, 0, 0, 0))],
        out_shape=[jax.ShapeDtypeStruct((batch * seq_len, DN_V), BF16),
                   jax.ShapeDtypeStruct((batch, DN_HEADS, DN_DK, DN_DV), F32)],
        scratch_shapes=[pltpu.VMEM((DN_HEADS, DN_DK, DN_DV), F32),
                        pltpu.VMEM((TP, DN_V), F32),
                        pltpu.VMEM((NCH, DN_HEADS, c, DN_DV), F32),
                        pltpu.VMEM((NCH, DN_HEADS, 2 * c, DN_DK), BF16),
                        pltpu.VMEM((NCH, DN_HEADS, c, c), BF16),
                        pltpu.VMEM((NCH, DN_HEADS, c, DN_DK), BF16),
                        pltpu.VMEM((NCH * SUBLANES, LANES), F32)],
        compiler_params=_params(("arbitrary", "arbitrary")),
        name="delta",
    )(act, gb, szg, alog, dtb, dnw)


def _sample_pre_kernel(xr_ref, ggr_ref, qkv_ref, gb_ref, crnn_ref, cqkv_ref, h0_ref,
                       cw_ref, cb_ref, wa_ref, ba_ref, wx_ref, bx_ref, lam_ref,
                       cqw_ref, alog_ref, dtb_ref,
                       oa_ref, hn_ref, qn_ref, kn_ref, v_ref, eg_ref, beta_ref):
    xc = cb_ref[...] + cw_ref[CONV_W - 1:CONV_W, :] * xr_ref[...]
    for j in range(CONV_W - 1):
        xc = xc + cw_ref[j:j + 1, :] * crnn_ref[j]
    a, log_a, i = _rg_gates(xc, wa_ref, ba_ref[...], wx_ref, bx_ref[...], lam_ref[...])
    mult = jnp.sqrt(_neg_expm1(2.0 * log_a))
    h = a * h0_ref[...] + mult * (i * xc)
    hn_ref[...] = h
    oa_ref[...] = (h * ggr_ref[...].astype(F32)).astype(BF16)

    qc = cqw_ref[CONV_W - 1:CONV_W, :] * qkv_ref[...]
    for j in range(CONV_W - 1):
        qc = qc + cqw_ref[j:j + 1, :] * cqkv_ref[j]
    act = _silu(qc)
    for h_ in range(DN_HEADS):
        qh = act[:, h_ * DN_DK:(h_ + 1) * DN_DK]
        kh = act[:, DN_QK + h_ * DN_DK:DN_QK + (h_ + 1) * DN_DK]
        qn_ref[:, h_ * DN_DK:(h_ + 1) * DN_DK] = (
            qh * lax.rsqrt(jnp.sum(qh * qh, axis=-1, keepdims=True) + EPS) * (DN_DK ** -0.5))
        kn_ref[:, h_ * DN_DK:(h_ + 1) * DN_DK] = (
            kh * lax.rsqrt(jnp.sum(kh * kh, axis=-1, keepdims=True) + EPS))
    v_ref[...] = act[:, 2 * DN_QK:]
    gbv = gb_ref[...]
    eg_ref[...] = jnp.exp(-jnp.exp(alog_ref[...]) * _softplus(gbv + dtb_ref[...]))
    beta_ref[...] = _sigmoid(gbv)


def _sample_pre_call(xr, ggr, qkv, gb, crnn, cqkv, h0, cw, cb, wa, ba, wx, bx, lam, cqw, alog, dtb):
    n = xr.shape[0]
    args = (xr, ggr, qkv, gb, crnn, cqkv, h0, cw, cb, wa, ba, wx, bx, lam, cqw, alog, dtb)
    shp = lambda w: jax.ShapeDtypeStruct((n, w), F32)
    outs = [jax.ShapeDtypeStruct((n, D_RNN), BF16), shp(D_RNN), shp(DN_QK), shp(DN_QK), shp(DN_V), shp(LANES), shp(LANES)]
    return pl.pallas_call(
        _sample_pre_kernel,
        grid=(1,),
        in_specs=[_resident(a.shape) for a in args],
        out_specs=[pl.BlockSpec(o.shape, lambda i: (0, 0)) for o in outs],
        out_shape=outs,
        compiler_params=_params(("arbitrary",)),
        name="sample_pre",
    )(*args)


def _sample_state_kernel(qn_ref, kn_ref, v_ref, eg_ref, beta_ref, szg_ref, dnw_ref, s_ref,
                         ob_ref, sn_ref):
    srow = lax.broadcasted_iota(jnp.int32, (SUBLANES, DN_DK), 0)
    rows = lambda r0, r1: jnp.where(srow == 0, r0, jnp.where(srow == 1, r1, 0.0))
    chains = [(b, h) for b in range(SAMPLE_BB) for h in range(DN_HEADS)]
    sl = lambda h: slice(h * DN_DK, (h + 1) * DN_DK)
    q = {(b, h): qn_ref[0, b:b + 1, sl(h)] for b, h in chains}
    k = {(b, h): kn_ref[0, b:b + 1, sl(h)] for b, h in chains}
    eg = {(b, h): eg_ref[0, b:b + 1, h:h + 1] for b, h in chains}
    beta = {(b, h): beta_ref[0, b:b + 1, DN_HEADS + h:DN_HEADS + h + 1] for b, h in chains}
    ws_qs = {ch: _dot(rows((k[ch] * beta[ch]) * eg[ch], q[ch] * eg[ch]).astype(BF16),
                      s_ref[ch[0], ch[1]].astype(BF16)) for ch in chains}
    vnew = {(b, h): v_ref[0, b:b + 1, sl(h)] * beta[b, h] - ws_qs[b, h][0:1, :] for b, h in chains}
    for ch in chains:
        b, h = ch
        zero = jnp.zeros_like(k[ch])
        sn_ref[b, h] = s_ref[b, h] * eg[ch] + _dot_tn(rows(k[ch], zero).astype(BF16),
                                                     rows(vnew[ch], zero).astype(BF16))
    for ch in chains:
        b, h = ch
        o = ws_qs[ch][1:2, :] + jnp.sum(q[ch] * k[ch], axis=-1, keepdims=True) * vnew[ch]
        on = o * lax.rsqrt(jnp.mean(o * o, axis=-1, keepdims=True) + EPS) * dnw_ref[...]
        ob_ref[0, b:b + 1, sl(h)] = on * szg_ref[0, b:b + 1, sl(h)]


SAMPLE_BB = 8


def _sample_state_call(qn, kn, v, eg, beta, szg, dnw, s0):
    n = qn.shape[0]
    bb = SAMPLE_BB
    r3 = lambda a: a.reshape(n // bb, bb, a.shape[-1])
    vec = lambda w: pl.BlockSpec((1, bb, w), lambda i: (i, 0, 0))
    s_spec = pl.BlockSpec((bb, DN_HEADS, DN_DK, DN_DV), lambda i: (i, 0, 0, 0))
    ob, sn = pl.pallas_call(
        _sample_state_kernel,
        grid=(n // bb,),
        in_specs=[vec(DN_QK), vec(DN_QK), vec(DN_V), vec(LANES), vec(LANES), vec(DN_V),
                  _resident((1, DN_DV)), s_spec],
        out_specs=[vec(DN_V), s_spec],
        out_shape=[jax.ShapeDtypeStruct((n // bb, bb, DN_V), F32),
                   jax.ShapeDtypeStruct((n, DN_HEADS, DN_DK, DN_DV), F32)],
        compiler_params=_params(("arbitrary",)),
        name="sample_state",
    )(r3(qn), r3(kn), r3(v), r3(eg), r3(beta), r3(szg), dnw, s0)
    return ob.reshape(n, DN_V), sn


def _merge_kernel(x_ref, oa_ref, ob_ref, sga_ref, sgb_ref, ada_ref, wb_ref, wo_ref, o_ref, *, per_row, tps):
    ya = _dot(oa_ref[...], wb_ref[0].astype(BF16))
    yb = _dot(ob_ref[...], wb_ref[1].astype(BF16))
    merged = (sga_ref[...].astype(F32) * ya + sgb_ref[...].astype(F32) * yb).astype(BF16)
    gate = _ada_rows(ada_ref, 5, _tile_seq(per_row, tps))
    o_ref[...] = x_ref[...] + gate * _dot(merged, wo_ref[...].astype(BF16))


def _merge_call(x, oa, ob, sga, sgb, ada, wb, wo, *, per_row, tm, seq_len, n_dec):
    m = x.shape[0]
    row = pl.BlockSpec((tm, D_MODEL), lambda i: (i, 0))
    return pl.pallas_call(
        functools.partial(_merge_kernel, per_row=per_row, tps=seq_len // tm),
        grid=(m // tm,),
        in_specs=[row, row, row, row, row, _ada_spec(per_row, n_dec),
                  _resident((2, D_RNN, D_MODEL)), _resident((D_MODEL, D_MODEL))],
        out_specs=row,
        out_shape=jax.ShapeDtypeStruct((m, D_MODEL), F32),
        compiler_params=_params(("arbitrary",)),
        name="merge",
    )(x, oa, ob, sga, sgb, ada, wb, wo)


def kernel(x_prompt, x_sample, c_prompt, c_sample, state_rglru_h, state_rglru_conv, state_delta_S, state_delta_conv, w_ada, b_ada, norm_ffn1, w_ffn1_up, w_ffn1_down, norm_mix, w_in, conv_rnn_w, conv_rnn_b, rg_w_a, rg_b_a, rg_w_x, rg_b_x, rg_lambda, conv_qkv_w, dn_a_log, dn_dt_bias, dn_norm, w_branch, w_out, norm_ffn2, w_ffn2_up, w_ffn2_down, norm_final):
    batch, seq_len, _ = x_prompt.shape
    n_dec = x_sample.shape[0]
    assert w_ada.shape[0] == 1 and x_sample.shape[1] == 1 and seq_len % TP == 0

    row = lambda a: a.reshape(1, -1).astype(F32)
    wup1, wdn1 = w_ffn1_up[0], w_ffn1_down[0]
    wup2, wdn2 = w_ffn2_up[0], w_ffn2_down[0]
    w_t = jnp.transpose(w_in[0])
    n_ab = 2 * DN_HEADS
    assert w_t.shape[0] == AB_OFF + n_ab + N_HI
    w_lo = _wt_cast_call(w_t, 0, AB_OFF, AB_OFF)
    w_ab = _wt_cast_call(w_t, AB_OFF, n_ab, LANES)
    w_hi = _wt_cast_call(w_t, AB_OFF + n_ab, N_HI, N_HI)
    wb, wo = w_branch[0], w_out[0]
    wa, wx = rg_w_a[0].astype(BF16), rg_w_x[0].astype(BF16)
    lane_pad = lambda a: jnp.pad(a.reshape(1, -1).astype(F32), ((0, 0), (0, LANES - a.size)))
    alog = lane_pad(dn_a_log[0])
    dtb = lane_pad(dn_dt_bias[0])
    nf = row(norm_final)

    assert batch == SUBLANES and n_dec % SUBLANES == 0
    ada = _ada_call(jnp.concatenate([c_sample, c_prompt], axis=0), w_ada[0], row(b_ada[0]))

    mixer_w = (conv_rnn_w[0], row(conv_rnn_b[0]), wa, row(rg_b_a[0]), wx, row(rg_b_x[0]), row(rg_lambda[0]))

    kw = dict(per_row=False, seq_len=seq_len, tm=TP, n_dec=n_dec)
    xp = x_prompt.reshape(batch * seq_len, D_MODEL)
    xp, hmix = _ffn_call(xp, ada, row(norm_ffn1[0]), wup1, wdn1, row(norm_mix[0]), k0=0, post="mix", **kw)
    oa, act, szg, sga, sgb, gb, hp, cp, qp = _mixin_call(
        hmix, w_lo, w_hi, w_ab, *mixer_w, conv_qkv_w[0], batch=batch, seq_len=seq_len)
    ob, sp = _delta_call(act, gb, szg, alog, dtb, row(dn_norm[0]), batch=batch, seq_len=seq_len)
    xp = _merge_call(xp, oa, ob, sga, sgb, ada, wb, wo, **kw)
    y_prompt, = _ffn_call(xp, ada, row(norm_ffn2[0]), wup2, wdn2, nf, k0=6, post="final", **kw)

    kw = dict(per_row=True, seq_len=n_dec, tm=n_dec, n_dec=n_dec)
    xs = x_sample.reshape(n_dec, D_MODEL)
    xs, hmix_s = _ffn_call(xs, ada, row(norm_ffn1[0]), wup1, wdn1, row(norm_mix[0]), k0=0, post="mix", **kw)
    xr_s, ggr_s, qkv_s, szg_s, sga_s, sgb_s, gb_s = _inproj_call(hmix_s, w_lo, w_hi, w_ab, tm=n_dec)
    crnn = jnp.transpose(state_rglru_conv[0], (1, 0, 2))
    cqkv = jnp.transpose(state_delta_conv[0], (1, 0, 2))
    oa_s, hs, qn, kn, v, eg, beta = _sample_pre_call(
        xr_s, ggr_s, qkv_s, gb_s, crnn, cqkv, state_rglru_h[0], *mixer_w, conv_qkv_w[0], alog, dtb)
    ob_s, ss = _sample_state_call(qn, kn, v, eg, beta, szg_s.astype(F32), row(dn_norm[0]), state_delta_S[0])
    xs = _merge_call(xs, oa_s, ob_s.astype(BF16), sga_s, sgb_s, ada, wb, wo, **kw)
    y_sample, = _ffn_call(xs, ada, row(norm_ffn2[0]), wup2, wdn2, nf, k0=6, post="final", **kw)
    cs = jnp.concatenate([state_rglru_conv[0][:, 1:], xr_s[:, None, :]], axis=1)
    qs = jnp.concatenate([state_delta_conv[0][:, 1:], qkv_s[:, None, :]], axis=1)

    return (y_prompt.reshape(batch, seq_len, D_MODEL), y_sample.reshape(n_dec, 1, D_MODEL),
            hp.reshape(1, batch, D_RNN), cp[None], sp[None], qp[None],
            hs[None], cs[None], ss[None], qs[None])
```

```python
import functools

import jax
import jax.numpy as jnp
from jax import lax
from jax.experimental import pallas as pl
from jax.experimental.pallas import tpu as pltpu

D_MODEL = 1024
D_RNN = 1024
RG_BLOCKS = 8
RG_BLOCK_W = D_RNN // RG_BLOCKS
RG_C = 8.0
CONV_W = 4
DN_HEADS = 8
DN_DK = 128
DN_DV = 128
DN_QK = DN_HEADS * DN_DK
DN_V = DN_HEADS * DN_DV
DN_QKV = 2 * DN_QK + DN_V
DN_CHUNK = 64
D_FF = 2816
N_ADA = 9
EPS = 1e-6
LANES = 128
SUBLANES = 8
AB_OFF = D_RNN * 2 + DN_QKV
N_HI = 3 * D_MODEL

BF16 = jnp.bfloat16
F32 = jnp.float32
HI = lax.Precision.HIGHEST

VMEM_LIMIT = 58 * 1024 * 1024


def _params(sem):
    return pltpu.CompilerParams(dimension_semantics=sem, vmem_limit_bytes=VMEM_LIMIT)


def _resident(shape):
    nd = len(shape)
    return pl.BlockSpec(shape, lambda *_: (0,) * nd, pipeline_mode=pl.Buffered(1))


def _dot(a, b):
    return jnp.dot(a, b, preferred_element_type=F32)


def _dot_nt(a, b, precision=None):
    return lax.dot_general(a, b, (((1,), (1,)), ((), ())), precision=precision,
                           preferred_element_type=F32)


def _dot_tn(a, b, precision=None):
    return lax.dot_general(a, b, (((0,), (0,)), ((), ())), precision=precision,
                           preferred_element_type=F32)


def _sigmoid(x):
    return 0.5 * jnp.tanh(0.5 * x) + 0.5


def _silu(x):
    h = 0.5 * x
    return h + h * jnp.tanh(h)


def _softplus(x):
    return jnp.maximum(x, 0.0) + jnp.log1p(jnp.exp(-jnp.abs(x)))


def _neg_expm1(x):
    return -jnp.tanh(0.5 * x) * (jnp.exp(x) + 1.0)


def _ada_rows(ada_ref, k, seq):
    if seq is None:
        return ada_ref[k]
    return ada_ref[k, pl.ds(seq, 1), :]


def _tile_seq(per_row, tiles_per_seq):
    return None if per_row else pl.program_id(0) // tiles_per_seq


def _norm_mod(x, nw, shift, scale):
    ms = jnp.mean(x * x, axis=-1, keepdims=True)
    hn = x * lax.rsqrt(ms + EPS) * nw
    return hn * (1.0 + scale) + shift


def _ada_kernel(c_ref, w_ref, b_ref, o_ref):
    o_ref[...] = _dot(c_ref[...].astype(BF16), w_ref[...].astype(BF16)) + b_ref[...]


def _ada_call(c_all, w_ada, b_ada):
    n = c_all.shape[0]
    return pl.pallas_call(
        _ada_kernel,
        grid=(N_ADA,),
        in_specs=[pl.BlockSpec((n, D_MODEL), lambda j: (0, 0)),
                  pl.BlockSpec((D_MODEL, D_MODEL), lambda j: (0, j)),
                  pl.BlockSpec((1, D_MODEL), lambda j: (0, j))],
        out_specs=pl.BlockSpec((None, n, D_MODEL), lambda j: (j, 0, 0)),
        out_shape=jax.ShapeDtypeStruct((N_ADA, n, D_MODEL), F32),
        compiler_params=_params(("arbitrary",)),
        name="ada",
    )(c_all, w_ada, b_ada)


FFN_TF = 256


def _ffn_kernel(x_ref, ada_ref, nw_ref, wup_ref, wdn_ref, nf_ref, o_ref, *h_refs, k0, per_row, tps, post):
    seq = _tile_seq(per_row, tps)
    x = x_ref[...]
    h = _norm_mod(x, nw_ref[...], _ada_rows(ada_ref, k0, seq), _ada_rows(ada_ref, k0 + 1, seq)).astype(BF16)
    acc = jnp.zeros(x.shape, F32)
    for j in range(D_FF // FFN_TF):
        g = _dot(h, wup_ref[:, j * FFN_TF:(j + 1) * FFN_TF].astype(BF16))
        v = _dot(h, wup_ref[:, D_FF + j * FFN_TF:D_FF + (j + 1) * FFN_TF].astype(BF16))
        a = (_silu(g) * v).astype(BF16)
        acc = acc + _dot(a, wdn_ref[j * FFN_TF:(j + 1) * FFN_TF, :].astype(BF16))
    y = x + 0.5 * _ada_rows(ada_ref, k0 + 2, seq) * acc
    if post == "final":
        ms = jnp.mean(y * y, axis=-1, keepdims=True)
        y = y * lax.rsqrt(ms + EPS) * nf_ref[...]
    else:
        h_refs[0][...] = _norm_mod(y, nf_ref[...], _ada_rows(ada_ref, 3, seq), _ada_rows(ada_ref, 4, seq)).astype(BF16)
    o_ref[...] = y


def _ada_spec(per_row, n_dec):
    if per_row:
        return pl.BlockSpec((N_ADA, n_dec, D_MODEL), lambda i: (0, 0, 0))
    return pl.BlockSpec((N_ADA, SUBLANES, D_MODEL), lambda i: (0, n_dec // SUBLANES, 0))


def _ffn_call(x, ada, nw, wup, wdn, nf, *, k0, per_row, post, tm, seq_len, n_dec):
    m = x.shape[0]
    kern = functools.partial(_ffn_kernel, k0=k0, per_row=per_row, tps=seq_len // tm, post=post)
    row = pl.BlockSpec((tm, D_MODEL), lambda i: (i, 0))
    out_specs, out_shape = [row], [jax.ShapeDtypeStruct((m, D_MODEL), F32)]
    if post == "mix":
        out_specs, out_shape = out_specs + [row], out_shape + [jax.ShapeDtypeStruct((m, D_MODEL), BF16)]
    return pl.pallas_call(
        kern,
        grid=(m // tm,),
        in_specs=[pl.BlockSpec((tm, D_MODEL), lambda i: (i, 0)),
                  _ada_spec(per_row, n_dec),
                  _resident((1, D_MODEL)),
                  _resident((D_MODEL, 2 * D_FF)),
                  _resident((D_FF, D_MODEL)),
                  _resident((1, D_MODEL))],
        out_specs=out_specs,
        out_shape=out_shape,
        compiler_params=_params(("arbitrary",)),
        name="ffn",
    )(x, ada, nw, wup, wdn, nf)


TP = 512
NCH = TP // DN_CHUNK


WCAST_ROWS = 512


def _wt_cast_kernel(wt_ref, o_ref, *, n_valid):
    o = wt_ref[...].T
    if n_valid < o.shape[1]:
        o = jnp.where(lax.broadcasted_iota(jnp.int32, (1, o.shape[1]), 1) < n_valid, o, 0.0)
    o_ref[...] = o.astype(BF16)


def _wt_cast_call(wt, row0, n_valid, n_out):
    k = wt.shape[1]
    blk = min(WCAST_ROWS, n_out)
    return pl.pallas_call(
        functools.partial(_wt_cast_kernel, n_valid=min(n_valid, blk)),
        grid=(n_out // blk,),
        in_specs=[pl.BlockSpec((pl.Element(blk), pl.Element(k)),
                               lambda i: (pl.multiple_of(row0 + i * blk, SUBLANES), 0))],
        out_specs=pl.BlockSpec((k, blk), lambda i: (0, i)),
        out_shape=jax.ShapeDtypeStruct((k, n_out), BF16),
        compiler_params=_params(("arbitrary",)),
        name="wt_cast",
    )(wt)


def _proj_col(h, wlo_ref, whi_ref, c):
    n_lo = AB_OFF // D_MODEL
    w_ref, c = (wlo_ref, c) if c < n_lo else (whi_ref, c - n_lo)
    return _dot(h, w_ref[:, c * D_MODEL:(c + 1) * D_MODEL])


def _inproj_kernel(h_ref, wlo_ref, whi_ref, wab_ref,
                   xr_ref, ggr_ref, qkv_ref, szg_ref, sga_ref, sgb_ref, gb_ref):
    h = h_ref[...]
    col = functools.partial(_proj_col, h, wlo_ref, whi_ref)
    xr_ref[...] = col(0)
    ggr_ref[...] = jax.nn.gelu(col(1)).astype(BF16)
    for c in range(3):
        qkv_ref[:, c * D_MODEL:(c + 1) * D_MODEL] = col(2 + c)
    szg_ref[...] = _silu(col(5)).astype(BF16)
    sga_ref[...] = _sigmoid(col(6)).astype(BF16)
    sgb_ref[...] = _sigmoid(col(7)).astype(BF16)
    gb_ref[...] = _dot(h, wab_ref[...])


def _inproj_call(h, w_lo, w_hi, w_ab, *, tm):
    m = h.shape[0]
    row = lambda n: pl.BlockSpec((tm, n), lambda i: (i, 0))
    shp = lambda n, dt: jax.ShapeDtypeStruct((m, n), dt)
    return pl.pallas_call(
        _inproj_kernel,
        grid=(m // tm,),
        in_specs=[row(D_MODEL), _resident((D_MODEL, AB_OFF)), _resident((D_MODEL, N_HI)),
                  _resident((D_MODEL, LANES))],
        out_specs=[row(D_RNN), row(D_RNN), row(DN_QKV), row(DN_V), row(D_MODEL), row(D_MODEL), row(LANES)],
        out_shape=[shp(D_RNN, F32), shp(D_RNN, BF16), shp(DN_QKV, F32), shp(DN_V, BF16),
                   shp(D_MODEL, BF16), shp(D_MODEL, BF16), shp(LANES, F32)],
        compiler_params=_params(("arbitrary",)),
        name="inproj",
    )(h, w_lo, w_hi, w_ab)


def _causal_conv(x, carry, cw_ref, lanes):
    t, c = x.shape
    x3 = x.reshape(t // SUBLANES, SUBLANES, c)
    prev8 = carry[:, lanes]
    sub = lax.broadcasted_iota(jnp.int32, (1, SUBLANES, 1), 1)
    acc = cw_ref[CONV_W - 1:CONV_W, lanes][None] * x3
    for j in range(1, CONV_W):
        r = pltpu.roll(x3, j, axis=1)
        rp = jnp.concatenate([pltpu.roll(prev8, j, axis=0)[None], r[:-1]], axis=0)
        acc = acc + cw_ref[CONV_W - 1 - j:CONV_W - j, lanes][None] * jnp.where(sub >= j, r, rp)
    carry[:, lanes] = x[t - SUBLANES:, :]
    return acc.reshape(t, c)


def _rg_gates(xc, wa_ref, ba, wx_ref, bx, lam):
    ra, ix = [], []
    for n in range(RG_BLOCKS):
        xb = xc[:, n * RG_BLOCK_W:(n + 1) * RG_BLOCK_W].astype(BF16)
        ra.append(_dot(xb, wa_ref[n]))
        ix.append(_dot(xb, wx_ref[n]))
    r = _sigmoid(jnp.concatenate(ra, axis=1) + ba)
    i = _sigmoid(jnp.concatenate(ix, axis=1) + bx)
    log_a = (-RG_C) * r * _softplus(-lam)
    return jnp.exp(log_a), log_a, i


MIX_CONV_LANES = 512


def _mixin_kernel(h_ref, wlo_ref, whi_ref, wab_ref,
                  cw_ref, cb_ref, wa_ref, ba_ref, wx_ref, bx_ref, lam_ref, cqw_ref,
                  oa_ref, act_ref, szg_ref, sga_ref, sgb_ref, gb_ref, hl_ref, cs_ref, cq_ref,
                  xcar, qcar, hcar, acum_s, hloc_s, hin_s):
    t = pl.program_id(1)

    @pl.when(t == 0)
    def _():
        xcar[...] = jnp.zeros(xcar.shape, F32)
        qcar[...] = jnp.zeros(qcar.shape, F32)
        hcar[...] = jnp.zeros(hcar.shape, F32)

    h = h_ref[...]
    col = functools.partial(_proj_col, h, wlo_ref, whi_ref)
    groups = TP // SUBLANES
    sub = lax.broadcasted_iota(jnp.int32, (1, SUBLANES, 1), 1)
    row = lax.broadcasted_iota(jnp.int32, (TP, 1), 0)
    first = jnp.logical_and(row == 0, t == 0)

    def rglru_block(xr, n):
        ln = slice(n * RG_BLOCK_W, (n + 1) * RG_BLOCK_W)
        xc = _causal_conv(xr[:, ln], xcar, cw_ref, ln) + cb_ref[:, ln]
        xb = xc.astype(BF16)
        r = _sigmoid(_dot(xb, wa_ref[n]) + ba_ref[:, ln])
        i = _sigmoid(_dot(xb, wx_ref[n]) + bx_ref[:, ln])
        log_a = (-RG_C) * r * _softplus(-lam_ref[:, ln])
        a = jnp.exp(log_a)
        mult = jnp.where(first, 1.0, jnp.sqrt(_neg_expm1(2.0 * log_a)))
        a3 = a.reshape(groups, SUBLANES, RG_BLOCK_W)
        b3 = (mult * (i * xc)).reshape(groups, SUBLANES, RG_BLOCK_W)
        d = 1
        while d < SUBLANES:
            keep = sub >= d
            a_prev = jnp.where(keep, pltpu.roll(a3, d, axis=1), 1.0)
            b_prev = jnp.where(keep, pltpu.roll(b3, d, axis=1), 0.0)
            b3 = a3 * b_prev + b3
            a3 = a3 * a_prev
            d *= 2
        acum_s[:, ln] = a3.reshape(TP, RG_BLOCK_W)
        hloc_s[:, ln] = b3.reshape(TP, RG_BLOCK_W)

    def qkv_part(z, c):
        for l0 in range(0, D_MODEL, MIX_CONV_LANES):
            ln = slice(c * D_MODEL + l0, c * D_MODEL + l0 + MIX_CONV_LANES)
            act_ref[:, ln] = _silu(_causal_conv(z[:, l0:l0 + MIX_CONV_LANES], qcar, cqw_ref, ln))

    xr = col(0)
    for n in range(RG_BLOCKS):
        rglru_block(xr, n)
    cs_ref[0] = xcar[SUBLANES - (CONV_W - 1):, :]
    for c in range(DN_QKV // D_MODEL):
        qkv_part(col(2 + c), c)
    cq_ref[0] = qcar[SUBLANES - (CONV_W - 1):, :]

    hrow = hcar[0:1, :]
    for g in range(groups):
        hin_s[g * SUBLANES:(g + 1) * SUBLANES, :] = jnp.broadcast_to(hrow, (SUBLANES, D_RNN))
        e = (g + 1) * SUBLANES - 1
        hrow = acum_s[e:e + 1, :] * hrow + hloc_s[e:e + 1, :]
    hcar[0:1, :] = hrow
    hl_ref[0] = hrow

    y = hloc_s[...] + acum_s[...] * hin_s[...]
    oa_ref[...] = (y * jax.nn.gelu(col(1))).astype(BF16)
    szg_ref[...] = _silu(col(5)).astype(BF16)
    sga_ref[...] = _sigmoid(col(6)).astype(BF16)
    sgb_ref[...] = _sigmoid(col(7)).astype(BF16)
    gb_ref[...] = _dot(h, wab_ref[...])


def _mixin_call(h, w_lo, w_hi, w_ab, cw, cb, wa, ba, wx, bx, lam, cqw, *, batch, seq_len):
    nt = seq_len // TP
    m = batch * seq_len
    row = lambda n: pl.BlockSpec((TP, n), lambda b, t: (b * nt + t, 0))
    per_seq = lambda r, n: pl.BlockSpec((1, r, n), lambda b, t: (b, 0, 0))
    shp = lambda n, dt: jax.ShapeDtypeStruct((m, n), dt)
    blk = (RG_BLOCKS, RG_BLOCK_W, RG_BLOCK_W)
    return pl.pallas_call(
        _mixin_kernel,
        grid=(batch, nt),
        in_specs=[row(D_MODEL), _resident((D_MODEL, AB_OFF)), _resident((D_MODEL, N_HI)),
                  _resident((D_MODEL, LANES)),
                  _resident((CONV_W, D_RNN)), _resident((1, D_RNN)), _resident(blk), _resident((1, D_RNN)),
                  _resident(blk), _resident((1, D_RNN)), _resident((1, D_RNN)), _resident((CONV_W, DN_QKV))],
        out_specs=[row(D_RNN), row(DN_QKV), row(DN_V), row(D_MODEL), row(D_MODEL), row(LANES),
                   per_seq(1, D_RNN), per_seq(CONV_W - 1, D_RNN), per_seq(CONV_W - 1, DN_QKV)],
        out_shape=[shp(D_RNN, BF16), shp(DN_QKV, F32), shp(DN_V, BF16), shp(D_MODEL, BF16), shp(D_MODEL, BF16),
                   shp(LANES, F32),
                   jax.ShapeDtypeStruct((batch, 1, D_RNN), F32),
                   jax.ShapeDtypeStruct((batch, CONV_W - 1, D_RNN), F32),
                   jax.ShapeDtypeStruct((batch, CONV_W - 1, DN_QKV), F32)],
        scratch_shapes=[pltpu.VMEM((SUBLANES, D_RNN), F32),
                        pltpu.VMEM((SUBLANES, DN_QKV), F32),
                        pltpu.VMEM((SUBLANES, D_RNN), F32),
                        pltpu.VMEM((TP, D_RNN), F32),
                        pltpu.VMEM((TP, D_RNN), F32),
                        pltpu.VMEM((TP, D_RNN), F32)],
        compiler_params=_params(("arbitrary", "arbitrary")),
        name="mixin",
    )(h, w_lo, w_hi, w_ab, cw, cb, wa, ba, wx, bx, lam, cqw)


def _split(x):
    hi = x.astype(BF16)
    return hi, (x - hi.astype(F32)).astype(BF16)


def _dot3(a_hi, a_lo, b):
    n = b.shape[1]
    b_hi, b_lo = _split(b)
    r = _dot(a_hi, jnp.concatenate([b_hi, b_lo], axis=1))
    return r[:, :n] + r[:, n:] + _dot(a_lo, b_hi)


def _pair_mask(ti, tj, lvl):
    same = (ti >> (lvl + 1)) == (tj >> (lvl + 1))
    return jnp.logical_and(same, jnp.logical_and(((ti >> lvl) & 1) == 1, ((tj >> lvl) & 1) == 0))


DELTA_CPI = 4


def _delta_kernel(act_s, gb_ref, szg_ref, alog_ref, dtb_ref, dnw_ref,
                  ob_ref, s_ref,
                  s_acc, o_s, u_s, wq_s, attn_s, kd_s, egl_s, *, nt):
    t = pl.program_id(1)
    c = DN_CHUNK
    heads = range(DN_HEADS)

    @pl.when(t == 0)
    def _():
        s_acc[...] = jnp.zeros(s_acc.shape, F32)

    ti = lax.broadcasted_iota(jnp.int32, (c, 2 * c), 0)
    tj = lax.broadcasted_iota(jnp.int32, (c, 2 * c), 1)
    left = tj < c
    incl = jnp.logical_and(ti >= tj, left)
    strict = jnp.logical_and(ti > tj, left)
    eye_l = (ti == tj).astype(F32)
    cum_l = incl[:, :c].astype(F32)
    cum_r = jnp.logical_and(ti <= tj, left).astype(F32)
    last = c - 1
    zpad = jnp.zeros((c, DN_DK), BF16)
    neg_a = -jnp.exp(alog_ref[...])
    dtb = dtb_ref[...]

    def chunk_rows(ref, j, lanes):
        return ref[pl.ds(pl.multiple_of(j * c, c), c), lanes]

    def prep(it, _):
        js = [it * DELTA_CPI + k for k in range(DELTA_CPI)]
        gcum, gcum_t, beta = [], [], []
        for j in js:
            gbv = chunk_rows(gb_ref, j, slice(None))
            g = neg_a * _softplus(gbv + dtb)
            beta.append(_sigmoid(gbv))
            gcum.append(jnp.dot(cum_l, g, precision=HI, preferred_element_type=F32))
            gcum_t.append(_dot_tn(g, cum_r, precision=HI))
        chains = [(k, h) for k in range(DELTA_CPI) for h in heads]
        gc, decay, qn, kn, kn16, kb, vb = {}, {}, {}, {}, {}, {}, {}
        for k, h in chains:
            j = js[k]
            gc[k, h] = gcum[k][:, h:h + 1]
            decay[k, h] = jnp.exp(jnp.where(incl, gc[k, h] - gcum_t[k][h:h + 1, :], -jnp.inf))
            bh = beta[k][:, DN_HEADS + h:DN_HEADS + h + 1]
            qh = chunk_rows(act_s, j, slice(h * DN_DK, (h + 1) * DN_DK))
            kh = chunk_rows(act_s, j, slice(DN_QK + h * DN_DK, DN_QK + (h + 1) * DN_DK))
            vh = chunk_rows(act_s, j, slice(2 * DN_QK + h * DN_DV, 2 * DN_QK + (h + 1) * DN_DV))
            qn[k, h] = qh * lax.rsqrt(jnp.sum(qh * qh, axis=-1, keepdims=True) + EPS) * (DN_DK ** -0.5)
            kn[k, h] = kh * lax.rsqrt(jnp.sum(kh * kh, axis=-1, keepdims=True) + EPS)
            kn16[k, h] = kn[k, h].astype(BF16)
            kb[k, h] = kn[k, h] * bh
            vb[k, h] = vh * bh
        lm = {}
        for ch in chains:
            kq = _dot_nt(jnp.concatenate([kb[ch], qn[ch]], axis=0).astype(BF16),
                         jnp.concatenate([kn16[ch], zpad], axis=0))
            lm[ch] = jnp.where(strict, kq[:c] * decay[ch], 0.0)
            attn_s[js[ch[0]], ch[1]] = (kq[c:] * decay[ch])[:, :c].astype(BF16)
        d = {ch: eye_l - jnp.where(_pair_mask(ti, tj, 0), lm[ch], 0.0) for ch in chains}
        for lvl in range(1, c.bit_length() - 1):
            pair = _pair_mask(ti, tj, lvl)
            d16 = {ch: d[ch].astype(BF16) for ch in chains}
            ed = {ch: _dot(jnp.where(pair, lm[ch], 0.0).astype(BF16)[:, :c], d16[ch]) for ch in chains}
            d = {ch: d[ch] - _dot(d16[ch][:, :c], ed[ch].astype(BF16)) for ch in chains}
        eg = {ch: jnp.exp(gc[ch]) for ch in chains}
        a16 = {ch: d[ch].astype(BF16)[:, :c] for ch in chains}
        rhs = {ch: jnp.concatenate([vb[ch], kb[ch] * eg[ch]], axis=1) for ch in chains}
        x0 = {ch: _dot(a16[ch], rhs[ch].astype(BF16)) for ch in chains}
        res = {}
        for ch in chains:
            l_hi, l_lo = _split(lm[ch])
            res[ch] = rhs[ch] - x0[ch] - _dot3(l_hi[:, :c], l_lo[:, :c], x0[ch])
        for ch in chains:
            k, h = ch
            x = x0[ch] + _dot(a16[ch], res[ch].astype(BF16))
            u_s[js[k], h] = x[:, :DN_DV]
            wq_s[js[k], h] = jnp.concatenate([x[:, DN_DV:], qn[ch] * eg[ch]], axis=0).astype(BF16)
            gl = gcum[k][last:last + 1, h:h + 1]
            kd_s[js[k], h] = (kn[ch] * jnp.exp(gl - gc[ch])).astype(BF16)
        for k, j in enumerate(js):
            egl_s[pl.ds(pl.multiple_of(j * SUBLANES, SUBLANES), SUBLANES), :] = jnp.broadcast_to(
                jnp.exp(gcum[k][last:last + 1, :]), (SUBLANES, LANES))
        return 0

    lax.fori_loop(0, NCH // DELTA_CPI, prep, 0)

    def recur(j, _):
        egl = egl_s[pl.ds(pl.multiple_of(j * SUBLANES, SUBLANES), 1), :]
        s_old = [s_acc[h] for h in heads]
        wq = [_dot(wq_s[j, h], s_old[h].astype(BF16)) for h in heads]
        vnew16 = [(u_s[j, h] - wq[h][:c]).astype(BF16) for h in heads]
        o = [wq[h][c:] + _dot(attn_s[j, h], vnew16[h]) for h in heads]
        for h in heads:
            s_acc[h] = s_old[h] * egl[:, h:h + 1] + _dot_tn(kd_s[j, h], vnew16[h])
        for h in heads:
            on = o[h] * lax.rsqrt(jnp.mean(o[h] * o[h], axis=-1, keepdims=True) + EPS) * dnw_ref[...]
            o_s[pl.ds(pl.multiple_of(j * c, c), c), h * DN_DV:(h + 1) * DN_DV] = on
        return 0

    lax.fori_loop(0, NCH, recur, 0, unroll=True)

    ob_ref[...] = (o_s[...] * szg_ref[...].astype(F32)).astype(BF16)

    @pl.when(t == nt - 1)
    def _():
        s_ref[0] = s_acc[...]


def _delta_call(act, gb, szg, alog, dtb, dnw, *, batch, seq_len):
    c = DN_CHUNK
    nt = seq_len // TP
    row = lambda n: pl.BlockSpec((TP, n), lambda b, t: (b * nt + t, 0))
    return pl.pallas_call(
        functools.partial(_delta_kernel, nt=nt),
        grid=(batch, nt),
        in_specs=[row(DN_QKV), row(LANES), row(DN_V),
                  _resident((1, LANES)), _resident((1, LANES)), _resident((1, DN_DV))],
        out_specs=[row(DN_V),
                   pl.BlockSpec((1, DN_HEADS, DN_DK, DN_DV), lambda b, t: (b, 0, 0, 0))],
        out_shape=[jax.ShapeDtypeStruct((batch * seq_len, DN_V), BF16),
                   jax.ShapeDtypeStruct((batch, DN_HEADS, DN_DK, DN_DV), F32)],
        scratch_shapes=[pltpu.VMEM((DN_HEADS, DN_DK, DN_DV), F32),
                        pltpu.VMEM((TP, DN_V), F32),
                        pltpu.VMEM((NCH, DN_HEADS, c, DN_DV), F32),
                        pltpu.VMEM((NCH, DN_HEADS, 2 * c, DN_DK), BF16),
                        pltpu.VMEM((NCH, DN_HEADS, c, c), BF16),
                        pltpu.VMEM((NCH, DN_HEADS, c, DN_DK), BF16),
                        pltpu.VMEM((NCH * SUBLANES, LANES), F32)],
        compiler_params=_params(("arbitrary", "arbitrary")),
        name="delta",
    )(act, gb, szg, alog, dtb, dnw)


def _sample_pre_kernel(xr_ref, ggr_ref, qkv_ref, gb_ref, crnn_ref, cqkv_ref, h0_ref,
                       cw_ref, cb_ref, wa_ref, ba_ref, wx_ref, bx_ref, lam_ref,
                       cqw_ref, alog_ref, dtb_ref,
                       oa_ref, hn_ref, qn_ref, kn_ref, v_ref, eg_ref, beta_ref):
    xc = cb_ref[...] + cw_ref[CONV_W - 1:CONV_W, :] * xr_ref[...]
    for j in range(CONV_W - 1):
        xc = xc + cw_ref[j:j + 1, :] * crnn_ref[j]
    a, log_a, i = _rg_gates(xc, wa_ref, ba_ref[...], wx_ref, bx_ref[...], lam_ref[...])
    mult = jnp.sqrt(_neg_expm1(2.0 * log_a))
    h = a * h0_ref[...] + mult * (i * xc)
    hn_ref[...] = h
    oa_ref[...] = (h * ggr_ref[...].astype(F32)).astype(BF16)

    qc = cqw_ref[CONV_W - 1:CONV_W, :] * qkv_ref[...]
    for j in range(CONV_W - 1):
        qc = qc + cqw_ref[j:j + 1, :] * cqkv_ref[j]
    act = _silu(qc)
    for h_ in range(DN_HEADS):
        qh = act[:, h_ * DN_DK:(h_ + 1) * DN_DK]
        kh = act[:, DN_QK + h_ * DN_DK:DN_QK + (h_ + 1) * DN_DK]
        qn_ref[:, h_ * DN_DK:(h_ + 1) * DN_DK] = (
            qh * lax.rsqrt(jnp.sum(qh * qh, axis=-1, keepdims=True) + EPS) * (DN_DK ** -0.5))
        kn_ref[:, h_ * DN_DK:(h_ + 1) * DN_DK] = (
            kh * lax.rsqrt(jnp.sum(kh * kh, axis=-1, keepdims=True) + EPS))
    v_ref[...] = act[:, 2 * DN_QK:]
    gbv = gb_ref[...]
    eg_ref[...] = jnp.exp(-jnp.exp(alog_ref[...]) * _softplus(gbv + dtb_ref[...]))
    beta_ref[...] = _sigmoid(gbv)


def _sample_pre_call(xr, ggr, qkv, gb, crnn, cqkv, h0, cw, cb, wa, ba, wx, bx, lam, cqw, alog, dtb):
    n = xr.shape[0]
    args = (xr, ggr, qkv, gb, crnn, cqkv, h0, cw, cb, wa, ba, wx, bx, lam, cqw, alog, dtb)
    shp = lambda w: jax.ShapeDtypeStruct((n, w), F32)
    outs = [jax.ShapeDtypeStruct((n, D_RNN), BF16), shp(D_RNN), shp(DN_QK), shp(DN_QK), shp(DN_V), shp(LANES), shp(LANES)]
    return pl.pallas_call(
        _sample_pre_kernel,
        grid=(1,),
        in_specs=[_resident(a.shape) for a in args],
        out_specs=[pl.BlockSpec(o.shape, lambda i: (0, 0)) for o in outs],
        out_shape=outs,
        compiler_params=_params(("arbitrary",)),
        name="sample_pre",
    )(*args)


def _sample_state_kernel(qn_ref, kn_ref, v_ref, eg_ref, beta_ref, szg_ref, dnw_ref, s_ref,
                         ob_ref, sn_ref):
    srow = lax.broadcasted_iota(jnp.int32, (SUBLANES, DN_DK), 0)
    rows = lambda r0, r1: jnp.where(srow == 0, r0, jnp.where(srow == 1, r1, 0.0))
    chains = [(b, h) for b in range(SAMPLE_BB) for h in range(DN_HEADS)]
    sl = lambda h: slice(h * DN_DK, (h + 1) * DN_DK)
    q = {(b, h): qn_ref[0, b:b + 1, sl(h)] for b, h in chains}
    k = {(b, h): kn_ref[0, b:b + 1, sl(h)] for b, h in chains}
    eg = {(b, h): eg_ref[0, b:b + 1, h:h + 1] for b, h in chains}
    beta = {(b, h): beta_ref[0, b:b + 1, DN_HEADS + h:DN_HEADS + h + 1] for b, h in chains}
    ws_qs = {ch: _dot(rows((k[ch] * beta[ch]) * eg[ch], q[ch] * eg[ch]).astype(BF16),
                      s_ref[ch[0], ch[1]].astype(BF16)) for ch in chains}
    vnew = {(b, h): v_ref[0, b:b + 1, sl(h)] * beta[b, h] - ws_qs[b, h][0:1, :] for b, h in chains}
    for ch in chains:
        b, h = ch
        zero = jnp.zeros_like(k[ch])
        sn_ref[b, h] = s_ref[b, h] * eg[ch] + _dot_tn(rows(k[ch], zero).astype(BF16),
                                                     rows(vnew[ch], zero).astype(BF16))
    for ch in chains:
        b, h = ch
        o = ws_qs[ch][1:2, :] + jnp.sum(q[ch] * k[ch], axis=-1, keepdims=True) * vnew[ch]
        on = o * lax.rsqrt(jnp.mean(o * o, axis=-1, keepdims=True) + EPS) * dnw_ref[...]
        ob_ref[0, b:b + 1, sl(h)] = on * szg_ref[0, b:b + 1, sl(h)]


SAMPLE_BB = 8


def _sample_state_call(qn, kn, v, eg, beta, szg, dnw, s0):
    n = qn.shape[0]
    bb = SAMPLE_BB
    r3 = lambda a: a.reshape(n // bb, bb, a.shape[-1])
    vec = lambda w: pl.BlockSpec((1, bb, w), lambda i: (i, 0, 0))
    s_spec = pl.BlockSpec((bb, DN_HEADS, DN_DK, DN_DV), lambda i: (i, 0, 0, 0))
    ob, sn = pl.pallas_call(
        _sample_state_kernel,
        grid=(n // bb,),
        in_specs=[vec(DN_QK), vec(DN_QK), vec(DN_V), vec(LANES), vec(LANES), vec(DN_V),
                  _resident((1, DN_DV)), s_spec],
        out_specs=[vec(DN_V), s_spec],
        out_shape=[jax.ShapeDtypeStruct((n // bb, bb, DN_V), F32),
                   jax.ShapeDtypeStruct((n, DN_HEADS, DN_DK, DN_DV), F32)],
        compiler_params=_params(("arbitrary",)),
        name="sample_state",
    )(r3(qn), r3(kn), r3(v), r3(eg), r3(beta), r3(szg), dnw, s0)
    return ob.reshape(n, DN_V), sn


def _merge_kernel(x_ref, oa_ref, ob_ref, sga_ref, sgb_ref, ada_ref, wb_ref, wo_ref, o_ref, *, per_row, tps):
    ya = _dot(oa_ref[...], wb_ref[0].astype(BF16))
    yb = _dot(ob_ref[...], wb_ref[1].astype(BF16))
    merged = (sga_ref[...].astype(F32) * ya + sgb_ref[...].astype(F32) * yb).astype(BF16)
    gate = _ada_rows(ada_ref, 5, _tile_seq(per_row, tps))
    o_ref[...] = x_ref[...] + gate * _dot(merged, wo_ref[...].astype(BF16))


def _merge_call(x, oa, ob, sga, sgb, ada, wb, wo, *, per_row, tm, seq_len, n_dec):
    m = x.shape[0]
    row = pl.BlockSpec((tm, D_MODEL), lambda i: (i, 0))
    return pl.pallas_call(
        functools.partial(_merge_kernel, per_row=per_row, tps=seq_len // tm),
        grid=(m // tm,),
        in_specs=[row, row, row, row, row, _ada_spec(per_row, n_dec),
                  _resident((2, D_RNN, D_MODEL)), _resident((D_MODEL, D_MODEL))],
        out_specs=row,
        out_shape=jax.ShapeDtypeStruct((m, D_MODEL), F32),
        compiler_params=_params(("arbitrary",)),
        name="merge",
    )(x, oa, ob, sga, sgb, ada, wb, wo)


def kernel(x_prompt, x_sample, c_prompt, c_sample, state_rglru_h, state_rglru_conv, state_delta_S, state_delta_conv, w_ada, b_ada, norm_ffn1, w_ffn1_up, w_ffn1_down, norm_mix, w_in, conv_rnn_w, conv_rnn_b, rg_w_a, rg_b_a, rg_w_x, rg_b_x, rg_lambda, conv_qkv_w, dn_a_log, dn_dt_bias, dn_norm, w_branch, w_out, norm_ffn2, w_ffn2_up, w_ffn2_down, norm_final):
    batch, seq_len, _ = x_prompt.shape
    n_dec = x_sample.shape[0]
    assert w_ada.shape[0] == 1 and x_sample.shape[1] == 1 and seq_len % TP == 0

    row = lambda a: a.reshape(1, -1).astype(F32)
    wup1, wdn1 = w_ffn1_up[0], w_ffn1_down[0]
    wup2, wdn2 = w_ffn2_up[0], w_ffn2_down[0]
    w_t = jnp.transpose(w_in[0])
    n_ab = 2 * DN_HEADS
    assert w_t.shape[0] == AB_OFF + n_ab + N_HI
    w_lo = _wt_cast_call(w_t, 0, AB_OFF, AB_OFF)
    w_ab = _wt_cast_call(w_t, AB_OFF, n_ab, LANES)
    w_hi = _wt_cast_call(w_t, AB_OFF + n_ab, N_HI, N_HI)
    wb, wo = w_branch[0], w_out[0]
    wa, wx = rg_w_a[0].astype(BF16), rg_w_x[0].astype(BF16)
    lane_pad = lambda a: jnp.pad(a.reshape(1, -1).astype(F32), ((0, 0), (0, LANES - a.size)))
    alog = lane_pad(dn_a_log[0])
    dtb = lane_pad(dn_dt_bias[0])
    nf = row(norm_final)

    assert batch == SUBLANES and n_dec % SUBLANES == 0
    ada = _ada_call(jnp.concatenate([c_sample, c_prompt], axis=0), w_ada[0], row(b_ada[0]))

    mixer_w = (conv_rnn_w[0], row(conv_rnn_b[0]), wa, row(rg_b_a[0]), wx, row(rg_b_x[0]), row(rg_lambda[0]))

    kw = dict(per_row=False, seq_len=seq_len, tm=TP, n_dec=n_dec)
    xp = x_prompt.reshape(batch * seq_len, D_MODEL)
    xp, hmix = _ffn_call(xp, ada, row(norm_ffn1[0]), wup1, wdn1, row(norm_mix[0]), k0=0, post="mix", **kw)
    oa, act, szg, sga, sgb, gb, hp, cp, qp = _mixin_call(
        hmix, w_lo, w_hi, w_ab, *mixer_w, conv_qkv_w[0], batch=batch, seq_len=seq_len)
    ob, sp = _delta_call(act, gb, szg, alog, dtb, row(dn_norm[0]), batch=batch, seq_len=seq_len)
    xp = _merge_call(xp, oa, ob, sga, sgb, ada, wb, wo, **kw)
    y_prompt, = _ffn_call(xp, ada, row(norm_ffn2[0]), wup2, wdn2, nf, k0=6, post="final", **kw)

    kw = dict(per_row=True, seq_len=n_dec, tm=n_dec, n_dec=n_dec)
    xs = x_sample.reshape(n_dec, D_MODEL)
    xs, hmix_s = _ffn_call(xs, ada, row(norm_ffn1[0]), wup1, wdn1, row(norm_mix[0]), k0=0, post="mix", **kw)
    xr_s, ggr_s, qkv_s, szg_s, sga_s, sgb_s, gb_s = _inproj_call(hmix_s, w_lo, w_hi, w_ab, tm=n_dec)
    crnn = jnp.transpose(state_rglru_conv[0], (1, 0, 2))
    cqkv = jnp.transpose(state_delta_conv[0], (1, 0, 2))
    oa_s, hs, qn, kn, v, eg, beta = _sample_pre_call(
        xr_s, ggr_s, qkv_s, gb_s, crnn, cqkv, state_rglru_h[0], *mixer_w, conv_qkv_w[0], alog, dtb)
    ob_s, ss = _sample_state_call(qn, kn, v, eg, beta, szg_s.astype(F32), row(dn_norm[0]), state_delta_S[0])
    xs = _merge_call(xs, oa_s, ob_s.astype(BF16), sga_s, sgb_s, ada, wb, wo, **kw)
    y_sample, = _ffn_call(xs, ada, row(norm_ffn2[0]), wup2, wdn2, nf, k0=6, post="final", **kw)
    cs = jnp.concatenate([state_rglru_conv[0][:, 1:], xr_s[:, None, :]], axis=1)
    qs = jnp.concatenate([state_delta_conv[0][:, 1:], qkv_s[:, None, :]], axis=1)

    return (y_prompt.reshape(batch, seq_len, D_MODEL), y_sample.reshape(n_dec, 1, D_MODEL),
            hp.reshape(1, batch, D_RNN), cp[None], sp[None], qp[None],
            hs[None], cs[None], ss[None], qs[None])
```

```python
import functools

import jax
import jax.numpy as jnp
from jax import lax
from jax.experimental import pallas as pl
from jax.experimental.pallas import tpu as pltpu

D_MODEL = 1024
D_RNN = 1024
RG_BLOCKS = 8
RG_BLOCK_W = D_RNN // RG_BLOCKS
RG_C = 8.0
CONV_W = 4
DN_HEADS = 8
DN_DK = 128
DN_DV = 128
DN_QK = DN_HEADS * DN_DK
DN_V = DN_HEADS * DN_DV
DN_QKV = 2 * DN_QK + DN_V
DN_CHUNK = 64
D_FF = 2816
N_ADA = 9
EPS = 1e-6
LANES = 128
SUBLANES = 8
AB_OFF = D_RNN * 2 + DN_QKV
N_HI = 3 * D_MODEL

BF16 = jnp.bfloat16
F32 = jnp.float32
HI = lax.Precision.HIGHEST

VMEM_LIMIT = 58 * 1024 * 1024


def _params(sem):
    return pltpu.CompilerParams(dimension_semantics=sem, vmem_limit_bytes=VMEM_LIMIT)


def _resident(shape):
    nd = len(shape)
    return pl.BlockSpec(shape, lambda *_: (0,) * nd, pipeline_mode=pl.Buffered(1))


def _dot(a, b):
    return jnp.dot(a, b, preferred_element_type=F32)


def _dot_nt(a, b, precision=None):
    return lax.dot_general(a, b, (((1,), (1,)), ((), ())), precision=precision,
                           preferred_element_type=F32)


def _dot_tn(a, b, precision=None):
    return lax.dot_general(a, b, (((0,), (0,)), ((), ())), precision=precision,
                           preferred_element_type=F32)


def _sigmoid(x):
    return 0.5 * jnp.tanh(0.5 * x) + 0.5


def _silu(x):
    h = 0.5 * x
    return h + h * jnp.tanh(h)


def _softplus(x):
    return jnp.maximum(x, 0.0) + jnp.log1p(jnp.exp(-jnp.abs(x)))


def _neg_expm1(x):
    return -jnp.tanh(0.5 * x) * (jnp.exp(x) + 1.0)


def _ada_rows(ada_ref, k, seq):
    if seq is None:
        return ada_ref[k]
    return ada_ref[k, pl.ds(seq, 1), :]


def _tile_seq(per_row, tiles_per_seq):
    return None if per_row else pl.program_id(0) // tiles_per_seq


def _norm_mod(x, nw, shift, scale):
    ms = jnp.mean(x * x, axis=-1, keepdims=True)
    hn = x * lax.rsqrt(ms + EPS) * nw
    return hn * (1.0 + scale) + shift


def _ada_kernel(c_ref, w_ref, b_ref, o_ref):
    o_ref[...] = _dot(c_ref[...].astype(BF16), w_ref[...].astype(BF16)) + b_ref[...]


def _ada_call(c_all, w_ada, b_ada):
    n = c_all.shape[0]
    return pl.pallas_call(
        _ada_kernel,
        grid=(N_ADA,),
        in_specs=[pl.BlockSpec((n, D_MODEL), lambda j: (0, 0)),
                  pl.BlockSpec((D_MODEL, D_MODEL), lambda j: (0, j)),
                  pl.BlockSpec((1, D_MODEL), lambda j: (0, j))],
        out_specs=pl.BlockSpec((None, n, D_MODEL), lambda j: (j, 0, 0)),
        out_shape=jax.ShapeDtypeStruct((N_ADA, n, D_MODEL), F32),
        compiler_params=_params(("arbitrary",)),
        name="ada",
    )(c_all, w_ada, b_ada)


FFN_TF = 256


def _ffn_kernel(x_ref, ada_ref, nw_ref, wup_ref, wdn_ref, nf_ref, o_ref, *h_refs, k0, per_row, tps, post):
    seq = _tile_seq(per_row, tps)
    x = x_ref[...]
    h = _norm_mod(x, nw_ref[...], _ada_rows(ada_ref, k0, seq), _ada_rows(ada_ref, k0 + 1, seq)).astype(BF16)
    acc = jnp.zeros(x.shape, F32)
    for j in range(D_FF // FFN_TF):
        g = _dot(h, wup_ref[:, j * FFN_TF:(j + 1) * FFN_TF].astype(BF16))
        v = _dot(h, wup_ref[:, D_FF + j * FFN_TF:D_FF + (j + 1) * FFN_TF].astype(BF16))
        a = (_silu(g) * v).astype(BF16)
        acc = acc + _dot(a, wdn_ref[j * FFN_TF:(j + 1) * FFN_TF, :].astype(BF16))
    y = x + 0.5 * _ada_rows(ada_ref, k0 + 2, seq) * acc
    if post == "final":
        ms = jnp.mean(y * y, axis=-1, keepdims=True)
        y = y * lax.rsqrt(ms + EPS) * nf_ref[...]
    else:
        h_refs[0][...] = _norm_mod(y, nf_ref[...], _ada_rows(ada_ref, 3, seq), _ada_rows(ada_ref, 4, seq)).astype(BF16)
    o_ref[...] = y


def _ada_spec(per_row, n_dec):
    if per_row:
        return pl.BlockSpec((N_ADA, n_dec, D_MODEL), lambda i: (0, 0, 0))
    return pl.BlockSpec((N_ADA, SUBLANES, D_MODEL), lambda i: (0, n_dec // SUBLANES, 0))


def _ffn_call(x, ada, nw, wup, wdn, nf, *, k0, per_row, post, tm, seq_len, n_dec):
    m = x.shape[0]
    kern = functools.partial(_ffn_kernel, k0=k0, per_row=per_row, tps=seq_len // tm, post=post)
    row = pl.BlockSpec((tm, D_MODEL), lambda i: (i, 0))
    out_specs, out_shape = [row], [jax.ShapeDtypeStruct((m, D_MODEL), F32)]
    if post == "mix":
        out_specs, out_shape = out_specs + [row], out_shape + [jax.ShapeDtypeStruct((m, D_MODEL), BF16)]
    return pl.pallas_call(
        kern,
        grid=(m // tm,),
        in_specs=[pl.BlockSpec((tm, D_MODEL), lambda i: (i, 0)),
                  _ada_spec(per_row, n_dec),
                  _resident((1, D_MODEL)),
                  _resident((D_MODEL, 2 * D_FF)),
                  _resident((D_FF, D_MODEL)),
                  _resident((1, D_MODEL))],
        out_specs=out_specs,
        out_shape=out_shape,
        compiler_params=_params(("arbitrary",)),
        name="ffn",
    )(x, ada, nw, wup, wdn, nf)


TP = 512
NCH = TP // DN_CHUNK


WCAST_ROWS = 512


def _wt_cast_kernel(wt_ref, o_ref, *, n_valid):
    o = wt_ref[...].T
    if n_valid < o.shape[1]:
        o = jnp.where(lax.broadcasted_iota(jnp.int32, (1, o.shape[1]), 1) < n_valid, o, 0.0)
    o_ref[...] = o.astype(BF16)


def _wt_cast_call(wt, row0, n_valid, n_out):
    k = wt.shape[1]
    blk = min(WCAST_ROWS, n_out)
    return pl.pallas_call(
        functools.partial(_wt_cast_kernel, n_valid=min(n_valid, blk)),
        grid=(n_out // blk,),
        in_specs=[pl.BlockSpec((pl.Element(blk), pl.Element(k)),
                               lambda i: (pl.multiple_of(row0 + i * blk, SUBLANES), 0))],
        out_specs=pl.BlockSpec((k, blk), lambda i: (0, i)),
        out_shape=jax.ShapeDtypeStruct((k, n_out), BF16),
        compiler_params=_params(("arbitrary",)),
        name="wt_cast",
    )(wt)


def _proj_col(h, wlo_ref, whi_ref, c):
    n_lo = AB_OFF // D_MODEL
    w_ref, c = (wlo_ref, c) if c < n_lo else (whi_ref, c - n_lo)
    return _dot(h, w_ref[:, c * D_MODEL:(c + 1) * D_MODEL])


def _inproj_kernel(h_ref, wlo_ref, whi_ref, wab_ref,
                   xr_ref, ggr_ref, qkv_ref, szg_ref, sga_ref, sgb_ref, gb_ref):
    h = h_ref[...]
    col = functools.partial(_proj_col, h, wlo_ref, whi_ref)
    xr_ref[...] = col(0)
    ggr_ref[...] = jax.nn.gelu(col(1)).astype(BF16)
    for c in range(3):
        qkv_ref[:, c * D_MODEL:(c + 1) * D_MODEL] = col(2 + c)
    szg_ref[...] = _silu(col(5)).astype(BF16)
    sga_ref[...] = _sigmoid(col(6)).astype(BF16)
    sgb_ref[...] = _sigmoid(col(7)).astype(BF16)
    gb_ref[...] = _dot(h, wab_ref[...])


def _inproj_call(h, w_lo, w_hi, w_ab, *, tm):
    m = h.shape[0]
    row = lambda n: pl.BlockSpec((tm, n), lambda i: (i, 0))
    shp = lambda n, dt: jax.ShapeDtypeStruct((m, n), dt)
    return pl.pallas_call(
        _inproj_kernel,
        grid=(m // tm,),
        in_specs=[row(D_MODEL), _resident((D_MODEL, AB_OFF)), _resident((D_MODEL, N_HI)),
                  _resident((D_MODEL, LANES))],
        out_specs=[row(D_RNN), row(D_RNN), row(DN_QKV), row(DN_V), row(D_MODEL), row(D_MODEL), row(LANES)],
        out_shape=[shp(D_RNN, F32), shp(D_RNN, BF16), shp(DN_QKV, F32), shp(DN_V, BF16),
                   shp(D_MODEL, BF16), shp(D_MODEL, BF16), shp(LANES, F32)],
        compiler_params=_params(("arbitrary",)),
        name="inproj",
    )(h, w_lo, w_hi, w_ab)


def _causal_conv(x, carry, cw_ref, lanes):
    t, c = x.shape
    x3 = x.reshape(t // SUBLANES, SUBLANES, c)
    prev8 = carry[:, lanes]
    sub = lax.broadcasted_iota(jnp.int32, (1, SUBLANES, 1), 1)
    acc = cw_ref[CONV_W - 1:CONV_W, lanes][None] * x3
    for j in range(1, CONV_W):
        r = pltpu.roll(x3, j, axis=1)
        rp = jnp.concatenate([pltpu.roll(prev8, j, axis=0)[None], r[:-1]], axis=0)
        acc = acc + cw_ref[CONV_W - 1 - j:CONV_W - j, lanes][None] * jnp.where(sub >= j, r, rp)
    carry[:, lanes] = x[t - SUBLANES:, :]
    return acc.reshape(t, c)


def _rg_gates(xc, wa_ref, ba, wx_ref, bx, lam):
    ra, ix = [], []
    for n in range(RG_BLOCKS):
        xb = xc[:, n * RG_BLOCK_W:(n + 1) * RG_BLOCK_W].astype(BF16)
        ra.append(_dot(xb, wa_ref[n]))
        ix.append(_dot(xb, wx_ref[n]))
    r = _sigmoid(jnp.concatenate(ra, axis=1) + ba)
    i = _sigmoid(jnp.concatenate(ix, axis=1) + bx)
    log_a = (-RG_C) * r * _softplus(-lam)
    return jnp.exp(log_a), log_a, i


MIX_CONV_LANES = 512


def _mixin_kernel(h_ref, wlo_ref, whi_ref, wab_ref,
                  cw_ref, cb_ref, wa_ref, ba_ref, wx_ref, bx_ref, lam_ref, cqw_ref,
                  oa_ref, act_ref, szg_ref, sga_ref, sgb_ref, gb_ref, hl_ref, cs_ref, cq_ref,
                  xcar, qcar, hcar, acum_s, hloc_s, hin_s):
    t = pl.program_id(1)

    @pl.when(t == 0)
    def _():
        xcar[...] = jnp.zeros(xcar.shape, F32)
        qcar[...] = jnp.zeros(qcar.shape, F32)
        hcar[...] = jnp.zeros(hcar.shape, F32)

    h = h_ref[...]
    col = functools.partial(_proj_col, h, wlo_ref, whi_ref)
    groups = TP // SUBLANES
    sub = lax.broadcasted_iota(jnp.int32, (1, SUBLANES, 1), 1)
    row = lax.broadcasted_iota(jnp.int32, (TP, 1), 0)
    first = jnp.logical_and(row == 0, t == 0)

    def rglru_block(xr, n):
        ln = slice(n * RG_BLOCK_W, (n + 1) * RG_BLOCK_W)
        xc = _causal_conv(xr[:, ln], xcar, cw_ref, ln) + cb_ref[:, ln]
        xb = xc.astype(BF16)
        r = _sigmoid(_dot(xb, wa_ref[n]) + ba_ref[:, ln])
        i = _sigmoid(_dot(xb, wx_ref[n]) + bx_ref[:, ln])
        log_a = (-RG_C) * r * _softplus(-lam_ref[:, ln])
        a = jnp.exp(log_a)
        mult = jnp.where(first, 1.0, jnp.sqrt(_neg_expm1(2.0 * log_a)))
        a3 = a.reshape(groups, SUBLANES, RG_BLOCK_W)
        b3 = (mult * (i * xc)).reshape(groups, SUBLANES, RG_BLOCK_W)
        d = 1
        while d < SUBLANES:
            keep = sub >= d
            a_prev = jnp.where(keep, pltpu.roll(a3, d, axis=1), 1.0)
            b_prev = jnp.where(keep, pltpu.roll(b3, d, axis=1), 0.0)
            b3 = a3 * b_prev + b3
            a3 = a3 * a_prev
            d *= 2
        acum_s[:, ln] = a3.reshape(TP, RG_BLOCK_W)
        hloc_s[:, ln] = b3.reshape(TP, RG_BLOCK_W)

    def qkv_part(z, c):
        for l0 in range(0, D_MODEL, MIX_CONV_LANES):
            ln = slice(c * D_MODEL + l0, c * D_MODEL + l0 + MIX_CONV_LANES)
            act_ref[:, ln] = _silu(_causal_conv(z[:, l0:l0 + MIX_CONV_LANES], qcar, cqw_ref, ln))

    xr = col(0)
    for n in range(RG_BLOCKS):
        rglru_block(xr, n)
    cs_ref[0] = xcar[SUBLANES - (CONV_W - 1):, :]
    for c in range(DN_QKV // D_MODEL):
        qkv_part(col(2 + c), c)
    cq_ref[0] = qcar[SUBLANES - (CONV_W - 1):, :]

    hrow = hcar[0:1, :]
    for g in range(groups):
        hin_s[g * SUBLANES:(g + 1) * SUBLANES, :] = jnp.broadcast_to(hrow, (SUBLANES, D_RNN))
        e = (g + 1) * SUBLANES - 1
        hrow = acum_s[e:e + 1, :] * hrow + hloc_s[e:e + 1, :]
    hcar[0:1, :] = hrow
    hl_ref[0] = hrow

    y = hloc_s[...] + acum_s[...] * hin_s[...]
    oa_ref[...] = (y * jax.nn.gelu(col(1))).astype(BF16)
    szg_ref[...] = _silu(col(5)).astype(BF16)
    sga_ref[...] = _sigmoid(col(6)).astype(BF16)
    sgb_ref[...] = _sigmoid(col(7)).astype(BF16)
    gb_ref[...] = _dot(h, wab_ref[...])


def _mixin_call(h, w_lo, w_hi, w_ab, cw, cb, wa, ba, wx, bx, lam, cqw, *, batch, seq_len):
    nt = seq_len // TP
    m = batch * seq_len
    row = lambda n: pl.BlockSpec((TP, n), lambda b, t: (b * nt + t, 0))
    per_seq = lambda r, n: pl.BlockSpec((1, r, n), lambda b, t: (b, 0, 0))
    shp = lambda n, dt: jax.ShapeDtypeStruct((m, n), dt)
    blk = (RG_BLOCKS, RG_BLOCK_W, RG_BLOCK_W)
    return pl.pallas_call(
        _mixin_kernel,
        grid=(batch, nt),
        in_specs=[row(D_MODEL), _resident((D_MODEL, AB_OFF)), _resident((D_MODEL, N_HI)),
                  _resident((D_MODEL, LANES)),
                  _resident((CONV_W, D_RNN)), _resident((1, D_RNN)), _resident(blk), _resident((1, D_RNN)),
                  _resident(blk), _resident((1, D_RNN)), _resident((1, D_RNN)), _resident((CONV_W, DN_QKV))],
        out_specs=[row(D_RNN), row(DN_QKV), row(DN_V), row(D_MODEL), row(D_MODEL), row(LANES),
                   per_seq(1, D_RNN), per_seq(CONV_W - 1, D_RNN), per_seq(CONV_W - 1, DN_QKV)],
        out_shape=[shp(D_RNN, BF16), shp(DN_QKV, F32), shp(DN_V, BF16), shp(D_MODEL, BF16), shp(D_MODEL, BF16),
                   shp(LANES, F32),
                   jax.ShapeDtypeStruct((batch, 1, D_RNN), F32),
                   jax.ShapeDtypeStruct((batch, CONV_W - 1, D_RNN), F32),
                   jax.ShapeDtypeStruct((batch, CONV_W - 1, DN_QKV), F32)],
        scratch_shapes=[pltpu.VMEM((SUBLANES, D_RNN), F32),
                        pltpu.VMEM((SUBLANES, DN_QKV), F32),
                        pltpu.VMEM((SUBLANES, D_RNN), F32),
                        pltpu.VMEM((TP, D_RNN), F32),
                        pltpu.VMEM((TP, D_RNN), F32),
                        pltpu.VMEM((TP, D_RNN), F32)],
        compiler_params=_params(("arbitrary", "arbitrary")),
        name="mixin",
    )(h, w_lo, w_hi, w_ab, cw, cb, wa, ba, wx, bx, lam, cqw)


def _split(x):
    hi = x.astype(BF16)
    return hi, (x - hi.astype(F32)).astype(BF16)


def _dot3(a_hi, a_lo, b):
    n = b.shape[1]
    b_hi, b_lo = _split(b)
    r = _dot(a_hi, jnp.concatenate([b_hi, b_lo], axis=1))
    return r[:, :n] + r[:, n:] + _dot(a_lo, b_hi)


def _pair_mask(ti, tj, lvl):
    same = (ti >> (lvl + 1)) == (tj >> (lvl + 1))
    return jnp.logical_and(same, jnp.logical_and(((ti >> lvl) & 1) == 1, ((tj >> lvl) & 1) == 0))


DELTA_CPI = 4


def _delta_kernel(act_s, gb_ref, szg_ref, alog_ref, dtb_ref, dnw_ref,
                  ob_ref, s_ref,
                  s_acc, o_s, u_s, wq_s, attn_s, kd_s, egl_s, *, nt):
    t = pl.program_id(1)
    c = DN_CHUNK
    heads = range(DN_HEADS)

    @pl.when(t == 0)
    def _():
        s_acc[...] = jnp.zeros(s_acc.shape, F32)

    ti = lax.broadcasted_iota(jnp.int32, (c, 2 * c), 0)
    tj = lax.broadcasted_iota(jnp.int32, (c, 2 * c), 1)
    left = tj < c
    incl = jnp.logical_and(ti >= tj, left)
    strict = jnp.logical_and(ti > tj, left)
    eye_l = (ti == tj).astype(F32)
    cum_l = incl[:, :c].astype(F32)
    cum_r = jnp.logical_and(ti <= tj, left).astype(F32)
    last = c - 1
    zpad = jnp.zeros((c, DN_DK), BF16)
    neg_a = -jnp.exp(alog_ref[...])
    dtb = dtb_ref[...]

    def chunk_rows(ref, j, lanes):
        return ref[pl.ds(pl.multiple_of(j * c, c), c), lanes]

    def prep(it, _):
        js = [it * DELTA_CPI + k for k in range(DELTA_CPI)]
        gcum, gcum_t, beta = [], [], []
        for j in js:
            gbv = chunk_rows(gb_ref, j, slice(None))
            g = neg_a * _softplus(gbv + dtb)
            beta.append(_sigmoid(gbv))
            gcum.append(jnp.dot(cum_l, g, precision=HI, preferred_element_type=F32))
            gcum_t.append(_dot_tn(g, cum_r, precision=HI))
        chains = [(k, h) for k in range(DELTA_CPI) for h in heads]
        gc, decay, qn, kn, kn16, kb, vb = {}, {}, {}, {}, {}, {}, {}
        for k, h in chains:
            j = js[k]
            gc[k, h] = gcum[k][:, h:h + 1]
            decay[k, h] = jnp.exp(jnp.where(incl, gc[k, h] - gcum_t[k][h:h + 1, :], -jnp.inf))
            bh = beta[k][:, DN_HEADS + h:DN_HEADS + h + 1]
            qh = chunk_rows(act_s, j, slice(h * DN_DK, (h + 1) * DN_DK))
            kh = chunk_rows(act_s, j, slice(DN_QK + h * DN_DK, DN_QK + (h + 1) * DN_DK))
            vh = chunk_rows(act_s, j, slice(2 * DN_QK + h * DN_DV, 2 * DN_QK + (h + 1) * DN_DV))
            qn[k, h] = qh * lax.rsqrt(jnp.sum(qh * qh, axis=-1, keepdims=True) + EPS) * (DN_DK ** -0.5)
            kn[k, h] = kh * lax.rsqrt(jnp.sum(kh * kh, axis=-1, keepdims=True) + EPS)
            kn16[k, h] = kn[k, h].astype(BF16)
            kb[k, h] = kn[k, h] * bh
            vb[k, h] = vh * bh
        lm = {}
        for ch in chains:
            kq = _dot_nt(jnp.concatenate([kb[ch], qn[ch]], axis=0).astype(BF16),
                         jnp.concatenate([kn16[ch], zpad], axis=0))
            lm[ch] = jnp.where(strict, kq[:c] * decay[ch], 0.0)
            attn_s[js[ch[0]], ch[1]] = (kq[c:] * decay[ch])[:, :c].astype(BF16)
        d = {ch: eye_l - jnp.where(_pair_mask(ti, tj, 0), lm[ch], 0.0) for ch in chains}
        for lvl in range(1, c.bit_length() - 1):
            pair = _pair_mask(ti, tj, lvl)
            d16 = {ch: d[ch].astype(BF16) for ch in chains}
            ed = {ch: _dot(jnp.where(pair, lm[ch], 0.0).astype(BF16)[:, :c], d16[ch]) for ch in chains}
            d = {ch: d[ch] - _dot(d16[ch][:, :c], ed[ch].astype(BF16)) for ch in chains}
        eg = {ch: jnp.exp(gc[ch]) for ch in chains}
        a16 = {ch: d[ch].astype(BF16)[:, :c] for ch in chains}
        rhs = {ch: jnp.concatenate([vb[ch], kb[ch] * eg[ch]], axis=1) for ch in chains}
        x0 = {ch: _dot(a16[ch], rhs[ch].astype(BF16)) for ch in chains}
        res = {}
        for ch in chains:
            l_hi, l_lo = _split(lm[ch])
            res[ch] = rhs[ch] - x0[ch] - _dot3(l_hi[:, :c], l_lo[:, :c], x0[ch])
        for ch in chains:
            k, h = ch
            x = x0[ch] + _dot(a16[ch], res[ch].astype(BF16))
            u_s[js[k], h] = x[:, :DN_DV]
            wq_s[js[k], h] = jnp.concatenate([x[:, DN_DV:], qn[ch] * eg[ch]], axis=0).astype(BF16)
            gl = gcum[k][last:last + 1, h:h + 1]
            kd_s[js[k], h] = (kn[ch] * jnp.exp(gl - gc[ch])).astype(BF16)
        for k, j in enumerate(js):
            egl_s[pl.ds(pl.multiple_of(j * SUBLANES, SUBLANES), SUBLANES), :] = jnp.broadcast_to(
                jnp.exp(gcum[k][last:last + 1, :]), (SUBLANES, LANES))
        return 0

    lax.fori_loop(0, NCH // DELTA_CPI, prep, 0, unroll=True)

    def recur(j, _):
        egl = egl_s[pl.ds(pl.multiple_of(j * SUBLANES, SUBLANES), 1), :]
        s_old = [s_acc[h] for h in heads]
        wq = [_dot(wq_s[j, h], s_old[h].astype(BF16)) for h in heads]
        vnew16 = [(u_s[j, h] - wq[h][:c]).astype(BF16) for h in heads]
        o = [wq[h][c:] + _dot(attn_s[j, h], vnew16[h]) for h in heads]
        for h in heads:
            s_acc[h] = s_old[h] * egl[:, h:h + 1] + _dot_tn(kd_s[j, h], vnew16[h])
        for h in heads:
            on = o[h] * lax.rsqrt(jnp.mean(o[h] * o[h], axis=-1, keepdims=True) + EPS) * dnw_ref[...]
            o_s[pl.ds(pl.multiple_of(j * c, c), c), h * DN_DV:(h + 1) * DN_DV] = on
        return 0

    lax.fori_loop(0, NCH, recur, 0, unroll=True)

    ob_ref[...] = (o_s[...] * szg_ref[...].astype(F32)).astype(BF16)

    @pl.when(t == nt - 1)
    def _():
        s_ref[0] = s_acc[...]


def _delta_call(act, gb, szg, alog, dtb, dnw, *, batch, seq_len):
    c = DN_CHUNK
    nt = seq_len // TP
    row = lambda n: pl.BlockSpec((TP, n), lambda b, t: (b * nt + t, 0))
    return pl.pallas_call(
        functools.partial(_delta_kernel, nt=nt),
        grid=(batch, nt),
        in_specs=[row(DN_QKV), row(LANES), row(DN_V),
                  _resident((1, LANES)), _resident((1, LANES)), _resident((1, DN_DV))],
        out_specs=[row(DN_V),
                   pl.BlockSpec((1, DN_HEADS, DN_DK, DN_DV), lambda b, t: (b, 0, 0, 0))],
        out_shape=[jax.ShapeDtypeStruct((batch * seq_len, DN_V), BF16),
                   jax.ShapeDtypeStruct((batch, DN_HEADS, DN_DK, DN_DV), F32)],
        scratch_shapes=[pltpu.VMEM((DN_HEADS, DN_DK, DN_DV), F32),
                        pltpu.VMEM((TP, DN_V), F32),
                        pltpu.VMEM((NCH, DN_HEADS, c, DN_DV), F32),
                        pltpu.VMEM((NCH, DN_HEADS, 2 * c, DN_DK), BF16),
                        pltpu.VMEM((NCH, DN_HEADS, c, c), BF16),
                        pltpu.VMEM((NCH, DN_HEADS, c, DN_DK), BF16),
                        pltpu.VMEM((NCH * SUBLANES, LANES), F32)],
        compiler_params=_params(("arbitrary", "arbitrary")),
        name="delta",
    )(act, gb, szg, alog, dtb, dnw)


def _sample_pre_kernel(xr_ref, ggr_ref, qkv_ref, gb_ref, crnn_ref, cqkv_ref, h0_ref,
                       cw_ref, cb_ref, wa_ref, ba_ref, wx_ref, bx_ref, lam_ref,
                       cqw_ref, alog_ref, dtb_ref,
                       oa_ref, hn_ref, qn_ref, kn_ref, v_ref, eg_ref, beta_ref):
    xc = cb_ref[...] + cw_ref[CONV_W - 1:CONV_W, :] * xr_ref[...]
    for j in range(CONV_W - 1):
        xc = xc + cw_ref[j:j + 1, :] * crnn_ref[j]
    a, log_a, i = _rg_gates(xc, wa_ref, ba_ref[...], wx_ref, bx_ref[...], lam_ref[...])
    mult = jnp.sqrt(_neg_expm1(2.0 * log_a))
    h = a * h0_ref[...] + mult * (i * xc)
    hn_ref[...] = h
    oa_ref[...] = (h * ggr_ref[...].astype(F32)).astype(BF16)

    qc = cqw_ref[CONV_W - 1:CONV_W, :] * qkv_ref[...]
    for j in range(CONV_W - 1):
        qc = qc + cqw_ref[j:j + 1, :] * cqkv_ref[j]
    act = _silu(qc)
    for h_ in range(DN_HEADS):
        qh = act[:, h_ * DN_DK:(h_ + 1) * DN_DK]
        kh = act[:, DN_QK + h_ * DN_DK:DN_QK + (h_ + 1) * DN_DK]
        qn_ref[:, h_ * DN_DK:(h_ + 1) * DN_DK] = (
            qh * lax.rsqrt(jnp.sum(qh * qh, axis=-1, keepdims=True) + EPS) * (DN_DK ** -0.5))
        kn_ref[:, h_ * DN_DK:(h_ + 1) * DN_DK] = (
            kh * lax.rsqrt(jnp.sum(kh * kh, axis=-1, keepdims=True) + EPS))
    v_ref[...] = act[:, 2 * DN_QK:]
    gbv = gb_ref[...]
    eg_ref[...] = jnp.exp(-jnp.exp(alog_ref[...]) * _softplus(gbv + dtb_ref[...]))
    beta_ref[...] = _sigmoid(gbv)


def _sample_pre_call(xr, ggr, qkv, gb, crnn, cqkv, h0, cw, cb, wa, ba, wx, bx, lam, cqw, alog, dtb):
    n = xr.shape[0]
    args = (xr, ggr, qkv, gb, crnn, cqkv, h0, cw, cb, wa, ba, wx, bx, lam, cqw, alog, dtb)
    shp = lambda w: jax.ShapeDtypeStruct((n, w), F32)
    outs = [jax.ShapeDtypeStruct((n, D_RNN), BF16), shp(D_RNN), shp(DN_QK), shp(DN_QK), shp(DN_V), shp(LANES), shp(LANES)]
    return pl.pallas_call(
        _sample_pre_kernel,
        grid=(1,),
        in_specs=[_resident(a.shape) for a in args],
        out_specs=[pl.BlockSpec(o.shape, lambda i: (0, 0)) for o in outs],
        out_shape=outs,
        compiler_params=_params(("arbitrary",)),
        name="sample_pre",
    )(*args)


def _sample_state_kernel(qn_ref, kn_ref, v_ref, eg_ref, beta_ref, szg_ref, dnw_ref, s_ref,
                         ob_ref, sn_ref):
    srow = lax.broadcasted_iota(jnp.int32, (SUBLANES, DN_DK), 0)
    rows = lambda r0, r1: jnp.where(srow == 0, r0, jnp.where(srow == 1, r1, 0.0))
    chains = [(b, h) for b in range(SAMPLE_BB) for h in range(DN_HEADS)]
    sl = lambda h: slice(h * DN_DK, (h + 1) * DN_DK)
    q = {(b, h): qn_ref[0, b:b + 1, sl(h)] for b, h in chains}
    k = {(b, h): kn_ref[0, b:b + 1, sl(h)] for b, h in chains}
    eg = {(b, h): eg_ref[0, b:b + 1, h:h + 1] for b, h in chains}
    beta = {(b, h): beta_ref[0, b:b + 1, DN_HEADS + h:DN_HEADS + h + 1] for b, h in chains}
    ws_qs = {ch: _dot(rows((k[ch] * beta[ch]) * eg[ch], q[ch] * eg[ch]).astype(BF16),
                      s_ref[ch[0], ch[1]].astype(BF16)) for ch in chains}
    vnew = {(b, h): v_ref[0, b:b + 1, sl(h)] * beta[b, h] - ws_qs[b, h][0:1, :] for b, h in chains}
    for ch in chains:
        b, h = ch
        zero = jnp.zeros_like(k[ch])
        sn_ref[b, h] = s_ref[b, h] * eg[ch] + _dot_tn(rows(k[ch], zero).astype(BF16),
                                                     rows(vnew[ch], zero).astype(BF16))
    for ch in chains:
        b, h = ch
        o = ws_qs[ch][1:2, :] + jnp.sum(q[ch] * k[ch], axis=-1, keepdims=True) * vnew[ch]
        on = o * lax.rsqrt(jnp.mean(o * o, axis=-1, keepdims=True) + EPS) * dnw_ref[...]
        ob_ref[0, b:b + 1, sl(h)] = on * szg_ref[0, b:b + 1, sl(h)]


SAMPLE_BB = 8


def _sample_state_call(qn, kn, v, eg, beta, szg, dnw, s0):
    n = qn.shape[0]
    bb = SAMPLE_BB
    r3 = lambda a: a.reshape(n // bb, bb, a.shape[-1])
    vec = lambda w: pl.BlockSpec((1, bb, w), lambda i: (i, 0, 0))
    s_spec = pl.BlockSpec((bb, DN_HEADS, DN_DK, DN_DV), lambda i: (i, 0, 0, 0))
    ob, sn = pl.pallas_call(
        _sample_state_kernel,
        grid=(n // bb,),
        in_specs=[vec(DN_QK), vec(DN_QK), vec(DN_V), vec(LANES), vec(LANES), vec(DN_V),
                  _resident((1, DN_DV)), s_spec],
        out_specs=[vec(DN_V), s_spec],
        out_shape=[jax.ShapeDtypeStruct((n // bb, bb, DN_V), F32),
                   jax.ShapeDtypeStruct((n, DN_HEADS, DN_DK, DN_DV), F32)],
        compiler_params=_params(("arbitrary",)),
        name="sample_state",
    )(r3(qn), r3(kn), r3(v), r3(eg), r3(beta), r3(szg), dnw, s0)
    return ob.reshape(n, DN_V), sn


def _merge_kernel(x_ref, oa_ref, ob_ref, sga_ref, sgb_ref, ada_ref, wb_ref, wo_ref, o_ref, *, per_row, tps):
    ya = _dot(oa_ref[...], wb_ref[0].astype(BF16))
    yb = _dot(ob_ref[...], wb_ref[1].astype(BF16))
    merged = (sga_ref[...].astype(F32) * ya + sgb_ref[...].astype(F32) * yb).astype(BF16)
    gate = _ada_rows(ada_ref, 5, _tile_seq(per_row, tps))
    o_ref[...] = x_ref[...] + gate * _dot(merged, wo_ref[...].astype(BF16))


def _merge_call(x, oa, ob, sga, sgb, ada, wb, wo, *, per_row, tm, seq_len, n_dec):
    m = x.shape[0]
    row = pl.BlockSpec((tm, D_MODEL), lambda i: (i, 0))
    return pl.pallas_call(
        functools.partial(_merge_kernel, per_row=per_row, tps=seq_len // tm),
        grid=(m // tm,),
        in_specs=[row, row, row, row, row, _ada_spec(per_row, n_dec),
                  _resident((2, D_RNN, D_MODEL)), _resident((D_MODEL, D_MODEL))],
        out_specs=row,
        out_shape=jax.ShapeDtypeStruct((m, D_MODEL), F32),
        compiler_params=_params(("arbitrary",)),
        name="merge",
    )(x, oa, ob, sga, sgb, ada, wb, wo)


def kernel(x_prompt, x_sample, c_prompt, c_sample, state_rglru_h, state_rglru_conv, state_delta_S, state_delta_conv, w_ada, b_ada, norm_ffn1, w_ffn1_up, w_ffn1_down, norm_mix, w_in, conv_rnn_w, conv_rnn_b, rg_w_a, rg_b_a, rg_w_x, rg_b_x, rg_lambda, conv_qkv_w, dn_a_log, dn_dt_bias, dn_norm, w_branch, w_out, norm_ffn2, w_ffn2_up, w_ffn2_down, norm_final):
    batch, seq_len, _ = x_prompt.shape
    n_dec = x_sample.shape[0]
    assert w_ada.shape[0] == 1 and x_sample.shape[1] == 1 and seq_len % TP == 0

    row = lambda a: a.reshape(1, -1).astype(F32)
    wup1, wdn1 = w_ffn1_up[0], w_ffn1_down[0]
    wup2, wdn2 = w_ffn2_up[0], w_ffn2_down[0]
    w_t = jnp.transpose(w_in[0])
    n_ab = 2 * DN_HEADS
    assert w_t.shape[0] == AB_OFF + n_ab + N_HI
    w_lo = _wt_cast_call(w_t, 0, AB_OFF, AB_OFF)
    w_ab = _wt_cast_call(w_t, AB_OFF, n_ab, LANES)
    w_hi = _wt_cast_call(w_t, AB_OFF + n_ab, N_HI, N_HI)
    wb, wo = w_branch[0], w_out[0]
    wa, wx = rg_w_a[0].astype(BF16), rg_w_x[0].astype(BF16)
    lane_pad = lambda a: jnp.pad(a.reshape(1, -1).astype(F32), ((0, 0), (0, LANES - a.size)))
    alog = lane_pad(dn_a_log[0])
    dtb = lane_pad(dn_dt_bias[0])
    nf = row(norm_final)

    assert batch == SUBLANES and n_dec % SUBLANES == 0
    ada = _ada_call(jnp.concatenate([c_sample, c_prompt], axis=0), w_ada[0], row(b_ada[0]))

    mixer_w = (conv_rnn_w[0], row(conv_rnn_b[0]), wa, row(rg_b_a[0]), wx, row(rg_b_x[0]), row(rg_lambda[0]))

    kw = dict(per_row=False, seq_len=seq_len, tm=TP, n_dec=n_dec)
    xp = x_prompt.reshape(batch * seq_len, D_MODEL)
    xp, hmix = _ffn_call(xp, ada, row(norm_ffn1[0]), wup1, wdn1, row(norm_mix[0]), k0=0, post="mix", **kw)
    oa, act, szg, sga, sgb, gb, hp, cp, qp = _mixin_call(
        hmix, w_lo, w_hi, w_ab, *mixer_w, conv_qkv_w[0], batch=batch, seq_len=seq_len)
    ob, sp = _delta_call(act, gb, szg, alog, dtb, row(dn_norm[0]), batch=batch, seq_len=seq_len)
    xp = _merge_call(xp, oa, ob, sga, sgb, ada, wb, wo, **kw)
    y_prompt, = _ffn_call(xp, ada, row(norm_ffn2[0]), wup2, wdn2, nf, k0=6, post="final", **kw)

    kw = dict(per_row=True, seq_len=n_dec, tm=n_dec, n_dec=n_dec)
    xs = x_sample.reshape(n_dec, D_MODEL)
    xs, hmix_s = _ffn_call(xs, ada, row(norm_ffn1[0]), wup1, wdn1, row(norm_mix[0]), k0=0, post="mix", **kw)
    xr_s, ggr_s, qkv_s, szg_s, sga_s, sgb_s, gb_s = _inproj_call(hmix_s, w_lo, w_hi, w_ab, tm=n_dec)
    crnn = jnp.transpose(state_rglru_conv[0], (1, 0, 2))
    cqkv = jnp.transpose(state_delta_conv[0], (1, 0, 2))
    oa_s, hs, qn, kn, v, eg, beta = _sample_pre_call(
        xr_s, ggr_s, qkv_s, gb_s, crnn, cqkv, state_rglru_h[0], *mixer_w, conv_qkv_w[0], alog, dtb)
    ob_s, ss = _sample_state_call(qn, kn, v, eg, beta, szg_s.astype(F32), row(dn_norm[0]), state_delta_S[0])
    xs = _merge_call(xs, oa_s, ob_s.astype(BF16), sga_s, sgb_s, ada, wb, wo, **kw)
    y_sample, = _ffn_call(xs, ada, row(norm_ffn2[0]), wup2, wdn2, nf, k0=6, post="final", **kw)
    cs = jnp.concatenate([state_rglru_conv[0][:, 1:], xr_s[:, None, :]], axis=1)
    qs = jnp.concatenate([state_delta_conv[0][:, 1:], qkv_s[:, None, :]], axis=1)

    return (y_prompt.reshape(batch, seq_len, D_MODEL), y_sample.reshape(n_dec, 1, D_MODEL),
            hp.reshape(1, batch, D_RNN), cp[None], sp[None], qp[None],
            hs[None], cs[None], ss[None], qs[None])
```

```python
import functools

import jax
import jax.numpy as jnp
from jax import lax
from jax.experimental import pallas as pl
from jax.experimental.pallas import tpu as pltpu

D_MODEL = 1024
D_RNN = 1024
RG_BLOCKS = 8
RG_BLOCK_W = D_RNN // RG_BLOCKS
RG_C = 8.0
CONV_W = 4
DN_HEADS = 8
DN_DK = 128
DN_DV = 128
DN_QK = DN_HEADS * DN_DK
DN_V = DN_HEADS * DN_DV
DN_QKV = 2 * DN_QK + DN_V
DN_CHUNK = 64
D_FF = 2816
N_ADA = 9
EPS = 1e-6
LANES = 128
SUBLANES = 8
AB_OFF = D_RNN * 2 + DN_QKV
N_HI = 3 * D_MODEL

BF16 = jnp.bfloat16
F32 = jnp.float32
HI = lax.Precision.HIGHEST

VMEM_LIMIT = 58 * 1024 * 1024


def _params(sem):
    return pltpu.CompilerParams(dimension_semantics=sem, vmem_limit_bytes=VMEM_LIMIT)


def _resident(shape):
    nd = len(shape)
    return pl.BlockSpec(shape, lambda *_: (0,) * nd, pipeline_mode=pl.Buffered(1))


def _dot(a, b):
    return jnp.dot(a, b, preferred_element_type=F32)


def _dot_nt(a, b, precision=None):
    return lax.dot_general(a, b, (((1,), (1,)), ((), ())), precision=precision,
                           preferred_element_type=F32)


def _dot_tn(a, b, precision=None):
    return lax.dot_general(a, b, (((0,), (0,)), ((), ())), precision=precision,
                           preferred_element_type=F32)


def _sigmoid(x):
    return 0.5 * jnp.tanh(0.5 * x) + 0.5


def _silu(x):
    h = 0.5 * x
    return h + h * jnp.tanh(h)


def _softplus(x):
    return jnp.maximum(x, 0.0) + jnp.log1p(jnp.exp(-jnp.abs(x)))


def _neg_expm1(x):
    return -jnp.tanh(0.5 * x) * (jnp.exp(x) + 1.0)


def _ada_rows(ada_ref, k, seq):
    if seq is None:
        return ada_ref[k]
    return ada_ref[k, pl.ds(seq, 1), :]


def _tile_seq(per_row, tiles_per_seq):
    return None if per_row else pl.program_id(0) // tiles_per_seq


def _norm_mod(x, nw, shift, scale):
    ms = jnp.mean(x * x, axis=-1, keepdims=True)
    hn = x * lax.rsqrt(ms + EPS) * nw
    return hn * (1.0 + scale) + shift


def _ada_kernel(c_ref, w_ref, b_ref, o_ref):
    o_ref[...] = _dot(c_ref[...].astype(BF16), w_ref[...].astype(BF16)) + b_ref[...]


def _ada_call(c_all, w_ada, b_ada):
    n = c_all.shape[0]
    return pl.pallas_call(
        _ada_kernel,
        grid=(N_ADA,),
        in_specs=[pl.BlockSpec((n, D_MODEL), lambda j: (0, 0)),
                  pl.BlockSpec((D_MODEL, D_MODEL), lambda j: (0, j)),
                  pl.BlockSpec((1, D_MODEL), lambda j: (0, j))],
        out_specs=pl.BlockSpec((None, n, D_MODEL), lambda j: (j, 0, 0)),
        out_shape=jax.ShapeDtypeStruct((N_ADA, n, D_MODEL), F32),
        compiler_params=_params(("arbitrary",)),
        name="ada",
    )(c_all, w_ada, b_ada)


FFN_TF = 256


def _ffn_kernel(x_ref, ada_ref, nw_ref, wup_ref, wdn_ref, nf_ref, o_ref, *h_refs, k0, per_row, tps, post):
    seq = _tile_seq(per_row, tps)
    x = x_ref[...]
    h = _norm_mod(x, nw_ref[...], _ada_rows(ada_ref, k0, seq), _ada_rows(ada_ref, k0 + 1, seq)).astype(BF16)
    acc = jnp.zeros(x.shape, F32)
    for j in range(D_FF // FFN_TF):
        g = _dot(h, wup_ref[:, j * FFN_TF:(j + 1) * FFN_TF].astype(BF16))
        v = _dot(h, wup_ref[:, D_FF + j * FFN_TF:D_FF + (j + 1) * FFN_TF].astype(BF16))
        a = (_silu(g) * v).astype(BF16)
        acc = acc + _dot(a, wdn_ref[j * FFN_TF:(j + 1) * FFN_TF, :].astype(BF16))
    y = x + 0.5 * _ada_rows(ada_ref, k0 + 2, seq) * acc
    if post == "final":
        ms = jnp.mean(y * y, axis=-1, keepdims=True)
        y = y * lax.rsqrt(ms + EPS) * nf_ref[...]
    else:
        h_refs[0][...] = _norm_mod(y, nf_ref[...], _ada_rows(ada_ref, 3, seq), _ada_rows(ada_ref, 4, seq)).astype(BF16)
    o_ref[...] = y


def _ada_spec(per_row, n_dec):
    if per_row:
        return pl.BlockSpec((N_ADA, n_dec, D_MODEL), lambda i: (0, 0, 0))
    return pl.BlockSpec((N_ADA, SUBLANES, D_MODEL), lambda i: (0, n_dec // SUBLANES, 0))


def _ffn_call(x, ada, nw, wup, wdn, nf, *, k0, per_row, post, tm, seq_len, n_dec):
    m = x.shape[0]
    kern = functools.partial(_ffn_kernel, k0=k0, per_row=per_row, tps=seq_len // tm, post=post)
    row = pl.BlockSpec((tm, D_MODEL), lambda i: (i, 0))
    out_specs, out_shape = [row], [jax.ShapeDtypeStruct((m, D_MODEL), F32)]
    if post == "mix":
        out_specs, out_shape = out_specs + [row], out_shape + [jax.ShapeDtypeStruct((m, D_MODEL), BF16)]
    return pl.pallas_call(
        kern,
        grid=(m // tm,),
        in_specs=[pl.BlockSpec((tm, D_MODEL), lambda i: (i, 0)),
                  _ada_spec(per_row, n_dec),
                  _resident((1, D_MODEL)),
                  _resident((D_MODEL, 2 * D_FF)),
                  _resident((D_FF, D_MODEL)),
                  _resident((1, D_MODEL))],
        out_specs=out_specs,
        out_shape=out_shape,
        compiler_params=_params(("arbitrary",)),
        name="ffn",
    )(x, ada, nw, wup, wdn, nf)


TP = 512
NCH = TP // DN_CHUNK


WCAST_ROWS = 512


def _wt_cast_kernel(wt_ref, o_ref, *, n_valid):
    o = wt_ref[...].T
    if n_valid < o.shape[1]:
        o = jnp.where(lax.broadcasted_iota(jnp.int32, (1, o.shape[1]), 1) < n_valid, o, 0.0)
    o_ref[...] = o.astype(BF16)


def _wt_cast_call(wt, row0, n_valid, n_out):
    k = wt.shape[1]
    blk = min(WCAST_ROWS, n_out)
    return pl.pallas_call(
        functools.partial(_wt_cast_kernel, n_valid=min(n_valid, blk)),
        grid=(n_out // blk,),
        in_specs=[pl.BlockSpec((pl.Element(blk), pl.Element(k)),
                               lambda i: (pl.multiple_of(row0 + i * blk, SUBLANES), 0))],
        out_specs=pl.BlockSpec((k, blk), lambda i: (0, i)),
        out_shape=jax.ShapeDtypeStruct((k, n_out), BF16),
        compiler_params=_params(("arbitrary",)),
        name="wt_cast",
    )(wt)


def _proj_col(h, wlo_ref, whi_ref, c):
    n_lo = AB_OFF // D_MODEL
    w_ref, c = (wlo_ref, c) if c < n_lo else (whi_ref, c - n_lo)
    return _dot(h, w_ref[:, c * D_MODEL:(c + 1) * D_MODEL])


def _inproj_kernel(h_ref, wlo_ref, whi_ref, wab_ref,
                   xr_ref, ggr_ref, qkv_ref, szg_ref, sga_ref, sgb_ref, gb_ref):
    h = h_ref[...]
    col = functools.partial(_proj_col, h, wlo_ref, whi_ref)
    xr_ref[...] = col(0)
    ggr_ref[...] = jax.nn.gelu(col(1)).astype(BF16)
    for c in range(3):
        qkv_ref[:, c * D_MODEL:(c + 1) * D_MODEL] = col(2 + c)
    szg_ref[...] = _silu(col(5)).astype(BF16)
    sga_ref[...] = _sigmoid(col(6)).astype(BF16)
    sgb_ref[...] = _sigmoid(col(7)).astype(BF16)
    gb_ref[...] = _dot(h, wab_ref[...])


def _inproj_call(h, w_lo, w_hi, w_ab, *, tm):
    m = h.shape[0]
    row = lambda n: pl.BlockSpec((tm, n), lambda i: (i, 0))
    shp = lambda n, dt: jax.ShapeDtypeStruct((m, n), dt)
    return pl.pallas_call(
        _inproj_kernel,
        grid=(m // tm,),
        in_specs=[row(D_MODEL), _resident((D_MODEL, AB_OFF)), _resident((D_MODEL, N_HI)),
                  _resident((D_MODEL, LANES))],
        out_specs=[row(D_RNN), row(D_RNN), row(DN_QKV), row(DN_V), row(D_MODEL), row(D_MODEL), row(LANES)],
        out_shape=[shp(D_RNN, F32), shp(D_RNN, BF16), shp(DN_QKV, F32), shp(DN_V, BF16),
                   shp(D_MODEL, BF16), shp(D_MODEL, BF16), shp(LANES, F32)],
        compiler_params=_params(("arbitrary",)),
        name="inproj",
    )(h, w_lo, w_hi, w_ab)


def _causal_conv(x, carry, cw_ref, lanes):
    t, c = x.shape
    x3 = x.reshape(t // SUBLANES, SUBLANES, c)
    prev8 = carry[:, lanes]
    sub = lax.broadcasted_iota(jnp.int32, (1, SUBLANES, 1), 1)
    acc = cw_ref[CONV_W - 1:CONV_W, lanes][None] * x3
    for j in range(1, CONV_W):
        r = pltpu.roll(x3, j, axis=1)
        rp = jnp.concatenate([pltpu.roll(prev8, j, axis=0)[None], r[:-1]], axis=0)
        acc = acc + cw_ref[CONV_W - 1 - j:CONV_W - j, lanes][None] * jnp.where(sub >= j, r, rp)
    carry[:, lanes] = x[t - SUBLANES:, :]
    return acc.reshape(t, c)


def _rg_gates(xc, wa_ref, ba, wx_ref, bx, lam):
    ra, ix = [], []
    for n in range(RG_BLOCKS):
        xb = xc[:, n * RG_BLOCK_W:(n + 1) * RG_BLOCK_W].astype(BF16)
        ra.append(_dot(xb, wa_ref[n]))
        ix.append(_dot(xb, wx_ref[n]))
    r = _sigmoid(jnp.concatenate(ra, axis=1) + ba)
    i = _sigmoid(jnp.concatenate(ix, axis=1) + bx)
    log_a = (-RG_C) * r * _softplus(-lam)
    return jnp.exp(log_a), log_a, i


MIX_CONV_LANES = 512


def _mixin_kernel(h_ref, wlo_ref, whi_ref, wab_ref,
                  cw_ref, cb_ref, wa_ref, ba_ref, wx_ref, bx_ref, lam_ref, cqw_ref,
                  oa_ref, act_ref, szg_ref, sga_ref, sgb_ref, gb_ref, hl_ref, cs_ref, cq_ref,
                  xcar, qcar, hcar, acum_s, hloc_s, hin_s):
    t = pl.program_id(1)

    @pl.when(t == 0)
    def _():
        xcar[...] = jnp.zeros(xcar.shape, F32)
        qcar[...] = jnp.zeros(qcar.shape, F32)
        hcar[...] = jnp.zeros(hcar.shape, F32)

    h = h_ref[...]
    col = functools.partial(_proj_col, h, wlo_ref, whi_ref)
    groups = TP // SUBLANES
    sub = lax.broadcasted_iota(jnp.int32, (1, SUBLANES, 1), 1)
    row = lax.broadcasted_iota(jnp.int32, (TP, 1), 0)
    first = jnp.logical_and(row == 0, t == 0)

    def rglru_block(xr, n):
        ln = slice(n * RG_BLOCK_W, (n + 1) * RG_BLOCK_W)
        xc = _causal_conv(xr[:, ln], xcar, cw_ref, ln) + cb_ref[:, ln]
        xb = xc.astype(BF16)
        r = _sigmoid(_dot(xb, wa_ref[n]) + ba_ref[:, ln])
        i = _sigmoid(_dot(xb, wx_ref[n]) + bx_ref[:, ln])
        log_a = (-RG_C) * r * _softplus(-lam_ref[:, ln])
        a = jnp.exp(log_a)
        mult = jnp.where(first, 1.0, jnp.sqrt(_neg_expm1(2.0 * log_a)))
        a3 = a.reshape(groups, SUBLANES, RG_BLOCK_W)
        b3 = (mult * (i * xc)).reshape(groups, SUBLANES, RG_BLOCK_W)
        d = 1
        while d < SUBLANES:
            keep = sub >= d
            a_prev = jnp.where(keep, pltpu.roll(a3, d, axis=1), 1.0)
            b_prev = jnp.where(keep, pltpu.roll(b3, d, axis=1), 0.0)
            b3 = a3 * b_prev + b3
            a3 = a3 * a_prev
            d *= 2
        acum_s[:, ln] = a3.reshape(TP, RG_BLOCK_W)
        hloc_s[:, ln] = b3.reshape(TP, RG_BLOCK_W)

    def qkv_part(z, c):
        for l0 in range(0, D_MODEL, MIX_CONV_LANES):
            ln = slice(c * D_MODEL + l0, c * D_MODEL + l0 + MIX_CONV_LANES)
            act_ref[:, ln] = _silu(_causal_conv(z[:, l0:l0 + MIX_CONV_LANES], qcar, cqw_ref, ln))

    xr = col(0)
    for n in range(RG_BLOCKS):
        rglru_block(xr, n)
    cs_ref[0] = xcar[SUBLANES - (CONV_W - 1):, :]
    for c in range(DN_QKV // D_MODEL):
        qkv_part(col(2 + c), c)
    cq_ref[0] = qcar[SUBLANES - (CONV_W - 1):, :]

    hrow = hcar[0:1, :]
    for g in range(groups):
        hin_s[g * SUBLANES:(g + 1) * SUBLANES, :] = jnp.broadcast_to(hrow, (SUBLANES, D_RNN))
        e = (g + 1) * SUBLANES - 1
        hrow = acum_s[e:e + 1, :] * hrow + hloc_s[e:e + 1, :]
    hcar[0:1, :] = hrow
    hl_ref[0] = hrow

    y = hloc_s[...] + acum_s[...] * hin_s[...]
    oa_ref[...] = (y * jax.nn.gelu(col(1))).astype(BF16)
    szg_ref[...] = _silu(col(5)).astype(BF16)
    sga_ref[...] = _sigmoid(col(6)).astype(BF16)
    sgb_ref[...] = _sigmoid(col(7)).astype(BF16)
    gb_ref[...] = _dot(h, wab_ref[...])


def _mixin_call(h, w_lo, w_hi, w_ab, cw, cb, wa, ba, wx, bx, lam, cqw, *, batch, seq_len):
    nt = seq_len // TP
    m = batch * seq_len
    row = lambda n: pl.BlockSpec((TP, n), lambda b, t: (b * nt + t, 0))
    per_seq = lambda r, n: pl.BlockSpec((1, r, n), lambda b, t: (b, 0, 0))
    shp = lambda n, dt: jax.ShapeDtypeStruct((m, n), dt)
    blk = (RG_BLOCKS, RG_BLOCK_W, RG_BLOCK_W)
    return pl.pallas_call(
        _mixin_kernel,
        grid=(batch, nt),
        in_specs=[row(D_MODEL), _resident((D_MODEL, AB_OFF)), _resident((D_MODEL, N_HI)),
                  _resident((D_MODEL, LANES)),
                  _resident((CONV_W, D_RNN)), _resident((1, D_RNN)), _resident(blk), _resident((1, D_RNN)),
                  _resident(blk), _resident((1, D_RNN)), _resident((1, D_RNN)), _resident((CONV_W, DN_QKV))],
        out_specs=[row(D_RNN), row(DN_QKV), row(DN_V), row(D_MODEL), row(D_MODEL), row(LANES),
                   per_seq(1, D_RNN), per_seq(CONV_W - 1, D_RNN), per_seq(CONV_W - 1, DN_QKV)],
        out_shape=[shp(D_RNN, BF16), shp(DN_QKV, F32), shp(DN_V, BF16), shp(D_MODEL, BF16), shp(D_MODEL, BF16),
                   shp(LANES, F32),
                   jax.ShapeDtypeStruct((batch, 1, D_RNN), F32),
                   jax.ShapeDtypeStruct((batch, CONV_W - 1, D_RNN), F32),
                   jax.ShapeDtypeStruct((batch, CONV_W - 1, DN_QKV), F32)],
        scratch_shapes=[pltpu.VMEM((SUBLANES, D_RNN), F32),
                        pltpu.VMEM((SUBLANES, DN_QKV), F32),
                        pltpu.VMEM((SUBLANES, D_RNN), F32),
                        pltpu.VMEM((TP, D_RNN), F32),
                        pltpu.VMEM((TP, D_RNN), F32),
                        pltpu.VMEM((TP, D_RNN), F32)],
        compiler_params=_params(("arbitrary", "arbitrary")),
        name="mixin",
    )(h, w_lo, w_hi, w_ab, cw, cb, wa, ba, wx, bx, lam, cqw)


def _split(x):
    hi = x.astype(BF16)
    return hi, (x - hi.astype(F32)).astype(BF16)


def _dot3(a_hi, a_lo, b):
    n = b.shape[1]
    b_hi, b_lo = _split(b)
    r = _dot(a_hi, jnp.concatenate([b_hi, b_lo], axis=1))
    return r[:, :n] + r[:, n:] + _dot(a_lo, b_hi)


def _pair_mask(ti, tj, lvl):
    same = (ti >> (lvl + 1)) == (tj >> (lvl + 1))
    return jnp.logical_and(same, jnp.logical_and(((ti >> lvl) & 1) == 1, ((tj >> lvl) & 1) == 0))


DELTA_CPI = 4


def _delta_kernel(act_s, gb_ref, szg_ref, alog_ref, dtb_ref, dnw_ref,
                  ob_ref, s_ref,
                  s_acc, o_s, u_s, wq_s, attn_s, kd_s, egl_s, *, nt):
    t = pl.program_id(1)
    c = DN_CHUNK
    heads = range(DN_HEADS)

    @pl.when(t == 0)
    def _():
        s_acc[...] = jnp.zeros(s_acc.shape, F32)

    ti = lax.broadcasted_iota(jnp.int32, (c, 2 * c), 0)
    tj = lax.broadcasted_iota(jnp.int32, (c, 2 * c), 1)
    left = tj < c
    incl = jnp.logical_and(ti >= tj, left)
    strict = jnp.logical_and(ti > tj, left)
    eye_l = (ti == tj).astype(F32)
    cum_l = incl[:, :c].astype(F32)
    cum_r = jnp.logical_and(ti <= tj, left).astype(F32)
    last = c - 1
    zpad = jnp.zeros((c, DN_DK), BF16)
    neg_a = -jnp.exp(alog_ref[...])
    dtb = dtb_ref[...]

    def chunk_rows(ref, j, lanes):
        return ref[pl.ds(pl.multiple_of(j * c, c), c), lanes]

    def prep(it, _):
        js = [it * DELTA_CPI + k for k in range(DELTA_CPI)]
        gcum, gcum_t, beta = [], [], []
        for j in js:
            gbv = chunk_rows(gb_ref, j, slice(None))
            g = neg_a * _softplus(gbv + dtb)
            beta.append(_sigmoid(gbv))
            gcum.append(jnp.dot(cum_l, g, precision=HI, preferred_element_type=F32))
            gcum_t.append(_dot_tn(g, cum_r, precision=HI))
        chains = [(k, h) for k in range(DELTA_CPI) for h in heads]
        gc, decay, qn, kn, kn16, kb, vb = {}, {}, {}, {}, {}, {}, {}
        for k, h in chains:
            j = js[k]
            gc[k, h] = gcum[k][:, h:h + 1]
            decay[k, h] = jnp.exp(jnp.where(incl, gc[k, h] - gcum_t[k][h:h + 1, :], -jnp.inf))
            bh = beta[k][:, DN_HEADS + h:DN_HEADS + h + 1]
            qh = chunk_rows(act_s, j, slice(h * DN_DK, (h + 1) * DN_DK))
            kh = chunk_rows(act_s, j, slice(DN_QK + h * DN_DK, DN_QK + (h + 1) * DN_DK))
            vh = chunk_rows(act_s, j, slice(2 * DN_QK + h * DN_DV, 2 * DN_QK + (h + 1) * DN_DV))
            qn[k, h] = qh * lax.rsqrt(jnp.sum(qh * qh, axis=-1, keepdims=True) + EPS) * (DN_DK ** -0.5)
            kn[k, h] = kh * lax.rsqrt(jnp.sum(kh * kh, axis=-1, keepdims=True) + EPS)
            kn16[k, h] = kn[k, h].astype(BF16)
            kb[k, h] = kn[k, h] * bh
            vb[k, h] = vh * bh
        lm = {}
        for ch in chains:
            kq = _dot_nt(jnp.concatenate([kb[ch], qn[ch]], axis=0).astype(BF16),
                         jnp.concatenate([kn16[ch], zpad], axis=0))
            lm[ch] = jnp.where(strict, kq[:c] * decay[ch], 0.0)
            attn_s[js[ch[0]], ch[1]] = (kq[c:] * decay[ch])[:, :c].astype(BF16)
        d = {ch: eye_l - jnp.where(_pair_mask(ti, tj, 0), lm[ch], 0.0) for ch in chains}
        for lvl in range(1, c.bit_length() - 1):
            pair = _pair_mask(ti, tj, lvl)
            d16 = {ch: d[ch].astype(BF16) for ch in chains}
            ed = {ch: _dot(jnp.where(pair, lm[ch], 0.0).astype(BF16)[:, :c], d16[ch]) for ch in chains}
            d = {ch: d[ch] - _dot(d16[ch][:, :c], ed[ch].astype(BF16)) for ch in chains}
        eg = {ch: jnp.exp(gc[ch]) for ch in chains}
        a16 = {ch: d[ch].astype(BF16)[:, :c] for ch in chains}
        rhs = {ch: jnp.concatenate([vb[ch], kb[ch] * eg[ch]], axis=1) for ch in chains}
        x0 = {ch: _dot(a16[ch], rhs[ch].astype(BF16)) for ch in chains}
        res = {}
        for ch in chains:
            l_hi, l_lo = _split(lm[ch])
            res[ch] = rhs[ch] - x0[ch] - _dot3(l_hi[:, :c], l_lo[:, :c], x0[ch])
        for ch in chains:
            k, h = ch
            x = x0[ch] + _dot(a16[ch], res[ch].astype(BF16))
            u_s[js[k], h] = x[:, :DN_DV]
            wq_s[js[k], h] = jnp.concatenate([x[:, DN_DV:], qn[ch] * eg[ch]], axis=0).astype(BF16)
            gl = gcum[k][last:last + 1, h:h + 1]
            kd_s[js[k], h] = (kn[ch] * jnp.exp(gl - gc[ch])).astype(BF16)
        for k, j in enumerate(js):
            egl_s[pl.ds(pl.multiple_of(j * SUBLANES, SUBLANES), SUBLANES), :] = jnp.broadcast_to(
                jnp.exp(gcum[k][last:last + 1, :]), (SUBLANES, LANES))
        return 0

    lax.fori_loop(0, NCH // DELTA_CPI, prep, 0, unroll=True)

    def recur(j, _):
        egl = egl_s[pl.ds(pl.multiple_of(j * SUBLANES, SUBLANES), 1), :]
        s_old = [s_acc[h] for h in heads]
        wq = [_dot(wq_s[j, h], s_old[h].astype(BF16)) for h in heads]
        vnew16 = [(u_s[j, h] - wq[h][:c]).astype(BF16) for h in heads]
        o = [wq[h][c:] + _dot(attn_s[j, h], vnew16[h]) for h in heads]
        for h in heads:
            s_acc[h] = s_old[h] * egl[:, h:h + 1] + _dot_tn(kd_s[j, h], vnew16[h])
        for h in heads:
            on = o[h] * lax.rsqrt(jnp.mean(o[h] * o[h], axis=-1, keepdims=True) + EPS) * dnw_ref[...]
            o_s[pl.ds(pl.multiple_of(j * c, c), c), h * DN_DV:(h + 1) * DN_DV] = on
        return 0

    lax.fori_loop(0, NCH, recur, 0, unroll=True)

    ob_ref[...] = (o_s[...] * szg_ref[...].astype(F32)).astype(BF16)

    @pl.when(t == nt - 1)
    def _():
        s_ref[0] = s_acc[...]


def _delta_call(act, gb, szg, alog, dtb, dnw, *, batch, seq_len):
    c = DN_CHUNK
    nt = seq_len // TP
    row = lambda n: pl.BlockSpec((TP, n), lambda b, t: (b * nt + t, 0))
    return pl.pallas_call(
        functools.partial(_delta_kernel, nt=nt),
        grid=(batch, nt),
        in_specs=[row(DN_QKV), row(LANES), row(DN_V),
                  _resident((1, LANES)), _resident((1, LANES)), _resident((1, DN_DV))],
        out_specs=[row(DN_V),
                   pl.BlockSpec((1, DN_HEADS, DN_DK, DN_DV), lambda b, t: (b, 0, 0, 0))],
        out_shape=[jax.ShapeDtypeStruct((batch * seq_len, DN_V), BF16),
                   jax.ShapeDtypeStruct((batch, DN_HEADS, DN_DK, DN_DV), F32)],
        scratch_shapes=[pltpu.VMEM((DN_HEADS, DN_DK, DN_DV), F32),
                        pltpu.VMEM((TP, DN_V), F32),
                        pltpu.VMEM((NCH, DN_HEADS, c, DN_DV), F32),
                        pltpu.VMEM((NCH, DN_HEADS, 2 * c, DN_DK), BF16),
                        pltpu.VMEM((NCH, DN_HEADS, c, c), BF16),
                        pltpu.VMEM((NCH, DN_HEADS, c, DN_DK), BF16),
                        pltpu.VMEM((NCH * SUBLANES, LANES), F32)],
        compiler_params=_params(("arbitrary", "arbitrary")),
        name="delta",
    )(act, gb, szg, alog, dtb, dnw)


def _sample_pre_kernel(xr_ref, ggr_ref, qkv_ref, gb_ref, crnn_ref, cqkv_ref, h0_ref,
                       cw_ref, cb_ref, wa_ref, ba_ref, wx_ref, bx_ref, lam_ref,
                       cqw_ref, alog_ref, dtb_ref,
                       oa_ref, hn_ref, qn_ref, kn_ref, v_ref, eg_ref, beta_ref):
    xc = cb_ref[...] + cw_ref[CONV_W - 1:CONV_W, :] * xr_ref[...]
    for j in range(CONV_W - 1):
        xc = xc + cw_ref[j:j + 1, :] * crnn_ref[j]
    a, log_a, i = _rg_gates(xc, wa_ref, ba_ref[...], wx_ref, bx_ref[...], lam_ref[...])
    mult = jnp.sqrt(_neg_expm1(2.0 * log_a))
    h = a * h0_ref[...] + mult * (i * xc)
    hn_ref[...] = h
    oa_ref[...] = (h * ggr_ref[...].astype(F32)).astype(BF16)

    qc = cqw_ref[CONV_W - 1:CONV_W, :] * qkv_ref[...]
    for j in range(CONV_W - 1):
        qc = qc + cqw_ref[j:j + 1, :] * cqkv_ref[j]
    act = _silu(qc)
    for h_ in range(DN_HEADS):
        qh = act[:, h_ * DN_DK:(h_ + 1) * DN_DK]
        kh = act[:, DN_QK + h_ * DN_DK:DN_QK + (h_ + 1) * DN_DK]
        qn_ref[:, h_ * DN_DK:(h_ + 1) * DN_DK] = (
            qh * lax.rsqrt(jnp.sum(qh * qh, axis=-1, keepdims=True) + EPS) * (DN_DK ** -0.5))
        kn_ref[:, h_ * DN_DK:(h_ + 1) * DN_DK] = (
            kh * lax.rsqrt(jnp.sum(kh * kh, axis=-1, keepdims=True) + EPS))
    v_ref[...] = act[:, 2 * DN_QK:]
    gbv = gb_ref[...]
    eg_ref[...] = jnp.exp(-jnp.exp(alog_ref[...]) * _softplus(gbv + dtb_ref[...]))
    beta_ref[...] = _sigmoid(gbv)


def _sample_pre_call(xr, ggr, qkv, gb, crnn, cqkv, h0, cw, cb, wa, ba, wx, bx, lam, cqw, alog, dtb):
    n = xr.shape[0]
    args = (xr, ggr, qkv, gb, crnn, cqkv, h0, cw, cb, wa, ba, wx, bx, lam, cqw, alog, dtb)
    shp = lambda w: jax.ShapeDtypeStruct((n, w), F32)
    outs = [jax.ShapeDtypeStruct((n, D_RNN), BF16), shp(D_RNN), shp(DN_QK), shp(DN_QK), shp(DN_V), shp(LANES), shp(LANES)]
    return pl.pallas_call(
        _sample_pre_kernel,
        grid=(1,),
        in_specs=[_resident(a.shape) for a in args],
        out_specs=[pl.BlockSpec(o.shape, lambda i: (0, 0)) for o in outs],
        out_shape=outs,
        compiler_params=_params(("arbitrary",)),
        name="sample_pre",
    )(*args)


def _sample_state_kernel(qn_ref, kn_ref, v_ref, eg_ref, beta_ref, szg_ref, dnw_ref, s_ref,
                         ob_ref, sn_ref):
    srow = lax.broadcasted_iota(jnp.int32, (SUBLANES, DN_DK), 0)
    rows = lambda r0, r1: jnp.where(srow == 0, r0, jnp.where(srow == 1, r1, 0.0))
    chains = [(b, h) for b in range(SAMPLE_BB) for h in range(DN_HEADS)]
    sl = lambda h: slice(h * DN_DK, (h + 1) * DN_DK)
    q = {(b, h): qn_ref[0, b:b + 1, sl(h)] for b, h in chains}
    k = {(b, h): kn_ref[0, b:b + 1, sl(h)] for b, h in chains}
    eg = {(b, h): eg_ref[0, b:b + 1, h:h + 1] for b, h in chains}
    beta = {(b, h): beta_ref[0, b:b + 1, DN_HEADS + h:DN_HEADS + h + 1] for b, h in chains}
    ws_qs = {ch: _dot(rows((k[ch] * beta[ch]) * eg[ch], q[ch] * eg[ch]).astype(BF16),
                      s_ref[ch[0], ch[1]].astype(BF16)) for ch in chains}
    vnew = {(b, h): v_ref[0, b:b + 1, sl(h)] * beta[b, h] - ws_qs[b, h][0:1, :] for b, h in chains}
    for ch in chains:
        b, h = ch
        zero = jnp.zeros_like(k[ch])
        sn_ref[b, h] = s_ref[b, h] * eg[ch] + _dot_tn(rows(k[ch], zero).astype(BF16),
                                                     rows(vnew[ch], zero).astype(BF16))
    for ch in chains:
        b, h = ch
        o = ws_qs[ch][1:2, :] + jnp.sum(q[ch] * k[ch], axis=-1, keepdims=True) * vnew[ch]
        on = o * lax.rsqrt(jnp.mean(o * o, axis=-1, keepdims=True) + EPS) * dnw_ref[...]
        ob_ref[0, b:b + 1, sl(h)] = on * szg_ref[0, b:b + 1, sl(h)]


SAMPLE_BB = 8


def _sample_state_call(qn, kn, v, eg, beta, szg, dnw, s0):
    n = qn.shape[0]
    bb = SAMPLE_BB
    r3 = lambda a: a.reshape(n // bb, bb, a.shape[-1])
    vec = lambda w: pl.BlockSpec((1, bb, w), lambda i: (i, 0, 0))
    s_spec = pl.BlockSpec((bb, DN_HEADS, DN_DK, DN_DV), lambda i: (i, 0, 0, 0))
    ob, sn = pl.pallas_call(
        _sample_state_kernel,
        grid=(n // bb,),
        in_specs=[vec(DN_QK), vec(DN_QK), vec(DN_V), vec(LANES), vec(LANES), vec(DN_V),
                  _resident((1, DN_DV)), s_spec],
        out_specs=[vec(DN_V), s_spec],
        out_shape=[jax.ShapeDtypeStruct((n // bb, bb, DN_V), F32),
                   jax.ShapeDtypeStruct((n, DN_HEADS, DN_DK, DN_DV), F32)],
        compiler_params=_params(("arbitrary",)),
        name="sample_state",
    )(r3(qn), r3(kn), r3(v), r3(eg), r3(beta), r3(szg), dnw, s0)
    return ob.reshape(n, DN_V), sn


def _merge_kernel(x_ref, oa_ref, ob_ref, sga_ref, sgb_ref, ada_ref, wb_ref, wo_ref, o_ref, *, per_row, tps):
    ya = _dot(oa_ref[...], wb_ref[0].astype(BF16))
    yb = _dot(ob_ref[...], wb_ref[1].astype(BF16))
    merged = (sga_ref[...].astype(F32) * ya + sgb_ref[...].astype(F32) * yb).astype(BF16)
    gate = _ada_rows(ada_ref, 5, _tile_seq(per_row, tps))
    o_ref[...] = x_ref[...] + gate * _dot(merged, wo_ref[...].astype(BF16))


def _merge_call(x, oa, ob, sga, sgb, ada, wb, wo, *, per_row, tm, seq_len, n_dec):
    m = x.shape[0]
    row = pl.BlockSpec((tm, D_MODEL), lambda i: (i, 0))
    return pl.pallas_call(
        functools.partial(_merge_kernel, per_row=per_row, tps=seq_len // tm),
        grid=(m // tm,),
        in_specs=[row, row, row, row, row, _ada_spec(per_row, n_dec),
                  _resident((2, D_RNN, D_MODEL)), _resident((D_MODEL, D_MODEL))],
        out_specs=row,
        out_shape=jax.ShapeDtypeStruct((m, D_MODEL), F32),
        compiler_params=_params(("arbitrary",)),
        name="merge",
    )(x, oa, ob, sga, sgb, ada, wb, wo)


def kernel(x_prompt, x_sample, c_prompt, c_sample, state_rglru_h, state_rglru_conv, state_delta_S, state_delta_conv, w_ada, b_ada, norm_ffn1, w_ffn1_up, w_ffn1_down, norm_mix, w_in, conv_rnn_w, conv_rnn_b, rg_w_a, rg_b_a, rg_w_x, rg_b_x, rg_lambda, conv_qkv_w, dn_a_log, dn_dt_bias, dn_norm, w_branch, w_out, norm_ffn2, w_ffn2_up, w_ffn2_down, norm_final):
    batch, seq_len, _ = x_prompt.shape
    n_dec = x_sample.shape[0]
    assert w_ada.shape[0] == 1 and x_sample.shape[1] == 1 and seq_len % TP == 0

    row = lambda a: a.reshape(1, -1).astype(F32)
    wup1, wdn1 = w_ffn1_up[0], w_ffn1_down[0]
    wup2, wdn2 = w_ffn2_up[0], w_ffn2_down[0]
    w_t = jnp.transpose(w_in[0])
    n_ab = 2 * DN_HEADS
    assert w_t.shape[0] == AB_OFF + n_ab + N_HI
    w_lo = _wt_cast_call(w_t, 0, AB_OFF, AB_OFF)
    w_ab = _wt_cast_call(w_t, AB_OFF, n_ab, LANES)
    w_hi = _wt_cast_call(w_t, AB_OFF + n_ab, N_HI, N_HI)
    wb, wo = w_branch[0], w_out[0]
    wa, wx = rg_w_a[0].astype(BF16), rg_w_x[0].astype(BF16)
    lane_pad = lambda a: jnp.pad(a.reshape(1, -1).astype(F32), ((0, 0), (0, LANES - a.size)))
    alog = lane_pad(dn_a_log[0])
    dtb = lane_pad(dn_dt_bias[0])
    nf = row(norm_final)

    assert batch == SUBLANES and n_dec % SUBLANES == 0
    ada = _ada_call(jnp.concatenate([c_sample, c_prompt], axis=0), w_ada[0], row(b_ada[0]))

    mixer_w = (conv_rnn_w[0], row(conv_rnn_b[0]), wa, row(rg_b_a[0]), wx, row(rg_b_x[0]), row(rg_lambda[0]))

    kw = dict(per_row=False, seq_len=seq_len, tm=TP, n_dec=n_dec)
    xp = x_prompt.reshape(batch * seq_len, D_MODEL)
    xp, hmix = _ffn_call(xp, ada, row(norm_ffn1[0]), wup1, wdn1, row(norm_mix[0]), k0=0, post="mix", **kw)
    oa, act, szg, sga, sgb, gb, hp, cp, qp = _mixin_call(
        hmix, w_lo, w_hi, w_ab, *mixer_w, conv_qkv_w[0], batch=batch, seq_len=seq_len)
    ob, sp = _delta_call(act, gb, szg, alog, dtb, row(dn_norm[0]), batch=batch, seq_len=seq_len)
    xp = _merge_call(xp, oa, ob, sga, sgb, ada, wb, wo, **dict(kw, tm=2 * TP))
    y_prompt, = _ffn_call(xp, ada, row(norm_ffn2[0]), wup2, wdn2, nf, k0=6, post="final", **kw)

    kw = dict(per_row=True, seq_len=n_dec, tm=n_dec, n_dec=n_dec)
    xs = x_sample.reshape(n_dec, D_MODEL)
    xs, hmix_s = _ffn_call(xs, ada, row(norm_ffn1[0]), wup1, wdn1, row(norm_mix[0]), k0=0, post="mix", **kw)
    xr_s, ggr_s, qkv_s, szg_s, sga_s, sgb_s, gb_s = _inproj_call(hmix_s, w_lo, w_hi, w_ab, tm=n_dec)
    crnn = jnp.transpose(state_rglru_conv[0], (1, 0, 2))
    cqkv = jnp.transpose(state_delta_conv[0], (1, 0, 2))
    oa_s, hs, qn, kn, v, eg, beta = _sample_pre_call(
        xr_s, ggr_s, qkv_s, gb_s, crnn, cqkv, state_rglru_h[0], *mixer_w, conv_qkv_w[0], alog, dtb)
    ob_s, ss = _sample_state_call(qn, kn, v, eg, beta, szg_s.astype(F32), row(dn_norm[0]), state_delta_S[0])
    xs = _merge_call(xs, oa_s, ob_s.astype(BF16), sga_s, sgb_s, ada, wb, wo, **kw)
    y_sample, = _ffn_call(xs, ada, row(norm_ffn2[0]), wup2, wdn2, nf, k0=6, post="final", **kw)
    cs = jnp.concatenate([state_rglru_conv[0][:, 1:], xr_s[:, None, :]], axis=1)
    qs = jnp.concatenate([state_delta_conv[0][:, 1:], qkv_s[:, None, :]], axis=1)

    return (y_prompt.reshape(batch, seq_len, D_MODEL), y_sample.reshape(n_dec, 1, D_MODEL),
            hp.reshape(1, batch, D_RNN), cp[None], sp[None], qp[None],
            hs[None], cs[None], ss[None], qs[None])
```
